```python
import jax, jax.numpy as jnp
from jax import lax
import numpy as np

D_MODEL = 1024
BATCH = 1
SEQ = 16384
DEPTH = 4

HEAD_DIM = 64
ATTN_SCALE = HEAD_DIM ** -0.5
NORM_EPS = 1e-6
NEG_INF = -1e30

ROPE_THETA = 500000.0
ROPE_DIM = HEAD_DIM // 4

SWA_HEADS = 8
SWA_WIDTH = SWA_HEADS * HEAD_DIM
DILATED_PATTERNS = ((128, 1), (512, 4), (2048, 16))
SWA_BLOCK = 128

SSM_HEADS = 8
SSM_HEAD_DIM = 64
SSM_WIDTH = SSM_HEADS * SSM_HEAD_DIM
SSM_STATE = 128
SSM_GROUPS = 2
SSM_CONV = 4
SSM_CHUNK = 256
SSM_CONV_DIM = SSM_WIDTH + 2 * SSM_GROUPS * SSM_STATE

MOBA_HEADS = 8
MOBA_WIDTH = MOBA_HEADS * HEAD_DIM
MOBA_BLOCK = 256
MOBA_TOPK = 3
MOBA_QCHUNK = 128

MIX_WIDTH = SWA_WIDTH + SSM_WIDTH + MOBA_WIDTH
IN_PROJ = 3 * SWA_WIDTH + 3 * MOBA_WIDTH + SSM_WIDTH + SSM_CONV_DIM + SSM_HEADS

FFN_HIDDEN = -(-8 * D_MODEL // (3 * 256)) * 256

kernel_name = "hybrid_dilated_ssd_moba_trunk"


def _rms_norm(x, g):
    x32 = x.astype(jnp.float32)
    y = x32 * lax.rsqrt(jnp.mean(x32 * x32, axis=-1, keepdims=True) + NORM_EPS)
    return (y * g.astype(jnp.float32)).astype(x.dtype)


def _rope_tables(seq):
    pos = jnp.arange(seq, dtype=jnp.float32)
    inv = ROPE_THETA ** (-jnp.arange(0, ROPE_DIM, 2, dtype=jnp.float32) / ROPE_DIM)
    ang = pos[:, None] * inv[None, :]
    return jnp.cos(ang), jnp.sin(ang)


def _apply_rope(t, cos, sin):
    half = ROPE_DIM // 2
    xr = t[..., :ROPE_DIM].astype(jnp.float32)
    x1, x2 = xr[..., :half], xr[..., half:]
    rot = jnp.concatenate([x1 * cos - x2 * sin, x2 * cos + x1 * sin], axis=-1)
    return jnp.concatenate([rot.astype(t.dtype), t[..., ROPE_DIM:]], axis=-1)


def _split_heads(t, n_heads):
    b, s, _ = t.shape
    return t.reshape(b, s, n_heads, HEAD_DIM).transpose(0, 2, 1, 3)


def _merge_heads(t):
    b, h, s, d = t.shape
    return t.transpose(0, 2, 1, 3).reshape(b, s, h * d)


def _prep_qk(t, n_heads, gain, cos, sin):
    return _apply_rope(_rms_norm(_split_heads(t, n_heads), gain), cos, sin)


def _dilated_branch(q, k, v, window, dilation):
    b, h, s, dh = q.shape
    band = window // dilation
    assert band <= SWA_BLOCK
    span = dilation * SWA_BLOCK
    s_pad = -(-s // span) * span
    pad = [(0, 0), (0, 0), (0, s_pad - s), (0, 0)]

    def to_blocks(t):
        t = jnp.pad(t, pad).reshape(b, h, s_pad // dilation, dilation, dh)
        t = jnp.swapaxes(t, 2, 3)
        return t.reshape(b, h, dilation, -1, SWA_BLOCK, dh)

    def with_prev(t):
        prev = jnp.pad(t, [(0, 0)] * 3 + [(1, 0), (0, 0), (0, 0)])[:, :, :, :-1]
        return jnp.concatenate([prev, t], axis=4)

    qb, kb, vb = to_blocks(q), to_blocks(k), to_blocks(v)
    kk, vv = with_prev(kb), with_prev(vb)
    nb = qb.shape[3]
    sc = jnp.einsum('bhrnqd,bhrnkd->bhrnqk', qb, kk,
                    preferred_element_type=jnp.float32) * ATTN_SCALE
    qi = jnp.arange(SWA_BLOCK)[:, None]
    ki = jnp.arange(2 * SWA_BLOCK)[None, :]
    dist = qi + SWA_BLOCK - ki
    band_mask = (dist >= 0) & (dist <= band)
    blk_ok = (jnp.arange(nb)[:, None, None] > 0) | (ki >= SWA_BLOCK)[None]
    mask = band_mask[None] & blk_ok
    sc = jnp.where(mask, sc, NEG_INF)
    m = jnp.max(sc, axis=-1, keepdims=True)
    p = jnp.exp(sc - m)
    l = jnp.sum(p, axis=-1, keepdims=True)
    num = jnp.einsum('bhrnqk,bhrnkd->bhrnqd', p.astype(v.dtype), vv,
                     preferred_element_type=jnp.float32)

    def from_blocks(t):
        t = t.reshape(b, h, dilation, -1, t.shape[-1])
        t = jnp.swapaxes(t, 2, 3).reshape(b, h, s_pad, -1)
        return t[:, :, :s]

    return from_blocks(num), from_blocks(m), from_blocks(l)


def _dilated_attention(q, k, v):
    parts = [_dilated_branch(q, k, v, w, d) for (w, d) in DILATED_PATTERNS]
    m_all = parts[0][1]
    for _, m_i, _ in parts[1:]:
        m_all = jnp.maximum(m_all, m_i)
    num = 0.0
    den = 0.0
    for n_i, m_i, l_i in parts:
        w_i = jnp.exp(m_i - m_all)
        num = num + w_i * n_i
        den = den + w_i * l_i
    return (num / den).astype(q.dtype)


def _moba_attention(q, k, v):
    b, h, s, dh = q.shape
    s_pad = -(-s // MOBA_BLOCK) * MOBA_BLOCK
    pad = [(0, 0), (0, 0), (0, s_pad - s), (0, 0)]
    qp, kp, vp = jnp.pad(q, pad), jnp.pad(k, pad), jnp.pad(v, pad)
    nb = s_pad // MOBA_BLOCK
    kblk = kp.reshape(b, h, nb, MOBA_BLOCK, dh)
    vblk = vp.reshape(b, h, nb, MOBA_BLOCK, dh)
    kmean = jnp.mean(kblk.astype(jnp.float32), axis=3)
    gate = jnp.einsum('bhsd,bhnd->bhsn', qp, kmean.astype(qp.dtype),
                      preferred_element_type=jnp.float32)
    qblk = jnp.arange(s_pad) // MOBA_BLOCK
    past = jnp.arange(nb)[None, :] < qblk[:, None]
    gate = jnp.where(past, gate, NEG_INF)
    n_cols = max(nb, MOBA_TOPK)
    if n_cols > nb:
        gate = jnp.pad(gate, [(0, 0)] * 3 + [(0, n_cols - nb)], constant_values=NEG_INF)
    _, sel = lax.top_k(gate, MOBA_TOPK)
    valid = sel < qblk[:, None]
    sel = jnp.minimum(sel, nb - 1)

    nq = s_pad // MOBA_QCHUNK

    def chunks(t):
        return t.reshape(b, h, nq, MOBA_QCHUNK, t.shape[-1]).transpose(2, 0, 1, 3, 4)

    gather = jax.vmap(jax.vmap(lambda blocks, ids: blocks[ids]))
    n_sel = MOBA_TOPK * MOBA_BLOCK

    def step(args):
        qc, sc_idx, vc, ci = args
        kg = gather(kblk, sc_idx)
        vg = gather(vblk, sc_idx)
        own = (ci * MOBA_QCHUNK) // MOBA_BLOCK
        ko = lax.dynamic_index_in_dim(kblk, own, axis=2, keepdims=False)
        vo = lax.dynamic_index_in_dim(vblk, own, axis=2, keepdims=False)
        s_sel = jnp.einsum('bhqd,bhqjkd->bhqjk', qc, kg,
                           preferred_element_type=jnp.float32) * ATTN_SCALE
        s_sel = jnp.where(vc[..., None], s_sel, NEG_INF).reshape(b, h, MOBA_QCHUNK, n_sel)
        s_own = jnp.einsum('bhqd,bhkd->bhqk', qc, ko,
                           preferred_element_type=jnp.float32) * ATTN_SCALE
        qpos = ci * MOBA_QCHUNK + jnp.arange(MOBA_QCHUNK)
        kpos = own * MOBA_BLOCK + jnp.arange(MOBA_BLOCK)
        s_own = jnp.where(kpos[None, :] <= qpos[:, None], s_own, NEG_INF)
        p = jax.nn.softmax(jnp.concatenate([s_sel, s_own], axis=-1), axis=-1).astype(v.dtype)
        p_sel = p[..., :n_sel].reshape(b, h, MOBA_QCHUNK, MOBA_TOPK, MOBA_BLOCK)
        out = (jnp.einsum('bhqjk,bhqjkd->bhqd', p_sel, vg, preferred_element_type=jnp.float32)
               + jnp.einsum('bhqk,bhkd->bhqd', p[..., n_sel:], vo,
                            preferred_element_type=jnp.float32))
        return out.astype(q.dtype)

    out = lax.map(step, (chunks(qp), chunks(sel), chunks(valid), jnp.arange(nq)))
    out = out.transpose(1, 2, 0, 3, 4).reshape(b, h, s_pad, dh)
    return out[:, :, :s]


def _ssd_mixer(z, xbc, dt_raw, conv_w, conv_b, dt_bias, a_log, d_skip, norm_w):
    b, s, _ = xbc.shape
    xbc = lax.conv_general_dilated(
        xbc, conv_w[:, None, :], window_strides=(1,), padding=((SSM_CONV - 1, 0),),
        dimension_numbers=('NWC', 'WIO', 'NWC'), feature_group_count=SSM_CONV_DIM) + conv_b
    xbc = jax.nn.silu(xbc).astype(jnp.float32)
    gn = SSM_GROUPS * SSM_STATE
    xs = xbc[..., :SSM_WIDTH].reshape(b, s, SSM_HEADS, SSM_HEAD_DIM)
    bm = xbc[..., SSM_WIDTH:SSM_WIDTH + gn].reshape(b, s, SSM_GROUPS, SSM_STATE)
    cm = xbc[..., SSM_WIDTH + gn:].reshape(b, s, SSM_GROUPS, SSM_STATE)
    dt = jax.nn.softplus(dt_raw.astype(jnp.float32) + dt_bias.astype(jnp.float32))
    a = -jnp.exp(a_log.astype(jnp.float32))

    s_pad = -(-s // SSM_CHUNK) * SSM_CHUNK
    nc = s_pad // SSM_CHUNK

    def pad_seq(t):
        return jnp.pad(t, [(0, 0), (0, s_pad - s)] + [(0, 0)] * (t.ndim - 2))

    x_c = pad_seq(xs).reshape(b, nc, SSM_CHUNK, SSM_HEADS, SSM_HEAD_DIM)
    dt_c = pad_seq(dt).reshape(b, nc, SSM_CHUNK, SSM_HEADS)
    b_c = pad_seq(bm).reshape(b, nc, SSM_CHUNK, SSM_GROUPS, SSM_STATE)
    c_c = pad_seq(cm).reshape(b, nc, SSM_CHUNK, SSM_GROUPS, SSM_STATE)
    hpg = SSM_HEADS // SSM_GROUPS
    b_h = jnp.repeat(b_c, hpg, axis=3)
    c_h = jnp.repeat(c_c, hpg, axis=3)

    da_cs = jnp.cumsum((dt_c * a).transpose(0, 3, 1, 2), axis=-1)
    xdt = x_c * dt_c[..., None]
    seg = da_cs[..., :, None] - da_cs[..., None, :]
    causal = jnp.tril(jnp.ones((SSM_CHUNK, SSM_CHUNK), dtype=bool))
    decay = jnp.exp(jnp.where(causal, seg, NEG_INF))
    cb = jnp.einsum('bclgn,bcsgn->bgcls', c_c, b_c)
    cb = jnp.repeat(cb, hpg, axis=1)
    y_diag = jnp.einsum('bhcls,bcshp->bclhp', cb * decay, xdt)

    decay_to_end = jnp.exp(da_cs[..., -1:] - da_cs)
    states = jnp.einsum('bclhn,bhcl,bclhp->bchpn', b_h, decay_to_end, xdt)
    chunk_decay = jnp.exp(da_cs[..., -1])

    def scan_fn(hst, inp):
        st, dec = inp
        return hst * dec[..., None, None] + st, hst

    h0 = jnp.zeros((b, SSM_HEADS, SSM_HEAD_DIM, SSM_STATE), jnp.float32)
    _, h_in = lax.scan(scan_fn, h0, (states.transpose(1, 0, 2, 3, 4),
                                     chunk_decay.transpose(2, 0, 1)))
    h_in = h_in.transpose(1, 0, 2, 3, 4)
    y_off = jnp.einsum('bclhn,bchpn,bhcl->bclhp', c_h, h_in, jnp.exp(da_cs))
    y = y_diag + y_off + x_c * d_skip.astype(jnp.float32)[:, None]
    y = y.reshape(b, s_pad, SSM_WIDTH)[:, :s]

    g = y * jax.nn.silu(z.astype(jnp.float32))
    g = g.reshape(b, s, SSM_GROUPS, SSM_WIDTH // SSM_GROUPS)
    g = g * lax.rsqrt(jnp.mean(g * g, axis=-1, keepdims=True) + NORM_EPS)
    g = g.reshape(b, s, SSM_WIDTH) * norm_w.astype(jnp.float32)
    return g.astype(z.dtype)


def _token_mixer(h, w_in, qn_swa, kn_swa, qn_moba, kn_moba, conv_w, conv_b,
                 dt_bias, a_log, d_skip, ssm_norm, w_out, cos, sin):
    proj = h @ w_in
    sizes = [SWA_WIDTH] * 3 + [MOBA_WIDTH] * 3 + [SSM_WIDTH, SSM_CONV_DIM, SSM_HEADS]
    q_a, k_a, v_a, q_c, k_c, v_c, z, xbc, dt_raw = jnp.split(
        proj, np.cumsum(sizes)[:-1].tolist(), axis=-1)
    y_a = _dilated_attention(_prep_qk(q_a, SWA_HEADS, qn_swa, cos, sin),
                             _prep_qk(k_a, SWA_HEADS, kn_swa, cos, sin),
                             _split_heads(v_a, SWA_HEADS))
    y_b = _ssd_mixer(z, xbc, dt_raw, conv_w, conv_b, dt_bias, a_log, d_skip, ssm_norm)
    y_c = _moba_attention(_prep_qk(q_c, MOBA_HEADS, qn_moba, cos, sin),
                          _prep_qk(k_c, MOBA_HEADS, kn_moba, cos, sin),
                          _split_heads(v_c, MOBA_HEADS))
    mix = jnp.concatenate([_merge_heads(y_a), y_b, _merge_heads(y_c)], axis=-1)
    return mix @ w_out


def setup_inputs(seed: int = 0) -> dict:
    key = jax.random.key(seed)
    ks = jax.random.split(key, 24)
    f32 = jnp.float32

    def nrm(k, shape, scale):
        return jax.random.normal(k, shape, f32) * scale

    dt0 = jnp.exp(jax.random.uniform(ks[13], (DEPTH, SSM_HEADS), f32,
                                     np.log(1e-3), np.log(1e-1)))
    return {
        "x": nrm(ks[0], (BATCH, SEQ, D_MODEL), 1.0),
        "c": nrm(ks[1], (BATCH, D_MODEL), 1.0),
        "ada_w": nrm(ks[2], (DEPTH, D_MODEL, 6 * D_MODEL), 0.5 * D_MODEL ** -0.5),
        "ada_b": nrm(ks[3], (DEPTH, 6 * D_MODEL), 0.02),
        "norm_mix": 1.0 + nrm(ks[4], (DEPTH, D_MODEL), 0.02),
        "w_in": nrm(ks[5], (DEPTH, D_MODEL, IN_PROJ), D_MODEL ** -0.5),
        "qn_swa": 1.0 + nrm(ks[6], (DEPTH, HEAD_DIM), 0.02),
        "kn_swa": 1.0 + nrm(ks[7], (DEPTH, HEAD_DIM), 0.02),
        "qn_moba": 1.0 + nrm(ks[8], (DEPTH, HEAD_DIM), 0.02),
        "kn_moba": 1.0 + nrm(ks[9], (DEPTH, HEAD_DIM), 0.02),
        "conv_w": nrm(ks[10], (DEPTH, SSM_CONV, SSM_CONV_DIM), SSM_CONV ** -0.5),
        "conv_b": nrm(ks[11], (DEPTH, SSM_CONV_DIM), 0.02),
        "dt_bias": dt0 + jnp.log(-jnp.expm1(-dt0)),
        "a_log": jnp.log(jax.random.uniform(ks[12], (DEPTH, SSM_HEADS), f32, 1.0, 16.0)),
        "d_skip": 1.0 + nrm(ks[14], (DEPTH, SSM_HEADS), 0.1),
        "ssm_norm": 1.0 + nrm(ks[15], (DEPTH, SSM_WIDTH), 0.02),
        "w_out": nrm(ks[16], (DEPTH, MIX_WIDTH, D_MODEL), MIX_WIDTH ** -0.5),
        "norm_ffn": 1.0 + nrm(ks[17], (DEPTH, D_MODEL), 0.02),
        "w_gate": nrm(ks[18], (DEPTH, D_MODEL, FFN_HIDDEN), D_MODEL ** -0.5),
        "w_up": nrm(ks[19], (DEPTH, D_MODEL, FFN_HIDDEN), D_MODEL ** -0.5),
        "w_down": nrm(ks[20], (DEPTH, FFN_HIDDEN, D_MODEL), FFN_HIDDEN ** -0.5),
    }


def reference(x, c, ada_w, ada_b, norm_mix, w_in, qn_swa, kn_swa, qn_moba, kn_moba,
              conv_w, conv_b, dt_bias, a_log, d_skip, ssm_norm, w_out, norm_ffn,
              w_gate, w_up, w_down):
    cos, sin = _rope_tables(x.shape[1])
    c_act = jax.nn.silu(c)
    for l in range(DEPTH):
        mod = (c_act @ ada_w[l] + ada_b[l])[:, None, :]
        shift_m, scale_m, gate_m, shift_f, scale_f, gate_f = jnp.split(mod, 6, axis=-1)
        h = _rms_norm(x, norm_mix[l]) * (1.0 + scale_m) + shift_m
        y = _token_mixer(h, w_in[l], qn_swa[l], kn_swa[l], qn_moba[l], kn_moba[l],
                         conv_w[l], conv_b[l], dt_bias[l], a_log[l], d_skip[l],
                         ssm_norm[l], w_out[l], cos, sin)
        x = x + gate_m * y
        h = _rms_norm(x, norm_ffn[l]) * (1.0 + scale_f) + shift_f
        ffn = (jax.nn.silu(h @ w_gate[l]) * (h @ w_up[l])) @ w_down[l]
        x = x + gate_f * ffn
    return x
```

```python
import functools

import jax
import jax.numpy as jnp
from jax import lax
from jax.experimental import pallas as pl
from jax.experimental.pallas import tpu as pltpu

F32 = jnp.float32
BF16 = jnp.bfloat16

D_MODEL = 1024
HEAD_DIM = 64
N_HEADS = 8
WIDTH = N_HEADS * HEAD_DIM
ATTN_SCALE = HEAD_DIM ** -0.5
NORM_EPS = 1e-6
NEG = -1e30

ROPE_THETA = 500000.0
ROPE_DIM = HEAD_DIM // 4
ROPE_HALF = ROPE_DIM // 2

DILATIONS = (1, 4, 16)
SWA_BLOCK = 128

SSM_STATE = 128
SSM_GROUPS = 2
SSM_CONV = 4
SSM_CHUNK = 256
SSM_CONV_DIM = WIDTH + 2 * SSM_GROUPS * SSM_STATE

MOBA_BLOCK = 256
MOBA_TOPK = 3

LANES = 128
DT_PAD = LANES
IN_PROJ_PAD = 6 * WIDTH + WIDTH + SSM_CONV_DIM + DT_PAD
OFF_A, OFF_C, OFF_Z, OFF_XBC, OFF_DT = 0, 3 * WIDTH, 6 * WIDTH, 7 * WIDTH, 7 * WIDTH + SSM_CONV_DIM

VMEM_LIMIT = 56 * 1024 * 1024


def _cparams(*sem):
    return pltpu.CompilerParams(dimension_semantics=sem, vmem_limit_bytes=VMEM_LIMIT)


def _split3(a):
    hi = a.astype(BF16)
    r1 = a - hi.astype(F32)
    mid = r1.astype(BF16)
    lo = (r1 - mid.astype(F32)).astype(BF16)
    return hi, mid, lo


def _dot(a, b):
    return jnp.dot(a, b, preferred_element_type=F32)


def _dot_nt(a, b):
    return lax.dot_general(a, b, (((1,), (1,)), ((), ())), preferred_element_type=F32)


def _dot_exact_rhs(a, b_bf16):
    hi, mid, lo = _split3(a)
    return _dot(hi, b_bf16) + _dot(mid, b_bf16) + _dot(lo, b_bf16)


def _dot_f32(a, b):
    ah, am, al = _split3(a)
    bh, bm, bl = _split3(b)
    return (_dot(ah, bh) + (_dot(ah, bm) + _dot(am, bh))
            + (_dot(ah, bl) + _dot(am, bm) + _dot(al, bh)))


def _silu(x):
    return x / (1.0 + jnp.exp(-x))


def _softplus(x):
    return jnp.maximum(x, 0.0) + jnp.log1p(jnp.exp(-jnp.abs(x)))


def _ada_kernel(c_ref, w_ref, b_ref, o_ref):
    c = c_ref[...]
    o_ref[0] = _dot_f32(_silu(c), w_ref[0]) + b_ref[0]


def _ada_modulation(c, ada_w, ada_b):
    depth, d, n = ada_w.shape
    tn = 1024
    c8 = jnp.broadcast_to(c, (8, d))
    out = pl.pallas_call(
        _ada_kernel,
        grid=(depth, n // tn),
        in_specs=[pl.BlockSpec((8, d), lambda l, j: (0, 0)),
                  pl.BlockSpec((1, d, tn), lambda l, j: (l, 0, j)),
                  pl.BlockSpec((1, 1, tn), lambda l, j: (l, 0, j))],
        out_specs=pl.BlockSpec((1, 8, tn), lambda l, j: (l, 0, j)),
        out_shape=jax.ShapeDtypeStruct((depth, 8, n), F32),
        compiler_params=_cparams("arbitrary", "arbitrary"),
        name="ada_modulation",
    )(c8, ada_w, ada_b.reshape(depth, 1, n))
    return out[:, 0:1, :]


def _mod_norm(x, g, sc, sh):
    var = jnp.mean(x * x, axis=-1, keepdims=True)
    return (x * lax.rsqrt(var + NORM_EPS) * g) * (1.0 + sc) + sh


def _inproj_kernel(x_ref, g_ref, sc_ref, sh_ref, w_ref, oa_ref, oc_ref, oz_ref, ox_ref, od_ref):
    h = _mod_norm(x_ref[...], g_ref[...], sc_ref[...], sh_ref[...]).astype(BF16)
    oa_ref[...] = _dot(h, w_ref[:, OFF_A:OFF_C])
    oc_ref[...] = _dot(h, w_ref[:, OFF_C:OFF_Z])
    oz_ref[...] = _dot(h, w_ref[:, OFF_Z:OFF_XBC])
    ox_ref[...] = _dot(h, w_ref[:, OFF_XBC:OFF_DT])
    od_ref[...] = _dot(h, w_ref[:, OFF_DT:IN_PROJ_PAD])


def _in_projection(x, g, sc, sh, w_pad):
    s, d = x.shape
    tm = 256
    row = lambda n: pl.BlockSpec((tm, n), lambda i: (i, 0))
    vec = pl.BlockSpec((1, d), lambda i: (0, 0))
    return pl.pallas_call(
        _inproj_kernel,
        grid=(s // tm,),
        in_specs=[row(d), vec, vec, vec, pl.BlockSpec((d, IN_PROJ_PAD), lambda i: (0, 0))],
        out_specs=[row(3 * WIDTH), row(3 * WIDTH), row(WIDTH), row(SSM_CONV_DIM), row(DT_PAD)],
        out_shape=[jax.ShapeDtypeStruct((s, n), F32)
                   for n in (3 * WIDTH, 3 * WIDTH, WIDTH, SSM_CONV_DIM, DT_PAD)],
        compiler_params=_cparams("arbitrary"),
        name="in_projection",
    )(x, g, sc, sh, w_pad)


def _head_norm_rope(x, gain, hsum, cos_t, sin_lo, sin_hi):
    ss = _dot_exact_rhs(x * x, hsum)
    y = x * lax.rsqrt(ss * (1.0 / HEAD_DIM) + NORM_EPS) * gain
    return (y * cos_t + pltpu.roll(y, ROPE_HALF, 1) * sin_hi
            + pltpu.roll(y, LANES - ROPE_HALF, 1) * sin_lo)


def _prep_kernel(a_ref, c_ref, gqa_ref, gka_ref, gqc_ref, gkc_ref, hsum_ref, cos_ref, slo_ref, shi_ref,
                 qa_ref, ka_ref, va_ref, qc_ref, kc_ref, vc_ref, km_ref):
    hsum = hsum_ref[...]
    cos_t, slo, shi = cos_ref[...], slo_ref[...], shi_ref[...]
    rope = lambda x, g: _head_norm_rope(x, g, hsum, cos_t, slo, shi)
    for p in range(WIDTH // LANES):
        lo, hi = p * LANES, (p + 1) * LANES
        qa_ref[:, lo:hi] = (rope(a_ref[:, lo:hi], gqa_ref[...]) * ATTN_SCALE).astype(BF16)
        ka_ref[:, lo:hi] = rope(a_ref[:, WIDTH + lo:WIDTH + hi], gka_ref[...]).astype(BF16)
        qc_ref[:, lo:hi] = rope(c_ref[:, lo:hi], gqc_ref[...])
        kc = rope(c_ref[:, WIDTH + lo:WIDTH + hi], gkc_ref[...])
        kc_ref[:, lo:hi] = kc
        km_ref[0, :, lo:hi] = jnp.mean(kc, axis=0, keepdims=True)
    va_ref[...] = a_ref[:, 2 * WIDTH:].astype(BF16)
    vc_ref[...] = c_ref[:, 2 * WIDTH:].astype(BF16)


def _prep_qkv(qkv_a, qkv_c, gains, hsum, rope_tabs):
    s = qkv_a.shape[0]
    tm = MOBA_BLOCK
    row = lambda n: pl.BlockSpec((tm, n), lambda i: (i, 0))
    cst = lambda r, n: pl.BlockSpec((r, n), lambda i: (0, 0))
    return pl.pallas_call(
        _prep_kernel,
        grid=(s // tm,),
        in_specs=[row(3 * WIDTH), row(3 * WIDTH)] + [cst(1, LANES)] * 4 + [cst(LANES, LANES)]
                 + [row(LANES)] * 3,
        out_specs=[row(WIDTH)] * 6 + [pl.BlockSpec((1, 1, WIDTH), lambda i: (i, 0, 0))],
        out_shape=[jax.ShapeDtypeStruct((s, WIDTH), BF16)] * 3
                  + [jax.ShapeDtypeStruct((s, WIDTH), F32)] * 2
                  + [jax.ShapeDtypeStruct((s, WIDTH), BF16),
                     jax.ShapeDtypeStruct((s // tm, 1, WIDTH), F32)],
        compiler_params=_cparams("arbitrary"),
        name="prep_qkv",
    )(qkv_a, qkv_c, *gains, hsum, *rope_tabs)


def _dilated_kernel(*refs, first, last, n_sub):
    if first:
        q_ref, kc_ref, kp_ref, vc_ref, vp_ref = refs[:5]
        outs = refs[5:]
    else:
        q_ref, kc_ref, kp_ref, vc_ref, vp_ref, m_ref, l_ref, a_ref = refs[:8]
        outs = refs[8:]
    n = pl.program_id(1)
    lane = lax.broadcasted_iota(jnp.int32, (SWA_BLOCK, LANES), 1)
    head_a = lane < HEAD_DIM
    qi = lax.broadcasted_iota(jnp.int32, (SWA_BLOCK, 2 * SWA_BLOCK), 0)
    ki = lax.broadcasted_iota(jnp.int32, (SWA_BLOCK, 2 * SWA_BLOCK), 1)
    band = (ki >= qi) & (ki <= qi + SWA_BLOCK)
    for b in range(n_sub):
        rows = slice(b * SWA_BLOCK, (b + 1) * SWA_BLOCK)
        q = q_ref[rows, :].astype(F32)
        if b == 0:
            k_cat = jnp.concatenate([kp_ref[...], kc_ref[rows, :]], axis=0)
            v_cat = jnp.concatenate([vp_ref[...], vc_ref[rows, :]], axis=0)
            mask = band & ((ki >= SWA_BLOCK) | (n > 0))
        else:
            prev_rows = slice((b - 1) * SWA_BLOCK, (b + 1) * SWA_BLOCK)
            k_cat, v_cat = kc_ref[prev_rows, :], vc_ref[prev_rows, :]
            mask = band
        if first:
            m_in = jnp.full((SWA_BLOCK, LANES), NEG, F32)
            l_in = jnp.zeros((SWA_BLOCK, LANES), F32)
            a_in = jnp.zeros((SWA_BLOCK, LANES), F32)
        else:
            m_in, l_in, a_in = m_ref[rows, :], l_ref[rows, :], a_ref[rows, :]
        parts = []
        for sel, col in ((head_a, 0), (~head_a, HEAD_DIM)):
            s = _dot_nt(jnp.where(sel, q, 0.0).astype(BF16), k_cat)
            s = jnp.where(mask, s, NEG)
            m_new = jnp.maximum(m_in[:, col:col + 1], jnp.max(s, axis=-1, keepdims=True))
            p = jnp.exp(s - m_new)
            parts.append((m_new, jnp.sum(p, axis=-1, keepdims=True), _dot(p.astype(BF16), v_cat)))
        (m_a, s_a, pv_a), (m_b, s_b, pv_b) = parts
        m_new = jnp.where(head_a, m_a, m_b)
        alpha = jnp.exp(m_in - m_new)
        l_new = alpha * l_in + jnp.where(head_a, s_a, s_b)
        a_new = alpha * a_in + jnp.where(head_a, pv_a, pv_b)
        if last:
            outs[0][rows, :] = (a_new / l_new).astype(BF16)
        else:
            outs[0][rows, :] = m_new
            outs[1][rows, :] = l_new
            outs[2][rows, :] = a_new


def _dilated_branch(q, k, v, state, dilation, last):
    s = q.shape[0]
    m_rows, cols = s // dilation, dilation * WIDTH
    t = min(512, m_rows)
    n_sub = t // SWA_BLOCK
    view = lambda x: x.reshape(m_rows, cols)
    cur = pl.BlockSpec((t, LANES), lambda c, n: (n, c))
    prev = pl.BlockSpec((SWA_BLOCK, LANES), lambda c, n: (jnp.maximum(n * n_sub - 1, 0), c))
    first = state is None
    ins = [view(q), view(k), view(k), view(v), view(v)] + ([] if first else [view(x) for x in state])
    in_specs = [cur, cur, prev, cur, prev] + ([] if first else [cur] * 3)
    if last:
        out_shape = [jax.ShapeDtypeStruct((m_rows, cols), BF16)]
    else:
        out_shape = [jax.ShapeDtypeStruct((m_rows, cols), F32)] * 3
    out = pl.pallas_call(
        functools.partial(_dilated_kernel, first=first, last=last, n_sub=n_sub),
        grid=(cols // LANES, m_rows // t),
        in_specs=in_specs,
        out_specs=[cur] * len(out_shape),
        out_shape=out_shape,
        compiler_params=_cparams("arbitrary", "arbitrary"),
        name=f"dilated_attention_d{dilation}",
    )(*ins)
    return [x.reshape(s, WIDTH) for x in out]


def _dilated_attention(q, k, v):
    state = None
    for i, d in enumerate(DILATIONS):
        state = _dilated_branch(q, k, v, state, d, last=(i == len(DILATIONS) - 1))
    return state[0]


def _moba_gate_kernel(q_ref, k_ref, km_ref, qaug_ref, kaug_ref):
    i = pl.program_id(0)
    shape = (MOBA_BLOCK, LANES)
    lane = lax.broadcasted_iota(jnp.int32, shape, 1)
    is_feat = lane < HEAD_DIM
    blk = lane - HEAD_DIM
    past = (~is_feat) & (blk < i)
    own = blk == i
    ninf = float("-inf")
    for h in range(N_HEADS):
        cols = slice((h // 2) * LANES, (h // 2 + 1) * LANES)
        q, k = q_ref[:, cols], k_ref[:, cols]
        if h % 2:
            q, k = pltpu.roll(q, HEAD_DIM, 1), pltpu.roll(k, HEAD_DIM, 1)
        gate = _dot_f32(jnp.where(is_feat, q, 0.0), km_ref[h])
        cand = jnp.where(past, gate, ninf)
        chosen = own
        for _ in range(MOBA_TOPK):
            best = jnp.max(cand, axis=-1, keepdims=True)
            idx = jnp.min(jnp.where(cand == best, lane, 2 * LANES), axis=-1, keepdims=True)
            pick = (lane == idx) & (best > ninf)
            chosen = chosen | pick
            cand = jnp.where(pick, ninf, cand)
        bias = jnp.where(chosen, 0.0, NEG)
        qaug_ref[h] = jnp.where(is_feat, q * ATTN_SCALE, bias).astype(BF16)
        kaug_ref[h] = jnp.where(is_feat, k, jnp.where(own, 1.0, 0.0)).astype(BF16)


def _moba_gate(qc, kc, km_mats):
    s = qc.shape[0]
    row = pl.BlockSpec((MOBA_BLOCK, WIDTH), lambda i: (i, 0))
    aug = pl.BlockSpec((N_HEADS, MOBA_BLOCK, LANES), lambda i: (0, i, 0))
    return pl.pallas_call(
        _moba_gate_kernel,
        grid=(s // MOBA_BLOCK,),
        in_specs=[row, row, pl.BlockSpec((N_HEADS, LANES, LANES), lambda i: (0, 0, 0))],
        out_specs=[aug, aug],
        out_shape=[jax.ShapeDtypeStruct((N_HEADS, s, LANES), BF16)] * 2,
        compiler_params=_cparams("arbitrary"),
        name="moba_gate",
    )(qc, kc, km_mats)


def _moba_attn_kernel(qa_ref, qb_ref, ka_ref, kb_ref, v_ref, o_ref):
    i = pl.program_id(1)
    ri = lax.broadcasted_iota(jnp.int32, (MOBA_BLOCK, MOBA_BLOCK), 0)
    ci = lax.broadcasted_iota(jnp.int32, (MOBA_BLOCK, MOBA_BLOCK), 1)
    causal = ci <= ri
    outs = []
    for q_ref, k_ref in ((qa_ref, ka_ref), (qb_ref, kb_ref)):
        q = q_ref[0]
        own = pl.ds(pl.multiple_of(i * MOBA_BLOCK, MOBA_BLOCK), MOBA_BLOCK)
        s = jnp.where(causal, _dot_nt(q, k_ref[0, own, :]), NEG)
        m0 = jnp.max(s, axis=-1, keepdims=True)
        p = jnp.exp(s - m0)
        l0 = jnp.sum(p, axis=-1, keepdims=True)
        acc0 = _dot(p.astype(BF16), v_ref[own, :])

        def body(j, carry, q=q, k_ref=k_ref):
            m, l, acc = carry
            rows = pl.ds(pl.multiple_of(j * MOBA_BLOCK, MOBA_BLOCK), MOBA_BLOCK)
            s = _dot_nt(q, k_ref[0, rows, :])
            m_new = jnp.maximum(m, jnp.max(s, axis=-1, keepdims=True))
            alpha = jnp.exp(m - m_new)
            p = jnp.exp(s - m_new)
            l = alpha * l + jnp.sum(p, axis=-1, keepdims=True)
            acc = alpha * acc + _dot(p.astype(BF16), v_ref[rows, :])
            return m_new, l, acc

        _, l, acc = lax.fori_loop(0, i, body, (m0, l0, acc0))
        outs.append(acc / l)
    lane = lax.broadcasted_iota(jnp.int32, (MOBA_BLOCK, LANES), 1)
    o_ref[...] = jnp.where(lane < HEAD_DIM, outs[0], outs[1]).astype(BF16)


def _moba_attention(q_aug, k_aug, v):
    s = v.shape[0]
    qspec = lambda par: pl.BlockSpec((1, MOBA_BLOCK, LANES), lambda p, i: (2 * p + par, i, 0))
    kspec = lambda par: pl.BlockSpec((1, s, LANES), lambda p, i: (2 * p + par, 0, 0))
    return pl.pallas_call(
        _moba_attn_kernel,
        grid=(N_HEADS // 2, s // MOBA_BLOCK),
        in_specs=[qspec(0), qspec(1), kspec(0), kspec(1), pl.BlockSpec((s, LANES), lambda p, i: (0, p))],
        out_specs=pl.BlockSpec((MOBA_BLOCK, LANES), lambda p, i: (i, p)),
        out_shape=jax.ShapeDtypeStruct((s, WIDTH), BF16),
        compiler_params=_cparams("arbitrary", "arbitrary"),
        name="moba_attention",
    )(q_aug, q_aug, k_aug, k_aug, v)


def _ssd_kernel(z_ref, xbc_ref, dt_ref, cw_ref, cb_ref, dtb_ref, alog_ref, dsk_ref, nw_ref, exp_ref, tri_ref,
                o_ref, state_ref, halo_ref):
    L = SSM_CHUNK

    @pl.when(pl.program_id(0) == 0)
    def _():
        state_ref[...] = jnp.zeros_like(state_ref)
        halo_ref[...] = jnp.zeros_like(halo_ref)

    cur = xbc_ref[...]
    prev8 = halo_ref[...]
    row8 = lax.broadcasted_iota(jnp.int32, (8, SSM_CONV_DIM), 0)
    conv = cb_ref[...] + cur * cw_ref[SSM_CONV - 1:SSM_CONV, :]
    for k in range(1, SSM_CONV):
        rolled = pltpu.roll(cur, k, 0)
        head = jnp.where(row8 < k, pltpu.roll(prev8, k, 0), rolled[0:8, :])
        shifted = jnp.concatenate([head, rolled[8:, :]], axis=0)
        conv = conv + shifted * cw_ref[SSM_CONV - 1 - k:SSM_CONV - k, :]
    halo_ref[...] = cur[L - 8:, :]
    act = _silu(conv)
    xs, bm, cm = act[:, :WIDTH], act[:, WIDTH:WIDTH + 2 * SSM_STATE], act[:, WIDTH + 2 * SSM_STATE:]

    dt = _softplus(dt_ref[...] + dtb_ref[...])
    da = dt * (-jnp.exp(alog_ref[...]))
    cs = _dot_exact_rhs_left(tri_ref[...], da)
    expand = exp_ref[...]
    cs_b = _dot_exact_rhs(cs, expand)
    dt_b = _dot_exact_rhs(dt, expand)
    cs_last = cs_b[L - 1:L, :]
    xdt = xs * dt_b
    cs_t = cs.T

    ri = lax.broadcasted_iota(jnp.int32, (L, L), 0)
    ci = lax.broadcasted_iota(jnp.int32, (L, L), 1)
    causal = ci <= ri
    lane = lax.broadcasted_iota(jnp.int32, (L, LANES), 1)
    bmb, cmb = bm.astype(BF16), cm.astype(BF16)
    xdtb = xdt.astype(BF16)
    y_pairs = []
    for p in range(N_HEADS // 2):
        g = (2 * p) // (N_HEADS // SSM_GROUPS)
        grp = slice(g * SSM_STATE, (g + 1) * SSM_STATE)
        cbm = _dot_nt(cmb[:, grp], bmb[:, grp])
        x_pair = xdtb[:, p * LANES:(p + 1) * LANES]
        halves = []
        for h in (2 * p, 2 * p + 1):
            seg = cs[:, h:h + 1] - cs_t[h:h + 1, :]
            decay = jnp.exp(jnp.where(causal, seg, NEG))
            halves.append(_dot((cbm * decay).astype(BF16), x_pair))
        y_pairs.append(jnp.where(lane < HEAD_DIM, halves[0], halves[1]))
    y_diag = jnp.concatenate(y_pairs, axis=1)

    state = state_ref[...]
    stb = state.astype(BF16)
    half = WIDTH // SSM_GROUPS
    y_off = jnp.concatenate(
        [_dot(cmb[:, g * SSM_STATE:(g + 1) * SSM_STATE], stb[:, g * half:(g + 1) * half])
         for g in range(SSM_GROUPS)], axis=1) * jnp.exp(cs_b)
    y = y_diag + y_off + xs * dsk_ref[...]

    w = (xdt * jnp.exp(cs_last - cs_b)).astype(BF16)
    bm_t = bm.T.astype(BF16)
    new = jnp.concatenate(
        [_dot(bm_t[g * SSM_STATE:(g + 1) * SSM_STATE, :], w[:, g * half:(g + 1) * half])
         for g in range(SSM_GROUPS)], axis=1)
    state_ref[...] = state * jnp.exp(cs_last) + new

    gt = y * _silu(z_ref[...])
    outs = []
    for g in range(SSM_GROUPS):
        gg = gt[:, g * half:(g + 1) * half]
        outs.append(gg * lax.rsqrt(jnp.mean(gg * gg, axis=-1, keepdims=True) + NORM_EPS))
    o_ref[...] = (jnp.concatenate(outs, axis=1) * nw_ref[...]).astype(BF16)


def _dot_exact_rhs_left(a_bf16, b):
    hi, mid, lo = _split3(b)
    return _dot(a_bf16, hi) + _dot(a_bf16, mid) + _dot(a_bf16, lo)


def _ssd_mixer(z, xbc, dt_raw, conv_w, conv_b, dt_bias, a_log, d_skip, norm_w, expand, tri):
    s = z.shape[0]
    L = SSM_CHUNK
    row = lambda n: pl.BlockSpec((L, n), lambda c: (c, 0))
    cst = lambda r, n: pl.BlockSpec((r, n), lambda c: (0, 0))
    return pl.pallas_call(
        _ssd_kernel,
        grid=(s // L,),
        in_specs=[row(WIDTH), row(SSM_CONV_DIM), row(DT_PAD), cst(SSM_CONV, SSM_CONV_DIM),
                  cst(1, SSM_CONV_DIM), cst(1, DT_PAD), cst(1, DT_PAD), cst(1, WIDTH), cst(1, WIDTH),
                  cst(DT_PAD, WIDTH), cst(L, L)],
        out_specs=row(WIDTH),
        out_shape=jax.ShapeDtypeStruct((s, WIDTH), BF16),
        scratch_shapes=[pltpu.VMEM((SSM_STATE, WIDTH), F32), pltpu.VMEM((8, SSM_CONV_DIM), F32)],
        compiler_params=_cparams("arbitrary"),
        name="ssd_mixer",
    )(z, xbc, dt_raw, conv_w, conv_b, dt_bias, a_log, d_skip, norm_w, expand, tri)


def _outproj_kernel(x_ref, ya_ref, yb_ref, yc_ref, w_ref, gate_ref, o_ref):
    y = (_dot(ya_ref[...], w_ref[0:WIDTH, :]) + _dot(yb_ref[...], w_ref[WIDTH:2 * WIDTH, :])
         + _dot(yc_ref[...], w_ref[2 * WIDTH:, :]))
    o_ref[...] = x_ref[...] + gate_ref[...] * y


def _out_projection(x, ya, yb, yc, w_out, gate):
    s, d = x.shape
    tm = 512
    row = lambda n: pl.BlockSpec((tm, n), lambda i: (i, 0))
    return pl.pallas_call(
        _outproj_kernel,
        grid=(s // tm,),
        in_specs=[row(d), row(WIDTH), row(WIDTH), row(WIDTH),
                  pl.BlockSpec((3 * WIDTH, d), lambda i: (0, 0)), pl.BlockSpec((1, d), lambda i: (0, 0))],
        out_specs=row(d),
        out_shape=jax.ShapeDtypeStruct((s, d), F32),
        compiler_params=_cparams("arbitrary"),
        name="out_projection",
    )(x, ya, yb, yc, w_out, gate)


def _ffn_kernel(x_ref, g_ref, sc_ref, sh_ref, gate_ref, wg_ref, wu_ref, wd_ref, o_ref, *, chunk):
    x = x_ref[...]
    h = _mod_norm(x, g_ref[...], sc_ref[...], sh_ref[...]).astype(BF16)
    hidden = wg_ref.shape[1]
    acc = jnp.zeros(x.shape, F32)
    for c in range(hidden // chunk):
        cols = slice(c * chunk, (c + 1) * chunk)
        a = _silu(_dot(h, wg_ref[:, cols])) * _dot(h, wu_ref[:, cols])
        acc = acc + _dot(a.astype(BF16), wd_ref[cols, :])
    o_ref[...] = x + gate_ref[...] * acc


def _ffn(x, g, sc, sh, gate, w_gate, w_up, w_down):
    s, d = x.shape
    hidden = w_gate.shape[1]
    tm = 256
    row = pl.BlockSpec((tm, d), lambda i: (i, 0))
    vec = pl.BlockSpec((1, d), lambda i: (0, 0))
    return pl.pallas_call(
        functools.partial(_ffn_kernel, chunk=256),
        grid=(s // tm,),
        in_specs=[row, vec, vec, vec, vec,
                  pl.BlockSpec((d, hidden), lambda i: (0, 0)), pl.BlockSpec((d, hidden), lambda i: (0, 0)),
                  pl.BlockSpec((hidden, d), lambda i: (0, 0))],
        out_specs=row,
        out_shape=jax.ShapeDtypeStruct((s, d), F32),
        compiler_params=_cparams("arbitrary"),
        name="ffn",
    )(x, g, sc, sh, gate, w_gate, w_up, w_down)


def _rope_tables(seq):
    pos = jnp.arange(seq, dtype=F32)
    inv = ROPE_THETA ** (-jnp.arange(0, ROPE_DIM, 2, dtype=F32) / ROPE_DIM)
    ang = pos[:, None] * inv[None, :]
    cos, sin = jnp.cos(ang), jnp.sin(ang)
    ones = jnp.ones((seq, HEAD_DIM - ROPE_DIM), F32)
    zeros = jnp.zeros((seq, HEAD_DIM - ROPE_DIM), F32)
    zh = jnp.zeros((seq, ROPE_HALF), F32)
    cos_h = jnp.concatenate([cos, cos, ones], axis=1)
    sin_lo = jnp.concatenate([-sin, zh, zeros], axis=1)
    sin_hi = jnp.concatenate([zh, sin, zeros], axis=1)
    two = lambda t: jnp.concatenate([t, t], axis=1)
    return two(cos_h), two(sin_lo), two(sin_hi)


def _moba_mean_matrices(kmean):
    nb = kmean.shape[0]
    km = kmean.reshape(nb, N_HEADS, HEAD_DIM).transpose(1, 2, 0)
    return jnp.pad(km, ((0, 0), (0, LANES - HEAD_DIM), (HEAD_DIM, LANES - HEAD_DIM - nb)))


def kernel(x, c, ada_w, ada_b, norm_mix, w_in, qn_swa, kn_swa, qn_moba, kn_moba, conv_w, conv_b, dt_bias,
           a_log, d_skip, ssm_norm, w_out, norm_ffn, w_gate, w_up, w_down):
    batch, seq, d = x.shape
    depth = ada_w.shape[0]
    assert batch == 1 and d == D_MODEL
    assert seq % (DILATIONS[-1] * SWA_BLOCK) == 0 and seq // MOBA_BLOCK <= LANES - HEAD_DIM
    xs = x.reshape(seq, d)

    mod = _ada_modulation(c, ada_w, ada_b)
    rope_tabs = _rope_tables(seq)
    lane_head = jnp.arange(LANES) // HEAD_DIM
    hsum = (lane_head[:, None] == lane_head[None, :]).astype(BF16)
    expand = (jnp.arange(DT_PAD)[:, None] == (jnp.arange(WIDTH) // HEAD_DIM)[None, :]).astype(BF16)
    tri = (jnp.arange(SSM_CHUNK)[:, None] >= jnp.arange(SSM_CHUNK)[None, :]).astype(BF16)
    pad8 = lambda v: jnp.pad(v, (0, DT_PAD - N_HEADS)).reshape(1, DT_PAD)
    two = lambda g: jnp.concatenate([g, g]).reshape(1, LANES)

    for l in range(depth):
        shift_m, scale_m, gate_m, shift_f, scale_f, gate_f = [mod[l, :, i * d:(i + 1) * d] for i in range(6)]
        w_pad = jnp.pad(w_in[l], ((0, 0), (0, IN_PROJ_PAD - w_in.shape[2]))).astype(BF16)
        qkv_a, qkv_c, z, xbc, dt_raw = _in_projection(xs, norm_mix[l].reshape(1, d), scale_m, shift_m, w_pad)

        gains = [two(g[l]) for g in (qn_swa, kn_swa, qn_moba, kn_moba)]
        qa, ka, va, qc, kc, vc, kmean = _prep_qkv(qkv_a, qkv_c, gains, hsum, rope_tabs)

        y_a = _dilated_attention(qa, ka, va)
        y_b = _ssd_mixer(z, xbc, dt_raw, conv_w[l], conv_b[l].reshape(1, -1), pad8(dt_bias[l]), pad8(a_log[l]),
                         jnp.repeat(d_skip[l], HEAD_DIM).reshape(1, WIDTH), ssm_norm[l].reshape(1, WIDTH),
                         expand, tri)
        q_aug, k_aug = _moba_gate(qc, kc, _moba_mean_matrices(kmean))
        y_c = _moba_attention(q_aug, k_aug, vc)

        xs = _out_projection(xs, y_a, y_b, y_c, w_out[l].astype(BF16), gate_m)
        xs = _ffn(xs, norm_ffn[l].reshape(1, d), scale_f, shift_f, gate_f,
                  w_gate[l].astype(BF16), w_up[l].astype(BF16), w_down[l].astype(BF16))
    return xs.reshape(batch, seq, d)
```

```python
import functools

import jax
import jax.numpy as jnp
from jax import lax
from jax.experimental import pallas as pl
from jax.experimental.pallas import tpu as pltpu

F32 = jnp.float32
BF16 = jnp.bfloat16

D_MODEL = 1024
HEAD_DIM = 64
N_HEADS = 8
WIDTH = N_HEADS * HEAD_DIM
ATTN_SCALE = HEAD_DIM ** -0.5
NORM_EPS = 1e-6
NEG = -1e30

ROPE_THETA = 500000.0
ROPE_DIM = HEAD_DIM // 4
ROPE_HALF = ROPE_DIM // 2

DILATIONS = (1, 4, 16)
SWA_BLOCK = 128
SWA_SPAN = DILATIONS[-1] * SWA_BLOCK
SWA_UNROLL = 3

SSM_STATE = 128
SSM_GROUPS = 2
SSM_CONV = 4
SSM_CHUNK = 256
SSM_CONV_DIM = WIDTH + 2 * SSM_GROUPS * SSM_STATE

MOBA_BLOCK = 256
MOBA_TOPK = 3
MOBA_TILE = 2 * MOBA_BLOCK
MOBA_KC = 128
MOBA_QC = 256
MOBA_VROWS = HEAD_DIM + 16

LANES = 128
DT_PAD = LANES
IN_PROJ_PAD = 6 * WIDTH + WIDTH + SSM_CONV_DIM + DT_PAD
OFF_A, OFF_C, OFF_Z, OFF_XBC, OFF_DT = 0, 3 * WIDTH, 6 * WIDTH, 7 * WIDTH, 7 * WIDTH + SSM_CONV_DIM

VMEM_LIMIT = 56 * 1024 * 1024


def _cparams(*sem):
    return pltpu.CompilerParams(dimension_semantics=sem, vmem_limit_bytes=VMEM_LIMIT)


def _split3(a):
    hi = a.astype(BF16)
    r1 = a - hi.astype(F32)
    mid = r1.astype(BF16)
    lo = (r1 - mid.astype(F32)).astype(BF16)
    return hi, mid, lo


def _dot(a, b):
    return jnp.dot(a, b, preferred_element_type=F32)


def _dot_nt(a, b):
    return lax.dot_general(a, b, (((1,), (1,)), ((), ())), preferred_element_type=F32)


def _dot_exact_rhs(a, b_bf16):
    hi, mid, lo = _split3(a)
    return _dot(hi, b_bf16) + _dot(mid, b_bf16) + _dot(lo, b_bf16)


def _dot_exact_lhs(a_bf16, b):
    hi, mid, lo = _split3(b)
    return _dot(a_bf16, hi) + _dot(a_bf16, mid) + _dot(a_bf16, lo)


def _dot_f32(a, b, dot=_dot):
    ah, am, al = _split3(a)
    bh, bm, bl = _split3(b)
    return (dot(ah, bh) + (dot(ah, bm) + dot(am, bh))
            + (dot(ah, bl) + dot(am, bm) + dot(al, bh)))


def _silu(x):
    return x / (1.0 + jnp.exp(-x))


def _softplus(x):
    return jnp.maximum(x, 0.0) + jnp.log1p(jnp.exp(-jnp.abs(x)))


def _ada_kernel(c_ref, w_ref, b_ref, o_ref):
    c = c_ref[...]
    o_ref[0] = _dot_f32(_silu(c), w_ref[0]) + b_ref[0]


def _ada_modulation(c, ada_w, ada_b):
    depth, d, n = ada_w.shape
    tn = 1024
    c8 = jnp.broadcast_to(c, (8, d))
    out = pl.pallas_call(
        _ada_kernel,
        grid=(depth, n // tn),
        in_specs=[pl.BlockSpec((8, d), lambda l, j: (0, 0)),
                  pl.BlockSpec((1, d, tn), lambda l, j: (l, 0, j)),
                  pl.BlockSpec((1, 1, tn), lambda l, j: (l, 0, j))],
        out_specs=pl.BlockSpec((1, 8, tn), lambda l, j: (l, 0, j)),
        out_shape=jax.ShapeDtypeStruct((depth, 8, n), F32),
        compiler_params=_cparams("arbitrary", "arbitrary"),
        name="ada_modulation",
    )(c8, ada_w, ada_b.reshape(depth, 1, n))
    return out[:, 0:1, :]


def _mod_norm(x, g, sc, sh):
    var = jnp.mean(x * x, axis=-1, keepdims=True)
    return (x * lax.rsqrt(var + NORM_EPS) * g) * (1.0 + sc) + sh


def _inproj_kernel(x_ref, g_ref, sc_ref, sh_ref, w_ref, oa_ref, oc_ref, oz_ref, ox_ref, od_ref):
    h = _mod_norm(x_ref[...], g_ref[...], sc_ref[...], sh_ref[...]).astype(BF16)
    oa_ref[...] = _dot(h, w_ref[:, OFF_A:OFF_C])
    oc_ref[...] = _dot(h, w_ref[:, OFF_C:OFF_Z])
    oz_ref[...] = _dot(h, w_ref[:, OFF_Z:OFF_XBC])
    ox_ref[...] = _dot(h, w_ref[:, OFF_XBC:OFF_DT])
    od_ref[...] = _dot(h, w_ref[:, OFF_DT:IN_PROJ_PAD])


def _in_projection(x, g, sc, sh, w_pad):
    s, d = x.shape
    tm = 256
    row = lambda n: pl.BlockSpec((tm, n), lambda i: (i, 0))
    vec = pl.BlockSpec((1, d), lambda i: (0, 0))
    return pl.pallas_call(
        _inproj_kernel,
        grid=(s // tm,),
        in_specs=[row(d), vec, vec, vec, pl.BlockSpec((d, IN_PROJ_PAD), lambda i: (0, 0))],
        out_specs=[row(3 * WIDTH), row(3 * WIDTH), row(WIDTH), row(SSM_CONV_DIM), row(DT_PAD)],
        out_shape=[jax.ShapeDtypeStruct((s, n), F32)
                   for n in (3 * WIDTH, 3 * WIDTH, WIDTH, SSM_CONV_DIM, DT_PAD)],
        compiler_params=_cparams("arbitrary"),
        name="in_projection",
    )(x, g, sc, sh, w_pad)


def _head_norm_rope(x, gain, hsum, cos_t, sin_lo, sin_hi):
    ss = _dot_exact_rhs(x * x, hsum)
    y = x * lax.rsqrt(ss * (1.0 / HEAD_DIM) + NORM_EPS) * gain
    return (y * cos_t + pltpu.roll(y, ROPE_HALF, 1) * sin_hi
            + pltpu.roll(y, LANES - ROPE_HALF, 1) * sin_lo)


def _prep_kernel(a_ref, c_ref, gqa_ref, gka_ref, gqc_ref, gkc_ref, hsum_ref, cos_ref, slo_ref, shi_ref,
                 qa_ref, ka_ref, qc_ref, kc_ref, km_ref):
    hsum = hsum_ref[...]
    cos_t, slo, shi = cos_ref[...], slo_ref[...], shi_ref[...]
    rope = lambda x, g: _head_norm_rope(x, g, hsum, cos_t, slo, shi)
    for p in range(WIDTH // LANES):
        lo, hi = p * LANES, (p + 1) * LANES
        qa_ref[:, lo:hi] = rope(a_ref[:, lo:hi], gqa_ref[...]) * ATTN_SCALE
        ka_ref[:, lo:hi] = rope(a_ref[:, WIDTH + lo:WIDTH + hi], gka_ref[...])
        qc_ref[:, lo:hi] = rope(c_ref[:, lo:hi], gqc_ref[...])
        kc = rope(c_ref[:, WIDTH + lo:WIDTH + hi], gkc_ref[...])
        kc_ref[:, lo:hi] = kc
        km_ref[0, :, lo:hi] = jnp.mean(kc, axis=0, keepdims=True)


def _prep_qkv(qkv_a, qkv_c, gains, hsum, rope_tabs):
    s = qkv_a.shape[0]
    tm = MOBA_BLOCK
    row = lambda n: pl.BlockSpec((tm, n), lambda i: (i, 0))
    cst = lambda r, n: pl.BlockSpec((r, n), lambda i: (0, 0))
    return pl.pallas_call(
        _prep_kernel,
        grid=(s // tm,),
        in_specs=[row(2 * WIDTH), row(2 * WIDTH)] + [cst(1, LANES)] * 4 + [cst(LANES, LANES)]
                 + [row(LANES)] * 3,
        out_specs=[row(WIDTH)] * 4 + [pl.BlockSpec((1, 1, WIDTH), lambda i: (i, 0, 0))],
        out_shape=[jax.ShapeDtypeStruct((s, WIDTH), F32)] * 4
                  + [jax.ShapeDtypeStruct((s // tm, 1, WIDTH), F32)],
        compiler_params=_cparams("arbitrary"),
        name="prep_qkv",
    )(qkv_a, qkv_c, *gains, hsum, *rope_tabs)


def _dilated_kernel(q_ref, kc_ref, kp_ref, vc_ref, vp_ref, o_ref, m_s, l_s, a_s):
    has_prev = pl.program_id(1) > 0
    lane = lax.broadcasted_iota(jnp.int32, (SWA_BLOCK, LANES), 1)
    head_a = lane < HEAD_DIM
    qi = lax.broadcasted_iota(jnp.int32, (SWA_BLOCK, 2 * SWA_BLOCK), 0)
    ki = lax.broadcasted_iota(jnp.int32, (SWA_BLOCK, 2 * SWA_BLOCK), 1)
    band = (ki >= qi) & (ki <= qi + SWA_BLOCK)
    band_edge = band & ((ki >= SWA_BLOCK) | has_prev)

    def attend(rows, k_cat, v_cat, mask, init):
        q = q_ref[rows, :]
        kb, vb = k_cat.astype(BF16), v_cat.astype(BF16)
        parts = []
        for sel in (head_a, ~head_a):
            s = jnp.where(mask, _dot_nt(jnp.where(sel, q, 0.0).astype(BF16), kb), NEG)
            m_loc = jnp.max(s, axis=-1, keepdims=True)
            p = jnp.exp(s - m_loc)
            parts.append((m_loc, jnp.sum(p, axis=-1, keepdims=True), _dot(p.astype(BF16), vb)))
        (m_a, s_a, pv_a), (m_b, s_b, pv_b) = parts
        m_new = jnp.where(head_a, m_a, m_b)
        l_new = jnp.where(head_a, s_a, s_b)
        a_new = jnp.where(head_a, pv_a, pv_b)
        if not init:
            m_in, m_loc = m_s[rows, :], m_new
            m_new = jnp.maximum(m_in, m_loc)
            alpha, beta = jnp.exp(m_in - m_new), jnp.exp(m_loc - m_new)
            l_new = alpha * l_s[rows, :] + beta * l_new
            a_new = alpha * a_s[rows, :] + beta * a_new
        m_s[rows, :] = m_new
        l_s[rows, :] = l_new
        a_s[rows, :] = a_new

    def edge_unit(r, d, init):
        stride = None if d == 1 else d
        cur = pl.ds(r, SWA_BLOCK, stride)
        prev = pl.ds(SWA_SPAN - SWA_BLOCK * d + r, SWA_BLOCK, stride)
        k_cat = jnp.concatenate([kp_ref[prev, :], kc_ref[cur, :]], axis=0)
        v_cat = jnp.concatenate([vp_ref[prev, :], vc_ref[cur, :]], axis=0)
        attend(cur, k_cat, v_cat, band_edge, init)

    def inner_unit(r, nb, d, init):
        stride = None if d == 1 else d
        rows = pl.ds(SWA_BLOCK * d * nb + r, SWA_BLOCK, stride)
        keys = pl.ds(SWA_BLOCK * d * (nb - 1) + r, 2 * SWA_BLOCK, stride)
        attend(rows, kc_ref[keys, :], vc_ref[keys, :], band, init)

    for d in DILATIONS:
        init = d == 1
        n_blocks = SWA_SPAN // (SWA_BLOCK * d)

        def sub_body(r, carry, d=d, init=init, n_blocks=n_blocks):
            edge_unit(r, d, init)
            if n_blocks > 1:
                def blk_body(nb, c2):
                    inner_unit(r, nb, d, init)
                    return c2
                lax.fori_loop(1, n_blocks, blk_body, 0, unroll=SWA_UNROLL)
            return carry

        lax.fori_loop(0, d, sub_body, 0, unroll=4 if n_blocks == 1 else 1)
    o_ref[...] = (a_s[...] / l_s[...]).astype(BF16)


def _dilated_attention(q, k, qkv):
    s = q.shape[0]
    n_pairs = WIDTH // LANES
    cur = lambda off: pl.BlockSpec((SWA_SPAN, LANES), lambda p, n: (n, off + p))
    prev = lambda off: pl.BlockSpec((SWA_SPAN, LANES), lambda p, n: (jnp.maximum(n - 1, 0), off + p))
    return pl.pallas_call(
        _dilated_kernel,
        grid=(n_pairs, s // SWA_SPAN),
        in_specs=[cur(0), cur(0), prev(0), cur(2 * n_pairs), prev(2 * n_pairs)],
        out_specs=cur(0),
        out_shape=jax.ShapeDtypeStruct((s, WIDTH), BF16),
        scratch_shapes=[pltpu.VMEM((SWA_SPAN, LANES), F32)] * 3,
        compiler_params=_cparams("arbitrary", "arbitrary"),
        name="dilated_attention",
    )(q, k, k, qkv, qkv)


def _moba_gate_kernel(q_ref, k_ref, v_ref, km_ref, qaug_ref, kaug_ref, vt_ref):
    i = pl.program_id(0)
    lane = lax.broadcasted_iota(jnp.int32, (MOBA_BLOCK, LANES), 1)
    is_feat = lane < HEAD_DIM
    own_lane = (lane - HEAD_DIM) == i
    row = lax.broadcasted_iota(jnp.int32, (LANES, MOBA_BLOCK), 0)
    past = (row >= HEAD_DIM) & (row - HEAD_DIM < i)
    own_row = (row - HEAD_DIM) == i
    ones_rows = jnp.where(
        lax.broadcasted_iota(jnp.int32, (MOBA_VROWS - HEAD_DIM, MOBA_BLOCK), 0) == 0, 1.0, 0.0)
    ninf = float("-inf")
    for h in range(N_HEADS):
        cols = slice((h // 2) * LANES, (h // 2 + 1) * LANES)
        q, k = q_ref[:, cols], k_ref[:, cols]
        v_t = v_ref[:, cols].T
        if h % 2:
            q, k = pltpu.roll(q, HEAD_DIM, 1), pltpu.roll(k, HEAD_DIM, 1)
            v_t = v_t[HEAD_DIM:, :]
        else:
            v_t = v_t[:HEAD_DIM, :]
        gate_t = _dot_f32(km_ref[h], jnp.where(is_feat, q, 0.0), dot=_dot_nt)
        cand = jnp.where(past, gate_t, ninf)
        chosen = own_row
        for _ in range(MOBA_TOPK):
            best = jnp.max(cand, axis=0, keepdims=True)
            idx = jnp.min(jnp.where(cand == best, row, 2 * LANES), axis=0, keepdims=True)
            pick = (row == idx) & (best > ninf)
            chosen = chosen | pick
            cand = jnp.where(pick, ninf, cand)
        bias = jnp.where(chosen, 0.0, NEG).T
        qaug_ref[h] = jnp.where(is_feat, q * ATTN_SCALE, bias).astype(BF16)
        kaug_ref[h] = jnp.where(is_feat, k, jnp.where(own_lane, 1.0, 0.0)).astype(BF16)
        vt_ref[h, 0] = jnp.concatenate([v_t, ones_rows], axis=0).astype(BF16)


def _moba_gate(qc, kc, qkv_c, km_rows):
    s = qc.shape[0]
    row = pl.BlockSpec((MOBA_BLOCK, WIDTH), lambda i: (i, 0))
    aug = pl.BlockSpec((N_HEADS, MOBA_BLOCK, LANES), lambda i: (0, i, 0))
    return pl.pallas_call(
        _moba_gate_kernel,
        grid=(s // MOBA_BLOCK,),
        in_specs=[row, row, pl.BlockSpec((MOBA_BLOCK, WIDTH), lambda i: (i, 2)),
                  pl.BlockSpec((N_HEADS, LANES, LANES), lambda i: (0, 0, 0))],
        out_specs=[aug, aug, pl.BlockSpec((N_HEADS, 1, MOBA_VROWS, MOBA_BLOCK), lambda i: (0, i // 2, 0, i % 2))],
        out_shape=[jax.ShapeDtypeStruct((N_HEADS, s, LANES), BF16)] * 2
                  + [jax.ShapeDtypeStruct((N_HEADS, s // MOBA_TILE, MOBA_VROWS, MOBA_TILE), BF16)],
        compiler_params=_cparams("arbitrary"),
        name="moba_gate",
    )(qc, kc, qkv_c, km_rows)


def _moba_attn_kernel(qa_ref, qb_ref, ka_ref, kb_ref, va_ref, vb_ref, o_ref):
    t = pl.program_id(1)
    n_kc = MOBA_TILE // MOBA_KC
    heads = ((qa_ref, ka_ref, va_ref), (qb_ref, kb_ref, vb_ref))
    chains = [(hh, qh) for hh in range(2) for qh in range(MOBA_TILE // MOBA_QC)]
    qs = [heads[hh][0][0, qh * MOBA_QC:(qh + 1) * MOBA_QC, :] for hh, qh in chains]
    ki = lax.broadcasted_iota(jnp.int32, (MOBA_KC, MOBA_QC), 0)
    qi = lax.broadcasted_iota(jnp.int32, (MOBA_KC, MOBA_QC), 1)

    def chunk(q, k_ref, v_ref, tile, c, m, acc, mask=None):
        rows = pl.ds(pl.multiple_of(tile * MOBA_TILE + c * MOBA_KC, MOBA_KC), MOBA_KC)
        s = _dot_nt(k_ref[0, rows, :], q)
        if mask is not None:
            s = jnp.where(mask, s, NEG)
        m_new = jnp.max(s, axis=0, keepdims=True)
        if m is not None:
            m_new = jnp.maximum(m, m_new)
        pv = _dot(v_ref[0, tile, :, c * MOBA_KC:(c + 1) * MOBA_KC], jnp.exp(s - m_new).astype(BF16))
        if m is not None:
            pv = jnp.exp(m - m_new) * acc + pv
        return m_new, pv

    init = []
    for (hh, qh), q in zip(chains, qs):
        _, k_ref, v_ref = heads[hh]
        m = acc = None
        first = qh * (MOBA_QC // MOBA_KC)
        order = list(range(first, first + MOBA_QC // MOBA_KC)) + list(range(first))
        for c in order:
            diag = c >= first
            mask = (ki + (c - first) * MOBA_KC <= qi) if diag else None
            m, acc = chunk(q, k_ref, v_ref, t, c, m, acc, mask)
        init += [m, acc]

    def scores(n, tile, c):
        rows = pl.ds(pl.multiple_of(tile * MOBA_TILE + c * MOBA_KC, MOBA_KC), MOBA_KC)
        return _dot_nt(heads[chains[n][0]][1][0, rows, :], qs[n])

    def body(j, carry):
        ms, accs = list(carry[0::2]), list(carry[1::2])
        s_next = [scores(n, j, 0) for n in range(len(chains))]
        for c in range(n_kc):
            s_cur = s_next
            if c + 1 < n_kc:
                s_next = [scores(n, j, c + 1) for n in range(len(chains))]
            ps = []
            for n, s in enumerate(s_cur):
                m_new = jnp.maximum(ms[n], jnp.max(s, axis=0, keepdims=True))
                ps.append((jnp.exp(ms[n] - m_new), jnp.exp(s - m_new).astype(BF16)))
                ms[n] = m_new
            for n, (alpha, p) in enumerate(ps):
                v_ref = heads[chains[n][0]][2]
                accs[n] = alpha * accs[n] + _dot(v_ref[0, j, :, c * MOBA_KC:(c + 1) * MOBA_KC], p)
        return tuple(x for pair in zip(ms, accs) for x in pair)

    fin = lax.fori_loop(0, t, body, tuple(init))
    outs = []
    for hh in range(2):
        accs = [fin[2 * n + 1] for n, (h2, _) in enumerate(chains) if h2 == hh]
        outs.append(jnp.concatenate([a[:HEAD_DIM, :] / a[HEAD_DIM:HEAD_DIM + 1, :] for a in accs], axis=1))
    o_ref[...] = jnp.concatenate(outs, axis=0).T.astype(BF16)


def _moba_attention(q_aug, k_aug, vt_aug):
    s = q_aug.shape[1]
    n_tiles = s // MOBA_TILE
    qspec = lambda par: pl.BlockSpec((1, MOBA_TILE, LANES), lambda p, t: (2 * p + par, t, 0))
    kspec = lambda par: pl.BlockSpec((1, s, LANES), lambda p, t: (2 * p + par, 0, 0))
    vspec = lambda par: pl.BlockSpec((1, n_tiles, MOBA_VROWS, MOBA_TILE), lambda p, t: (2 * p + par, 0, 0, 0))
    return pl.pallas_call(
        _moba_attn_kernel,
        grid=(N_HEADS // 2, n_tiles),
        in_specs=[qspec(0), qspec(1), kspec(0), kspec(1), vspec(0), vspec(1)],
        out_specs=pl.BlockSpec((MOBA_TILE, LANES), lambda p, t: (t, p)),
        out_shape=jax.ShapeDtypeStruct((s, WIDTH), BF16),
        compiler_params=_cparams("arbitrary", "arbitrary"),
        name="moba_attention",
    )(q_aug, q_aug, k_aug, k_aug, vt_aug, vt_aug)


def _ssd_kernel(z_ref, xbc_ref, dt_ref, cw_ref, cb_ref, dtb_ref, alog_ref, dsk_ref, nw_ref, exp_ref, tri_ref,
                o_ref, state_ref, halo_ref):
    L = SSM_CHUNK

    @pl.when(pl.program_id(0) == 0)
    def _():
        state_ref[...] = jnp.zeros_like(state_ref)
        halo_ref[...] = jnp.zeros_like(halo_ref)

    cur = xbc_ref[...]
    prev8 = halo_ref[...]
    row8 = lax.broadcasted_iota(jnp.int32, (8, SSM_CONV_DIM), 0)
    conv = cb_ref[...] + cur * cw_ref[SSM_CONV - 1:SSM_CONV, :]
    for k in range(1, SSM_CONV):
        rolled = pltpu.roll(cur, k, 0)
        head = jnp.where(row8 < k, pltpu.roll(prev8, k, 0), rolled[0:8, :])
        shifted = jnp.concatenate([head, rolled[8:, :]], axis=0)
        conv = conv + shifted * cw_ref[SSM_CONV - 1 - k:SSM_CONV - k, :]
    halo_ref[...] = cur[L - 8:, :]
    act = _silu(conv)
    xs, bm, cm = act[:, :WIDTH], act[:, WIDTH:WIDTH + 2 * SSM_STATE], act[:, WIDTH + 2 * SSM_STATE:]

    dt = _softplus(dt_ref[...] + dtb_ref[...])
    da = dt * (-jnp.exp(alog_ref[...]))
    cs = _dot_exact_lhs(tri_ref[...], da)
    expand = exp_ref[...]
    cs_b = _dot_exact_rhs(cs, expand)
    dt_b = _dot_exact_rhs(dt, expand)
    cs_last = cs_b[L - 1:L, :]
    xdt = xs * dt_b
    cs_t = cs.T

    ri = lax.broadcasted_iota(jnp.int32, (L, L), 0)
    ci = lax.broadcasted_iota(jnp.int32, (L, L), 1)
    causal = ci <= ri
    lane = lax.broadcasted_iota(jnp.int32, (L, LANES), 1)
    bmb, cmb = bm.astype(BF16), cm.astype(BF16)
    xdtb = xdt.astype(BF16)
    y_pairs = []
    for p in range(N_HEADS // 2):
        g = (2 * p) // (N_HEADS // SSM_GROUPS)
        grp = slice(g * SSM_STATE, (g + 1) * SSM_STATE)
        cbm = _dot_nt(cmb[:, grp], bmb[:, grp])
        x_pair = xdtb[:, p * LANES:(p + 1) * LANES]
        halves = []
        for h in (2 * p, 2 * p + 1):
            seg = cs[:, h:h + 1] - cs_t[h:h + 1, :]
            decay = jnp.exp(jnp.where(causal, seg, NEG))
            halves.append(_dot((cbm * decay).astype(BF16), x_pair))
        y_pairs.append(jnp.where(lane < HEAD_DIM, halves[0], halves[1]))
    y_diag = jnp.concatenate(y_pairs, axis=1)

    state = state_ref[...]
    stb = state.astype(BF16)
    half = WIDTH // SSM_GROUPS
    y_off = jnp.concatenate(
        [_dot(cmb[:, g * SSM_STATE:(g + 1) * SSM_STATE], stb[:, g * half:(g + 1) * half])
         for g in range(SSM_GROUPS)], axis=1) * jnp.exp(cs_b)
    y = y_diag + y_off + xs * dsk_ref[...]

    w = (xdt * jnp.exp(cs_last - cs_b)).astype(BF16)
    bm_t = bm.T.astype(BF16)
    new = jnp.concatenate(
        [_dot(bm_t[g * SSM_STATE:(g + 1) * SSM_STATE, :], w[:, g * half:(g + 1) * half])
         for g in range(SSM_GROUPS)], axis=1)
    state_ref[...] = state * jnp.exp(cs_last) + new

    gt = y * _silu(z_ref[...])
    outs = []
    for g in range(SSM_GROUPS):
        gg = gt[:, g * half:(g + 1) * half]
        outs.append(gg * lax.rsqrt(jnp.mean(gg * gg, axis=-1, keepdims=True) + NORM_EPS))
    o_ref[...] = (jnp.concatenate(outs, axis=1) * nw_ref[...]).astype(BF16)


def _ssd_mixer(z, xbc, dt_raw, conv_w, conv_b, dt_bias, a_log, d_skip, norm_w, expand, tri):
    s = z.shape[0]
    L = SSM_CHUNK
    row = lambda n: pl.BlockSpec((L, n), lambda c: (c, 0))
    cst = lambda r, n: pl.BlockSpec((r, n), lambda c: (0, 0))
    return pl.pallas_call(
        _ssd_kernel,
        grid=(s // L,),
        in_specs=[row(WIDTH), row(SSM_CONV_DIM), row(DT_PAD), cst(SSM_CONV, SSM_CONV_DIM),
                  cst(1, SSM_CONV_DIM), cst(1, DT_PAD), cst(1, DT_PAD), cst(1, WIDTH), cst(1, WIDTH),
                  cst(DT_PAD, WIDTH), cst(L, L)],
        out_specs=row(WIDTH),
        out_shape=jax.ShapeDtypeStruct((s, WIDTH), BF16),
        scratch_shapes=[pltpu.VMEM((SSM_STATE, WIDTH), F32), pltpu.VMEM((8, SSM_CONV_DIM), F32)],
        compiler_params=_cparams("arbitrary"),
        name="ssd_mixer",
    )(z, xbc, dt_raw, conv_w, conv_b, dt_bias, a_log, d_skip, norm_w, expand, tri)


def _outproj_kernel(x_ref, ya_ref, yb_ref, yc_ref, w_ref, gate_ref, o_ref):
    y = (_dot(ya_ref[...], w_ref[0:WIDTH, :]) + _dot(yb_ref[...], w_ref[WIDTH:2 * WIDTH, :])
         + _dot(yc_ref[...], w_ref[2 * WIDTH:, :]))
    o_ref[...] = x_ref[...] + gate_ref[...] * y


def _out_projection(x, ya, yb, yc, w_out, gate):
    s, d = x.shape
    tm = 512
    row = lambda n: pl.BlockSpec((tm, n), lambda i: (i, 0))
    return pl.pallas_call(
        _outproj_kernel,
        grid=(s // tm,),
        in_specs=[row(d), row(WIDTH), row(WIDTH), row(WIDTH),
                  pl.BlockSpec((3 * WIDTH, d), lambda i: (0, 0)), pl.BlockSpec((1, d), lambda i: (0, 0))],
        out_specs=row(d),
        out_shape=jax.ShapeDtypeStruct((s, d), F32),
        compiler_params=_cparams("arbitrary"),
        name="out_projection",
    )(x, ya, yb, yc, w_out, gate)


def _ffn_kernel(x_ref, g_ref, sc_ref, sh_ref, gate_ref, wg_ref, wu_ref, wd_ref, o_ref, *, chunk):
    x = x_ref[...]
    h = _mod_norm(x, g_ref[...], sc_ref[...], sh_ref[...]).astype(BF16)
    hidden = wg_ref.shape[1]
    acc = jnp.zeros(x.shape, F32)
    for c in range(hidden // chunk):
        cols = slice(c * chunk, (c + 1) * chunk)
        a = _silu(_dot(h, wg_ref[:, cols])) * _dot(h, wu_ref[:, cols])
        acc = acc + _dot(a.astype(BF16), wd_ref[cols, :])
    o_ref[...] = x + gate_ref[...] * acc


def _ffn(x, g, sc, sh, gate, w_gate, w_up, w_down):
    s, d = x.shape
    hidden = w_gate.shape[1]
    tm = 256
    row = pl.BlockSpec((tm, d), lambda i: (i, 0))
    vec = pl.BlockSpec((1, d), lambda i: (0, 0))
    return pl.pallas_call(
        functools.partial(_ffn_kernel, chunk=256),
        grid=(s // tm,),
        in_specs=[row, vec, vec, vec, vec,
                  pl.BlockSpec((d, hidden), lambda i: (0, 0)), pl.BlockSpec((d, hidden), lambda i: (0, 0)),
                  pl.BlockSpec((hidden, d), lambda i: (0, 0))],
        out_specs=row,
        out_shape=jax.ShapeDtypeStruct((s, d), F32),
        compiler_params=_cparams("arbitrary"),
        name="ffn",
    )(x, g, sc, sh, gate, w_gate, w_up, w_down)


def _rope_tables(seq):
    pos = jnp.arange(seq, dtype=F32)
    inv = ROPE_THETA ** (-jnp.arange(0, ROPE_DIM, 2, dtype=F32) / ROPE_DIM)
    ang = pos[:, None] * inv[None, :]
    cos, sin = jnp.cos(ang), jnp.sin(ang)
    ones = jnp.ones((seq, HEAD_DIM - ROPE_DIM), F32)
    zeros = jnp.zeros((seq, HEAD_DIM - ROPE_DIM), F32)
    zh = jnp.zeros((seq, ROPE_HALF), F32)
    cos_h = jnp.concatenate([cos, cos, ones], axis=1)
    sin_lo = jnp.concatenate([-sin, zh, zeros], axis=1)
    sin_hi = jnp.concatenate([zh, sin, zeros], axis=1)
    two = lambda t: jnp.concatenate([t, t], axis=1)
    return two(cos_h), two(sin_lo), two(sin_hi)


def _moba_mean_rows(kmean):
    nb = kmean.shape[0]
    km = kmean.reshape(nb, N_HEADS, HEAD_DIM).transpose(1, 0, 2)
    return jnp.pad(km, ((0, 0), (HEAD_DIM, LANES - HEAD_DIM - nb), (0, LANES - HEAD_DIM)))


def kernel(x, c, ada_w, ada_b, norm_mix, w_in, qn_swa, kn_swa, qn_moba, kn_moba, conv_w, conv_b, dt_bias,
           a_log, d_skip, ssm_norm, w_out, norm_ffn, w_gate, w_up, w_down):
    batch, seq, d = x.shape
    depth = ada_w.shape[0]
    assert batch == 1 and d == D_MODEL
    assert seq % SWA_SPAN == 0 and seq // MOBA_BLOCK <= LANES - HEAD_DIM
    xs = x.reshape(seq, d)

    mod = _ada_modulation(c, ada_w, ada_b)
    rope_tabs = _rope_tables(seq)
    lane_head = jnp.arange(LANES) // HEAD_DIM
    hsum = (lane_head[:, None] == lane_head[None, :]).astype(BF16)
    expand = (jnp.arange(DT_PAD)[:, None] == (jnp.arange(WIDTH) // HEAD_DIM)[None, :]).astype(BF16)
    tri = (jnp.arange(SSM_CHUNK)[:, None] >= jnp.arange(SSM_CHUNK)[None, :]).astype(BF16)
    pad8 = lambda v: jnp.pad(v, (0, DT_PAD - N_HEADS)).reshape(1, DT_PAD)
    two = lambda g: jnp.concatenate([g, g]).reshape(1, LANES)

    for l in range(depth):
        shift_m, scale_m, gate_m, shift_f, scale_f, gate_f = [mod[l, :, i * d:(i + 1) * d] for i in range(6)]
        w_pad = jnp.pad(w_in[l], ((0, 0), (0, IN_PROJ_PAD - w_in.shape[2]))).astype(BF16)
        qkv_a, qkv_c, z, xbc, dt_raw = _in_projection(xs, norm_mix[l].reshape(1, d), scale_m, shift_m, w_pad)

        gains = [two(g[l]) for g in (qn_swa, kn_swa, qn_moba, kn_moba)]
        qa, ka, qc, kc, kmean = _prep_qkv(qkv_a, qkv_c, gains, hsum, rope_tabs)

        y_a = _dilated_attention(qa, ka, qkv_a)
        y_b = _ssd_mixer(z, xbc, dt_raw, conv_w[l], conv_b[l].reshape(1, -1), pad8(dt_bias[l]), pad8(a_log[l]),
                         jnp.repeat(d_skip[l], HEAD_DIM).reshape(1, WIDTH), ssm_norm[l].reshape(1, WIDTH),
                         expand, tri)
        q_aug, k_aug, vt_aug = _moba_gate(qc, kc, qkv_c, _moba_mean_rows(kmean))
        y_c = _moba_attention(q_aug, k_aug, vt_aug)

        xs = _out_projection(xs, y_a, y_b, y_c, w_out[l].astype(BF16), gate_m)
        xs = _ffn(xs, norm_ffn[l].reshape(1, d), scale_f, shift_f, gate_f,
                  w_gate[l].astype(BF16), w_up[l].astype(BF16), w_down[l].astype(BF16))
    return xs.reshape(batch, seq, d)
```

```python
import functools

import jax
import jax.numpy as jnp
from jax import lax
from jax.experimental import pallas as pl
from jax.experimental.pallas import tpu as pltpu
from jax.experimental.pallas import tpu_sc as plsc

F32 = jnp.float32
BF16 = jnp.bfloat16

D_MODEL = 1024
HEAD_DIM = 64
N_HEADS = 8
WIDTH = N_HEADS * HEAD_DIM
ATTN_SCALE = HEAD_DIM ** -0.5
NORM_EPS = 1e-6
NEG = -1e30

ROPE_THETA = 500000.0
ROPE_DIM = HEAD_DIM // 4
ROPE_HALF = ROPE_DIM // 2

DILATIONS = (1, 4, 16)
SWA_BLOCK = 128
SWA_SPAN = DILATIONS[-1] * SWA_BLOCK
SWA_UNROLL = 3

SSM_STATE = 128
SSM_GROUPS = 2
SSM_CONV = 4
SSM_CHUNK = 256
SSM_CONV_DIM = WIDTH + 2 * SSM_GROUPS * SSM_STATE

MOBA_BLOCK = 256
MOBA_TOPK = 3
MOBA_PAIR_TILE = 128
MOBA_GROUP = 16
MOBA_MAX_BLOCKS = 64
MOBA_VROWS = HEAD_DIM + 16

LANES = 128
DT_PAD = LANES
IN_PROJ_PAD = 6 * WIDTH + WIDTH + SSM_CONV_DIM + DT_PAD
OFF_A, OFF_C, OFF_Z, OFF_XBC, OFF_DT = 0, 3 * WIDTH, 6 * WIDTH, 7 * WIDTH, 7 * WIDTH + SSM_CONV_DIM

VMEM_LIMIT = 56 * 1024 * 1024


_ORDER_ONLY = pl.BlockSpec(memory_space=pl.ANY)


def _cparams(*sem):
    return pltpu.CompilerParams(dimension_semantics=sem, vmem_limit_bytes=VMEM_LIMIT)


def _split3(a):
    hi = a.astype(BF16)
    r1 = a - hi.astype(F32)
    mid = r1.astype(BF16)
    lo = (r1 - mid.astype(F32)).astype(BF16)
    return hi, mid, lo


def _dot(a, b):
    return jnp.dot(a, b, preferred_element_type=F32)


def _dot_nt(a, b):
    return lax.dot_general(a, b, (((1,), (1,)), ((), ())), preferred_element_type=F32)


def _dot_exact_rhs(a, b_bf16):
    hi, mid, lo = _split3(a)
    return _dot(hi, b_bf16) + _dot(mid, b_bf16) + _dot(lo, b_bf16)


def _dot_exact_lhs(a_bf16, b):
    hi, mid, lo = _split3(b)
    return _dot(a_bf16, hi) + _dot(a_bf16, mid) + _dot(a_bf16, lo)


def _dot_f32(a, b, dot=_dot):
    ah, am, al = _split3(a)
    bh, bm, bl = _split3(b)
    return (dot(ah, bh) + (dot(ah, bm) + dot(am, bh))
            + (dot(ah, bl) + dot(am, bm) + dot(al, bh)))


def _dot_bf16x3(a, b, dot=_dot):
    ah = a.astype(BF16)
    al = (a - ah.astype(F32)).astype(BF16)
    bh = b.astype(BF16)
    bl = (b - bh.astype(F32)).astype(BF16)
    return dot(ah, bh) + (dot(ah, bl) + dot(al, bh))


def _silu(x):
    return x / (1.0 + jnp.exp(-x))


def _softplus(x):
    return jnp.maximum(x, 0.0) + jnp.log1p(jnp.exp(-jnp.abs(x)))


def _ada_kernel(c_ref, w_ref, b_ref, o_ref):
    c = c_ref[...]
    o_ref[0] = _dot_f32(_silu(c), w_ref[0]) + b_ref[0]


def _ada_modulation(c, ada_w, ada_b):
    depth, d, n = ada_w.shape
    tn = 1024
    c8 = jnp.broadcast_to(c, (8, d))
    out = pl.pallas_call(
        _ada_kernel,
        grid=(depth, n // tn),
        in_specs=[pl.BlockSpec((8, d), lambda l, j: (0, 0)),
                  pl.BlockSpec((1, d, tn), lambda l, j: (l, 0, j)),
                  pl.BlockSpec((1, 1, tn), lambda l, j: (l, 0, j))],
        out_specs=pl.BlockSpec((1, 8, tn), lambda l, j: (l, 0, j)),
        out_shape=jax.ShapeDtypeStruct((depth, 8, n), F32),
        compiler_params=_cparams("arbitrary", "arbitrary"),
        name="ada_modulation",
    )(c8, ada_w, ada_b.reshape(depth, 1, n))
    return out[:, 0:1, :]


def _mod_norm(x, g, sc, sh):
    var = jnp.mean(x * x, axis=-1, keepdims=True)
    return (x * lax.rsqrt(var + NORM_EPS) * g) * (1.0 + sc) + sh


def _inproj_kernel(x_ref, g_ref, sc_ref, sh_ref, w_ref, oa_ref, oc_ref, oz_ref, ox_ref, od_ref):
    h = _mod_norm(x_ref[...], g_ref[...], sc_ref[...], sh_ref[...]).astype(BF16)
    oa_ref[...] = _dot(h, w_ref[:, OFF_A:OFF_C])
    oc_ref[...] = _dot(h, w_ref[:, OFF_C:OFF_Z])
    oz_ref[...] = _dot(h, w_ref[:, OFF_Z:OFF_XBC])
    ox_ref[...] = _dot(h, w_ref[:, OFF_XBC:OFF_DT])
    od_ref[...] = _dot(h, w_ref[:, OFF_DT:IN_PROJ_PAD])


def _in_projection(x, g, sc, sh, w_all, layer):
    s, d = x.shape
    tm = 256
    row = lambda n: pl.BlockSpec((tm, n), lambda i: (i, 0))
    vec = pl.BlockSpec((1, d), lambda i: (0, 0))
    return pl.pallas_call(
        _inproj_kernel,
        grid=(s // tm,),
        in_specs=[row(d), vec, vec, vec, pl.BlockSpec((None, d, IN_PROJ_PAD), lambda i: (layer, 0, 0))],
        out_specs=[row(3 * WIDTH), row(3 * WIDTH), row(WIDTH), row(SSM_CONV_DIM), row(DT_PAD)],
        out_shape=[jax.ShapeDtypeStruct((s, n), F32)
                   for n in (3 * WIDTH, 3 * WIDTH, WIDTH, SSM_CONV_DIM, DT_PAD)],
        compiler_params=_cparams("arbitrary"),
        name="in_projection",
    )(x, g, sc, sh, w_all)


def _head_norm_rope(x, gain, hsum, cos_t, sin_lo, sin_hi):
    ss = _dot_exact_rhs(x * x, hsum)
    y = x * lax.rsqrt(ss * (1.0 / HEAD_DIM) + NORM_EPS) * gain
    return (y * cos_t + pltpu.roll(y, ROPE_HALF, 1) * sin_hi
            + pltpu.roll(y, LANES - ROPE_HALF, 1) * sin_lo)


def _prep_kernel(a_ref, c_ref, gqa_ref, gka_ref, gqc_ref, gkc_ref, hsum_ref, cos_ref, slo_ref, shi_ref,
                 qa_ref, ka_ref, qc_ref, kc_ref, km_ref):
    hsum = hsum_ref[...]
    cos_t, slo, shi = cos_ref[...], slo_ref[...], shi_ref[...]
    rope = lambda x, g: _head_norm_rope(x, g, hsum, cos_t, slo, shi)
    for p in range(WIDTH // LANES):
        lo, hi = p * LANES, (p + 1) * LANES
        qa_ref[:, lo:hi] = rope(a_ref[:, lo:hi], gqa_ref[...]) * ATTN_SCALE
        ka_ref[:, lo:hi] = rope(a_ref[:, WIDTH + lo:WIDTH + hi], gka_ref[...])
        qc_ref[:, lo:hi] = rope(c_ref[:, lo:hi], gqc_ref[...])
        kc = rope(c_ref[:, WIDTH + lo:WIDTH + hi], gkc_ref[...])
        kc_ref[:, lo:hi] = kc
        km_ref[0, :, lo:hi] = jnp.mean(kc, axis=0, keepdims=True)


def _prep_qkv(qkv_a, qkv_c, gains, hsum, rope_tabs):
    s = qkv_a.shape[0]
    tm = MOBA_BLOCK
    row = lambda n: pl.BlockSpec((tm, n), lambda i: (i, 0))
    cst = lambda r, n: pl.BlockSpec((r, n), lambda i: (0, 0))
    return pl.pallas_call(
        _prep_kernel,
        grid=(s // tm,),
        in_specs=[row(2 * WIDTH), row(2 * WIDTH)] + [cst(1, LANES)] * 4 + [cst(LANES, LANES)]
                 + [row(LANES)] * 3,
        out_specs=[row(WIDTH)] * 4 + [pl.BlockSpec((1, 1, WIDTH), lambda i: (i, 0, 0))],
        out_shape=[jax.ShapeDtypeStruct((s, WIDTH), F32)] * 4
                  + [jax.ShapeDtypeStruct((s // tm, 1, WIDTH), F32)],
        compiler_params=_cparams("arbitrary"),
        name="prep_qkv",
    )(qkv_a, qkv_c, *gains, hsum, *rope_tabs)


def _dilated_kernel(q_ref, kc_ref, kp_ref, vc_ref, vp_ref, _after_ref, o_ref, m_s, l_s, a_s):
    has_prev = pl.program_id(1) > 0
    lane = lax.broadcasted_iota(jnp.int32, (SWA_BLOCK, LANES), 1)
    head_a = lane < HEAD_DIM
    qi = lax.broadcasted_iota(jnp.int32, (SWA_BLOCK, 2 * SWA_BLOCK), 0)
    ki = lax.broadcasted_iota(jnp.int32, (SWA_BLOCK, 2 * SWA_BLOCK), 1)
    band = (ki >= qi) & (ki <= qi + SWA_BLOCK)
    band_edge = band & ((ki >= SWA_BLOCK) | has_prev)

    def attend(rows, k_cat, v_cat, mask, init):
        q = q_ref[rows, :]
        kb, vb = k_cat.astype(BF16), v_cat.astype(BF16)
        parts = []
        for sel in (head_a, ~head_a):
            s = jnp.where(mask, _dot_nt(jnp.where(sel, q, 0.0).astype(BF16), kb), NEG)
            m_loc = jnp.max(s, axis=-1, keepdims=True)
            p = jnp.exp(s - m_loc)
            parts.append((m_loc, jnp.sum(p, axis=-1, keepdims=True), _dot(p.astype(BF16), vb)))
        (m_a, s_a, pv_a), (m_b, s_b, pv_b) = parts
        m_new = jnp.where(head_a, m_a, m_b)
        l_new = jnp.where(head_a, s_a, s_b)
        a_new = jnp.where(head_a, pv_a, pv_b)
        if not init:
            m_in, m_loc = m_s[rows, :], m_new
            m_new = jnp.maximum(m_in, m_loc)
            alpha, beta = jnp.exp(m_in - m_new), jnp.exp(m_loc - m_new)
            l_new = alpha * l_s[rows, :] + beta * l_new
            a_new = alpha * a_s[rows, :] + beta * a_new
        m_s[rows, :] = m_new
        l_s[rows, :] = l_new
        a_s[rows, :] = a_new

    def edge_unit(r, d, init):
        stride = None if d == 1 else d
        cur = pl.ds(r, SWA_BLOCK, stride)
        prev = pl.ds(SWA_SPAN - SWA_BLOCK * d + r, SWA_BLOCK, stride)
        k_cat = jnp.concatenate([kp_ref[prev, :], kc_ref[cur, :]], axis=0)
        v_cat = jnp.concatenate([vp_ref[prev, :], vc_ref[cur, :]], axis=0)
        attend(cur, k_cat, v_cat, band_edge, init)

    def inner_unit(r, nb, d, init):
        stride = None if d == 1 else d
        rows = pl.ds(SWA_BLOCK * d * nb + r, SWA_BLOCK, stride)
        keys = pl.ds(SWA_BLOCK * d * (nb - 1) + r, 2 * SWA_BLOCK, stride)
        attend(rows, kc_ref[keys, :], vc_ref[keys, :], band, init)

    for d in DILATIONS:
        init = d == 1
        n_blocks = SWA_SPAN // (SWA_BLOCK * d)

        def sub_body(r, carry, d=d, init=init, n_blocks=n_blocks):
            edge_unit(r, d, init)
            if n_blocks > 1:
                def blk_body(nb, c2):
                    inner_unit(r, nb, d, init)
                    return c2
                lax.fori_loop(1, n_blocks, blk_body, 0, unroll=SWA_UNROLL)
            return carry

        lax.fori_loop(0, d, sub_body, 0, unroll=4 if n_blocks == 1 else 1)
    o_ref[...] = (a_s[...] / l_s[...]).astype(BF16)


def _dilated_attention(q, k, qkv, after):
    s = q.shape[0]
    n_pairs = WIDTH // LANES
    cur = lambda off: pl.BlockSpec((SWA_SPAN, LANES), lambda p, n: (n, off + p))
    prev = lambda off: pl.BlockSpec((SWA_SPAN, LANES), lambda p, n: (jnp.maximum(n - 1, 0), off + p))
    return pl.pallas_call(
        _dilated_kernel,
        grid=(n_pairs, s // SWA_SPAN),
        in_specs=[cur(0), cur(0), prev(0), cur(2 * n_pairs), prev(2 * n_pairs), _ORDER_ONLY],
        out_specs=cur(0),
        out_shape=jax.ShapeDtypeStruct((s, WIDTH), BF16),
        scratch_shapes=[pltpu.VMEM((SWA_SPAN, LANES), F32)] * 3,
        compiler_params=_cparams("arbitrary", "arbitrary"),
        cost_estimate=pl.CostEstimate(
            flops=len(DILATIONS) * 4 * s * 2 * SWA_BLOCK * WIDTH, transcendentals=len(DILATIONS) * s * 2 * SWA_BLOCK * N_HEADS,
            bytes_accessed=5 * s * WIDTH * 4 + s * WIDTH * 2),
        name="dilated_attention",
    )(q, k, k, qkv, qkv, after)


def _moba_route_kernel(q_ref, k_ref, v_ref, km_ref, sut_ref, ones_ref,
                       qrow_ref, kpad_ref, vt_ref, sel_ref, rank_ref, cnt_ref, base_ref):
    i = pl.program_id(0)

    @pl.when(i == 0)
    def _():
        base_ref[...] = jnp.zeros_like(base_ref)

    lane = lax.broadcasted_iota(jnp.int32, (MOBA_BLOCK, LANES), 1)
    is_feat = lane < HEAD_DIM
    row = lax.broadcasted_iota(jnp.int32, (MOBA_MAX_BLOCKS, MOBA_BLOCK), 0)
    past = row < i
    ones_rows = jnp.where(
        lax.broadcasted_iota(jnp.int32, (MOBA_VROWS - HEAD_DIM, MOBA_BLOCK), 0) == 0, 1.0, 0.0)
    ninf = float("-inf")
    for h in range(N_HEADS):
        cols = slice((h // 2) * LANES, (h // 2 + 1) * LANES)
        q, k = q_ref[:, cols], k_ref[:, cols]
        v_t = v_ref[:, cols].T
        if h % 2:
            q, k = pltpu.roll(q, HEAD_DIM, 1), pltpu.roll(k, HEAD_DIM, 1)
            v_t = v_t[HEAD_DIM:, :]
        else:
            v_t = v_t[:HEAD_DIM, :]
        gate_t = _dot_bf16x3(km_ref[h], jnp.where(is_feat, q, 0.0), dot=_dot_nt)
        cand = jnp.where(past, gate_t, ninf)
        picks = []
        for _ in range(MOBA_TOPK):
            best = jnp.max(cand, axis=0, keepdims=True)
            idx = jnp.min(jnp.where(cand == best, row, 2 * LANES), axis=0, keepdims=True)
            ok = best > ninf
            pick = (row == idx) & ok
            cand = jnp.where(pick, ninf, cand)
            picks.append((idx, ok, pick))
        chosen = jnp.where(picks[0][2] | picks[1][2] | picks[2][2], 1.0, 0.0).astype(BF16)
        base = base_ref[h]
        rank_full = base + _dot(chosen, sut_ref[...])
        base_ref[h] = base + _dot(chosen, ones_ref[...])
        for r, (idx, ok, pick) in enumerate(picks):
            rank = jnp.sum(jnp.where(pick, rank_full, 0.0), axis=0, keepdims=True)
            sel_ref[h, r:r + 1, :] = jnp.where(ok, idx, -1)
            rank_ref[h, r:r + 1, :] = rank.astype(jnp.int32)
        qrow_ref[h] = jnp.where(is_feat, q * ATTN_SCALE, 0.0)
        kpad_ref[h] = jnp.where(is_feat, k, 0.0).astype(BF16)
        vt_ref[h, 0] = jnp.concatenate([v_t, ones_rows], axis=0).astype(BF16)

    @pl.when(i == pl.num_programs(0) - 1)
    def _():
        cnt_ref[...] = base_ref[...]


def _moba_route(qc, kc, qkv_c, km_rows):
    s = qc.shape[0]
    nb = s // MOBA_BLOCK
    row = pl.BlockSpec((MOBA_BLOCK, WIDTH), lambda i: (i, 0))
    per_head = lambda dt: (pl.BlockSpec((N_HEADS, MOBA_BLOCK, LANES), lambda i: (0, i, 0)),
                           jax.ShapeDtypeStruct((N_HEADS, s, LANES), dt))
    picks = (pl.BlockSpec((N_HEADS, MOBA_TOPK, MOBA_BLOCK), lambda i: (0, 0, i)),
             jax.ShapeDtypeStruct((N_HEADS, MOBA_TOPK, s), jnp.int32))
    outs = [per_head(F32), per_head(BF16),
            (pl.BlockSpec((N_HEADS, 1, MOBA_VROWS, MOBA_BLOCK), lambda i: (0, i, 0, 0)),
             jax.ShapeDtypeStruct((N_HEADS, nb, MOBA_VROWS, MOBA_BLOCK), BF16)),
            picks, picks,
            (pl.BlockSpec((N_HEADS, MOBA_MAX_BLOCKS, MOBA_BLOCK), lambda i: (0, 0, 0)),
             jax.ShapeDtypeStruct((N_HEADS, MOBA_MAX_BLOCKS, MOBA_BLOCK), F32))]
    qi = jnp.arange(MOBA_BLOCK)
    strict_upper = (qi[:, None] < qi[None, :]).astype(BF16)
    all_ones = jnp.ones((MOBA_BLOCK, MOBA_BLOCK), BF16)
    sq = pl.BlockSpec((MOBA_BLOCK, MOBA_BLOCK), lambda i: (0, 0))
    return pl.pallas_call(
        _moba_route_kernel,
        grid=(nb,),
        in_specs=[row, row, pl.BlockSpec((MOBA_BLOCK, WIDTH), lambda i: (i, 2)),
                  pl.BlockSpec((N_HEADS, MOBA_MAX_BLOCKS, LANES), lambda i: (0, 0, 0)), sq, sq],
        out_specs=[o[0] for o in outs],
        out_shape=[o[1] for o in outs],
        scratch_shapes=[pltpu.VMEM((N_HEADS, MOBA_MAX_BLOCKS, MOBA_BLOCK), F32)],
        compiler_params=_cparams("arbitrary"),
        name="moba_route",
    )(qc, kc, qkv_c, km_rows, strict_upper, all_ones)


def _moba_tiles_per_head(seq):
    tiles = MOBA_TOPK * seq // MOBA_PAIR_TILE + seq // MOBA_BLOCK
    return (tiles // MOBA_GROUP + 1) * MOBA_GROUP


def _moba_dispatch(sel, rank, counts, n_tiles):
    n_heads, nb = counts.shape
    padded = (counts + MOBA_PAIR_TILE - 1) // MOBA_PAIR_TILE * MOBA_PAIR_TILE
    ends = jnp.cumsum(padded, axis=1)
    offs = ends - padded
    rows_per_head = n_tiles * MOBA_PAIR_TILE
    head_base = (jnp.arange(n_heads, dtype=jnp.int32) * rows_per_head)[:, None, None]
    blocks = jnp.arange(nb, dtype=jnp.int32)
    off_sel = jnp.sum(jnp.where(sel[..., None] == blocks, offs[:, None, None, :], 0), axis=-1)
    pos = head_base + jnp.where(sel >= 0, off_sel + rank, rows_per_head - MOBA_PAIR_TILE)
    tile_start = jnp.arange(n_tiles, dtype=jnp.int32) * MOBA_PAIR_TILE
    blk = jnp.sum(ends[:, None, :] <= tile_start[None, :, None], axis=2)
    tile_blk = jnp.where(tile_start[None, :] < ends[:, -1:], blk, -1)
    return pos.reshape(-1).astype(jnp.int32), tile_blk.reshape(-1).astype(jnp.int32)


def _sc_workers():
    info = plsc.get_sparse_core_info()
    return info.num_cores, info.num_cores * info.num_subcores


def _sc_scatter_rows(rows, pos, n_out):
    n_src = rows.shape[0]
    seq = n_src // N_HEADS
    n_cores, n_workers = _sc_workers()
    win = MOBA_PAIR_TILE
    wins_per_worker = n_src // win // n_workers
    wins_per_head = seq // win
    assert n_src % (win * n_workers) == 0
    mesh = plsc.VectorSubcoreMesh(core_axis_name="c", subcore_axis_name="s")

    depth = 2
    assert wins_per_worker % depth == 0

    @functools.partial(
        pl.kernel, mesh=mesh, out_type=jax.ShapeDtypeStruct((n_out, LANES), F32), name="moba_scatter_rows",
        cost_estimate=pl.CostEstimate(flops=0, transcendentals=0,
                                      bytes_accessed=(1 + MOBA_TOPK) * n_src * LANES * 4 + pos.size * 4),
        scratch_types=[pltpu.VMEM((win,), jnp.int32)] * (depth * MOBA_TOPK)
                      + [pltpu.VMEM((win, LANES), F32)] * depth + [pltpu.SemaphoreType.DMA] * (depth + 1))
    def scatter(src_hbm, idx_hbm, out_hbm, *scratch):
        idx_v = scratch[:depth * MOBA_TOPK]
        rows_v = scratch[depth * MOBA_TOPK:depth * MOBA_TOPK + depth]
        load_sems, store_sem = scratch[-depth - 1:-1], scratch[-1]
        wid = lax.axis_index("s") * n_cores + lax.axis_index("c")

        @pl.loop(0, wins_per_worker, step=depth)
        def _(w):
            loads = []
            for u in range(depth):
                src_win = wid * wins_per_worker + w + u
                h, qw = src_win // wins_per_head, src_win % wins_per_head
                cps = [pltpu.async_copy(src_hbm.at[pl.ds(src_win * win, win)], rows_v[u], load_sems[u])]
                for r in range(MOBA_TOPK):
                    cps.append(pltpu.async_copy(
                        idx_hbm.at[pl.ds((h * MOBA_TOPK + r) * seq + qw * win, win)],
                        idx_v[u * MOBA_TOPK + r], load_sems[u]))
                loads.append(cps)
            stores = []
            for u in range(depth):
                for cp in loads[u]:
                    cp.wait()
                stores += [pltpu.async_copy(rows_v[u], out_hbm.at[idx_v[u * MOBA_TOPK + r]], store_sem)
                           for r in range(MOBA_TOPK)]
            for cp in stores:
                cp.wait()

    return scatter(rows, pos)


def _sc_gather_rows(table, pos):
    n = pos.shape[0]
    n_cores, n_workers = _sc_workers()
    win = MOBA_PAIR_TILE
    wins_per_worker = n // win // n_workers
    assert n % (win * n_workers) == 0
    mesh = plsc.VectorSubcoreMesh(core_axis_name="c", subcore_axis_name="s")

    depth = 4
    assert wins_per_worker % depth == 0

    @functools.partial(
        pl.kernel, mesh=mesh, out_type=jax.ShapeDtypeStruct((n, LANES), F32), name="moba_gather_rows",
        cost_estimate=pl.CostEstimate(flops=0, transcendentals=0, bytes_accessed=2 * n * LANES * 4 + n * 4),
        scratch_types=[pltpu.VMEM((win,), jnp.int32)] * depth + [pltpu.VMEM((win, LANES), F32)] * depth
                      + [pltpu.SemaphoreType.DMA] * (depth + 2))
    def gather(table_hbm, idx_hbm, out_hbm, *scratch):
        idx_v, rows_v = scratch[:depth], scratch[depth:2 * depth]
        gather_sems, idx_sem, out_sem = scratch[2 * depth:3 * depth], scratch[-2], scratch[-1]
        wid = lax.axis_index("s") * n_cores + lax.axis_index("c")

        @pl.loop(0, wins_per_worker, step=depth)
        def _(w):
            offs = [(wid * wins_per_worker + w + u) * win for u in range(depth)]
            loads = [pltpu.async_copy(idx_hbm.at[pl.ds(offs[u], win)], idx_v[u], idx_sem) for u in range(depth)]
            for cp in loads:
                cp.wait()
            gathers = [pltpu.async_copy(table_hbm.at[idx_v[u]], rows_v[u], gather_sems[u]) for u in range(depth)]
            stores = []
            for u in range(depth):
                gathers[u].wait()
                stores.append(pltpu.async_copy(rows_v[u], out_hbm.at[pl.ds(offs[u], win)], out_sem))
            for cp in stores:
                cp.wait()

    return gather(table, pos)


def _partial_cols(pv, m):
    n = pv.shape[1]
    stacked = jnp.concatenate([pv, jnp.zeros((LANES - MOBA_VROWS, n), F32)], axis=0)
    row = lax.broadcasted_iota(jnp.int32, (LANES, n), 0)
    return jnp.where(row == HEAD_DIM + 1, m, stacked)


def _partial_rows(pv, m):
    return _partial_cols(pv, m).T


def _moba_sparse_kernel(blk_ref, q_ref, k_ref, v_ref, _after_ref, o_ref, *, n_groups):
    h, g = pl.program_id(0), pl.program_id(1)
    first = (h * n_groups + g) * MOBA_GROUP

    @pl.when(blk_ref[first] >= 0)
    def _():
        work = []
        for u in range(MOBA_GROUP):
            blk = jnp.maximum(blk_ref[first + u], 0)
            rows = pl.ds(pl.multiple_of(blk * MOBA_BLOCK, MOBA_BLOCK), MOBA_BLOCK)
            q = q_ref[u * MOBA_PAIR_TILE:(u + 1) * MOBA_PAIR_TILE, :].astype(BF16)
            work.append((blk, _dot_nt(k_ref[0, rows, :], q)))
        for u, (blk, s) in enumerate(work):
            m = jnp.max(s, axis=0, keepdims=True)
            pv = _dot(v_ref[0, blk], jnp.exp(s - m).astype(BF16))
            o_ref[u * MOBA_PAIR_TILE:(u + 1) * MOBA_PAIR_TILE, :] = _partial_rows(pv, m)

    @pl.when(g == n_groups - 1)
    def _():
        lane = lax.broadcasted_iota(jnp.int32, (MOBA_PAIR_TILE, LANES), 1)
        o_ref[(MOBA_GROUP - 1) * MOBA_PAIR_TILE:, :] = jnp.where(lane == HEAD_DIM + 1, NEG, 0.0)


def _moba_sparse(q_sorted, k_pad, vt_aug, tile_blk, n_tiles, after):
    n_heads, s, _ = k_pad.shape
    n_groups = n_tiles // MOBA_GROUP
    rows = pl.BlockSpec((MOBA_GROUP * MOBA_PAIR_TILE, LANES), lambda h, g, blk: (h * n_groups + g, 0))
    grid_spec = pltpu.PrefetchScalarGridSpec(
        num_scalar_prefetch=1, grid=(n_heads, n_groups),
        in_specs=[rows, pl.BlockSpec((1, s, LANES), lambda h, g, blk: (h, 0, 0)),
                  pl.BlockSpec((1, s // MOBA_BLOCK, MOBA_VROWS, MOBA_BLOCK), lambda h, g, blk: (h, 0, 0, 0)),
                  _ORDER_ONLY],
        out_specs=rows)
    return pl.pallas_call(
        functools.partial(_moba_sparse_kernel, n_groups=n_groups),
        grid_spec=grid_spec,
        out_shape=jax.ShapeDtypeStruct(q_sorted.shape, F32),
        compiler_params=_cparams("arbitrary", "arbitrary"),
        name="moba_sparse",
    )(tile_blk, q_sorted, k_pad, vt_aug, after)


def _moba_merge_kernel(qa_ref, qb_ref, ka_ref, kb_ref, va_ref, vb_ref, ga_ref, gb_ref, _after_ref, o_ref):
    ki = lax.broadcasted_iota(jnp.int32, (MOBA_BLOCK, MOBA_BLOCK), 0)
    qi = lax.broadcasted_iota(jnp.int32, (MOBA_BLOCK, MOBA_BLOCK), 1)
    outs = []
    for q_ref, k_ref, v_ref, g_ref in ((qa_ref, ka_ref, va_ref, ga_ref), (qb_ref, kb_ref, vb_ref, gb_ref)):
        s = jnp.where(ki <= qi, _dot_nt(k_ref[0], q_ref[0].astype(BF16)), NEG)
        m = jnp.max(s, axis=0, keepdims=True)
        parts = [_partial_cols(_dot(v_ref[0, 0], jnp.exp(s - m).astype(BF16)), m)]
        parts += [g_ref[0, r].T for r in range(MOBA_TOPK)]
        ms = [x[HEAD_DIM + 1:HEAD_DIM + 2, :] for x in parts]
        m_all = functools.reduce(jnp.maximum, ms)
        acc = sum(jnp.exp(mi - m_all) * x for mi, x in zip(ms, parts))
        outs.append(acc[:HEAD_DIM, :] / acc[HEAD_DIM:HEAD_DIM + 1, :])
    o_ref[...] = jnp.concatenate(outs, axis=0).T.astype(BF16)


def _moba_merge(q_rows, k_pad, vt_aug, gathered, after):
    n_heads, s, _ = q_rows.shape
    blk = lambda par: pl.BlockSpec((1, MOBA_BLOCK, LANES), lambda p, i: (2 * p + par, i, 0))
    vspec = lambda par: pl.BlockSpec((1, 1, MOBA_VROWS, MOBA_BLOCK), lambda p, i: (2 * p + par, i, 0, 0))
    gspec = lambda par: pl.BlockSpec((1, MOBA_TOPK, MOBA_BLOCK, LANES), lambda p, i: (2 * p + par, 0, i, 0))
    return pl.pallas_call(
        _moba_merge_kernel,
        grid=(n_heads // 2, s // MOBA_BLOCK),
        in_specs=[blk(0), blk(1), blk(0), blk(1), vspec(0), vspec(1), gspec(0), gspec(1), _ORDER_ONLY],
        out_specs=pl.BlockSpec((MOBA_BLOCK, LANES), lambda p, i: (i, p)),
        out_shape=jax.ShapeDtypeStruct((s, WIDTH), BF16),
        compiler_params=_cparams("arbitrary", "arbitrary"),
        name="moba_merge",
    )(q_rows, q_rows, k_pad, k_pad, vt_aug, vt_aug, gathered, gathered, after)


def _moba_attention(qc, kc, qkv_c, kmean, overlap_a, overlap_b):
    s = qc.shape[0]
    n_tiles = _moba_tiles_per_head(s)
    q_rows, k_pad, vt_aug, sel, rank, counts_raw = _moba_route(qc, kc, qkv_c, _moba_mean_rows(kmean))
    counts = counts_raw[:, :s // MOBA_BLOCK, 0].astype(jnp.int32)
    pos, tile_blk = _moba_dispatch(sel, rank, counts, n_tiles)
    q_sorted = _sc_scatter_rows(q_rows.reshape(N_HEADS * s, LANES), pos, N_HEADS * n_tiles * MOBA_PAIR_TILE)
    out_a = overlap_a(counts_raw)
    partial = _moba_sparse(q_sorted, k_pad, vt_aug, tile_blk, n_tiles, out_a)
    gathered = _sc_gather_rows(partial, pos).reshape(N_HEADS, MOBA_TOPK, s, LANES)
    out_b = overlap_b(partial)
    return out_a, out_b, _moba_merge(q_rows, k_pad, vt_aug, gathered, out_b)


def _ssd_kernel(z_ref, xbc_ref, dt_ref, cw_ref, cb_ref, dtb_ref, alog_ref, dsk_ref, nw_ref, exp_ref, tri_ref,
                _after_ref, o_ref, state_ref, halo_ref):
    L = SSM_CHUNK

    @pl.when(pl.program_id(0) == 0)
    def _():
        state_ref[...] = jnp.zeros_like(state_ref)
        halo_ref[...] = jnp.zeros_like(halo_ref)

    cur = xbc_ref[...]
    prev8 = halo_ref[...]
    row8 = lax.broadcasted_iota(jnp.int32, (8, SSM_CONV_DIM), 0)
    conv = cb_ref[...] + cur * cw_ref[SSM_CONV - 1:SSM_CONV, :]
    for k in range(1, SSM_CONV):
        rolled = pltpu.roll(cur, k, 0)
        head = jnp.where(row8 < k, pltpu.roll(prev8, k, 0), rolled[0:8, :])
        shifted = jnp.concatenate([head, rolled[8:, :]], axis=0)
        conv = conv + shifted * cw_ref[SSM_CONV - 1 - k:SSM_CONV - k, :]
    halo_ref[...] = cur[L - 8:, :]
    act = _silu(conv)
    xs, bm, cm = act[:, :WIDTH], act[:, WIDTH:WIDTH + 2 * SSM_STATE], act[:, WIDTH + 2 * SSM_STATE:]

    dt = _softplus(dt_ref[...] + dtb_ref[...])
    da = dt * (-jnp.exp(alog_ref[...]))
    cs = _dot_exact_lhs(tri_ref[...], da)
    expand = exp_ref[...]
    cs_b = _dot_exact_rhs(cs, expand)
    dt_b = _dot_exact_rhs(dt, expand)
    cs_last = cs_b[L - 1:L, :]
    xdt = xs * dt_b
    cs_t = cs.T

    ri = lax.broadcasted_iota(jnp.int32, (L, L), 0)
    ci = lax.broadcasted_iota(jnp.int32, (L, L), 1)
    causal = ci <= ri
    lane = lax.broadcasted_iota(jnp.int32, (L, LANES), 1)
    bmb, cmb = bm.astype(BF16), cm.astype(BF16)
    xdtb = xdt.astype(BF16)
    y_pairs = []
    for p in range(N_HEADS // 2):
        g = (2 * p) // (N_HEADS // SSM_GROUPS)
        grp = slice(g * SSM_STATE, (g + 1) * SSM_STATE)
        cbm = _dot_nt(cmb[:, grp], bmb[:, grp])
        x_pair = xdtb[:, p * LANES:(p + 1) * LANES]
        halves = []
        for h in (2 * p, 2 * p + 1):
            seg = cs[:, h:h + 1] - cs_t[h:h + 1, :]
            decay = jnp.exp(jnp.where(causal, seg, NEG))
            halves.append(_dot((cbm * decay).astype(BF16), x_pair))
        y_pairs.append(jnp.where(lane < HEAD_DIM, halves[0], halves[1]))
    y_diag = jnp.concatenate(y_pairs, axis=1)

    state = state_ref[...]
    stb = state.astype(BF16)
    half = WIDTH // SSM_GROUPS
    y_off = jnp.concatenate(
        [_dot(cmb[:, g * SSM_STATE:(g + 1) * SSM_STATE], stb[:, g * half:(g + 1) * half])
         for g in range(SSM_GROUPS)], axis=1) * jnp.exp(cs_b)
    y = y_diag + y_off + xs * dsk_ref[...]

    w = (xdt * jnp.exp(cs_last - cs_b)).astype(BF16)
    bm_t = bm.T.astype(BF16)
    new = jnp.concatenate(
        [_dot(bm_t[g * SSM_STATE:(g + 1) * SSM_STATE, :], w[:, g * half:(g + 1) * half])
         for g in range(SSM_GROUPS)], axis=1)
    state_ref[...] = state * jnp.exp(cs_last) + new

    gt = y * _silu(z_ref[...])
    outs = []
    for g in range(SSM_GROUPS):
        gg = gt[:, g * half:(g + 1) * half]
        outs.append(gg * lax.rsqrt(jnp.mean(gg * gg, axis=-1, keepdims=True) + NORM_EPS))
    o_ref[...] = (jnp.concatenate(outs, axis=1) * nw_ref[...]).astype(BF16)


def _ssd_mixer(z, xbc, dt_raw, conv_w, conv_b, dt_bias, a_log, d_skip, norm_w, expand, tri, after):
    s = z.shape[0]
    L = SSM_CHUNK
    row = lambda n: pl.BlockSpec((L, n), lambda c: (c, 0))
    cst = lambda r, n: pl.BlockSpec((r, n), lambda c: (0, 0))
    return pl.pallas_call(
        _ssd_kernel,
        grid=(s // L,),
        in_specs=[row(WIDTH), row(SSM_CONV_DIM), row(DT_PAD), cst(SSM_CONV, SSM_CONV_DIM),
                  cst(1, SSM_CONV_DIM), cst(1, DT_PAD), cst(1, DT_PAD), cst(1, WIDTH), cst(1, WIDTH),
                  cst(DT_PAD, WIDTH), cst(L, L), _ORDER_ONLY],
        out_specs=row(WIDTH),
        out_shape=jax.ShapeDtypeStruct((s, WIDTH), BF16),
        scratch_shapes=[pltpu.VMEM((SSM_STATE, WIDTH), F32), pltpu.VMEM((8, SSM_CONV_DIM), F32)],
        compiler_params=_cparams("arbitrary"),
        cost_estimate=pl.CostEstimate(
            flops=2 * s * L * (SSM_GROUPS * SSM_STATE + WIDTH + 2 * N_HEADS * LANES) + 4 * s * SSM_STATE * WIDTH,
            transcendentals=s * (L * N_HEADS + SSM_CONV_DIM + 3 * WIDTH),
            bytes_accessed=s * (WIDTH + SSM_CONV_DIM + DT_PAD) * 4 + s * WIDTH * 2),
        name="ssd_mixer",
    )(z, xbc, dt_raw, conv_w, conv_b, dt_bias, a_log, d_skip, norm_w, expand, tri, after)


def _outproj_kernel(x_ref, ya_ref, yb_ref, yc_ref, w_ref, gate_ref, o_ref):
    y = (_dot(ya_ref[...], w_ref[0:WIDTH, :]) + _dot(yb_ref[...], w_ref[WIDTH:2 * WIDTH, :])
         + _dot(yc_ref[...], w_ref[2 * WIDTH:, :]))
    o_ref[...] = x_ref[...] + gate_ref[...] * y


def _out_projection(x, ya, yb, yc, w_all, gate, layer):
    s, d = x.shape
    tm = 512
    row = lambda n: pl.BlockSpec((tm, n), lambda i: (i, 0))
    return pl.pallas_call(
        _outproj_kernel,
        grid=(s // tm,),
        in_specs=[row(d), row(WIDTH), row(WIDTH), row(WIDTH),
                  pl.BlockSpec((None, 3 * WIDTH, d), lambda i: (layer, 0, 0)), pl.BlockSpec((1, d), lambda i: (0, 0))],
        out_specs=row(d),
        out_shape=jax.ShapeDtypeStruct((s, d), F32),
        compiler_params=_cparams("arbitrary"),
        name="out_projection",
    )(x, ya, yb, yc, w_all, gate)


def _ffn_kernel(x_ref, g_ref, sc_ref, sh_ref, gate_ref, wg_ref, wu_ref, wd_ref, o_ref, *, chunk):
    x = x_ref[...]
    h = _mod_norm(x, g_ref[...], sc_ref[...], sh_ref[...]).astype(BF16)
    hidden = wg_ref.shape[1]
    acc = jnp.zeros(x.shape, F32)
    for c in range(hidden // chunk):
        cols = slice(c * chunk, (c + 1) * chunk)
        a = _silu(_dot(h, wg_ref[:, cols])) * _dot(h, wu_ref[:, cols])
        acc = acc + _dot(a.astype(BF16), wd_ref[cols, :])
    o_ref[...] = x + gate_ref[...] * acc


def _ffn(x, g, sc, sh, gate, w_gate, w_up, w_down, layer):
    s, d = x.shape
    hidden = w_gate.shape[2]
    tm = 256
    row = pl.BlockSpec((tm, d), lambda i: (i, 0))
    vec = pl.BlockSpec((1, d), lambda i: (0, 0))
    return pl.pallas_call(
        functools.partial(_ffn_kernel, chunk=256),
        grid=(s // tm,),
        in_specs=[row, vec, vec, vec, vec,
                  pl.BlockSpec((None, d, hidden), lambda i: (layer, 0, 0)),
                  pl.BlockSpec((None, d, hidden), lambda i: (layer, 0, 0)),
                  pl.BlockSpec((None, hidden, d), lambda i: (layer, 0, 0))],
        out_specs=row,
        out_shape=jax.ShapeDtypeStruct((s, d), F32),
        compiler_params=_cparams("arbitrary"),
        name="ffn",
    )(x, g, sc, sh, gate, w_gate, w_up, w_down)


def _rope_tables(seq):
    pos = jnp.arange(seq, dtype=F32)
    inv = ROPE_THETA ** (-jnp.arange(0, ROPE_DIM, 2, dtype=F32) / ROPE_DIM)
    ang = pos[:, None] * inv[None, :]
    cos, sin = jnp.cos(ang), jnp.sin(ang)
    ones = jnp.ones((seq, HEAD_DIM - ROPE_DIM), F32)
    zeros = jnp.zeros((seq, HEAD_DIM - ROPE_DIM), F32)
    zh = jnp.zeros((seq, ROPE_HALF), F32)
    cos_h = jnp.concatenate([cos, cos, ones], axis=1)
    sin_lo = jnp.concatenate([-sin, zh, zeros], axis=1)
    sin_hi = jnp.concatenate([zh, sin, zeros], axis=1)
    two = lambda t: jnp.concatenate([t, t], axis=1)
    return two(cos_h), two(sin_lo), two(sin_hi)


def _moba_mean_rows(kmean):
    nb = kmean.shape[0]
    km = kmean.reshape(nb, N_HEADS, HEAD_DIM).transpose(1, 0, 2)
    return jnp.pad(km, ((0, 0), (0, MOBA_MAX_BLOCKS - nb), (0, LANES - HEAD_DIM)))


def kernel(x, c, ada_w, ada_b, norm_mix, w_in, qn_swa, kn_swa, qn_moba, kn_moba, conv_w, conv_b, dt_bias,
           a_log, d_skip, ssm_norm, w_out, norm_ffn, w_gate, w_up, w_down):
    batch, seq, d = x.shape
    depth = ada_w.shape[0]
    assert batch == 1 and d == D_MODEL
    assert seq % SWA_SPAN == 0 and seq // MOBA_BLOCK <= MOBA_MAX_BLOCKS
    xs = x.reshape(seq, d)

    mod = _ada_modulation(c, ada_w, ada_b)
    rope_tabs = _rope_tables(seq)
    lane_head = jnp.arange(LANES) // HEAD_DIM
    hsum = (lane_head[:, None] == lane_head[None, :]).astype(BF16)
    expand = (jnp.arange(DT_PAD)[:, None] == (jnp.arange(WIDTH) // HEAD_DIM)[None, :]).astype(BF16)
    tri = (jnp.arange(SSM_CHUNK)[:, None] >= jnp.arange(SSM_CHUNK)[None, :]).astype(BF16)
    pad8 = lambda v: jnp.pad(v, (0, DT_PAD - N_HEADS)).reshape(1, DT_PAD)
    two = lambda g: jnp.concatenate([g, g]).reshape(1, LANES)

    w_in_b = jnp.pad(w_in, ((0, 0), (0, 0), (0, IN_PROJ_PAD - w_in.shape[2]))).astype(BF16)
    w_out_b, w_gate_b, w_up_b, w_down_b = (w.astype(BF16) for w in (w_out, w_gate, w_up, w_down))

    for l in range(depth):
        shift_m, scale_m, gate_m, shift_f, scale_f, gate_f = [mod[l, :, i * d:(i + 1) * d] for i in range(6)]
        qkv_a, qkv_c, z, xbc, dt_raw = _in_projection(xs, norm_mix[l].reshape(1, d), scale_m, shift_m, w_in_b, l)

        gains = [two(g[l]) for g in (qn_swa, kn_swa, qn_moba, kn_moba)]
        qa, ka, qc, kc, kmean = _prep_qkv(qkv_a, qkv_c, gains, hsum, rope_tabs)

        dilated = functools.partial(_dilated_attention, qa, ka, qkv_a)
        ssd = functools.partial(
            _ssd_mixer, z, xbc, dt_raw, conv_w[l], conv_b[l].reshape(1, -1), pad8(dt_bias[l]), pad8(a_log[l]),
            jnp.repeat(d_skip[l], HEAD_DIM).reshape(1, WIDTH), ssm_norm[l].reshape(1, WIDTH), expand, tri)
        y_b, y_a, y_c = _moba_attention(qc, kc, qkv_c, kmean, ssd, dilated)

        xs = _out_projection(xs, y_a, y_b, y_c, w_out_b, gate_m, l)
        xs = _ffn(xs, norm_ffn[l].reshape(1, d), scale_f, shift_f, gate_f,
                  w_gate_b, w_up_b, w_down_b, l)
    return xs.reshape(batch, seq, d)
```

```python
import functools

import jax
import jax.numpy as jnp
from jax import lax
from jax.experimental import pallas as pl
from jax.experimental.pallas import tpu as pltpu
from jax.experimental.pallas import tpu_sc as plsc

F32 = jnp.float32
BF16 = jnp.bfloat16

D_MODEL = 1024
HEAD_DIM = 64
N_HEADS = 8
WIDTH = N_HEADS * HEAD_DIM
ATTN_SCALE = HEAD_DIM ** -0.5
NORM_EPS = 1e-6
NEG = -1e30

ROPE_THETA = 500000.0
ROPE_DIM = HEAD_DIM // 4
ROPE_HALF = ROPE_DIM // 2

DILATIONS = (1, 4, 16)
SWA_BLOCK = 128
SWA_SPAN = DILATIONS[-1] * SWA_BLOCK
SWA_UNROLL = 3

SSM_STATE = 128
SSM_GROUPS = 2
SSM_CONV = 4
SSM_CHUNK = 256
SSM_CONV_DIM = WIDTH + 2 * SSM_GROUPS * SSM_STATE

MOBA_BLOCK = 256
MOBA_TOPK = 3
MOBA_PAIR_TILE = 128
MOBA_GROUP = 16
SC_INDEX_WINDOW = 128
MOBA_MAX_BLOCKS = 64
MOBA_MERGE_BLOCKS = 2
MOBA_VROWS = HEAD_DIM + 16

LANES = 128
DT_PAD = LANES
IN_PROJ_PAD = 6 * WIDTH + WIDTH + SSM_CONV_DIM + DT_PAD
OFF_A, OFF_C, OFF_Z, OFF_XBC, OFF_DT = 0, 3 * WIDTH, 6 * WIDTH, 7 * WIDTH, 7 * WIDTH + SSM_CONV_DIM

VMEM_LIMIT = 56 * 1024 * 1024


_ORDER_ONLY = pl.BlockSpec(memory_space=pl.ANY)


def _resident(block_shape, index_map):
    return pl.BlockSpec(block_shape, index_map, pipeline_mode=pl.Buffered(1))


def _cparams(*sem):
    return pltpu.CompilerParams(dimension_semantics=sem, vmem_limit_bytes=VMEM_LIMIT)


def _split3(a):
    hi = a.astype(BF16)
    r1 = a - hi.astype(F32)
    mid = r1.astype(BF16)
    lo = (r1 - mid.astype(F32)).astype(BF16)
    return hi, mid, lo


def _dot(a, b):
    return jnp.dot(a, b, preferred_element_type=F32)


def _dot_nt(a, b):
    return lax.dot_general(a, b, (((1,), (1,)), ((), ())), preferred_element_type=F32)


def _dot_exact_rhs(a, b_bf16):
    hi, mid, lo = _split3(a)
    return _dot(hi, b_bf16) + _dot(mid, b_bf16) + _dot(lo, b_bf16)


def _dot_exact_lhs(a_bf16, b):
    hi, mid, lo = _split3(b)
    return _dot(a_bf16, hi) + _dot(a_bf16, mid) + _dot(a_bf16, lo)


def _dot_f32(a, b, dot=_dot):
    ah, am, al = _split3(a)
    bh, bm, bl = _split3(b)
    return (dot(ah, bh) + (dot(ah, bm) + dot(am, bh))
            + (dot(ah, bl) + dot(am, bm) + dot(al, bh)))


def _dot_bf16x3(a, b, dot=_dot):
    ah = a.astype(BF16)
    al = (a - ah.astype(F32)).astype(BF16)
    bh = b.astype(BF16)
    bl = (b - bh.astype(F32)).astype(BF16)
    return dot(ah, bh) + (dot(ah, bl) + dot(al, bh))


def _silu(x):
    return x / (1.0 + jnp.exp(-x))


def _softplus(x):
    return jnp.maximum(x, 0.0) + jnp.log1p(jnp.exp(-jnp.abs(x)))


def _ada_kernel(c_ref, w_ref, b_ref, o_ref):
    c = c_ref[...]
    o_ref[0] = _dot_f32(_silu(c), w_ref[0]) + b_ref[0]


def _ada_modulation(c, ada_w, ada_b):
    depth, d, n = ada_w.shape
    tn = 1024
    c8 = jnp.broadcast_to(c, (8, d))
    out = pl.pallas_call(
        _ada_kernel,
        grid=(depth, n // tn),
        in_specs=[pl.BlockSpec((8, d), lambda l, j: (0, 0)),
                  pl.BlockSpec((1, d, tn), lambda l, j: (l, 0, j)),
                  pl.BlockSpec((1, 1, tn), lambda l, j: (l, 0, j))],
        out_specs=pl.BlockSpec((1, 8, tn), lambda l, j: (l, 0, j)),
        out_shape=jax.ShapeDtypeStruct((depth, 8, n), F32),
        compiler_params=_cparams("arbitrary", "arbitrary"),
        name="ada_modulation",
    )(c8, ada_w, ada_b.reshape(depth, 1, n))
    return out[:, 0:1, :]


def _mod_norm(x, g, sc, sh):
    var = jnp.mean(x * x, axis=-1, keepdims=True)
    return (x * lax.rsqrt(var + NORM_EPS) * g) * (1.0 + sc) + sh


def _inproj_kernel(x_ref, g_ref, sc_ref, sh_ref, w_ref, oa_ref, oc_ref, oz_ref, ox_ref, od_ref):
    h = _mod_norm(x_ref[...], g_ref[...], sc_ref[...], sh_ref[...]).astype(BF16)
    oa_ref[...] = _dot(h, w_ref[:, OFF_A:OFF_C])
    oc_ref[...] = _dot(h, w_ref[:, OFF_C:OFF_Z])
    oz_ref[...] = _dot(h, w_ref[:, OFF_Z:OFF_XBC])
    ox_ref[...] = _dot(h, w_ref[:, OFF_XBC:OFF_DT])
    od_ref[...] = _dot(h, w_ref[:, OFF_DT:IN_PROJ_PAD])


def _in_projection(x, g, sc, sh, w_all, layer):
    s, d = x.shape
    tm = 512
    row = lambda n: pl.BlockSpec((tm, n), lambda i: (i, 0))
    vec = pl.BlockSpec((1, d), lambda i: (0, 0))
    return pl.pallas_call(
        _inproj_kernel,
        grid=(s // tm,),
        in_specs=[row(d), vec, vec, vec, _resident((None, d, IN_PROJ_PAD), lambda i: (layer, 0, 0))],
        out_specs=[row(3 * WIDTH), row(3 * WIDTH), row(WIDTH), row(SSM_CONV_DIM), row(DT_PAD)],
        out_shape=[jax.ShapeDtypeStruct((s, n), F32)
                   for n in (3 * WIDTH, 3 * WIDTH, WIDTH, SSM_CONV_DIM, DT_PAD)],
        compiler_params=_cparams("arbitrary"),
        name="in_projection",
    )(x, g, sc, sh, w_all)


def _head_norm_rope(x, gain, hsum, cos_t, sin_lo, sin_hi):
    ss = _dot_exact_rhs(x * x, hsum)
    y = x * lax.rsqrt(ss * (1.0 / HEAD_DIM) + NORM_EPS) * gain
    return (y * cos_t + pltpu.roll(y, ROPE_HALF, 1) * sin_hi
            + pltpu.roll(y, LANES - ROPE_HALF, 1) * sin_lo)


def _prep_kernel(a_ref, c_ref, gqa_ref, gka_ref, gqc_ref, gkc_ref, hsum_ref, cos_ref, slo_ref, shi_ref,
                 qa_ref, ka_ref, qc_ref, kc_ref, km_ref):
    hsum = hsum_ref[...]
    cos_t, slo, shi = cos_ref[...], slo_ref[...], shi_ref[...]
    rope = lambda x, g: _head_norm_rope(x, g, hsum, cos_t, slo, shi)
    for p in range(WIDTH // LANES):
        lo, hi = p * LANES, (p + 1) * LANES
        qa_ref[:, lo:hi] = rope(a_ref[:, lo:hi], gqa_ref[...]) * ATTN_SCALE
        ka_ref[:, lo:hi] = rope(a_ref[:, WIDTH + lo:WIDTH + hi], gka_ref[...])
        qc_ref[:, lo:hi] = rope(c_ref[:, lo:hi], gqc_ref[...])
        kc = rope(c_ref[:, WIDTH + lo:WIDTH + hi], gkc_ref[...])
        kc_ref[:, lo:hi] = kc
        km_ref[0, :, lo:hi] = jnp.mean(kc, axis=0, keepdims=True)


def _prep_qkv(qkv_a, qkv_c, gains, hsum, rope_tabs):
    s = qkv_a.shape[0]
    tm = MOBA_BLOCK
    row = lambda n: pl.BlockSpec((tm, n), lambda i: (i, 0))
    cst = lambda r, n: pl.BlockSpec((r, n), lambda i: (0, 0))
    return pl.pallas_call(
        _prep_kernel,
        grid=(s // tm,),
        in_specs=[row(2 * WIDTH), row(2 * WIDTH)] + [cst(1, LANES)] * 4 + [cst(LANES, LANES)]
                 + [row(LANES)] * 3,
        out_specs=[row(WIDTH)] * 4 + [pl.BlockSpec((1, 1, WIDTH), lambda i: (i, 0, 0))],
        out_shape=[jax.ShapeDtypeStruct((s, WIDTH), F32)] * 4
                  + [jax.ShapeDtypeStruct((s // tm, 1, WIDTH), F32)],
        compiler_params=_cparams("arbitrary"),
        name="prep_qkv",
    )(qkv_a, qkv_c, *gains, hsum, *rope_tabs)


def _dilated_kernel(q_ref, kc_ref, kp_ref, vc_ref, vp_ref, _after_ref, o_ref, m_s, l_s, a_s):
    has_prev = pl.program_id(1) > 0
    lane = lax.broadcasted_iota(jnp.int32, (SWA_BLOCK, LANES), 1)
    head_a = lane < HEAD_DIM
    qi = lax.broadcasted_iota(jnp.int32, (SWA_BLOCK, 2 * SWA_BLOCK), 0)
    ki = lax.broadcasted_iota(jnp.int32, (SWA_BLOCK, 2 * SWA_BLOCK), 1)
    band = (ki >= qi) & (ki <= qi + SWA_BLOCK)
    band_edge = band & ((ki >= SWA_BLOCK) | has_prev)

    def attend(rows, k_cat, v_cat, mask, init):
        q = q_ref[rows, :]
        kb, vb = k_cat.astype(BF16), v_cat.astype(BF16)
        parts = []
        for sel in (head_a, ~head_a):
            s = jnp.where(mask, _dot_nt(jnp.where(sel, q, 0.0).astype(BF16), kb), NEG)
            m_loc = jnp.max(s, axis=-1, keepdims=True)
            p = jnp.exp(s - m_loc)
            parts.append((m_loc, jnp.sum(p, axis=-1, keepdims=True), _dot(p.astype(BF16), vb)))
        (m_a, s_a, pv_a), (m_b, s_b, pv_b) = parts
        m_new = jnp.where(head_a, m_a, m_b)
        l_new = jnp.where(head_a, s_a, s_b)
        a_new = jnp.where(head_a, pv_a, pv_b)
        if not init:
            m_in, m_loc = m_s[rows, :], m_new
            m_new = jnp.maximum(m_in, m_loc)
            alpha, beta = jnp.exp(m_in - m_new), jnp.exp(m_loc - m_new)
            l_new = alpha * l_s[rows, :] + beta * l_new
            a_new = alpha * a_s[rows, :] + beta * a_new
        m_s[rows, :] = m_new
        l_s[rows, :] = l_new
        a_s[rows, :] = a_new

    def edge_unit(r, d, init):
        stride = None if d == 1 else d
        cur = pl.ds(r, SWA_BLOCK, stride)
        prev = pl.ds(SWA_SPAN - SWA_BLOCK * d + r, SWA_BLOCK, stride)
        k_cat = jnp.concatenate([kp_ref[prev, :], kc_ref[cur, :]], axis=0)
        v_cat = jnp.concatenate([vp_ref[prev, :], vc_ref[cur, :]], axis=0)
        attend(cur, k_cat, v_cat, band_edge, init)

    def inner_unit(r, nb, d, init):
        stride = None if d == 1 else d
        rows = pl.ds(SWA_BLOCK * d * nb + r, SWA_BLOCK, stride)
        keys = pl.ds(SWA_BLOCK * d * (nb - 1) + r, 2 * SWA_BLOCK, stride)
        attend(rows, kc_ref[keys, :], vc_ref[keys, :], band, init)

    for d in DILATIONS:
        init = d == 1
        n_blocks = SWA_SPAN // (SWA_BLOCK * d)

        def sub_body(r, carry, d=d, init=init, n_blocks=n_blocks):
            edge_unit(r, d, init)
            if n_blocks > 1:
                def blk_body(nb, c2):
                    inner_unit(r, nb, d, init)
                    return c2
                lax.fori_loop(1, n_blocks, blk_body, 0, unroll=SWA_UNROLL)
            return carry

        lax.fori_loop(0, d, sub_body, 0, unroll=4 if n_blocks == 1 else 1)
    o_ref[...] = (a_s[...] / l_s[...]).astype(BF16)


def _dilated_attention(q, k, qkv, pair0, n_pairs, after):
    s = q.shape[0]
    all_pairs = WIDTH // LANES
    cur = lambda off: pl.BlockSpec((SWA_SPAN, LANES), lambda p, n: (n, off + p))
    prev = lambda off: pl.BlockSpec((SWA_SPAN, LANES), lambda p, n: (jnp.maximum(n - 1, 0), off + p))
    v0 = 2 * all_pairs + pair0
    width = n_pairs * LANES
    return pl.pallas_call(
        _dilated_kernel,
        grid=(n_pairs, s // SWA_SPAN),
        in_specs=[cur(pair0), cur(pair0), prev(pair0), cur(v0), prev(v0), _ORDER_ONLY],
        out_specs=cur(0),
        out_shape=jax.ShapeDtypeStruct((s, width), BF16),
        scratch_shapes=[pltpu.VMEM((SWA_SPAN, LANES), F32)] * 3,
        compiler_params=_cparams("arbitrary", "arbitrary"),
        cost_estimate=pl.CostEstimate(
            flops=len(DILATIONS) * 4 * s * 2 * SWA_BLOCK * width,
            transcendentals=len(DILATIONS) * s * 2 * SWA_BLOCK * 2 * n_pairs,
            bytes_accessed=5 * s * width * 4 + s * width * 2),
        name="dilated_attention",
    )(q, k, k, qkv, qkv, after)


def _moba_route_kernel(q_ref, k_ref, v_ref, km_ref, sut_ref, ones_ref,
                       qrow_ref, kpad_ref, vt_ref, sel_ref, rank_ref, cnt_ref, base_ref):
    i = pl.program_id(0)

    @pl.when(i == 0)
    def _():
        base_ref[...] = jnp.zeros_like(base_ref)

    lane = lax.broadcasted_iota(jnp.int32, (MOBA_BLOCK, LANES), 1)
    is_feat = lane < HEAD_DIM
    row = lax.broadcasted_iota(jnp.int32, (MOBA_MAX_BLOCKS, MOBA_BLOCK), 0)
    past = row < i
    ones_rows = jnp.where(
        lax.broadcasted_iota(jnp.int32, (MOBA_VROWS - HEAD_DIM, MOBA_BLOCK), 0) == 0, 1.0, 0.0)
    ninf = float("-inf")
    for h in range(N_HEADS):
        cols = slice((h // 2) * LANES, (h // 2 + 1) * LANES)
        q, k = q_ref[:, cols], k_ref[:, cols]
        v_t = v_ref[:, cols].T
        if h % 2:
            q, k = pltpu.roll(q, HEAD_DIM, 1), pltpu.roll(k, HEAD_DIM, 1)
            v_t = v_t[HEAD_DIM:, :]
        else:
            v_t = v_t[:HEAD_DIM, :]
        gate_t = _dot_bf16x3(km_ref[h], jnp.where(is_feat, q, 0.0), dot=_dot_nt)
        cand = jnp.where(past, gate_t, ninf)
        picks = []
        for _ in range(MOBA_TOPK):
            best = jnp.max(cand, axis=0, keepdims=True)
            idx = jnp.min(jnp.where(cand == best, row, 2 * LANES), axis=0, keepdims=True)
            ok = best > ninf
            pick = (row == idx) & ok
            cand = jnp.where(pick, ninf, cand)
            picks.append((idx, ok, pick))
        chosen = jnp.where(picks[0][2] | picks[1][2] | picks[2][2], 1.0, 0.0).astype(BF16)
        base = base_ref[h]
        rank_full = base + _dot(chosen, sut_ref[...])
        base_ref[h] = base + _dot(chosen, ones_ref[...])
        for r, (idx, ok, pick) in enumerate(picks):
            rank = jnp.sum(jnp.where(pick, rank_full, 0.0), axis=0, keepdims=True)
            sel_ref[h, r:r + 1, :] = jnp.where(ok, idx, -1)
            rank_ref[h, r:r + 1, :] = rank.astype(jnp.int32)
        qrow_ref[h] = jnp.where(is_feat, q * ATTN_SCALE, 0.0)
        kpad_ref[h] = jnp.where(is_feat, k, 0.0).astype(BF16)
        vt_ref[h, 0] = jnp.concatenate([v_t, ones_rows], axis=0).astype(BF16)

    @pl.when(i == pl.num_programs(0) - 1)
    def _():
        cnt_ref[...] = base_ref[...]


def _moba_route(qc, kc, qkv_c, km_rows):
    s = qc.shape[0]
    nb = s // MOBA_BLOCK
    row = pl.BlockSpec((MOBA_BLOCK, WIDTH), lambda i: (i, 0))
    per_head = lambda dt: (pl.BlockSpec((N_HEADS, MOBA_BLOCK, LANES), lambda i: (0, i, 0)),
                           jax.ShapeDtypeStruct((N_HEADS, s, LANES), dt))
    picks = (pl.BlockSpec((N_HEADS, MOBA_TOPK, MOBA_BLOCK), lambda i: (0, 0, i)),
             jax.ShapeDtypeStruct((N_HEADS, MOBA_TOPK, s), jnp.int32))
    outs = [per_head(F32), per_head(BF16),
            (pl.BlockSpec((N_HEADS, 1, MOBA_VROWS, MOBA_BLOCK), lambda i: (0, i, 0, 0)),
             jax.ShapeDtypeStruct((N_HEADS, nb, MOBA_VROWS, MOBA_BLOCK), BF16)),
            picks, picks,
            (pl.BlockSpec((N_HEADS, MOBA_MAX_BLOCKS, MOBA_BLOCK), lambda i: (0, 0, 0)),
             jax.ShapeDtypeStruct((N_HEADS, MOBA_MAX_BLOCKS, MOBA_BLOCK), F32))]
    qi = jnp.arange(MOBA_BLOCK)
    strict_upper = (qi[:, None] < qi[None, :]).astype(BF16)
    all_ones = jnp.ones((MOBA_BLOCK, MOBA_BLOCK), BF16)
    sq = pl.BlockSpec((MOBA_BLOCK, MOBA_BLOCK), lambda i: (0, 0))
    return pl.pallas_call(
        _moba_route_kernel,
        grid=(nb,),
        in_specs=[row, row, pl.BlockSpec((MOBA_BLOCK, WIDTH), lambda i: (i, 2)),
                  pl.BlockSpec((N_HEADS, MOBA_MAX_BLOCKS, LANES), lambda i: (0, 0, 0)), sq, sq],
        out_specs=[o[0] for o in outs],
        out_shape=[o[1] for o in outs],
        scratch_shapes=[pltpu.VMEM((N_HEADS, MOBA_MAX_BLOCKS, MOBA_BLOCK), F32)],
        compiler_params=_cparams("arbitrary"),
        name="moba_route",
    )(qc, kc, qkv_c, km_rows, strict_upper, all_ones)


def _moba_tiles_per_head(seq):
    tiles = MOBA_TOPK * seq // MOBA_PAIR_TILE + seq // MOBA_BLOCK
    return (tiles // MOBA_GROUP + 1) * MOBA_GROUP


def _moba_dispatch(sel, rank, counts, n_tiles):
    n_heads, nb = counts.shape
    padded = (counts + MOBA_PAIR_TILE - 1) // MOBA_PAIR_TILE * MOBA_PAIR_TILE
    ends = jnp.cumsum(padded, axis=1)
    offs = ends - padded
    rows_per_head = n_tiles * MOBA_PAIR_TILE
    head_base = (jnp.arange(n_heads, dtype=jnp.int32) * rows_per_head)[:, None, None]
    blocks = jnp.arange(nb, dtype=jnp.int32)
    off_sel = jnp.sum(jnp.where(sel[..., None] == blocks, offs[:, None, None, :], 0), axis=-1)
    pos = head_base + jnp.where(sel >= 0, off_sel + rank, rows_per_head - MOBA_PAIR_TILE)
    tile_start = jnp.arange(n_tiles, dtype=jnp.int32) * MOBA_PAIR_TILE
    blk = jnp.sum(ends[:, None, :] <= tile_start[None, :, None], axis=2)
    tile_blk = jnp.where(tile_start[None, :] < ends[:, -1:], blk, -1)
    return pos.reshape(-1).astype(jnp.int32), tile_blk.reshape(-1).astype(jnp.int32)


def _sc_workers():
    info = plsc.get_sparse_core_info()
    return info.num_cores, info.num_cores * info.num_subcores


def _sc_scatter_rows(rows, pos, n_out):
    n_src = rows.shape[0]
    seq = n_src // N_HEADS
    n_cores, n_workers = _sc_workers()
    win = SC_INDEX_WINDOW
    wins_per_worker = n_src // win // n_workers
    wins_per_head = seq // win
    assert n_src % (win * n_workers) == 0
    mesh = plsc.VectorSubcoreMesh(core_axis_name="c", subcore_axis_name="s")

    depth = 2
    assert wins_per_worker % depth == 0

    @functools.partial(
        pl.kernel, mesh=mesh, out_type=jax.ShapeDtypeStruct((n_out, LANES), F32), name="moba_scatter_rows",
        cost_estimate=pl.CostEstimate(flops=0, transcendentals=0,
                                      bytes_accessed=(1 + MOBA_TOPK) * n_src * LANES * 4 + pos.size * 4),
        scratch_types=[pltpu.VMEM((win,), jnp.int32)] * (depth * MOBA_TOPK)
                      + [pltpu.VMEM((win, LANES), F32)] * depth + [pltpu.SemaphoreType.DMA] * (depth + 1))
    def scatter(src_hbm, idx_hbm, out_hbm, *scratch):
        idx_v = scratch[:depth * MOBA_TOPK]
        rows_v = scratch[depth * MOBA_TOPK:depth * MOBA_TOPK + depth]
        load_sems, store_sem = scratch[-depth - 1:-1], scratch[-1]
        wid = lax.axis_index("s") * n_cores + lax.axis_index("c")

        @pl.loop(0, wins_per_worker, step=depth)
        def _(w):
            loads = []
            for u in range(depth):
                src_win = wid * wins_per_worker + w + u
                h, qw = src_win // wins_per_head, src_win % wins_per_head
                cps = [pltpu.async_copy(src_hbm.at[pl.ds(src_win * win, win)], rows_v[u], load_sems[u])]
                for r in range(MOBA_TOPK):
                    cps.append(pltpu.async_copy(
                        idx_hbm.at[pl.ds((h * MOBA_TOPK + r) * seq + qw * win, win)],
                        idx_v[u * MOBA_TOPK + r], load_sems[u]))
                loads.append(cps)
            stores = []
            for u in range(depth):
                for cp in loads[u]:
                    cp.wait()
                stores += [pltpu.async_copy(rows_v[u], out_hbm.at[idx_v[u * MOBA_TOPK + r]], store_sem)
                           for r in range(MOBA_TOPK)]
            for cp in stores:
                cp.wait()

    return scatter(rows, pos)


def _sc_gather_rows(table, pos):
    n = pos.shape[0]
    n_cores, n_workers = _sc_workers()
    win = SC_INDEX_WINDOW
    wins_per_worker = n // win // n_workers
    assert n % (win * n_workers) == 0
    mesh = plsc.VectorSubcoreMesh(core_axis_name="c", subcore_axis_name="s")

    depth = 4
    assert wins_per_worker % depth == 0

    @functools.partial(
        pl.kernel, mesh=mesh, out_type=jax.ShapeDtypeStruct((n, LANES), F32), name="moba_gather_rows",
        cost_estimate=pl.CostEstimate(flops=0, transcendentals=0, bytes_accessed=2 * n * LANES * 4 + n * 4),
        scratch_types=[pltpu.VMEM((win,), jnp.int32)] * depth + [pltpu.VMEM((win, LANES), F32)] * depth
                      + [pltpu.SemaphoreType.DMA] * (depth + 2))
    def gather(table_hbm, idx_hbm, out_hbm, *scratch):
        idx_v, rows_v = scratch[:depth], scratch[depth:2 * depth]
        gather_sems, idx_sem, out_sem = scratch[2 * depth:3 * depth], scratch[-2], scratch[-1]
        wid = lax.axis_index("s") * n_cores + lax.axis_index("c")

        @pl.loop(0, wins_per_worker, step=depth)
        def _(w):
            offs = [(wid * wins_per_worker + w + u) * win for u in range(depth)]
            loads = [pltpu.async_copy(idx_hbm.at[pl.ds(offs[u], win)], idx_v[u], idx_sem) for u in range(depth)]
            for cp in loads:
                cp.wait()
            gathers = [pltpu.async_copy(table_hbm.at[idx_v[u]], rows_v[u], gather_sems[u]) for u in range(depth)]
            stores = []
            for u in range(depth):
                gathers[u].wait()
                stores.append(pltpu.async_copy(rows_v[u], out_hbm.at[pl.ds(offs[u], win)], out_sem))
            for cp in stores:
                cp.wait()

    return gather(table, pos)


def _partial_cols(pv, m):
    n = pv.shape[1]
    stacked = jnp.concatenate([pv, jnp.zeros((LANES - MOBA_VROWS, n), F32)], axis=0)
    row = lax.broadcasted_iota(jnp.int32, (LANES, n), 0)
    return jnp.where(row == HEAD_DIM + 1, m, stacked)


def _partial_rows(pv, m):
    return _partial_cols(pv, m).T


def _moba_sparse_kernel(blk_ref, q_ref, k_ref, v_ref, _after_ref, o_ref, *, n_groups):
    h, g = pl.program_id(0), pl.program_id(1)
    first = (h * n_groups + g) * MOBA_GROUP

    @pl.when(blk_ref[first] >= 0)
    def _():
        work = []
        for u in range(MOBA_GROUP):
            blk = jnp.maximum(blk_ref[first + u], 0)
            rows = pl.ds(pl.multiple_of(blk * MOBA_BLOCK, MOBA_BLOCK), MOBA_BLOCK)
            q = q_ref[u * MOBA_PAIR_TILE:(u + 1) * MOBA_PAIR_TILE, :].astype(BF16)
            work.append((blk, _dot_nt(k_ref[0, rows, :], q)))
        for u, (blk, s) in enumerate(work):
            m = jnp.max(s, axis=0, keepdims=True)
            pv = _dot(v_ref[0, blk], jnp.exp(s - m).astype(BF16))
            o_ref[u * MOBA_PAIR_TILE:(u + 1) * MOBA_PAIR_TILE, :] = _partial_rows(pv, m)

    @pl.when(g == n_groups - 1)
    def _():
        lane = lax.broadcasted_iota(jnp.int32, (MOBA_PAIR_TILE, LANES), 1)
        o_ref[(MOBA_GROUP - 1) * MOBA_PAIR_TILE:, :] = jnp.where(lane == HEAD_DIM + 1, NEG, 0.0)


def _moba_sparse(q_sorted, k_pad, vt_aug, tile_blk, n_tiles, after):
    n_heads, s, _ = k_pad.shape
    n_groups = n_tiles // MOBA_GROUP
    rows = pl.BlockSpec((MOBA_GROUP * MOBA_PAIR_TILE, LANES), lambda h, g, blk: (h * n_groups + g, 0))
    grid_spec = pltpu.PrefetchScalarGridSpec(
        num_scalar_prefetch=1, grid=(n_heads, n_groups),
        in_specs=[rows, pl.BlockSpec((1, s, LANES), lambda h, g, blk: (h, 0, 0)),
                  pl.BlockSpec((1, s // MOBA_BLOCK, MOBA_VROWS, MOBA_BLOCK), lambda h, g, blk: (h, 0, 0, 0)),
                  _ORDER_ONLY],
        out_specs=rows)
    return pl.pallas_call(
        functools.partial(_moba_sparse_kernel, n_groups=n_groups),
        grid_spec=grid_spec,
        out_shape=jax.ShapeDtypeStruct(q_sorted.shape, F32),
        compiler_params=_cparams("arbitrary", "arbitrary"),
        name="moba_sparse",
    )(tile_blk, q_sorted, k_pad, vt_aug, after)


def _moba_merge_kernel(qa_ref, qb_ref, ka_ref, kb_ref, va_ref, vb_ref, ga_ref, gb_ref, _after_ref, o_ref):
    ki = lax.broadcasted_iota(jnp.int32, (MOBA_BLOCK, MOBA_BLOCK), 0)
    qi = lax.broadcasted_iota(jnp.int32, (MOBA_BLOCK, MOBA_BLOCK), 1)
    heads = ((qa_ref, ka_ref, va_ref, ga_ref), (qb_ref, kb_ref, vb_ref, gb_ref))
    chains = [(hh, b) for b in range(MOBA_MERGE_BLOCKS) for hh in range(2)]
    rows = lambda b: slice(b * MOBA_BLOCK, (b + 1) * MOBA_BLOCK)
    scores = [_dot_nt(heads[hh][1][0, rows(b), :], heads[hh][0][0, rows(b), :].astype(BF16))
              for hh, b in chains]
    outs = {}
    for (hh, b), s in zip(chains, scores):
        _, _, v_ref, g_ref = heads[hh]
        s = jnp.where(ki <= qi, s, NEG)
        m = jnp.max(s, axis=0, keepdims=True)
        parts = [_partial_cols(_dot(v_ref[0, b], jnp.exp(s - m).astype(BF16)), m)]
        parts += [g_ref[0, r, rows(b), :].T for r in range(MOBA_TOPK)]
        ms = [x[HEAD_DIM + 1:HEAD_DIM + 2, :] for x in parts]
        m_all = functools.reduce(jnp.maximum, ms)
        acc = sum(jnp.exp(mi - m_all) * x for mi, x in zip(ms, parts))
        outs[hh, b] = acc[:HEAD_DIM, :] / acc[HEAD_DIM:HEAD_DIM + 1, :]
    for b in range(MOBA_MERGE_BLOCKS):
        o_ref[rows(b), :] = jnp.concatenate([outs[0, b], outs[1, b]], axis=0).T.astype(BF16)


def _moba_merge(q_rows, k_pad, vt_aug, gathered, after):
    n_heads, s, _ = q_rows.shape
    t = MOBA_MERGE_BLOCKS * MOBA_BLOCK
    blk = lambda par: pl.BlockSpec((1, t, LANES), lambda p, i: (2 * p + par, i, 0))
    vspec = lambda par: pl.BlockSpec((1, MOBA_MERGE_BLOCKS, MOBA_VROWS, MOBA_BLOCK),
                                     lambda p, i: (2 * p + par, i, 0, 0))
    gspec = lambda par: pl.BlockSpec((1, MOBA_TOPK, t, LANES), lambda p, i: (2 * p + par, 0, i, 0))
    return pl.pallas_call(
        _moba_merge_kernel,
        grid=(n_heads // 2, s // t),
        in_specs=[blk(0), blk(1), blk(0), blk(1), vspec(0), vspec(1), gspec(0), gspec(1), _ORDER_ONLY],
        out_specs=pl.BlockSpec((t, LANES), lambda p, i: (i, p)),
        out_shape=jax.ShapeDtypeStruct((s, WIDTH), BF16),
        compiler_params=_cparams("arbitrary", "arbitrary"),
        name="moba_merge",
    )(q_rows, q_rows, k_pad, k_pad, vt_aug, vt_aug, gathered, gathered, after)


def _moba_attention(qc, kc, qkv_c, kmean, overlap_a, overlap_b):
    s = qc.shape[0]
    n_tiles = _moba_tiles_per_head(s)
    q_rows, k_pad, vt_aug, sel, rank, counts_raw = _moba_route(qc, kc, qkv_c, _moba_mean_rows(kmean))
    counts = counts_raw[:, :s // MOBA_BLOCK, 0].astype(jnp.int32)
    pos, tile_blk = _moba_dispatch(sel, rank, counts, n_tiles)
    q_sorted = _sc_scatter_rows(q_rows.reshape(N_HEADS * s, LANES), pos, N_HEADS * n_tiles * MOBA_PAIR_TILE)
    def chain(calls, anchor):
        outs = []
        for call in calls:
            outs.append(call(anchor))
            anchor = outs[-1]
        return outs

    out_a = chain(overlap_a, counts_raw)
    partial = _moba_sparse(q_sorted, k_pad, vt_aug, tile_blk, n_tiles, out_a[-1])
    gathered = _sc_gather_rows(partial, pos).reshape(N_HEADS, MOBA_TOPK, s, LANES)
    out_b = chain(overlap_b, partial)
    return out_a, out_b, _moba_merge(q_rows, k_pad, vt_aug, gathered, out_b[-1])


def _ssd_kernel(z_ref, xbc_ref, dt_ref, cw_ref, cb_ref, dtb_ref, alog_ref, dsk_ref, nw_ref, exp_ref, tri_ref,
                _after_ref, o_ref, state_ref, halo_ref):
    L = SSM_CHUNK

    @pl.when(pl.program_id(0) == 0)
    def _():
        state_ref[...] = jnp.zeros_like(state_ref)
        halo_ref[...] = jnp.zeros_like(halo_ref)

    cur = xbc_ref[...]
    prev8 = halo_ref[...]
    row8 = lax.broadcasted_iota(jnp.int32, (8, SSM_CONV_DIM), 0)
    conv = cb_ref[...] + cur * cw_ref[SSM_CONV - 1:SSM_CONV, :]
    for k in range(1, SSM_CONV):
        rolled = pltpu.roll(cur, k, 0)
        head = jnp.where(row8 < k, pltpu.roll(prev8, k, 0), rolled[0:8, :])
        shifted = jnp.concatenate([head, rolled[8:, :]], axis=0)
        conv = conv + shifted * cw_ref[SSM_CONV - 1 - k:SSM_CONV - k, :]
    halo_ref[...] = cur[L - 8:, :]
    act = _silu(conv)
    xs, bm, cm = act[:, :WIDTH], act[:, WIDTH:WIDTH + 2 * SSM_STATE], act[:, WIDTH + 2 * SSM_STATE:]

    dt = _softplus(dt_ref[...] + dtb_ref[...])
    da = dt * (-jnp.exp(alog_ref[...]))
    cs = _dot_exact_lhs(tri_ref[...], da)
    expand = exp_ref[...]
    cs_b = _dot_exact_rhs(cs, expand)
    dt_b = _dot_exact_rhs(dt, expand)
    cs_last = cs_b[L - 1:L, :]
    xdt = xs * dt_b
    cs_t = cs.T

    ri = lax.broadcasted_iota(jnp.int32, (L, L), 0)
    ci = lax.broadcasted_iota(jnp.int32, (L, L), 1)
    causal = ci <= ri
    lane = lax.broadcasted_iota(jnp.int32, (L, LANES), 1)
    bmb, cmb = bm.astype(BF16), cm.astype(BF16)
    xdtb = xdt.astype(BF16)
    y_pairs = []
    for p in range(N_HEADS // 2):
        g = (2 * p) // (N_HEADS // SSM_GROUPS)
        grp = slice(g * SSM_STATE, (g + 1) * SSM_STATE)
        cbm = _dot_nt(cmb[:, grp], bmb[:, grp])
        x_pair = xdtb[:, p * LANES:(p + 1) * LANES]
        halves = []
        for h in (2 * p, 2 * p + 1):
            seg = cs[:, h:h + 1] - cs_t[h:h + 1, :]
            decay = jnp.exp(jnp.where(causal, seg, NEG))
            halves.append(_dot((cbm * decay).astype(BF16), x_pair))
        y_pairs.append(jnp.where(lane < HEAD_DIM, halves[0], halves[1]))
    y_diag = jnp.concatenate(y_pairs, axis=1)

    state = state_ref[...]
    stb = state.astype(BF16)
    half = WIDTH // SSM_GROUPS
    y_off = jnp.concatenate(
        [_dot(cmb[:, g * SSM_STATE:(g + 1) * SSM_STATE], stb[:, g * half:(g + 1) * half])
         for g in range(SSM_GROUPS)], axis=1) * jnp.exp(cs_b)
    y = y_diag + y_off + xs * dsk_ref[...]

    w = (xdt * jnp.exp(cs_last - cs_b)).astype(BF16)
    bm_t = bm.T.astype(BF16)
    new = jnp.concatenate(
        [_dot(bm_t[g * SSM_STATE:(g + 1) * SSM_STATE, :], w[:, g * half:(g + 1) * half])
         for g in range(SSM_GROUPS)], axis=1)
    state_ref[...] = state * jnp.exp(cs_last) + new

    gt = y * _silu(z_ref[...])
    outs = []
    for g in range(SSM_GROUPS):
        gg = gt[:, g * half:(g + 1) * half]
        outs.append(gg * lax.rsqrt(jnp.mean(gg * gg, axis=-1, keepdims=True) + NORM_EPS))
    o_ref[...] = (jnp.concatenate(outs, axis=1) * nw_ref[...]).astype(BF16)


def _ssd_mixer(z, xbc, dt_raw, conv_w, conv_b, dt_bias, a_log, d_skip, norm_w, expand, tri, after):
    s = z.shape[0]
    L = SSM_CHUNK
    row = lambda n: pl.BlockSpec((L, n), lambda c: (c, 0))
    cst = lambda r, n: pl.BlockSpec((r, n), lambda c: (0, 0))
    return pl.pallas_call(
        _ssd_kernel,
        grid=(s // L,),
        in_specs=[row(WIDTH), row(SSM_CONV_DIM), row(DT_PAD), cst(SSM_CONV, SSM_CONV_DIM),
                  cst(1, SSM_CONV_DIM), cst(1, DT_PAD), cst(1, DT_PAD), cst(1, WIDTH), cst(1, WIDTH),
                  cst(DT_PAD, WIDTH), cst(L, L), _ORDER_ONLY],
        out_specs=row(WIDTH),
        out_shape=jax.ShapeDtypeStruct((s, WIDTH), BF16),
        scratch_shapes=[pltpu.VMEM((SSM_STATE, WIDTH), F32), pltpu.VMEM((8, SSM_CONV_DIM), F32)],
        compiler_params=_cparams("arbitrary"),
        cost_estimate=pl.CostEstimate(
            flops=2 * s * L * (SSM_GROUPS * SSM_STATE + WIDTH + 2 * N_HEADS * LANES) + 4 * s * SSM_STATE * WIDTH,
            transcendentals=s * (L * N_HEADS + SSM_CONV_DIM + 3 * WIDTH),
            bytes_accessed=s * (WIDTH + SSM_CONV_DIM + DT_PAD) * 4 + s * WIDTH * 2),
        name="ssd_mixer",
    )(z, xbc, dt_raw, conv_w, conv_b, dt_bias, a_log, d_skip, norm_w, expand, tri, after)


def _outproj_kernel(x_ref, *refs):
    *y_refs, w_ref, gate_ref, o_ref = refs
    y, row0 = 0.0, 0
    for y_ref in y_refs:
        y = y + _dot(y_ref[...], w_ref[row0:row0 + y_ref.shape[1], :])
        row0 += y_ref.shape[1]
    o_ref[...] = x_ref[...] + gate_ref[...] * y


def _out_projection(x, ys, w_all, gate, layer):
    s, d = x.shape
    tm = 512
    row = lambda n: pl.BlockSpec((tm, n), lambda i: (i, 0))
    assert sum(y.shape[1] for y in ys) == w_all.shape[1]
    return pl.pallas_call(
        _outproj_kernel,
        grid=(s // tm,),
        in_specs=[row(d)] + [row(y.shape[1]) for y in ys]
                 + [pl.BlockSpec((None, w_all.shape[1], d), lambda i: (layer, 0, 0)),
                    pl.BlockSpec((1, d), lambda i: (0, 0))],
        out_specs=row(d),
        out_shape=jax.ShapeDtypeStruct((s, d), F32),
        compiler_params=_cparams("arbitrary"),
        name="out_projection",
    )(x, *ys, w_all, gate)


def _ffn_kernel(x_ref, g_ref, sc_ref, sh_ref, gate_ref, wg_ref, wu_ref, wd_ref, o_ref, *, chunk):
    x = x_ref[...]
    h = _mod_norm(x, g_ref[...], sc_ref[...], sh_ref[...]).astype(BF16)
    hidden = wg_ref.shape[1]
    acc = jnp.zeros(x.shape, F32)
    for c in range(hidden // chunk):
        cols = slice(c * chunk, (c + 1) * chunk)
        a = _silu(_dot(h, wg_ref[:, cols])) * _dot(h, wu_ref[:, cols])
        acc = acc + _dot(a.astype(BF16), wd_ref[cols, :])
    o_ref[...] = x + gate_ref[...] * acc


def _ffn(x, g, sc, sh, gate, w_gate, w_up, w_down, layer):
    s, d = x.shape
    hidden = w_gate.shape[2]
    tm = 512
    row = pl.BlockSpec((tm, d), lambda i: (i, 0))
    vec = pl.BlockSpec((1, d), lambda i: (0, 0))
    return pl.pallas_call(
        functools.partial(_ffn_kernel, chunk=256),
        grid=(s // tm,),
        in_specs=[row, vec, vec, vec, vec,
                  _resident((None, d, hidden), lambda i: (layer, 0, 0)),
                  _resident((None, d, hidden), lambda i: (layer, 0, 0)),
                  _resident((None, hidden, d), lambda i: (layer, 0, 0))],
        out_specs=row,
        out_shape=jax.ShapeDtypeStruct((s, d), F32),
        compiler_params=_cparams("arbitrary"),
        name="ffn",
    )(x, g, sc, sh, gate, w_gate, w_up, w_down)


def _rope_tables(seq):
    pos = jnp.arange(seq, dtype=F32)
    inv = ROPE_THETA ** (-jnp.arange(0, ROPE_DIM, 2, dtype=F32) / ROPE_DIM)
    ang = pos[:, None] * inv[None, :]
    cos, sin = jnp.cos(ang), jnp.sin(ang)
    ones = jnp.ones((seq, HEAD_DIM - ROPE_DIM), F32)
    zeros = jnp.zeros((seq, HEAD_DIM - ROPE_DIM), F32)
    zh = jnp.zeros((seq, ROPE_HALF), F32)
    cos_h = jnp.concatenate([cos, cos, ones], axis=1)
    sin_lo = jnp.concatenate([-sin, zh, zeros], axis=1)
    sin_hi = jnp.concatenate([zh, sin, zeros], axis=1)
    two = lambda t: jnp.concatenate([t, t], axis=1)
    return two(cos_h), two(sin_lo), two(sin_hi)


def _moba_mean_rows(kmean):
    nb = kmean.shape[0]
    km = kmean.reshape(nb, N_HEADS, HEAD_DIM).transpose(1, 0, 2)
    return jnp.pad(km, ((0, 0), (0, MOBA_MAX_BLOCKS - nb), (0, LANES - HEAD_DIM)))


def kernel(x, c, ada_w, ada_b, norm_mix, w_in, qn_swa, kn_swa, qn_moba, kn_moba, conv_w, conv_b, dt_bias,
           a_log, d_skip, ssm_norm, w_out, norm_ffn, w_gate, w_up, w_down):
    batch, seq, d = x.shape
    depth = ada_w.shape[0]
    assert batch == 1 and d == D_MODEL
    assert seq % SWA_SPAN == 0 and seq // MOBA_BLOCK <= MOBA_MAX_BLOCKS
    xs = x.reshape(seq, d)

    mod = _ada_modulation(c, ada_w, ada_b)
    rope_tabs = _rope_tables(seq)
    lane_head = jnp.arange(LANES) // HEAD_DIM
    hsum = (lane_head[:, None] == lane_head[None, :]).astype(BF16)
    expand = (jnp.arange(DT_PAD)[:, None] == (jnp.arange(WIDTH) // HEAD_DIM)[None, :]).astype(BF16)
    tri = (jnp.arange(SSM_CHUNK)[:, None] >= jnp.arange(SSM_CHUNK)[None, :]).astype(BF16)
    pad8 = lambda v: jnp.pad(v, (0, DT_PAD - N_HEADS)).reshape(1, DT_PAD)
    two = lambda g: jnp.concatenate([g, g]).reshape(1, LANES)

    w_in_b = jnp.pad(w_in, ((0, 0), (0, 0), (0, IN_PROJ_PAD - w_in.shape[2]))).astype(BF16)
    w_out_b, w_gate_b, w_up_b, w_down_b = (w.astype(BF16) for w in (w_out, w_gate, w_up, w_down))

    for l in range(depth):
        shift_m, scale_m, gate_m, shift_f, scale_f, gate_f = [mod[l, :, i * d:(i + 1) * d] for i in range(6)]
        qkv_a, qkv_c, z, xbc, dt_raw = _in_projection(xs, norm_mix[l].reshape(1, d), scale_m, shift_m, w_in_b, l)

        gains = [two(g[l]) for g in (qn_swa, kn_swa, qn_moba, kn_moba)]
        qa, ka, qc, kc, kmean = _prep_qkv(qkv_a, qkv_c, gains, hsum, rope_tabs)

        half = WIDTH // LANES // 2
        dilated_lo = functools.partial(_dilated_attention, qa, ka, qkv_a, 0, half)
        dilated_hi = functools.partial(_dilated_attention, qa, ka, qkv_a, half, half)
        ssd = functools.partial(
            _ssd_mixer, z, xbc, dt_raw, conv_w[l], conv_b[l].reshape(1, -1), pad8(dt_bias[l]), pad8(a_log[l]),
            jnp.repeat(d_skip[l], HEAD_DIM).reshape(1, WIDTH), ssm_norm[l].reshape(1, WIDTH), expand, tri)
        (y_a_lo,), (y_a_hi, y_b), y_c = _moba_attention(qc, kc, qkv_c, kmean, (dilated_lo,), (dilated_hi, ssd))

        xs = _out_projection(xs, (y_a_lo, y_a_hi, y_b, y_c), w_out_b, gate_m, l)
        xs = _ffn(xs, norm_ffn[l].reshape(1, d), scale_f, shift_f, gate_f,
                  w_gate_b, w_up_b, w_down_b, l)
    return xs.reshape(batch, seq, d)
```

```python
import functools

import jax
import jax.numpy as jnp
from jax import lax
from jax.experimental import pallas as pl
from jax.experimental.pallas import tpu as pltpu
from jax.experimental.pallas import tpu_sc as plsc

F32 = jnp.float32
BF16 = jnp.bfloat16

D_MODEL = 1024
HEAD_DIM = 64
N_HEADS = 8
WIDTH = N_HEADS * HEAD_DIM
ATTN_SCALE = HEAD_DIM ** -0.5
NORM_EPS = 1e-6
NEG = -1e30

ROPE_THETA = 500000.0
ROPE_DIM = HEAD_DIM // 4
ROPE_HALF = ROPE_DIM // 2

DILATIONS = (1, 4, 16)
SWA_BLOCK = 128
SWA_SPAN = DILATIONS[-1] * SWA_BLOCK
SWA_GROUP = 4

SSM_STATE = 128
SSM_GROUPS = 2
SSM_CONV = 4
SSM_CHUNK = 256
SSM_CONV_DIM = WIDTH + 2 * SSM_GROUPS * SSM_STATE

MOBA_BLOCK = 256
MOBA_TOPK = 3
MOBA_PAIR_TILE = 128
MOBA_GROUP = 32
SC_INDEX_WINDOW = 128
MOBA_MAX_BLOCKS = 64
MOBA_MERGE_BLOCKS = 2
MOBA_VROWS = HEAD_DIM + 16

LANES = 128
DT_PAD = LANES
IN_PROJ_PAD = 6 * WIDTH + WIDTH + SSM_CONV_DIM + DT_PAD
OFF_A, OFF_C, OFF_Z, OFF_XBC, OFF_DT = 0, 3 * WIDTH, 6 * WIDTH, 7 * WIDTH, 7 * WIDTH + SSM_CONV_DIM

VMEM_LIMIT = 56 * 1024 * 1024


_ORDER_ONLY = pl.BlockSpec(memory_space=pl.ANY)


def _resident(block_shape, index_map):
    return pl.BlockSpec(block_shape, index_map, pipeline_mode=pl.Buffered(1))


def _cparams(*sem):
    return pltpu.CompilerParams(dimension_semantics=sem, vmem_limit_bytes=VMEM_LIMIT)


def _split3(a):
    hi = a.astype(BF16)
    r1 = a - hi.astype(F32)
    mid = r1.astype(BF16)
    lo = (r1 - mid.astype(F32)).astype(BF16)
    return hi, mid, lo


def _dot(a, b):
    return jnp.dot(a, b, preferred_element_type=F32)


def _dot_nt(a, b):
    return lax.dot_general(a, b, (((1,), (1,)), ((), ())), preferred_element_type=F32)


def _dot_exact_rhs(a, b_bf16):
    hi, mid, lo = _split3(a)
    return _dot(hi, b_bf16) + _dot(mid, b_bf16) + _dot(lo, b_bf16)


def _dot_exact_lhs(a_bf16, b):
    hi, mid, lo = _split3(b)
    return _dot(a_bf16, hi) + _dot(a_bf16, mid) + _dot(a_bf16, lo)


def _dot_f32(a, b, dot=_dot):
    ah, am, al = _split3(a)
    bh, bm, bl = _split3(b)
    return (dot(ah, bh) + (dot(ah, bm) + dot(am, bh))
            + (dot(ah, bl) + dot(am, bm) + dot(al, bh)))


def _dot_bf16x3(a, b, dot=_dot):
    ah = a.astype(BF16)
    al = (a - ah.astype(F32)).astype(BF16)
    bh = b.astype(BF16)
    bl = (b - bh.astype(F32)).astype(BF16)
    return dot(ah, bh) + (dot(ah, bl) + dot(al, bh))


def _silu(x):
    return x / (1.0 + jnp.exp(-x))


def _softplus(x):
    return jnp.maximum(x, 0.0) + jnp.log1p(jnp.exp(-jnp.abs(x)))


def _ada_kernel(c_ref, w_ref, b_ref, o_ref):
    c = c_ref[...]
    o_ref[0] = _dot_f32(_silu(c), w_ref[0]) + b_ref[0]


def _ada_modulation(c, ada_w, ada_b):
    depth, d, n = ada_w.shape
    tn = 1024
    c8 = jnp.broadcast_to(c, (8, d))
    out = pl.pallas_call(
        _ada_kernel,
        grid=(depth, n // tn),
        in_specs=[pl.BlockSpec((8, d), lambda l, j: (0, 0)),
                  pl.BlockSpec((1, d, tn), lambda l, j: (l, 0, j)),
                  pl.BlockSpec((1, 1, tn), lambda l, j: (l, 0, j))],
        out_specs=pl.BlockSpec((1, 8, tn), lambda l, j: (l, 0, j)),
        out_shape=jax.ShapeDtypeStruct((depth, 8, n), F32),
        compiler_params=_cparams("arbitrary", "arbitrary"),
        name="ada_modulation",
    )(c8, ada_w, ada_b.reshape(depth, 1, n))
    return out[:, 0:1, :]


def _mod_norm(x, g, sc, sh):
    var = jnp.mean(x * x, axis=-1, keepdims=True)
    return (x * lax.rsqrt(var + NORM_EPS) * g) * (1.0 + sc) + sh


def _inproj_kernel(x_ref, g_ref, sc_ref, sh_ref, w_ref, oa_ref, oc_ref, oz_ref, ox_ref, od_ref):
    h = _mod_norm(x_ref[...], g_ref[...], sc_ref[...], sh_ref[...]).astype(BF16)
    oa_ref[...] = _dot(h, w_ref[:, OFF_A:OFF_C])
    oc_ref[...] = _dot(h, w_ref[:, OFF_C:OFF_Z])
    oz_ref[...] = _dot(h, w_ref[:, OFF_Z:OFF_XBC])
    ox_ref[...] = _dot(h, w_ref[:, OFF_XBC:OFF_DT])
    od_ref[...] = _dot(h, w_ref[:, OFF_DT:IN_PROJ_PAD])


def _in_projection(x, g, sc, sh, w_all, layer):
    s, d = x.shape
    tm = 512
    row = lambda n: pl.BlockSpec((tm, n), lambda i: (i, 0))
    vec = pl.BlockSpec((1, d), lambda i: (0, 0))
    return pl.pallas_call(
        _inproj_kernel,
        grid=(s // tm,),
        in_specs=[row(d), vec, vec, vec, _resident((None, d, IN_PROJ_PAD), lambda i: (layer, 0, 0))],
        out_specs=[row(3 * WIDTH), row(3 * WIDTH), row(WIDTH), row(SSM_CONV_DIM), row(DT_PAD)],
        out_shape=[jax.ShapeDtypeStruct((s, n), F32)
                   for n in (3 * WIDTH, 3 * WIDTH, WIDTH, SSM_CONV_DIM, DT_PAD)],
        compiler_params=_cparams("arbitrary"),
        name="in_projection",
    )(x, g, sc, sh, w_all)


def _head_norm_rope(x, gain, hsum, cos_t, sin_lo, sin_hi):
    ss = _dot_exact_rhs(x * x, hsum)
    y = x * lax.rsqrt(ss * (1.0 / HEAD_DIM) + NORM_EPS) * gain
    return (y * cos_t + pltpu.roll(y, ROPE_HALF, 1) * sin_hi
            + pltpu.roll(y, LANES - ROPE_HALF, 1) * sin_lo)


def _prep_kernel(a_ref, c_ref, gqa_ref, gka_ref, gqc_ref, gkc_ref, hsum_ref, cos_ref, slo_ref, shi_ref,
                 qa_ref, ka_ref, qc_ref, kc_ref, km_ref):
    hsum = hsum_ref[...]
    cos_t, slo, shi = cos_ref[...], slo_ref[...], shi_ref[...]
    rope = lambda x, g: _head_norm_rope(x, g, hsum, cos_t, slo, shi)
    for p in range(WIDTH // LANES):
        lo, hi = p * LANES, (p + 1) * LANES
        qa_ref[:, lo:hi] = rope(a_ref[:, lo:hi], gqa_ref[...]) * ATTN_SCALE
        ka_ref[:, lo:hi] = rope(a_ref[:, WIDTH + lo:WIDTH + hi], gka_ref[...])
        qc_ref[:, lo:hi] = rope(c_ref[:, lo:hi], gqc_ref[...])
        kc = rope(c_ref[:, WIDTH + lo:WIDTH + hi], gkc_ref[...])
        kc_ref[:, lo:hi] = kc
        km_ref[0, :, lo:hi] = jnp.mean(kc, axis=0, keepdims=True)


def _prep_qkv(qkv_a, qkv_c, gains, hsum, rope_tabs):
    s = qkv_a.shape[0]
    tm = MOBA_BLOCK
    row = lambda n: pl.BlockSpec((tm, n), lambda i: (i, 0))
    cst = lambda r, n: pl.BlockSpec((r, n), lambda i: (0, 0))
    return pl.pallas_call(
        _prep_kernel,
        grid=(s // tm,),
        in_specs=[row(2 * WIDTH), row(2 * WIDTH)] + [cst(1, LANES)] * 4 + [cst(LANES, LANES)]
                 + [row(LANES)] * 3,
        out_specs=[row(WIDTH)] * 4 + [pl.BlockSpec((1, 1, WIDTH), lambda i: (i, 0, 0))],
        out_shape=[jax.ShapeDtypeStruct((s, WIDTH), F32)] * 4
                  + [jax.ShapeDtypeStruct((s // tm, 1, WIDTH), F32)],
        compiler_params=_cparams("arbitrary"),
        name="prep_qkv",
    )(qkv_a, qkv_c, *gains, hsum, *rope_tabs)


def _dilated_kernel(q_ref, kc_ref, kp_ref, vc_ref, vp_ref, _after_ref, o_ref, m_s, l_s, a_s):
    has_prev = pl.program_id(1) > 0
    lane = lax.broadcasted_iota(jnp.int32, (SWA_BLOCK, LANES), 1)
    head_a = lane < HEAD_DIM
    qi = lax.broadcasted_iota(jnp.int32, (SWA_BLOCK, 2 * SWA_BLOCK), 0)
    ki = lax.broadcasted_iota(jnp.int32, (SWA_BLOCK, 2 * SWA_BLOCK), 1)
    band = (ki >= qi) & (ki <= qi + SWA_BLOCK)
    band_edge = band & ((ki >= SWA_BLOCK) | has_prev)

    def edge_unit(r, d):
        stride = None if d == 1 else d
        cur = pl.ds(r, SWA_BLOCK, stride)
        prev = pl.ds(SWA_SPAN - SWA_BLOCK * d + r, SWA_BLOCK, stride)
        k_cat = jnp.concatenate([kp_ref[prev, :], kc_ref[cur, :]], axis=0)
        v_cat = jnp.concatenate([vp_ref[prev, :], vc_ref[cur, :]], axis=0)
        return cur, k_cat, v_cat, band_edge

    def inner_unit(r, nb, d):
        stride = None if d == 1 else d
        rows = pl.ds(SWA_BLOCK * d * nb + r, SWA_BLOCK, stride)
        keys = pl.ds(SWA_BLOCK * d * (nb - 1) + r, 2 * SWA_BLOCK, stride)
        return rows, kc_ref[keys, :], vc_ref[keys, :], band

    def attend(units, init):
        scores = []
        for rows, k_cat, _, mask in units:
            q = q_ref[rows, :]
            kb = k_cat.astype(BF16)
            scores.append([jnp.where(mask, _dot_nt(jnp.where(sel, q, 0.0).astype(BF16), kb), NEG)
                           for sel in (head_a, ~head_a)])
        probs = []
        for pair in scores:
            stats = []
            for s in pair:
                m_loc = jnp.max(s, axis=-1, keepdims=True)
                p = jnp.exp(s - m_loc)
                stats.append((m_loc, jnp.sum(p, axis=-1, keepdims=True), p.astype(BF16)))
            probs.append(stats)
        for (rows, _, v_cat, _), ((m_a, s_a, p_a), (m_b, s_b, p_b)) in zip(units, probs):
            vb = v_cat.astype(BF16)
            m_new = jnp.where(head_a, m_a, m_b)
            l_new = jnp.where(head_a, s_a, s_b)
            a_new = jnp.where(head_a, _dot(p_a, vb), _dot(p_b, vb))
            if not init:
                m_in, m_loc = m_s[rows, :], m_new
                m_new = jnp.maximum(m_in, m_loc)
                alpha, beta = jnp.exp(m_in - m_new), jnp.exp(m_loc - m_new)
                l_new = alpha * l_s[rows, :] + beta * l_new
                a_new = alpha * a_s[rows, :] + beta * a_new
            m_s[rows, :] = m_new
            l_s[rows, :] = l_new
            a_s[rows, :] = a_new

    g = SWA_GROUP
    for d in DILATIONS:
        init = d == 1
        n_blocks = SWA_SPAN // (SWA_BLOCK * d)
        if n_blocks == 1:
            def body(i, carry, d=d, init=init):
                attend([edge_unit(i * g + j, d) for j in range(g)], init)
                return carry
            lax.fori_loop(0, d // g, body, 0)
        else:
            def sub_body(r, carry, d=d, init=init, n_blocks=n_blocks):
                attend([edge_unit(r, d)] + [inner_unit(r, nb, d) for nb in range(1, g)], init)

                def body(i, c2):
                    attend([inner_unit(r, i * g + j, d) for j in range(g)], init)
                    return c2
                lax.fori_loop(1, n_blocks // g, body, 0)
                return carry
            lax.fori_loop(0, d, sub_body, 0)
    o_ref[...] = (a_s[...] / l_s[...]).astype(BF16)


def _dilated_attention(q, k, qkv, pair0, n_pairs, after):
    s = q.shape[0]
    all_pairs = WIDTH // LANES
    cur = lambda off: pl.BlockSpec((SWA_SPAN, LANES), lambda p, n: (n, off + p))
    prev = lambda off: pl.BlockSpec((SWA_SPAN, LANES), lambda p, n: (jnp.maximum(n - 1, 0), off + p))
    v0 = 2 * all_pairs + pair0
    width = n_pairs * LANES
    return pl.pallas_call(
        _dilated_kernel,
        grid=(n_pairs, s // SWA_SPAN),
        in_specs=[cur(pair0), cur(pair0), prev(pair0), cur(v0), prev(v0), _ORDER_ONLY],
        out_specs=cur(0),
        out_shape=jax.ShapeDtypeStruct((s, width), BF16),
        scratch_shapes=[pltpu.VMEM((SWA_SPAN, LANES), F32)] * 3,
        compiler_params=_cparams("arbitrary", "arbitrary"),
        cost_estimate=pl.CostEstimate(
            flops=len(DILATIONS) * 4 * s * 2 * SWA_BLOCK * width,
            transcendentals=len(DILATIONS) * s * 2 * SWA_BLOCK * 2 * n_pairs,
            bytes_accessed=5 * s * width * 4 + s * width * 2),
        name="dilated_attention",
    )(q, k, k, qkv, qkv, after)


def _moba_route_kernel(q_ref, k_ref, v_ref, km_ref, sut_ref, ones_ref,
                       qrow_ref, kpad_ref, vt_ref, sel_ref, rank_ref, cnt_ref, base_ref):
    i = pl.program_id(0)

    @pl.when(i == 0)
    def _():
        base_ref[...] = jnp.zeros_like(base_ref)

    lane = lax.broadcasted_iota(jnp.int32, (MOBA_BLOCK, LANES), 1)
    is_feat = lane < HEAD_DIM
    row = lax.broadcasted_iota(jnp.int32, (MOBA_MAX_BLOCKS, MOBA_BLOCK), 0)
    past = row < i
    ones_rows = jnp.where(
        lax.broadcasted_iota(jnp.int32, (MOBA_VROWS - HEAD_DIM, MOBA_BLOCK), 0) == 0, 1.0, 0.0)
    ninf = float("-inf")
    gates = []
    for h in range(N_HEADS):
        cols = slice((h // 2) * LANES, (h // 2 + 1) * LANES)
        q, k = q_ref[:, cols], k_ref[:, cols]
        v_t = v_ref[:, cols].T
        if h % 2:
            q, k = pltpu.roll(q, HEAD_DIM, 1), pltpu.roll(k, HEAD_DIM, 1)
            v_t = v_t[HEAD_DIM:, :]
        else:
            v_t = v_t[:HEAD_DIM, :]
        qrow_ref[h] = jnp.where(is_feat, q * ATTN_SCALE, 0.0)
        kpad_ref[h] = jnp.where(is_feat, k, 0.0).astype(BF16)
        vt_ref[h, 0] = jnp.concatenate([v_t, ones_rows], axis=0).astype(BF16)
        gates.append(_dot_bf16x3(km_ref[h], jnp.where(is_feat, q, 0.0), dot=_dot_nt))
    for h, gate_t in enumerate(gates):
        cand = jnp.where(past, gate_t, ninf)
        picks = []
        for _ in range(MOBA_TOPK):
            best = jnp.max(cand, axis=0, keepdims=True)
            idx = jnp.min(jnp.where(cand == best, row, 2 * LANES), axis=0, keepdims=True)
            ok = best > ninf
            pick = (row == idx) & ok
            cand = jnp.where(pick, ninf, cand)
            picks.append((idx, ok, pick))
        chosen = jnp.where(picks[0][2] | picks[1][2] | picks[2][2], 1.0, 0.0).astype(BF16)
        base = base_ref[h]
        rank_full = base + _dot(chosen, sut_ref[...])
        base_ref[h] = base + _dot(chosen, ones_ref[...])
        for r, (idx, ok, pick) in enumerate(picks):
            rank = jnp.sum(jnp.where(pick, rank_full, 0.0), axis=0, keepdims=True)
            sel_ref[h, r:r + 1, :] = jnp.where(ok, idx, -1)
            rank_ref[h, r:r + 1, :] = rank.astype(jnp.int32)

    @pl.when(i == pl.num_programs(0) - 1)
    def _():
        cnt_ref[...] = base_ref[...]


def _moba_route(qc, kc, qkv_c, km_rows):
    s = qc.shape[0]
    nb = s // MOBA_BLOCK
    row = pl.BlockSpec((MOBA_BLOCK, WIDTH), lambda i: (i, 0))
    per_head = lambda dt: (pl.BlockSpec((N_HEADS, MOBA_BLOCK, LANES), lambda i: (0, i, 0)),
                           jax.ShapeDtypeStruct((N_HEADS, s, LANES), dt))
    picks = (pl.BlockSpec((N_HEADS, MOBA_TOPK, MOBA_BLOCK), lambda i: (0, 0, i)),
             jax.ShapeDtypeStruct((N_HEADS, MOBA_TOPK, s), jnp.int32))
    outs = [per_head(F32), per_head(BF16),
            (pl.BlockSpec((N_HEADS, 1, MOBA_VROWS, MOBA_BLOCK), lambda i: (0, i, 0, 0)),
             jax.ShapeDtypeStruct((N_HEADS, nb, MOBA_VROWS, MOBA_BLOCK), BF16)),
            picks, picks,
            (pl.BlockSpec((N_HEADS, MOBA_MAX_BLOCKS, MOBA_BLOCK), lambda i: (0, 0, 0)),
             jax.ShapeDtypeStruct((N_HEADS, MOBA_MAX_BLOCKS, MOBA_BLOCK), F32))]
    qi = jnp.arange(MOBA_BLOCK)
    strict_upper = (qi[:, None] < qi[None, :]).astype(BF16)
    all_ones = jnp.ones((MOBA_BLOCK, MOBA_BLOCK), BF16)
    sq = pl.BlockSpec((MOBA_BLOCK, MOBA_BLOCK), lambda i: (0, 0))
    return pl.pallas_call(
        _moba_route_kernel,
        grid=(nb,),
        in_specs=[row, row, pl.BlockSpec((MOBA_BLOCK, WIDTH), lambda i: (i, 2)),
                  pl.BlockSpec((N_HEADS, MOBA_MAX_BLOCKS, LANES), lambda i: (0, 0, 0)), sq, sq],
        out_specs=[o[0] for o in outs],
        out_shape=[o[1] for o in outs],
        scratch_shapes=[pltpu.VMEM((N_HEADS, MOBA_MAX_BLOCKS, MOBA_BLOCK), F32)],
        compiler_params=_cparams("arbitrary"),
        name="moba_route",
    )(qc, kc, qkv_c, km_rows, strict_upper, all_ones)


def _moba_tiles_per_head(seq):
    tiles = MOBA_TOPK * seq // MOBA_PAIR_TILE + seq // MOBA_BLOCK
    return (tiles // MOBA_GROUP + 1) * MOBA_GROUP


def _moba_dispatch(sel, rank, counts, n_tiles):
    n_heads, nb = counts.shape
    padded = (counts + MOBA_PAIR_TILE - 1) // MOBA_PAIR_TILE * MOBA_PAIR_TILE
    ends = jnp.cumsum(padded, axis=1)
    offs = ends - padded
    rows_per_head = n_tiles * MOBA_PAIR_TILE
    head_base = (jnp.arange(n_heads, dtype=jnp.int32) * rows_per_head)[:, None, None]
    blocks = jnp.arange(nb, dtype=jnp.int32)
    off_sel = jnp.sum(jnp.where(sel[..., None] == blocks, offs[:, None, None, :], 0), axis=-1)
    pos = head_base + jnp.where(sel >= 0, off_sel + rank, rows_per_head - MOBA_PAIR_TILE)
    tile_start = jnp.arange(n_tiles, dtype=jnp.int32) * MOBA_PAIR_TILE
    blk = jnp.sum(ends[:, None, :] <= tile_start[None, :, None], axis=2)
    tile_blk = jnp.where(tile_start[None, :] < ends[:, -1:], blk, -1)
    return pos.reshape(-1).astype(jnp.int32), tile_blk.reshape(-1).astype(jnp.int32)


def _sc_workers():
    info = plsc.get_sparse_core_info()
    return info.num_cores, info.num_cores * info.num_subcores


def _sc_scatter_rows(rows, pos, n_out):
    n_src = rows.shape[0]
    seq = n_src // N_HEADS
    n_cores, n_workers = _sc_workers()
    win = SC_INDEX_WINDOW
    wins_per_worker = n_src // win // n_workers
    wins_per_head = seq // win
    assert n_src % (win * n_workers) == 0
    mesh = plsc.VectorSubcoreMesh(core_axis_name="c", subcore_axis_name="s")

    depth = 2
    assert wins_per_worker % depth == 0

    @functools.partial(
        pl.kernel, mesh=mesh, out_type=jax.ShapeDtypeStruct((n_out, LANES), F32), name="moba_scatter_rows",
        cost_estimate=pl.CostEstimate(flops=0, transcendentals=0,
                                      bytes_accessed=(1 + MOBA_TOPK) * n_src * LANES * 4 + pos.size * 4),
        scratch_types=[pltpu.VMEM((win,), jnp.int32)] * (depth * MOBA_TOPK)
                      + [pltpu.VMEM((win, LANES), F32)] * depth + [pltpu.SemaphoreType.DMA] * (depth + 1))
    def scatter(src_hbm, idx_hbm, out_hbm, *scratch):
        idx_v = scratch[:depth * MOBA_TOPK]
        rows_v = scratch[depth * MOBA_TOPK:depth * MOBA_TOPK + depth]
        load_sems, store_sem = scratch[-depth - 1:-1], scratch[-1]
        wid = lax.axis_index("s") * n_cores + lax.axis_index("c")

        @pl.loop(0, wins_per_worker, step=depth)
        def _(w):
            loads = []
            for u in range(depth):
                src_win = wid * wins_per_worker + w + u
                h, qw = src_win // wins_per_head, src_win % wins_per_head
                cps = [pltpu.async_copy(src_hbm.at[pl.ds(src_win * win, win)], rows_v[u], load_sems[u])]
                for r in range(MOBA_TOPK):
                    cps.append(pltpu.async_copy(
                        idx_hbm.at[pl.ds((h * MOBA_TOPK + r) * seq + qw * win, win)],
                        idx_v[u * MOBA_TOPK + r], load_sems[u]))
                loads.append(cps)
            stores = []
            for u in range(depth):
                for cp in loads[u]:
                    cp.wait()
                stores += [pltpu.async_copy(rows_v[u], out_hbm.at[idx_v[u * MOBA_TOPK + r]], store_sem)
                           for r in range(MOBA_TOPK)]
            for cp in stores:
                cp.wait()

    return scatter(rows, pos)


def _sc_gather_rows(table, pos):
    n = pos.shape[0]
    n_cores, n_workers = _sc_workers()
    win = SC_INDEX_WINDOW
    wins_per_worker = n // win // n_workers
    assert n % (win * n_workers) == 0
    mesh = plsc.VectorSubcoreMesh(core_axis_name="c", subcore_axis_name="s")

    depth = 4
    assert wins_per_worker % depth == 0

    @functools.partial(
        pl.kernel, mesh=mesh, out_type=jax.ShapeDtypeStruct((n, LANES), F32), name="moba_gather_rows",
        cost_estimate=pl.CostEstimate(flops=0, transcendentals=0, bytes_accessed=2 * n * LANES * 4 + n * 4),
        scratch_types=[pltpu.VMEM((win,), jnp.int32)] * depth + [pltpu.VMEM((win, LANES), F32)] * depth
                      + [pltpu.SemaphoreType.DMA] * (depth + 2))
    def gather(table_hbm, idx_hbm, out_hbm, *scratch):
        idx_v, rows_v = scratch[:depth], scratch[depth:2 * depth]
        gather_sems, idx_sem, out_sem = scratch[2 * depth:3 * depth], scratch[-2], scratch[-1]
        wid = lax.axis_index("s") * n_cores + lax.axis_index("c")

        @pl.loop(0, wins_per_worker, step=depth)
        def _(w):
            offs = [(wid * wins_per_worker + w + u) * win for u in range(depth)]
            loads = [pltpu.async_copy(idx_hbm.at[pl.ds(offs[u], win)], idx_v[u], idx_sem) for u in range(depth)]
            for cp in loads:
                cp.wait()
            gathers = [pltpu.async_copy(table_hbm.at[idx_v[u]], rows_v[u], gather_sems[u]) for u in range(depth)]
            stores = []
            for u in range(depth):
                gathers[u].wait()
                stores.append(pltpu.async_copy(rows_v[u], out_hbm.at[pl.ds(offs[u], win)], out_sem))
            for cp in stores:
                cp.wait()

    return gather(table, pos)


def _partial_cols(pv, m):
    n = pv.shape[1]
    stacked = jnp.concatenate([pv, jnp.zeros((LANES - MOBA_VROWS, n), F32)], axis=0)
    row = lax.broadcasted_iota(jnp.int32, (LANES, n), 0)
    return jnp.where(row == HEAD_DIM + 1, m, stacked)


def _partial_rows(pv, m):
    return _partial_cols(pv, m).T


def _moba_sparse_kernel(blk_ref, q_ref, k_ref, v_ref, _after_ref, o_ref, *, n_groups):
    h, g = pl.program_id(0), pl.program_id(1)
    first = (h * n_groups + g) * MOBA_GROUP

    @pl.when(blk_ref[first] >= 0)
    def _():
        work = []
        for u in range(MOBA_GROUP):
            blk = jnp.maximum(blk_ref[first + u], 0)
            rows = pl.ds(pl.multiple_of(blk * MOBA_BLOCK, MOBA_BLOCK), MOBA_BLOCK)
            q = q_ref[u * MOBA_PAIR_TILE:(u + 1) * MOBA_PAIR_TILE, :].astype(BF16)
            work.append((blk, _dot_nt(k_ref[0, rows, :], q)))
        for u, (blk, s) in enumerate(work):
            m = jnp.max(s, axis=0, keepdims=True)
            pv = _dot(v_ref[0, blk], jnp.exp(s - m).astype(BF16))
            o_ref[u * MOBA_PAIR_TILE:(u + 1) * MOBA_PAIR_TILE, :] = _partial_rows(pv, m)

    @pl.when(g == n_groups - 1)
    def _():
        lane = lax.broadcasted_iota(jnp.int32, (MOBA_PAIR_TILE, LANES), 1)
        o_ref[(MOBA_GROUP - 1) * MOBA_PAIR_TILE:, :] = jnp.where(lane == HEAD_DIM + 1, NEG, 0.0)


def _moba_sparse(q_sorted, k_pad, vt_aug, tile_blk, n_tiles, after):
    n_heads, s, _ = k_pad.shape
    n_groups = n_tiles // MOBA_GROUP
    rows = pl.BlockSpec((MOBA_GROUP * MOBA_PAIR_TILE, LANES), lambda h, g, blk: (h * n_groups + g, 0))
    grid_spec = pltpu.PrefetchScalarGridSpec(
        num_scalar_prefetch=1, grid=(n_heads, n_groups),
        in_specs=[rows, pl.BlockSpec((1, s, LANES), lambda h, g, blk: (h, 0, 0)),
                  pl.BlockSpec((1, s // MOBA_BLOCK, MOBA_VROWS, MOBA_BLOCK), lambda h, g, blk: (h, 0, 0, 0)),
                  _ORDER_ONLY],
        out_specs=rows)
    return pl.pallas_call(
        functools.partial(_moba_sparse_kernel, n_groups=n_groups),
        grid_spec=grid_spec,
        out_shape=jax.ShapeDtypeStruct(q_sorted.shape, F32),
        compiler_params=_cparams("arbitrary", "arbitrary"),
        name="moba_sparse",
    )(tile_blk, q_sorted, k_pad, vt_aug, after)


def _moba_merge_kernel(qa_ref, qb_ref, ka_ref, kb_ref, va_ref, vb_ref, ga_ref, gb_ref, _after_ref, o_ref):
    ki = lax.broadcasted_iota(jnp.int32, (MOBA_BLOCK, MOBA_BLOCK), 0)
    qi = lax.broadcasted_iota(jnp.int32, (MOBA_BLOCK, MOBA_BLOCK), 1)
    heads = ((qa_ref, ka_ref, va_ref, ga_ref), (qb_ref, kb_ref, vb_ref, gb_ref))
    chains = [(hh, b) for b in range(MOBA_MERGE_BLOCKS) for hh in range(2)]
    rows = lambda b: slice(b * MOBA_BLOCK, (b + 1) * MOBA_BLOCK)
    scores = [_dot_nt(heads[hh][1][0, rows(b), :], heads[hh][0][0, rows(b), :].astype(BF16))
              for hh, b in chains]
    outs = {}
    for (hh, b), s in zip(chains, scores):
        _, _, v_ref, g_ref = heads[hh]
        s = jnp.where(ki <= qi, s, NEG)
        m = jnp.max(s, axis=0, keepdims=True)
        parts = [_partial_cols(_dot(v_ref[0, b], jnp.exp(s - m).astype(BF16)), m)]
        parts += [g_ref[0, r, rows(b), :].T for r in range(MOBA_TOPK)]
        ms = [x[HEAD_DIM + 1:HEAD_DIM + 2, :] for x in parts]
        m_all = functools.reduce(jnp.maximum, ms)
        acc = sum(jnp.exp(mi - m_all) * x for mi, x in zip(ms, parts))
        outs[hh, b] = acc[:HEAD_DIM, :] / acc[HEAD_DIM:HEAD_DIM + 1, :]
    for b in range(MOBA_MERGE_BLOCKS):
        o_ref[rows(b), :] = jnp.concatenate([outs[0, b], outs[1, b]], axis=0).T.astype(BF16)


def _moba_merge(q_rows, k_pad, vt_aug, gathered, after):
    n_heads, s, _ = q_rows.shape
    t = MOBA_MERGE_BLOCKS * MOBA_BLOCK
    blk = lambda par: pl.BlockSpec((1, t, LANES), lambda p, i: (2 * p + par, i, 0))
    vspec = lambda par: pl.BlockSpec((1, MOBA_MERGE_BLOCKS, MOBA_VROWS, MOBA_BLOCK),
                                     lambda p, i: (2 * p + par, i, 0, 0))
    gspec = lambda par: pl.BlockSpec((1, MOBA_TOPK, t, LANES), lambda p, i: (2 * p + par, 0, i, 0))
    return pl.pallas_call(
        _moba_merge_kernel,
        grid=(n_heads // 2, s // t),
        in_specs=[blk(0), blk(1), blk(0), blk(1), vspec(0), vspec(1), gspec(0), gspec(1), _ORDER_ONLY],
        out_specs=pl.BlockSpec((t, LANES), lambda p, i: (i, p)),
        out_shape=jax.ShapeDtypeStruct((s, WIDTH), BF16),
        compiler_params=_cparams("arbitrary", "arbitrary"),
        name="moba_merge",
    )(q_rows, q_rows, k_pad, k_pad, vt_aug, vt_aug, gathered, gathered, after)


def _moba_attention(qc, kc, qkv_c, kmean, overlap_a, overlap_b):
    s = qc.shape[0]
    n_tiles = _moba_tiles_per_head(s)
    q_rows, k_pad, vt_aug, sel, rank, counts_raw = _moba_route(qc, kc, qkv_c, _moba_mean_rows(kmean))
    counts = counts_raw[:, :s // MOBA_BLOCK, 0].astype(jnp.int32)
    pos, tile_blk = _moba_dispatch(sel, rank, counts, n_tiles)
    q_sorted = _sc_scatter_rows(q_rows.reshape(N_HEADS * s, LANES), pos, N_HEADS * n_tiles * MOBA_PAIR_TILE)
    def chain(calls, anchor):
        outs = []
        for call in calls:
            outs.append(call(anchor))
            anchor = outs[-1]
        return outs

    out_a = chain(overlap_a, counts_raw)
    partial = _moba_sparse(q_sorted, k_pad, vt_aug, tile_blk, n_tiles, out_a[-1])
    gathered = _sc_gather_rows(partial, pos).reshape(N_HEADS, MOBA_TOPK, s, LANES)
    out_b = chain(overlap_b, partial)
    return out_a, out_b, _moba_merge(q_rows, k_pad, vt_aug, gathered, out_b[-1])


def _ssd_kernel(z_ref, xbc_ref, dt_ref, cw_ref, cb_ref, dtb_ref, alog_ref, dsk_ref, nw_ref, exp_ref, tri_ref,
                _after_ref, o_ref, state_ref, halo_ref):
    L = SSM_CHUNK

    @pl.when(pl.program_id(0) == 0)
    def _():
        state_ref[...] = jnp.zeros_like(state_ref)
        halo_ref[...] = jnp.zeros_like(halo_ref)

    cur = xbc_ref[...]
    prev8 = halo_ref[...]
    row8 = lax.broadcasted_iota(jnp.int32, (8, SSM_CONV_DIM), 0)
    conv = cb_ref[...] + cur * cw_ref[SSM_CONV - 1:SSM_CONV, :]
    for k in range(1, SSM_CONV):
        rolled = pltpu.roll(cur, k, 0)
        head = jnp.where(row8 < k, pltpu.roll(prev8, k, 0), rolled[0:8, :])
        shifted = jnp.concatenate([head, rolled[8:, :]], axis=0)
        conv = conv + shifted * cw_ref[SSM_CONV - 1 - k:SSM_CONV - k, :]
    halo_ref[...] = cur[L - 8:, :]
    act = _silu(conv)
    xs, bm, cm = act[:, :WIDTH], act[:, WIDTH:WIDTH + 2 * SSM_STATE], act[:, WIDTH + 2 * SSM_STATE:]

    dt = _softplus(dt_ref[...] + dtb_ref[...])
    da = dt * (-jnp.exp(alog_ref[...]))
    cs = _dot_exact_lhs(tri_ref[...], da)
    expand = exp_ref[...]
    cs_b = _dot_exact_rhs(cs, expand)
    dt_b = _dot_exact_rhs(dt, expand)
    cs_last = cs_b[L - 1:L, :]
    xdt = xs * dt_b
    cs_t = cs.T

    ri = lax.broadcasted_iota(jnp.int32, (L, L), 0)
    ci = lax.broadcasted_iota(jnp.int32, (L, L), 1)
    causal = ci <= ri
    lane = lax.broadcasted_iota(jnp.int32, (L, LANES), 1)
    bmb, cmb = bm.astype(BF16), cm.astype(BF16)
    xdtb = xdt.astype(BF16)
    y_pairs = []
    for p in range(N_HEADS // 2):
        g = (2 * p) // (N_HEADS // SSM_GROUPS)
        grp = slice(g * SSM_STATE, (g + 1) * SSM_STATE)
        cbm = _dot_nt(cmb[:, grp], bmb[:, grp])
        x_pair = xdtb[:, p * LANES:(p + 1) * LANES]
        halves = []
        for h in (2 * p, 2 * p + 1):
            seg = cs[:, h:h + 1] - cs_t[h:h + 1, :]
            decay = jnp.exp(jnp.where(causal, seg, NEG))
            halves.append(_dot((cbm * decay).astype(BF16), x_pair))
        y_pairs.append(jnp.where(lane < HEAD_DIM, halves[0], halves[1]))
    y_diag = jnp.concatenate(y_pairs, axis=1)

    state = state_ref[...]
    stb = state.astype(BF16)
    half = WIDTH // SSM_GROUPS
    y_off = jnp.concatenate(
        [_dot(cmb[:, g * SSM_STATE:(g + 1) * SSM_STATE], stb[:, g * half:(g + 1) * half])
         for g in range(SSM_GROUPS)], axis=1) * jnp.exp(cs_b)
    y = y_diag + y_off + xs * dsk_ref[...]

    w = (xdt * jnp.exp(cs_last - cs_b)).astype(BF16)
    bm_t = bm.T.astype(BF16)
    new = jnp.concatenate(
        [_dot(bm_t[g * SSM_STATE:(g + 1) * SSM_STATE, :], w[:, g * half:(g + 1) * half])
         for g in range(SSM_GROUPS)], axis=1)
    state_ref[...] = state * jnp.exp(cs_last) + new

    gt = y * _silu(z_ref[...])
    outs = []
    for g in range(SSM_GROUPS):
        gg = gt[:, g * half:(g + 1) * half]
        outs.append(gg * lax.rsqrt(jnp.mean(gg * gg, axis=-1, keepdims=True) + NORM_EPS))
    o_ref[...] = (jnp.concatenate(outs, axis=1) * nw_ref[...]).astype(BF16)


def _ssd_mixer(z, xbc, dt_raw, conv_w, conv_b, dt_bias, a_log, d_skip, norm_w, expand, tri, after):
    s = z.shape[0]
    L = SSM_CHUNK
    row = lambda n: pl.BlockSpec((L, n), lambda c: (c, 0))
    cst = lambda r, n: pl.BlockSpec((r, n), lambda c: (0, 0))
    return pl.pallas_call(
        _ssd_kernel,
        grid=(s // L,),
        in_specs=[row(WIDTH), row(SSM_CONV_DIM), row(DT_PAD), cst(SSM_CONV, SSM_CONV_DIM),
                  cst(1, SSM_CONV_DIM), cst(1, DT_PAD), cst(1, DT_PAD), cst(1, WIDTH), cst(1, WIDTH),
                  cst(DT_PAD, WIDTH), cst(L, L), _ORDER_ONLY],
        out_specs=row(WIDTH),
        out_shape=jax.ShapeDtypeStruct((s, WIDTH), BF16),
        scratch_shapes=[pltpu.VMEM((SSM_STATE, WIDTH), F32), pltpu.VMEM((8, SSM_CONV_DIM), F32)],
        compiler_params=_cparams("arbitrary"),
        cost_estimate=pl.CostEstimate(
            flops=2 * s * L * (SSM_GROUPS * SSM_STATE + WIDTH + 2 * N_HEADS * LANES) + 4 * s * SSM_STATE * WIDTH,
            transcendentals=s * (L * N_HEADS + SSM_CONV_DIM + 3 * WIDTH),
            bytes_accessed=s * (WIDTH + SSM_CONV_DIM + DT_PAD) * 4 + s * WIDTH * 2),
        name="ssd_mixer",
    )(z, xbc, dt_raw, conv_w, conv_b, dt_bias, a_log, d_skip, norm_w, expand, tri, after)


def _outproj_kernel(x_ref, *refs):
    *y_refs, w_ref, gate_ref, o_ref = refs
    y, row0 = 0.0, 0
    for y_ref in y_refs:
        y = y + _dot(y_ref[...], w_ref[row0:row0 + y_ref.shape[1], :])
        row0 += y_ref.shape[1]
    o_ref[...] = x_ref[...] + gate_ref[...] * y


def _out_projection(x, ys, w_all, gate, layer):
    s, d = x.shape
    tm = 512
    row = lambda n: pl.BlockSpec((tm, n), lambda i: (i, 0))
    assert sum(y.shape[1] for y in ys) == w_all.shape[1]
    return pl.pallas_call(
        _outproj_kernel,
        grid=(s // tm,),
        in_specs=[row(d)] + [row(y.shape[1]) for y in ys]
                 + [pl.BlockSpec((None, w_all.shape[1], d), lambda i: (layer, 0, 0)),
                    pl.BlockSpec((1, d), lambda i: (0, 0))],
        out_specs=row(d),
        out_shape=jax.ShapeDtypeStruct((s, d), F32),
        compiler_params=_cparams("arbitrary"),
        name="out_projection",
    )(x, *ys, w_all, gate)


def _ffn_kernel(x_ref, g_ref, sc_ref, sh_ref, gate_ref, wg_ref, wu_ref, wd_ref, o_ref, *, chunk):
    x = x_ref[...]
    h = _mod_norm(x, g_ref[...], sc_ref[...], sh_ref[...]).astype(BF16)
    hidden = wg_ref.shape[1]
    acc = jnp.zeros(x.shape, F32)
    for c in range(hidden // chunk):
        cols = slice(c * chunk, (c + 1) * chunk)
        a = _silu(_dot(h, wg_ref[:, cols])) * _dot(h, wu_ref[:, cols])
        acc = acc + _dot(a.astype(BF16), wd_ref[cols, :])
    o_ref[...] = x + gate_ref[...] * acc


def _ffn(x, g, sc, sh, gate, w_gate, w_up, w_down, layer):
    s, d = x.shape
    hidden = w_gate.shape[2]
    tm = 512
    row = pl.BlockSpec((tm, d), lambda i: (i, 0))
    vec = pl.BlockSpec((1, d), lambda i: (0, 0))
    return pl.pallas_call(
        functools.partial(_ffn_kernel, chunk=256),
        grid=(s // tm,),
        in_specs=[row, vec, vec, vec, vec,
                  _resident((None, d, hidden), lambda i: (layer, 0, 0)),
                  _resident((None, d, hidden), lambda i: (layer, 0, 0)),
                  _resident((None, hidden, d), lambda i: (layer, 0, 0))],
        out_specs=row,
        out_shape=jax.ShapeDtypeStruct((s, d), F32),
        compiler_params=_cparams("arbitrary"),
        name="ffn",
    )(x, g, sc, sh, gate, w_gate, w_up, w_down)


def _rope_tables(seq):
    pos = jnp.arange(seq, dtype=F32)
    inv = ROPE_THETA ** (-jnp.arange(0, ROPE_DIM, 2, dtype=F32) / ROPE_DIM)
    ang = pos[:, None] * inv[None, :]
    cos, sin = jnp.cos(ang), jnp.sin(ang)
    ones = jnp.ones((seq, HEAD_DIM - ROPE_DIM), F32)
    zeros = jnp.zeros((seq, HEAD_DIM - ROPE_DIM), F32)
    zh = jnp.zeros((seq, ROPE_HALF), F32)
    cos_h = jnp.concatenate([cos, cos, ones], axis=1)
    sin_lo = jnp.concatenate([-sin, zh, zeros], axis=1)
    sin_hi = jnp.concatenate([zh, sin, zeros], axis=1)
    two = lambda t: jnp.concatenate([t, t], axis=1)
    return two(cos_h), two(sin_lo), two(sin_hi)


def _moba_mean_rows(kmean):
    nb = kmean.shape[0]
    km = kmean.reshape(nb, N_HEADS, HEAD_DIM).transpose(1, 0, 2)
    return jnp.pad(km, ((0, 0), (0, MOBA_MAX_BLOCKS - nb), (0, LANES - HEAD_DIM)))


def kernel(x, c, ada_w, ada_b, norm_mix, w_in, qn_swa, kn_swa, qn_moba, kn_moba, conv_w, conv_b, dt_bias,
           a_log, d_skip, ssm_norm, w_out, norm_ffn, w_gate, w_up, w_down):
    batch, seq, d = x.shape
    depth = ada_w.shape[0]
    assert batch == 1 and d == D_MODEL
    assert seq % SWA_SPAN == 0 and seq // MOBA_BLOCK <= MOBA_MAX_BLOCKS
    xs = x.reshape(seq, d)

    mod = _ada_modulation(c, ada_w, ada_b)
    rope_tabs = _rope_tables(seq)
    lane_head = jnp.arange(LANES) // HEAD_DIM
    hsum = (lane_head[:, None] == lane_head[None, :]).astype(BF16)
    expand = (jnp.arange(DT_PAD)[:, None] == (jnp.arange(WIDTH) // HEAD_DIM)[None, :]).astype(BF16)
    tri = (jnp.arange(SSM_CHUNK)[:, None] >= jnp.arange(SSM_CHUNK)[None, :]).astype(BF16)
    pad8 = lambda v: jnp.pad(v, (0, DT_PAD - N_HEADS)).reshape(1, DT_PAD)
    two = lambda g: jnp.concatenate([g, g]).reshape(1, LANES)

    w_in_b = jnp.pad(w_in, ((0, 0), (0, 0), (0, IN_PROJ_PAD - w_in.shape[2]))).astype(BF16)
    w_out_b, w_gate_b, w_up_b, w_down_b = (w.astype(BF16) for w in (w_out, w_gate, w_up, w_down))

    for l in range(depth):
        shift_m, scale_m, gate_m, shift_f, scale_f, gate_f = [mod[l, :, i * d:(i + 1) * d] for i in range(6)]
        qkv_a, qkv_c, z, xbc, dt_raw = _in_projection(xs, norm_mix[l].reshape(1, d), scale_m, shift_m, w_in_b, l)

        gains = [two(g[l]) for g in (qn_swa, kn_swa, qn_moba, kn_moba)]
        qa, ka, qc, kc, kmean = _prep_qkv(qkv_a, qkv_c, gains, hsum, rope_tabs)

        half = WIDTH // LANES // 2
        dilated_lo = functools.partial(_dilated_attention, qa, ka, qkv_a, 0, half)
        dilated_hi = functools.partial(_dilated_attention, qa, ka, qkv_a, half, half)
        ssd = functools.partial(
            _ssd_mixer, z, xbc, dt_raw, conv_w[l], conv_b[l].reshape(1, -1), pad8(dt_bias[l]), pad8(a_log[l]),
            jnp.repeat(d_skip[l], HEAD_DIM).reshape(1, WIDTH), ssm_norm[l].reshape(1, WIDTH), expand, tri)
        (y_a_lo,), (y_a_hi, y_b), y_c = _moba_attention(qc, kc, qkv_c, kmean, (dilated_lo,), (dilated_hi, ssd))

        xs = _out_projection(xs, (y_a_lo, y_a_hi, y_b, y_c), w_out_b, gate_m, l)
        xs = _ffn(xs, norm_ffn[l].reshape(1, d), scale_f, shift_f, gate_f,
                  w_gate_b, w_up_b, w_down_b, l)
    return xs.reshape(batch, seq, d)
```

```python
import functools

import jax
import jax.numpy as jnp
from jax import lax
from jax.experimental import pallas as pl
from jax.experimental.pallas import tpu as pltpu
from jax.experimental.pallas import tpu_sc as plsc

F32 = jnp.float32
BF16 = jnp.bfloat16

D_MODEL = 1024
HEAD_DIM = 64
N_HEADS = 8
WIDTH = N_HEADS * HEAD_DIM
ATTN_SCALE = HEAD_DIM ** -0.5
NORM_EPS = 1e-6
NEG = -1e30

ROPE_THETA = 500000.0
ROPE_DIM = HEAD_DIM // 4
ROPE_HALF = ROPE_DIM // 2

DILATIONS = (1, 4, 16)
SWA_BLOCK = 128
SWA_SPAN = DILATIONS[-1] * SWA_BLOCK
SWA_GROUP = 4

SSM_STATE = 128
SSM_GROUPS = 2
SSM_CONV = 4
SSM_CHUNK = 256
SSM_CONV_DIM = WIDTH + 2 * SSM_GROUPS * SSM_STATE

MOBA_BLOCK = 256
MOBA_TOPK = 3
MOBA_PAIR_TILE = 128
MOBA_GROUP = 32
SC_INDEX_WINDOW = 128
MOBA_MAX_BLOCKS = 64
MOBA_MERGE_BLOCKS = 4
MOBA_VROWS = HEAD_DIM + 16

LANES = 128
DT_PAD = LANES
IN_PROJ_PAD = 6 * WIDTH + WIDTH + SSM_CONV_DIM + DT_PAD
OFF_A, OFF_C, OFF_Z, OFF_XBC, OFF_DT = 0, 3 * WIDTH, 6 * WIDTH, 7 * WIDTH, 7 * WIDTH + SSM_CONV_DIM

VMEM_LIMIT = 56 * 1024 * 1024


_ORDER_ONLY = pl.BlockSpec(memory_space=pl.ANY)


def _resident(block_shape, index_map):
    return pl.BlockSpec(block_shape, index_map, pipeline_mode=pl.Buffered(1))


def _cparams(*sem):
    return pltpu.CompilerParams(dimension_semantics=sem, vmem_limit_bytes=VMEM_LIMIT)


def _split3(a):
    hi = a.astype(BF16)
    r1 = a - hi.astype(F32)
    mid = r1.astype(BF16)
    lo = (r1 - mid.astype(F32)).astype(BF16)
    return hi, mid, lo


def _dot(a, b):
    return jnp.dot(a, b, preferred_element_type=F32)


def _dot_nt(a, b):
    return lax.dot_general(a, b, (((1,), (1,)), ((), ())), preferred_element_type=F32)


def _dot_exact_rhs(a, b_bf16):
    hi, mid, lo = _split3(a)
    return _dot(hi, b_bf16) + _dot(mid, b_bf16) + _dot(lo, b_bf16)


def _dot_exact_lhs(a_bf16, b):
    hi, mid, lo = _split3(b)
    return _dot(a_bf16, hi) + _dot(a_bf16, mid) + _dot(a_bf16, lo)


def _dot_f32(a, b, dot=_dot):
    ah, am, al = _split3(a)
    bh, bm, bl = _split3(b)
    return (dot(ah, bh) + (dot(ah, bm) + dot(am, bh))
            + (dot(ah, bl) + dot(am, bm) + dot(al, bh)))


def _dot_bf16x3(a, b, dot=_dot):
    ah = a.astype(BF16)
    al = (a - ah.astype(F32)).astype(BF16)
    bh = b.astype(BF16)
    bl = (b - bh.astype(F32)).astype(BF16)
    return dot(ah, bh) + (dot(ah, bl) + dot(al, bh))


def _silu(x):
    return x / (1.0 + jnp.exp(-x))


def _softplus(x):
    return jnp.maximum(x, 0.0) + jnp.log1p(jnp.exp(-jnp.abs(x)))


def _ada_kernel(c_ref, w_ref, b_ref, o_ref):
    c = c_ref[...]
    o_ref[0] = _dot_f32(_silu(c), w_ref[0]) + b_ref[0]


def _ada_modulation(c, ada_w, ada_b):
    depth, d, n = ada_w.shape
    tn = 1024
    c8 = jnp.broadcast_to(c, (8, d))
    out = pl.pallas_call(
        _ada_kernel,
        grid=(depth, n // tn),
        in_specs=[pl.BlockSpec((8, d), lambda l, j: (0, 0)),
                  pl.BlockSpec((1, d, tn), lambda l, j: (l, 0, j)),
                  pl.BlockSpec((1, 1, tn), lambda l, j: (l, 0, j))],
        out_specs=pl.BlockSpec((1, 8, tn), lambda l, j: (l, 0, j)),
        out_shape=jax.ShapeDtypeStruct((depth, 8, n), F32),
        compiler_params=_cparams("arbitrary", "arbitrary"),
        name="ada_modulation",
    )(c8, ada_w, ada_b.reshape(depth, 1, n))
    return out[:, 0:1, :]


def _mod_norm(x, g, sc, sh):
    var = jnp.mean(x * x, axis=-1, keepdims=True)
    return (x * lax.rsqrt(var + NORM_EPS) * g) * (1.0 + sc) + sh


def _inproj_kernel(x_ref, g_ref, sc_ref, sh_ref, w_ref, gqa_ref, gka_ref, gqc_ref, gkc_ref, hsum_ref,
                   cos_ref, slo_ref, shi_ref,
                   qa_ref, ka_ref, va_ref, qc_ref, kc_ref, vc_ref, z_ref, xbc_ref, dt_ref, km_ref):
    h = _mod_norm(x_ref[...], g_ref[...], sc_ref[...], sh_ref[...]).astype(BF16)
    proj = lambda off, n=WIDTH: _dot(h, w_ref[:, off:off + n])
    raw = [proj(OFF_A), proj(OFF_A + WIDTH), proj(OFF_C), proj(OFF_C + WIDTH)]
    va_ref[...] = proj(OFF_A + 2 * WIDTH)
    vc_ref[...] = proj(OFF_C + 2 * WIDTH)
    z_ref[...] = proj(OFF_Z)
    xbc_ref[...] = proj(OFF_XBC, SSM_CONV_DIM)
    dt_ref[...] = proj(OFF_DT, DT_PAD)

    hsum = hsum_ref[...]
    cos_t, slo, shi = cos_ref[...], slo_ref[...], shi_ref[...]
    rope = lambda x, gain_ref: _head_norm_rope(x, gain_ref[...], hsum, cos_t, slo, shi)
    n_blocks = x_ref.shape[0] // MOBA_BLOCK
    for p in range(WIDTH // LANES):
        lo, hi = p * LANES, (p + 1) * LANES
        qa_ref[:, lo:hi] = rope(raw[0][:, lo:hi], gqa_ref) * ATTN_SCALE
        ka_ref[:, lo:hi] = rope(raw[1][:, lo:hi], gka_ref)
        qc_ref[:, lo:hi] = rope(raw[2][:, lo:hi], gqc_ref)
        kc = rope(raw[3][:, lo:hi], gkc_ref)
        kc_ref[:, lo:hi] = kc
        for b in range(n_blocks):
            km_ref[b, :, lo:hi] = jnp.mean(kc[b * MOBA_BLOCK:(b + 1) * MOBA_BLOCK, :], axis=0, keepdims=True)


def _in_projection(x, g, sc, sh, w_all, layer, gains, hsum, rope_tabs):
    s, d = x.shape
    tm = 512
    assert tm % MOBA_BLOCK == 0
    row = lambda n: pl.BlockSpec((tm, n), lambda i: (i, 0))
    vec = pl.BlockSpec((1, d), lambda i: (0, 0))
    cst = lambda r, n: pl.BlockSpec((r, n), lambda i: (0, 0))
    widths = [WIDTH] * 7 + [SSM_CONV_DIM, DT_PAD]
    return pl.pallas_call(
        _inproj_kernel,
        grid=(s // tm,),
        in_specs=[row(d), vec, vec, vec, _resident((None, d, IN_PROJ_PAD), lambda i: (layer, 0, 0))]
                 + [cst(1, LANES)] * 4 + [cst(LANES, LANES)] + [row(LANES)] * 3,
        out_specs=[row(n) for n in widths]
                  + [pl.BlockSpec((tm // MOBA_BLOCK, 1, WIDTH), lambda i: (i, 0, 0))],
        out_shape=[jax.ShapeDtypeStruct((s, n), F32) for n in widths]
                  + [jax.ShapeDtypeStruct((s // MOBA_BLOCK, 1, WIDTH), F32)],
        compiler_params=_cparams("arbitrary"),
        name="in_projection",
    )(x, g, sc, sh, w_all, *gains, hsum, *rope_tabs)


def _head_norm_rope(x, gain, hsum, cos_t, sin_lo, sin_hi):
    sq = x * x
    sq_hi = sq.astype(BF16)
    ss = _dot(sq_hi, hsum) + _dot((sq - sq_hi.astype(F32)).astype(BF16), hsum)
    y = x * lax.rsqrt(ss * (1.0 / HEAD_DIM) + NORM_EPS) * gain
    return (y * cos_t + pltpu.roll(y, ROPE_HALF, 1) * sin_hi
            + pltpu.roll(y, LANES - ROPE_HALF, 1) * sin_lo)


def _dilated_kernel(q_ref, kc_ref, kp_ref, vc_ref, vp_ref, _after_ref, o_ref, m_s, l_s, a_s, q4_s, k4_s, v4_s):
    has_prev = pl.program_id(1) > 0
    lane = lax.broadcasted_iota(jnp.int32, (SWA_BLOCK, LANES), 1)
    head_a = lane < HEAD_DIM
    qi = lax.broadcasted_iota(jnp.int32, (SWA_BLOCK, 2 * SWA_BLOCK), 0)
    ki = lax.broadcasted_iota(jnp.int32, (SWA_BLOCK, 2 * SWA_BLOCK), 1)
    band = (ki >= qi) & (ki <= qi + SWA_BLOCK)
    band_edge = band & ((ki >= SWA_BLOCK) | has_prev)

    quarter = SWA_SPAN // 4
    for r4 in range(4):
        src = pl.ds(r4, quarter, 4)
        q4_s[r4 * quarter:(r4 + 1) * quarter, :] = q_ref[src, :]
        for dst, prev, cur in ((k4_s, kp_ref, kc_ref), (v4_s, vp_ref, vc_ref)):
            dst[2 * r4 * quarter:(2 * r4 + 1) * quarter, :] = prev[src, :]
            dst[(2 * r4 + 1) * quarter:(2 * r4 + 2) * quarter, :] = cur[src, :]

    def unit_d16(a, r4):
        rows = pl.ds(4 * a + r4, SWA_BLOCK, 16)
        q = q4_s[pl.ds(r4 * quarter + a, SWA_BLOCK, 4), :]
        prev = pl.ds(2 * r4 * quarter + a, SWA_BLOCK, 4)
        cur = pl.ds((2 * r4 + 1) * quarter + a, SWA_BLOCK, 4)
        k_cat = jnp.concatenate([k4_s[prev, :], k4_s[cur, :]], axis=0)
        v_cat = jnp.concatenate([v4_s[prev, :], v4_s[cur, :]], axis=0)
        return rows, q, k_cat, v_cat, band_edge

    def unit_d4(r, nb):
        rows = pl.ds(4 * SWA_BLOCK * nb + r, SWA_BLOCK, 4)
        q = q4_s[pl.ds(r * quarter + SWA_BLOCK * nb, SWA_BLOCK), :]
        keys = pl.ds((2 * r + 1) * quarter + SWA_BLOCK * (nb - 1), 2 * SWA_BLOCK)
        return rows, q, k4_s[keys, :], v4_s[keys, :], (band_edge if nb == 0 else band)

    def unit_d1(b, edge=False):
        rows = pl.ds(SWA_BLOCK * b, SWA_BLOCK)
        if edge:
            prev = pl.ds(SWA_SPAN - SWA_BLOCK, SWA_BLOCK)
            k_cat = jnp.concatenate([kp_ref[prev, :], kc_ref[rows, :]], axis=0)
            v_cat = jnp.concatenate([vp_ref[prev, :], vc_ref[rows, :]], axis=0)
            return rows, q_ref[rows, :], k_cat, v_cat, band_edge
        keys = pl.ds(SWA_BLOCK * (b - 1), 2 * SWA_BLOCK)
        return rows, q_ref[rows, :], kc_ref[keys, :], vc_ref[keys, :], band

    def attend(units, mode):
        scores = []
        for _, q, k_cat, _, mask in units:
            kb = k_cat.astype(BF16)
            scores.append([jnp.where(mask, _dot_nt(jnp.where(sel, q, 0.0).astype(BF16), kb), NEG)
                           for sel in (head_a, ~head_a)])
        probs = []
        for pair in scores:
            stats = []
            for s in pair:
                m_loc = jnp.max(s, axis=-1, keepdims=True)
                p = jnp.exp(s - m_loc)
                stats.append((m_loc, jnp.sum(p, axis=-1, keepdims=True), p.astype(BF16)))
            probs.append(stats)
        for (rows, _, _, v_cat, _), ((m_a, s_a, p_a), (m_b, s_b, p_b)) in zip(units, probs):
            vb = v_cat.astype(BF16)
            m_new = jnp.where(head_a, m_a, m_b)
            l_new = jnp.where(head_a, s_a, s_b)
            a_new = jnp.where(head_a, _dot(p_a, vb), _dot(p_b, vb))
            if mode != "init":
                m_in, m_loc = m_s[rows, :], m_new
                m_new = jnp.maximum(m_in, m_loc)
                alpha, beta = jnp.exp(m_in - m_new), jnp.exp(m_loc - m_new)
                l_new = alpha * l_s[rows, :] + beta * l_new
                a_new = alpha * a_s[rows, :] + beta * a_new
            if mode == "final":
                o_ref[rows, :] = (a_new / l_new).astype(BF16)
            else:
                m_s[rows, :] = m_new
                l_s[rows, :] = l_new
                a_s[rows, :] = a_new

    g = SWA_GROUP
    assert DILATIONS == (1, 4, 16) and g == 4 and SWA_SPAN // (4 * SWA_BLOCK) == g

    def body16(a, carry):
        attend([unit_d16(a, r4) for r4 in range(g)], "init")
        return carry
    lax.fori_loop(0, 16 // g, body16, 0)

    def body4(r, carry):
        attend([unit_d4(r, nb) for nb in range(g)], "fold")
        return carry
    lax.fori_loop(0, 4, body4, 0)

    attend([unit_d1(b, edge=(b == 0)) for b in range(g)], "final")

    def body1(i, carry):
        attend([unit_d1(i * g + j) for j in range(g)], "final")
        return carry
    lax.fori_loop(1, SWA_SPAN // SWA_BLOCK // g, body1, 0)


def _dilated_attention(q, k, v, pair0, n_pairs, after):
    s = q.shape[0]
    cur = lambda off: pl.BlockSpec((SWA_SPAN, LANES), lambda p, n: (n, off + p))
    prev = lambda off: pl.BlockSpec((SWA_SPAN, LANES), lambda p, n: (jnp.maximum(n - 1, 0), off + p))
    width = n_pairs * LANES
    return pl.pallas_call(
        _dilated_kernel,
        grid=(n_pairs, s // SWA_SPAN),
        in_specs=[cur(pair0), cur(pair0), prev(pair0), cur(pair0), prev(pair0), _ORDER_ONLY],
        out_specs=cur(0),
        out_shape=jax.ShapeDtypeStruct((s, width), BF16),
        scratch_shapes=[pltpu.VMEM((SWA_SPAN, LANES), F32)] * 4 + [pltpu.VMEM((2 * SWA_SPAN, LANES), F32)] * 2,
        compiler_params=_cparams("arbitrary", "arbitrary"),
        cost_estimate=pl.CostEstimate(
            flops=len(DILATIONS) * 4 * s * 2 * SWA_BLOCK * width,
            transcendentals=len(DILATIONS) * s * 2 * SWA_BLOCK * 2 * n_pairs,
            bytes_accessed=5 * s * width * 4 + s * width * 2),
        name="dilated_attention",
    )(q, k, k, v, v, after)


def _moba_route_kernel(q_ref, k_ref, v_ref, km_ref, sut_ref, ones_ref,
                       qrow_ref, kpad_ref, vt_ref, sel_ref, rank_ref, cnt_ref, base_ref):
    i = pl.program_id(0)

    @pl.when(i == 0)
    def _():
        base_ref[...] = jnp.zeros_like(base_ref)

    lane = lax.broadcasted_iota(jnp.int32, (MOBA_BLOCK, LANES), 1)
    is_feat = lane < HEAD_DIM
    row = lax.broadcasted_iota(jnp.int32, (MOBA_MAX_BLOCKS, MOBA_BLOCK), 0)
    past = row < i
    ones_rows = jnp.where(
        lax.broadcasted_iota(jnp.int32, (MOBA_VROWS - HEAD_DIM, MOBA_BLOCK), 0) == 0, 1.0, 0.0)
    ninf = float("-inf")
    gates = []
    for h in range(N_HEADS):
        cols = slice((h // 2) * LANES, (h // 2 + 1) * LANES)
        q, k = q_ref[:, cols], k_ref[:, cols]
        v_t = v_ref[:, cols].T
        if h % 2:
            q, k = pltpu.roll(q, HEAD_DIM, 1), pltpu.roll(k, HEAD_DIM, 1)
            v_t = v_t[HEAD_DIM:, :]
        else:
            v_t = v_t[:HEAD_DIM, :]
        qrow_ref[h] = jnp.where(is_feat, q * ATTN_SCALE, 0.0)
        kpad_ref[h] = jnp.where(is_feat, k, 0.0).astype(BF16)
        vt_ref[h, 0] = jnp.concatenate([v_t, ones_rows], axis=0).astype(BF16)
        gates.append(_dot_bf16x3(km_ref[h], jnp.where(is_feat, q, 0.0), dot=_dot_nt))
    for h, gate_t in enumerate(gates):
        cand = jnp.where(past, gate_t, ninf)
        picks = []
        for _ in range(MOBA_TOPK):
            best = jnp.max(cand, axis=0, keepdims=True)
            idx = jnp.min(jnp.where(cand == best, row, 2 * LANES), axis=0, keepdims=True)
            ok = best > ninf
            pick = (row == idx) & ok
            cand = jnp.where(pick, ninf, cand)
            picks.append((idx, ok, pick))
        chosen = jnp.where(picks[0][2] | picks[1][2] | picks[2][2], 1.0, 0.0).astype(BF16)
        base = base_ref[h]
        rank_full = base + _dot(chosen, sut_ref[...])
        base_ref[h] = base + _dot(chosen, ones_ref[...])
        for r, (idx, ok, pick) in enumerate(picks):
            rank = jnp.sum(jnp.where(pick, rank_full, 0.0), axis=0, keepdims=True)
            sel_ref[h, r:r + 1, :] = jnp.where(ok, idx, -1)
            rank_ref[h, r:r + 1, :] = rank.astype(jnp.int32)

    @pl.when(i == pl.num_programs(0) - 1)
    def _():
        cnt_ref[...] = base_ref[...]


def _moba_route(qc, kc, vc, km_rows):
    s = qc.shape[0]
    nb = s // MOBA_BLOCK
    row = pl.BlockSpec((MOBA_BLOCK, WIDTH), lambda i: (i, 0))
    per_head = lambda dt: (pl.BlockSpec((N_HEADS, MOBA_BLOCK, LANES), lambda i: (0, i, 0)),
                           jax.ShapeDtypeStruct((N_HEADS, s, LANES), dt))
    picks = (pl.BlockSpec((N_HEADS, MOBA_TOPK, MOBA_BLOCK), lambda i: (0, 0, i)),
             jax.ShapeDtypeStruct((N_HEADS, MOBA_TOPK, s), jnp.int32))
    outs = [per_head(F32), per_head(BF16),
            (pl.BlockSpec((N_HEADS, 1, MOBA_VROWS, MOBA_BLOCK), lambda i: (0, i, 0, 0)),
             jax.ShapeDtypeStruct((N_HEADS, nb, MOBA_VROWS, MOBA_BLOCK), BF16)),
            picks, picks,
            (pl.BlockSpec((N_HEADS, MOBA_MAX_BLOCKS, MOBA_BLOCK), lambda i: (0, 0, 0)),
             jax.ShapeDtypeStruct((N_HEADS, MOBA_MAX_BLOCKS, MOBA_BLOCK), F32))]
    qi = jnp.arange(MOBA_BLOCK)
    strict_upper = (qi[:, None] < qi[None, :]).astype(BF16)
    all_ones = jnp.ones((MOBA_BLOCK, MOBA_BLOCK), BF16)
    sq = pl.BlockSpec((MOBA_BLOCK, MOBA_BLOCK), lambda i: (0, 0))
    return pl.pallas_call(
        _moba_route_kernel,
        grid=(nb,),
        in_specs=[row, row, row,
                  pl.BlockSpec((N_HEADS, MOBA_MAX_BLOCKS, LANES), lambda i: (0, 0, 0)), sq, sq],
        out_specs=[o[0] for o in outs],
        out_shape=[o[1] for o in outs],
        scratch_shapes=[pltpu.VMEM((N_HEADS, MOBA_MAX_BLOCKS, MOBA_BLOCK), F32)],
        compiler_params=_cparams("arbitrary"),
        name="moba_route",
    )(qc, kc, vc, km_rows, strict_upper, all_ones)


def _moba_tiles_per_head(seq):
    tiles = MOBA_TOPK * seq // MOBA_PAIR_TILE + seq // MOBA_BLOCK
    return (tiles // MOBA_GROUP + 1) * MOBA_GROUP


def _moba_dispatch(sel, rank, counts, n_tiles):
    n_heads, nb = counts.shape
    padded = (counts + MOBA_PAIR_TILE - 1) // MOBA_PAIR_TILE * MOBA_PAIR_TILE
    ends = jnp.cumsum(padded, axis=1)
    offs = ends - padded
    rows_per_head = n_tiles * MOBA_PAIR_TILE
    head_base = (jnp.arange(n_heads, dtype=jnp.int32) * rows_per_head)[:, None, None]
    blocks = jnp.arange(nb, dtype=jnp.int32)
    off_sel = jnp.sum(jnp.where(sel[..., None] == blocks, offs[:, None, None, :], 0), axis=-1)
    pos = head_base + jnp.where(sel >= 0, off_sel + rank, rows_per_head - MOBA_PAIR_TILE)
    tile_start = jnp.arange(n_tiles, dtype=jnp.int32) * MOBA_PAIR_TILE
    blk = jnp.sum(ends[:, None, :] <= tile_start[None, :, None], axis=2)
    tile_blk = jnp.where(tile_start[None, :] < ends[:, -1:], blk, -1)
    return pos.reshape(-1).astype(jnp.int32), tile_blk.reshape(-1).astype(jnp.int32)


def _sc_workers():
    info = plsc.get_sparse_core_info()
    return info.num_cores, info.num_cores * info.num_subcores


def _sc_scatter_rows(rows, pos, n_out):
    n_src = rows.shape[0]
    seq = n_src // N_HEADS
    n_cores, n_workers = _sc_workers()
    win = SC_INDEX_WINDOW
    wins_per_worker = n_src // win // n_workers
    wins_per_head = seq // win
    assert n_src % (win * n_workers) == 0
    mesh = plsc.VectorSubcoreMesh(core_axis_name="c", subcore_axis_name="s")

    depth = 4
    assert wins_per_worker % depth == 0

    @functools.partial(
        pl.kernel, mesh=mesh, out_type=jax.ShapeDtypeStruct((n_out, LANES), F32), name="moba_scatter_rows",
        cost_estimate=pl.CostEstimate(flops=0, transcendentals=0,
                                      bytes_accessed=(1 + MOBA_TOPK) * n_src * LANES * 4 + pos.size * 4),
        scratch_types=[pltpu.VMEM((win,), jnp.int32)] * (depth * MOBA_TOPK)
                      + [pltpu.VMEM((win, LANES), F32)] * depth + [pltpu.SemaphoreType.DMA] * (depth + 1))
    def scatter(src_hbm, idx_hbm, out_hbm, *scratch):
        idx_v = scratch[:depth * MOBA_TOPK]
        rows_v = scratch[depth * MOBA_TOPK:depth * MOBA_TOPK + depth]
        load_sems, store_sem = scratch[-depth - 1:-1], scratch[-1]
        wid = lax.axis_index("s") * n_cores + lax.axis_index("c")

        @pl.loop(0, wins_per_worker, step=depth)
        def _(w):
            loads = []
            for u in range(depth):
                src_win = wid * wins_per_worker + w + u
                h, qw = src_win // wins_per_head, src_win % wins_per_head
                cps = [pltpu.async_copy(src_hbm.at[pl.ds(src_win * win, win)], rows_v[u], load_sems[u])]
                for r in range(MOBA_TOPK):
                    cps.append(pltpu.async_copy(
                        idx_hbm.at[pl.ds((h * MOBA_TOPK + r) * seq + qw * win, win)],
                        idx_v[u * MOBA_TOPK + r], load_sems[u]))
                loads.append(cps)
            stores = []
            for u in range(depth):
                for cp in loads[u]:
                    cp.wait()
                stores += [pltpu.async_copy(rows_v[u], out_hbm.at[idx_v[u * MOBA_TOPK + r]], store_sem)
                           for r in range(MOBA_TOPK)]
            for cp in stores:
                cp.wait()

    return scatter(rows, pos)


def _sc_gather_rows(table, pos):
    n = pos.shape[0]
    n_cores, n_workers = _sc_workers()
    win = SC_INDEX_WINDOW
    wins_per_worker = n // win // n_workers
    assert n % (win * n_workers) == 0
    mesh = plsc.VectorSubcoreMesh(core_axis_name="c", subcore_axis_name="s")

    depth = 4
    assert wins_per_worker % depth == 0

    @functools.partial(
        pl.kernel, mesh=mesh, out_type=jax.ShapeDtypeStruct((n, LANES), F32), name="moba_gather_rows",
        cost_estimate=pl.CostEstimate(flops=0, transcendentals=0, bytes_accessed=2 * n * LANES * 4 + n * 4),
        scratch_types=[pltpu.VMEM((win,), jnp.int32)] * depth + [pltpu.VMEM((win, LANES), F32)] * depth
                      + [pltpu.SemaphoreType.DMA] * (depth + 2))
    def gather(table_hbm, idx_hbm, out_hbm, *scratch):
        idx_v, rows_v = scratch[:depth], scratch[depth:2 * depth]
        gather_sems, idx_sem, out_sem = scratch[2 * depth:3 * depth], scratch[-2], scratch[-1]
        wid = lax.axis_index("s") * n_cores + lax.axis_index("c")

        @pl.loop(0, wins_per_worker, step=depth)
        def _(w):
            offs = [(wid * wins_per_worker + w + u) * win for u in range(depth)]
            loads = [pltpu.async_copy(idx_hbm.at[pl.ds(offs[u], win)], idx_v[u], idx_sem) for u in range(depth)]
            for cp in loads:
                cp.wait()
            gathers = [pltpu.async_copy(table_hbm.at[idx_v[u]], rows_v[u], gather_sems[u]) for u in range(depth)]
            stores = []
            for u in range(depth):
                gathers[u].wait()
                stores.append(pltpu.async_copy(rows_v[u], out_hbm.at[pl.ds(offs[u], win)], out_sem))
            for cp in stores:
                cp.wait()

    return gather(table, pos)


def _partial_cols(pv, m):
    n = pv.shape[1]
    stacked = jnp.concatenate([pv, jnp.zeros((LANES - MOBA_VROWS, n), F32)], axis=0)
    row = lax.broadcasted_iota(jnp.int32, (LANES, n), 0)
    return jnp.where(row == HEAD_DIM + 1, m, stacked)


def _partial_rows(pv, m):
    return _partial_cols(pv, m).T


def _moba_sparse_kernel(blk_ref, q_ref, k_ref, v_ref, _after_ref, o_ref, *, n_groups):
    h, g = pl.program_id(0), pl.program_id(1)
    first = (h * n_groups + g) * MOBA_GROUP

    @pl.when(blk_ref[first] >= 0)
    def _():
        work = []
        for u in range(MOBA_GROUP):
            blk = jnp.maximum(blk_ref[first + u], 0)
            rows = pl.ds(pl.multiple_of(blk * MOBA_BLOCK, MOBA_BLOCK), MOBA_BLOCK)
            q = q_ref[u * MOBA_PAIR_TILE:(u + 1) * MOBA_PAIR_TILE, :].astype(BF16)
            work.append((blk, _dot_nt(k_ref[0, rows, :], q)))
        for u, (blk, s) in enumerate(work):
            m = jnp.max(s, axis=0, keepdims=True)
            pv = _dot(v_ref[0, blk], jnp.exp(s - m).astype(BF16))
            o_ref[u * MOBA_PAIR_TILE:(u + 1) * MOBA_PAIR_TILE, :] = _partial_rows(pv, m)

    @pl.when(g == n_groups - 1)
    def _():
        lane = lax.broadcasted_iota(jnp.int32, (MOBA_PAIR_TILE, LANES), 1)
        o_ref[(MOBA_GROUP - 1) * MOBA_PAIR_TILE:, :] = jnp.where(lane == HEAD_DIM + 1, NEG, 0.0)


def _moba_sparse(q_sorted, k_pad, vt_aug, tile_blk, n_tiles, after):
    n_heads, s, _ = k_pad.shape
    n_groups = n_tiles // MOBA_GROUP
    rows = pl.BlockSpec((MOBA_GROUP * MOBA_PAIR_TILE, LANES), lambda h, g, blk: (h * n_groups + g, 0))
    grid_spec = pltpu.PrefetchScalarGridSpec(
        num_scalar_prefetch=1, grid=(n_heads, n_groups),
        in_specs=[rows, pl.BlockSpec((1, s, LANES), lambda h, g, blk: (h, 0, 0)),
                  pl.BlockSpec((1, s // MOBA_BLOCK, MOBA_VROWS, MOBA_BLOCK), lambda h, g, blk: (h, 0, 0, 0)),
                  _ORDER_ONLY],
        out_specs=rows)
    return pl.pallas_call(
        functools.partial(_moba_sparse_kernel, n_groups=n_groups),
        grid_spec=grid_spec,
        out_shape=jax.ShapeDtypeStruct(q_sorted.shape, F32),
        compiler_params=_cparams("arbitrary", "arbitrary"),
        name="moba_sparse",
    )(tile_blk, q_sorted, k_pad, vt_aug, after)


def _moba_merge_kernel(qa_ref, qb_ref, ka_ref, kb_ref, va_ref, vb_ref, ga_ref, gb_ref, _after_ref, o_ref):
    ki = lax.broadcasted_iota(jnp.int32, (MOBA_BLOCK, MOBA_BLOCK), 0)
    qi = lax.broadcasted_iota(jnp.int32, (MOBA_BLOCK, MOBA_BLOCK), 1)
    heads = ((qa_ref, ka_ref, va_ref, ga_ref), (qb_ref, kb_ref, vb_ref, gb_ref))
    chains = [(hh, b) for b in range(MOBA_MERGE_BLOCKS) for hh in range(2)]
    rows = lambda b: slice(b * MOBA_BLOCK, (b + 1) * MOBA_BLOCK)
    scores = [_dot_nt(heads[hh][1][0, rows(b), :], heads[hh][0][0, rows(b), :].astype(BF16))
              for hh, b in chains]
    outs = {}
    for (hh, b), s in zip(chains, scores):
        _, _, v_ref, g_ref = heads[hh]
        s = jnp.where(ki <= qi, s, NEG)
        m = jnp.max(s, axis=0, keepdims=True)
        parts = [_partial_cols(_dot(v_ref[0, b], jnp.exp(s - m).astype(BF16)), m)]
        parts += [g_ref[0, r, rows(b), :].T for r in range(MOBA_TOPK)]
        ms = [x[HEAD_DIM + 1:HEAD_DIM + 2, :] for x in parts]
        m_all = functools.reduce(jnp.maximum, ms)
        acc = sum(jnp.exp(mi - m_all) * x for mi, x in zip(ms, parts))
        outs[hh, b] = acc[:HEAD_DIM, :] / acc[HEAD_DIM:HEAD_DIM + 1, :]
    for b in range(MOBA_MERGE_BLOCKS):
        o_ref[rows(b), :] = jnp.concatenate([outs[0, b], outs[1, b]], axis=0).T.astype(BF16)


def _moba_merge(q_rows, k_pad, vt_aug, gathered, after):
    n_heads, s, _ = q_rows.shape
    t = MOBA_MERGE_BLOCKS * MOBA_BLOCK
    blk = lambda par: pl.BlockSpec((1, t, LANES), lambda p, i: (2 * p + par, i, 0))
    vspec = lambda par: pl.BlockSpec((1, MOBA_MERGE_BLOCKS, MOBA_VROWS, MOBA_BLOCK),
                                     lambda p, i: (2 * p + par, i, 0, 0))
    gspec = lambda par: pl.BlockSpec((1, MOBA_TOPK, t, LANES), lambda p, i: (2 * p + par, 0, i, 0))
    return pl.pallas_call(
        _moba_merge_kernel,
        grid=(n_heads // 2, s // t),
        in_specs=[blk(0), blk(1), blk(0), blk(1), vspec(0), vspec(1), gspec(0), gspec(1), _ORDER_ONLY],
        out_specs=pl.BlockSpec((t, LANES), lambda p, i: (i, p)),
        out_shape=jax.ShapeDtypeStruct((s, WIDTH), BF16),
        compiler_params=_cparams("arbitrary", "arbitrary"),
        name="moba_merge",
    )(q_rows, q_rows, k_pad, k_pad, vt_aug, vt_aug, gathered, gathered, after)


def _moba_attention(qc, kc, vc, kmean, overlap_a, overlap_b):
    s = qc.shape[0]
    n_tiles = _moba_tiles_per_head(s)
    q_rows, k_pad, vt_aug, sel, rank, counts_raw = _moba_route(qc, kc, vc, _moba_mean_rows(kmean))
    counts = counts_raw[:, :s // MOBA_BLOCK, 0].astype(jnp.int32)
    pos, tile_blk = _moba_dispatch(sel, rank, counts, n_tiles)
    q_sorted = _sc_scatter_rows(q_rows.reshape(N_HEADS * s, LANES), pos, N_HEADS * n_tiles * MOBA_PAIR_TILE)
    def chain(calls, anchor):
        outs = []
        for call in calls:
            outs.append(call(anchor))
            anchor = outs[-1]
        return outs

    out_a = chain(overlap_a, counts_raw)
    partial = _moba_sparse(q_sorted, k_pad, vt_aug, tile_blk, n_tiles, out_a[-1])
    gathered = _sc_gather_rows(partial, pos).reshape(N_HEADS, MOBA_TOPK, s, LANES)
    out_b = chain(overlap_b, partial)
    return out_a, out_b, _moba_merge(q_rows, k_pad, vt_aug, gathered, out_b[-1])


def _ssd_kernel(z_ref, xbc_ref, dt_ref, cw_ref, cb_ref, dtb_ref, alog_ref, dsk_ref, nw_ref, exp_ref, tri_ref,
                _after_ref, o_ref, state_ref, halo_ref):
    L = SSM_CHUNK

    @pl.when(pl.program_id(0) == 0)
    def _():
        state_ref[...] = jnp.zeros_like(state_ref)
        halo_ref[...] = jnp.zeros_like(halo_ref)

    cur = xbc_ref[...]
    prev8 = halo_ref[...]
    row8 = lax.broadcasted_iota(jnp.int32, (8, SSM_CONV_DIM), 0)
    conv = cb_ref[...] + cur * cw_ref[SSM_CONV - 1:SSM_CONV, :]
    for k in range(1, SSM_CONV):
        rolled = pltpu.roll(cur, k, 0)
        head = jnp.where(row8 < k, pltpu.roll(prev8, k, 0), rolled[0:8, :])
        shifted = jnp.concatenate([head, rolled[8:, :]], axis=0)
        conv = conv + shifted * cw_ref[SSM_CONV - 1 - k:SSM_CONV - k, :]
    halo_ref[...] = cur[L - 8:, :]
    act = _silu(conv)
    xs, bm, cm = act[:, :WIDTH], act[:, WIDTH:WIDTH + 2 * SSM_STATE], act[:, WIDTH + 2 * SSM_STATE:]

    dt = _softplus(dt_ref[...] + dtb_ref[...])
    da = dt * (-jnp.exp(alog_ref[...]))
    cs = _dot_exact_lhs(tri_ref[...], da)
    expand = exp_ref[...]
    cs_b = _dot_exact_rhs(cs, expand)
    dt_b = _dot_exact_rhs(dt, expand)
    cs_last = cs_b[L - 1:L, :]
    xdt = xs * dt_b
    cs_t = cs.T

    ri = lax.broadcasted_iota(jnp.int32, (L, L), 0)
    ci = lax.broadcasted_iota(jnp.int32, (L, L), 1)
    causal = ci <= ri
    lane = lax.broadcasted_iota(jnp.int32, (L, LANES), 1)
    bmb, cmb = bm.astype(BF16), cm.astype(BF16)
    xdtb = xdt.astype(BF16)
    y_pairs = []
    for p in range(N_HEADS // 2):
        g = (2 * p) // (N_HEADS // SSM_GROUPS)
        grp = slice(g * SSM_STATE, (g + 1) * SSM_STATE)
        cbm = _dot_nt(cmb[:, grp], bmb[:, grp])
        x_pair = xdtb[:, p * LANES:(p + 1) * LANES]
        halves = []
        for h in (2 * p, 2 * p + 1):
            seg = cs[:, h:h + 1] - cs_t[h:h + 1, :]
            decay = jnp.exp(jnp.where(causal, seg, NEG))
            halves.append(_dot((cbm * decay).astype(BF16), x_pair))
        y_pairs.append(jnp.where(lane < HEAD_DIM, halves[0], halves[1]))
    y_diag = jnp.concatenate(y_pairs, axis=1)

    state = state_ref[...]
    stb = state.astype(BF16)
    half = WIDTH // SSM_GROUPS
    y_off = jnp.concatenate(
        [_dot(cmb[:, g * SSM_STATE:(g + 1) * SSM_STATE], stb[:, g * half:(g + 1) * half])
         for g in range(SSM_GROUPS)], axis=1) * jnp.exp(cs_b)
    y = y_diag + y_off + xs * dsk_ref[...]

    w = (xdt * jnp.exp(cs_last - cs_b)).astype(BF16)
    bm_t = bm.T.astype(BF16)
    new = jnp.concatenate(
        [_dot(bm_t[g * SSM_STATE:(g + 1) * SSM_STATE, :], w[:, g * half:(g + 1) * half])
         for g in range(SSM_GROUPS)], axis=1)
    state_ref[...] = state * jnp.exp(cs_last) + new

    gt = y * _silu(z_ref[...])
    outs = []
    for g in range(SSM_GROUPS):
        gg = gt[:, g * half:(g + 1) * half]
        outs.append(gg * lax.rsqrt(jnp.mean(gg * gg, axis=-1, keepdims=True) + NORM_EPS))
    o_ref[...] = (jnp.concatenate(outs, axis=1) * nw_ref[...]).astype(BF16)


def _ssd_mixer(z, xbc, dt_raw, conv_w, conv_b, dt_bias, a_log, d_skip, norm_w, expand, tri, after):
    s = z.shape[0]
    L = SSM_CHUNK
    row = lambda n: pl.BlockSpec((L, n), lambda c: (c, 0))
    cst = lambda r, n: pl.BlockSpec((r, n), lambda c: (0, 0))
    return pl.pallas_call(
        _ssd_kernel,
        grid=(s // L,),
        in_specs=[row(WIDTH), row(SSM_CONV_DIM), row(DT_PAD), cst(SSM_CONV, SSM_CONV_DIM),
                  cst(1, SSM_CONV_DIM), cst(1, DT_PAD), cst(1, DT_PAD), cst(1, WIDTH), cst(1, WIDTH),
                  cst(DT_PAD, WIDTH), cst(L, L), _ORDER_ONLY],
        out_specs=row(WIDTH),
        out_shape=jax.ShapeDtypeStruct((s, WIDTH), BF16),
        scratch_shapes=[pltpu.VMEM((SSM_STATE, WIDTH), F32), pltpu.VMEM((8, SSM_CONV_DIM), F32)],
        compiler_params=_cparams("arbitrary"),
        cost_estimate=pl.CostEstimate(
            flops=2 * s * L * (SSM_GROUPS * SSM_STATE + WIDTH + 2 * N_HEADS * LANES) + 4 * s * SSM_STATE * WIDTH,
            transcendentals=s * (L * N_HEADS + SSM_CONV_DIM + 3 * WIDTH),
            bytes_accessed=s * (WIDTH + SSM_CONV_DIM + DT_PAD) * 4 + s * WIDTH * 2),
        name="ssd_mixer",
    )(z, xbc, dt_raw, conv_w, conv_b, dt_bias, a_log, d_skip, norm_w, expand, tri, after)


def _outproj_kernel(x_ref, *refs):
    *y_refs, w_ref, gate_ref, o_ref = refs
    y, row0 = 0.0, 0
    for y_ref in y_refs:
        y = y + _dot(y_ref[...], w_ref[row0:row0 + y_ref.shape[1], :])
        row0 += y_ref.shape[1]
    o_ref[...] = x_ref[...] + gate_ref[...] * y


def _out_projection(x, ys, w_all, gate, layer):
    s, d = x.shape
    tm = 512
    row = lambda n: pl.BlockSpec((tm, n), lambda i: (i, 0))
    assert sum(y.shape[1] for y in ys) == w_all.shape[1]
    return pl.pallas_call(
        _outproj_kernel,
        grid=(s // tm,),
        in_specs=[row(d)] + [row(y.shape[1]) for y in ys]
                 + [pl.BlockSpec((None, w_all.shape[1], d), lambda i: (layer, 0, 0)),
                    pl.BlockSpec((1, d), lambda i: (0, 0))],
        out_specs=row(d),
        out_shape=jax.ShapeDtypeStruct((s, d), F32),
        compiler_params=_cparams("arbitrary"),
        name="out_projection",
    )(x, *ys, w_all, gate)


def _ffn_kernel(x_ref, g_ref, sc_ref, sh_ref, gate_ref, wg_ref, wu_ref, wd_ref, o_ref, *, chunk):
    x = x_ref[...]
    h = _mod_norm(x, g_ref[...], sc_ref[...], sh_ref[...]).astype(BF16)
    hidden = wg_ref.shape[1]
    n_chunks = hidden // chunk
    cols = lambda c: slice(c * chunk, (c + 1) * chunk)
    gate_up = lambda c: (_dot(h, wg_ref[:, cols(c)]), _dot(h, wu_ref[:, cols(c)]))
    acc = jnp.zeros(x.shape, F32)
    nxt = gate_up(0)
    for c in range(n_chunks):
        g, u = nxt
        if c + 1 < n_chunks:
            nxt = gate_up(c + 1)
        acc = acc + _dot((_silu(g) * u).astype(BF16), wd_ref[cols(c), :])
    o_ref[...] = x + gate_ref[...] * acc


def _ffn(x, g, sc, sh, gate, w_gate, w_up, w_down, layer):
    s, d = x.shape
    hidden = w_gate.shape[2]
    tm = 512
    row = pl.BlockSpec((tm, d), lambda i: (i, 0))
    vec = pl.BlockSpec((1, d), lambda i: (0, 0))
    return pl.pallas_call(
        functools.partial(_ffn_kernel, chunk=256),
        grid=(s // tm,),
        in_specs=[row, vec, vec, vec, vec,
                  _resident((None, d, hidden), lambda i: (layer, 0, 0)),
                  _resident((None, d, hidden), lambda i: (layer, 0, 0)),
                  _resident((None, hidden, d), lambda i: (layer, 0, 0))],
        out_specs=row,
        out_shape=jax.ShapeDtypeStruct((s, d), F32),
        compiler_params=_cparams("arbitrary"),
        name="ffn",
    )(x, g, sc, sh, gate, w_gate, w_up, w_down)


def _rope_tables(seq):
    pos = jnp.arange(seq, dtype=F32)
    inv = ROPE_THETA ** (-jnp.arange(0, ROPE_DIM, 2, dtype=F32) / ROPE_DIM)
    ang = pos[:, None] * inv[None, :]
    cos, sin = jnp.cos(ang), jnp.sin(ang)
    ones = jnp.ones((seq, HEAD_DIM - ROPE_DIM), F32)
    zeros = jnp.zeros((seq, HEAD_DIM - ROPE_DIM), F32)
    zh = jnp.zeros((seq, ROPE_HALF), F32)
    cos_h = jnp.concatenate([cos, cos, ones], axis=1)
    sin_lo = jnp.concatenate([-sin, zh, zeros], axis=1)
    sin_hi = jnp.concatenate([zh, sin, zeros], axis=1)
    two = lambda t: jnp.concatenate([t, t], axis=1)
    return two(cos_h), two(sin_lo), two(sin_hi)


def _moba_mean_rows(kmean):
    nb = kmean.shape[0]
    km = kmean.reshape(nb, N_HEADS, HEAD_DIM).transpose(1, 0, 2)
    return jnp.pad(km, ((0, 0), (0, MOBA_MAX_BLOCKS - nb), (0, LANES - HEAD_DIM)))


def kernel(x, c, ada_w, ada_b, norm_mix, w_in, qn_swa, kn_swa, qn_moba, kn_moba, conv_w, conv_b, dt_bias,
           a_log, d_skip, ssm_norm, w_out, norm_ffn, w_gate, w_up, w_down):
    batch, seq, d = x.shape
    depth = ada_w.shape[0]
    assert batch == 1 and d == D_MODEL
    assert seq % SWA_SPAN == 0 and seq // MOBA_BLOCK <= MOBA_MAX_BLOCKS
    xs = x.reshape(seq, d)

    mod = _ada_modulation(c, ada_w, ada_b)
    rope_tabs = _rope_tables(seq)
    lane_head = jnp.arange(LANES) // HEAD_DIM
    hsum = (lane_head[:, None] == lane_head[None, :]).astype(BF16)
    expand = (jnp.arange(DT_PAD)[:, None] == (jnp.arange(WIDTH) // HEAD_DIM)[None, :]).astype(BF16)
    tri = (jnp.arange(SSM_CHUNK)[:, None] >= jnp.arange(SSM_CHUNK)[None, :]).astype(BF16)
    pad8 = lambda v: jnp.pad(v, (0, DT_PAD - N_HEADS)).reshape(1, DT_PAD)
    two = lambda g: jnp.concatenate([g, g]).reshape(1, LANES)

    w_in_b = jnp.pad(w_in, ((0, 0), (0, 0), (0, IN_PROJ_PAD - w_in.shape[2]))).astype(BF16)
    w_out_b, w_gate_b, w_up_b, w_down_b = (w.astype(BF16) for w in (w_out, w_gate, w_up, w_down))

    for l in range(depth):
        shift_m, scale_m, gate_m, shift_f, scale_f, gate_f = [mod[l, :, i * d:(i + 1) * d] for i in range(6)]
        gains = [two(g[l]) for g in (qn_swa, kn_swa, qn_moba, kn_moba)]
        qa, ka, va, qc, kc, vc, z, xbc, dt_raw, kmean = _in_projection(
            xs, norm_mix[l].reshape(1, d), scale_m, shift_m, w_in_b, l, gains, hsum, rope_tabs)

        n_pairs = WIDTH // LANES
        first = n_pairs - 1
        dilated_lo = functools.partial(_dilated_attention, qa, ka, va, 0, first)
        dilated_hi = functools.partial(_dilated_attention, qa, ka, va, first, n_pairs - first)
        ssd = functools.partial(
            _ssd_mixer, z, xbc, dt_raw, conv_w[l], conv_b[l].reshape(1, -1), pad8(dt_bias[l]), pad8(a_log[l]),
            jnp.repeat(d_skip[l], HEAD_DIM).reshape(1, WIDTH), ssm_norm[l].reshape(1, WIDTH), expand, tri)
        (y_a_lo,), (y_a_hi, y_b), y_c = _moba_attention(qc, kc, vc, kmean, (dilated_lo,), (dilated_hi, ssd))

        xs = _out_projection(xs, (y_a_lo, y_a_hi, y_b, y_c), w_out_b, gate_m, l)
        xs = _ffn(xs, norm_ffn[l].reshape(1, d), scale_f, shift_f, gate_f,
                  w_gate_b, w_up_b, w_down_b, l)
    return xs.reshape(batch, seq, d)
```

```python
import functools

import jax
import jax.numpy as jnp
from jax import lax
from jax.experimental import pallas as pl
from jax.experimental.pallas import tpu as pltpu
from jax.experimental.pallas import tpu_sc as plsc

F32 = jnp.float32
BF16 = jnp.bfloat16

D_MODEL = 1024
HEAD_DIM = 64
N_HEADS = 8
WIDTH = N_HEADS * HEAD_DIM
ATTN_SCALE = HEAD_DIM ** -0.5
NORM_EPS = 1e-6
NEG = -1e30

ROPE_THETA = 500000.0
ROPE_DIM = HEAD_DIM // 4
ROPE_HALF = ROPE_DIM // 2

DILATIONS = (1, 4, 16)
SWA_BLOCK = 128
SWA_SPAN = DILATIONS[-1] * SWA_BLOCK
SWA_GROUP = 4

SSM_STATE = 128
SSM_GROUPS = 2
SSM_CONV = 4
SSM_CHUNK = 256
SSM_CONV_DIM = WIDTH + 2 * SSM_GROUPS * SSM_STATE

MOBA_BLOCK = 256
MOBA_TOPK = 3
MOBA_PAIR_TILE = 128
MOBA_GROUP = 32
SC_INDEX_WINDOW = 128
MOBA_MAX_BLOCKS = 64
MOBA_MERGE_BLOCKS = 4
MOBA_VROWS = HEAD_DIM + 16

LANES = 128
DT_PAD = LANES
IN_PROJ_PAD = 6 * WIDTH + WIDTH + SSM_CONV_DIM + DT_PAD
OFF_A, OFF_C, OFF_Z, OFF_XBC, OFF_DT = 0, 3 * WIDTH, 6 * WIDTH, 7 * WIDTH, 7 * WIDTH + SSM_CONV_DIM

VMEM_LIMIT = 56 * 1024 * 1024


_ORDER_ONLY = pl.BlockSpec(memory_space=pl.ANY)


def _resident(block_shape, index_map):
    return pl.BlockSpec(block_shape, index_map, pipeline_mode=pl.Buffered(1))


def _cparams(*sem):
    return pltpu.CompilerParams(dimension_semantics=sem, vmem_limit_bytes=VMEM_LIMIT)


def _split3(a):
    hi = a.astype(BF16)
    r1 = a - hi.astype(F32)
    mid = r1.astype(BF16)
    lo = (r1 - mid.astype(F32)).astype(BF16)
    return hi, mid, lo


def _dot(a, b):
    return jnp.dot(a, b, preferred_element_type=F32)


def _dot_nt(a, b):
    return lax.dot_general(a, b, (((1,), (1,)), ((), ())), preferred_element_type=F32)


def _dot_exact_rhs(a, b_bf16):
    hi, mid, lo = _split3(a)
    return _dot(hi, b_bf16) + _dot(mid, b_bf16) + _dot(lo, b_bf16)


def _dot_exact_lhs(a_bf16, b):
    hi, mid, lo = _split3(b)
    return _dot(a_bf16, hi) + _dot(a_bf16, mid) + _dot(a_bf16, lo)


def _dot_f32(a, b, dot=_dot):
    ah, am, al = _split3(a)
    bh, bm, bl = _split3(b)
    return (dot(ah, bh) + (dot(ah, bm) + dot(am, bh))
            + (dot(ah, bl) + dot(am, bm) + dot(al, bh)))


def _dot_bf16x3(a, b, dot=_dot):
    ah = a.astype(BF16)
    al = (a - ah.astype(F32)).astype(BF16)
    bh = b.astype(BF16)
    bl = (b - bh.astype(F32)).astype(BF16)
    return dot(ah, bh) + (dot(ah, bl) + dot(al, bh))


def _silu(x):
    return x / (1.0 + jnp.exp(-x))


def _softplus(x):
    return jnp.maximum(x, 0.0) + jnp.log1p(jnp.exp(-jnp.abs(x)))


def _ada_kernel(c_ref, w_ref, b_ref, o_ref):
    c = c_ref[...]
    o_ref[0] = _dot_f32(_silu(c), w_ref[0]) + b_ref[0]


def _ada_modulation(c, ada_w, ada_b):
    depth, d, n = ada_w.shape
    tn = 1024
    c8 = jnp.broadcast_to(c, (8, d))
    out = pl.pallas_call(
        _ada_kernel,
        grid=(depth, n // tn),
        in_specs=[pl.BlockSpec((8, d), lambda l, j: (0, 0)),
                  pl.BlockSpec((1, d, tn), lambda l, j: (l, 0, j)),
                  pl.BlockSpec((1, 1, tn), lambda l, j: (l, 0, j))],
        out_specs=pl.BlockSpec((1, 8, tn), lambda l, j: (l, 0, j)),
        out_shape=jax.ShapeDtypeStruct((depth, 8, n), F32),
        compiler_params=_cparams("arbitrary", "arbitrary"),
        name="ada_modulation",
    )(c8, ada_w, ada_b.reshape(depth, 1, n))
    return out[:, 0:1, :]


def _mod_norm(x, g, sc, sh):
    var = jnp.mean(x * x, axis=-1, keepdims=True)
    return (x * lax.rsqrt(var + NORM_EPS) * g) * (1.0 + sc) + sh


def _inproj_kernel(x_ref, g_ref, sc_ref, sh_ref, w_ref, gqa_ref, gka_ref, gqc_ref, gkc_ref, hsum_ref,
                   cos_ref, slo_ref, shi_ref,
                   qa_ref, ka_ref, va_ref, qc_ref, kc_ref, vc_ref, z_ref, xbc_ref, dt_ref, km_ref):
    h = _mod_norm(x_ref[...], g_ref[...], sc_ref[...], sh_ref[...]).astype(BF16)
    proj = lambda off, n=WIDTH: _dot(h, w_ref[:, off:off + n])
    raw = [proj(OFF_A), proj(OFF_A + WIDTH), proj(OFF_C), proj(OFF_C + WIDTH)]
    va_ref[...] = proj(OFF_A + 2 * WIDTH)
    vc_ref[...] = proj(OFF_C + 2 * WIDTH)
    z_ref[...] = proj(OFF_Z)
    xbc_ref[...] = proj(OFF_XBC, SSM_CONV_DIM)
    dt_ref[...] = proj(OFF_DT, DT_PAD)

    hsum = hsum_ref[...]
    cos_t, slo, shi = cos_ref[...], slo_ref[...], shi_ref[...]
    rope = lambda x, gain_ref: _head_norm_rope(x, gain_ref[...], hsum, cos_t, slo, shi)
    n_blocks = x_ref.shape[0] // MOBA_BLOCK
    for p in range(WIDTH // LANES):
        lo, hi = p * LANES, (p + 1) * LANES
        qa_ref[:, lo:hi] = rope(raw[0][:, lo:hi], gqa_ref) * ATTN_SCALE
        ka_ref[:, lo:hi] = rope(raw[1][:, lo:hi], gka_ref)
        qc_ref[:, lo:hi] = rope(raw[2][:, lo:hi], gqc_ref)
        kc = rope(raw[3][:, lo:hi], gkc_ref)
        kc_ref[:, lo:hi] = kc
        for b in range(n_blocks):
            km_ref[b, :, lo:hi] = jnp.mean(kc[b * MOBA_BLOCK:(b + 1) * MOBA_BLOCK, :], axis=0, keepdims=True)


def _in_projection(x, g, sc, sh, w_all, layer, gains, hsum, rope_tabs):
    s, d = x.shape
    tm = 512
    assert tm % MOBA_BLOCK == 0
    row = lambda n: pl.BlockSpec((tm, n), lambda i: (i, 0))
    vec = pl.BlockSpec((1, d), lambda i: (0, 0))
    cst = lambda r, n: pl.BlockSpec((r, n), lambda i: (0, 0))
    widths = [WIDTH] * 7 + [SSM_CONV_DIM, DT_PAD]
    return pl.pallas_call(
        _inproj_kernel,
        grid=(s // tm,),
        in_specs=[row(d), vec, vec, vec, _resident((None, d, IN_PROJ_PAD), lambda i: (layer, 0, 0))]
                 + [cst(1, LANES)] * 4 + [cst(LANES, LANES)] + [row(LANES)] * 3,
        out_specs=[row(n) for n in widths]
                  + [pl.BlockSpec((tm // MOBA_BLOCK, 1, WIDTH), lambda i: (i, 0, 0))],
        out_shape=[jax.ShapeDtypeStruct((s, n), F32) for n in widths]
                  + [jax.ShapeDtypeStruct((s // MOBA_BLOCK, 1, WIDTH), F32)],
        compiler_params=_cparams("arbitrary"),
        name="in_projection",
    )(x, g, sc, sh, w_all, *gains, hsum, *rope_tabs)


def _head_norm_rope(x, gain, hsum, cos_t, sin_lo, sin_hi):
    sq = x * x
    sq_hi = sq.astype(BF16)
    ss = _dot(sq_hi, hsum) + _dot((sq - sq_hi.astype(F32)).astype(BF16), hsum)
    y = x * lax.rsqrt(ss * (1.0 / HEAD_DIM) + NORM_EPS) * gain
    return (y * cos_t + pltpu.roll(y, ROPE_HALF, 1) * sin_hi
            + pltpu.roll(y, LANES - ROPE_HALF, 1) * sin_lo)


def _dilated_kernel(q_ref, kc_ref, kp_ref, vc_ref, vp_ref, _after_ref, o_ref, m_s, l_s, a_s, q4_s, k4_s, v4_s):
    has_prev = pl.program_id(1) > 0
    lane = lax.broadcasted_iota(jnp.int32, (SWA_BLOCK, LANES), 1)
    head_a = lane < HEAD_DIM
    qi = lax.broadcasted_iota(jnp.int32, (SWA_BLOCK, 2 * SWA_BLOCK), 0)
    ki = lax.broadcasted_iota(jnp.int32, (SWA_BLOCK, 2 * SWA_BLOCK), 1)
    band = (ki >= qi) & (ki <= qi + SWA_BLOCK)
    band_edge = band & ((ki >= SWA_BLOCK) | has_prev)

    quarter = SWA_SPAN // 4
    for r4 in range(4):
        src = pl.ds(r4, quarter, 4)
        q4_s[r4 * quarter:(r4 + 1) * quarter, :] = q_ref[src, :]
        for dst, prev, cur in ((k4_s, kp_ref, kc_ref), (v4_s, vp_ref, vc_ref)):
            dst[2 * r4 * quarter:(2 * r4 + 1) * quarter, :] = prev[src, :]
            dst[(2 * r4 + 1) * quarter:(2 * r4 + 2) * quarter, :] = cur[src, :]

    def unit_d16(a, r4):
        rows = pl.ds(4 * a + r4, SWA_BLOCK, 16)
        q = q4_s[pl.ds(r4 * quarter + a, SWA_BLOCK, 4), :]
        prev = pl.ds(2 * r4 * quarter + a, SWA_BLOCK, 4)
        cur = pl.ds((2 * r4 + 1) * quarter + a, SWA_BLOCK, 4)
        k_cat = jnp.concatenate([k4_s[prev, :], k4_s[cur, :]], axis=0)
        v_cat = jnp.concatenate([v4_s[prev, :], v4_s[cur, :]], axis=0)
        return rows, q, k_cat, v_cat, band_edge

    def unit_d4(r, nb):
        rows = pl.ds(4 * SWA_BLOCK * nb + r, SWA_BLOCK, 4)
        q = q4_s[pl.ds(r * quarter + SWA_BLOCK * nb, SWA_BLOCK), :]
        keys = pl.ds((2 * r + 1) * quarter + SWA_BLOCK * (nb - 1), 2 * SWA_BLOCK)
        return rows, q, k4_s[keys, :], v4_s[keys, :], (band_edge if nb == 0 else band)

    def unit_d1(b, edge=False):
        rows = pl.ds(SWA_BLOCK * b, SWA_BLOCK)
        if edge:
            prev = pl.ds(SWA_SPAN - SWA_BLOCK, SWA_BLOCK)
            k_cat = jnp.concatenate([kp_ref[prev, :], kc_ref[rows, :]], axis=0)
            v_cat = jnp.concatenate([vp_ref[prev, :], vc_ref[rows, :]], axis=0)
            return rows, q_ref[rows, :], k_cat, v_cat, band_edge
        keys = pl.ds(SWA_BLOCK * (b - 1), 2 * SWA_BLOCK)
        return rows, q_ref[rows, :], kc_ref[keys, :], vc_ref[keys, :], band

    def attend(units, mode):
        scores = []
        for _, q, k_cat, _, mask in units:
            kb = k_cat.astype(BF16)
            scores.append([jnp.where(mask, _dot_nt(jnp.where(sel, q, 0.0).astype(BF16), kb), NEG)
                           for sel in (head_a, ~head_a)])
        probs = []
        for pair in scores:
            stats = []
            for s in pair:
                m_loc = jnp.max(s, axis=-1, keepdims=True)
                p = jnp.exp(s - m_loc)
                stats.append((m_loc, jnp.sum(p, axis=-1, keepdims=True), p.astype(BF16)))
            probs.append(stats)
        for (rows, _, _, v_cat, _), ((m_a, s_a, p_a), (m_b, s_b, p_b)) in zip(units, probs):
            vb = v_cat.astype(BF16)
            m_new = jnp.where(head_a, m_a, m_b)
            l_new = jnp.where(head_a, s_a, s_b)
            a_new = jnp.where(head_a, _dot(p_a, vb), _dot(p_b, vb))
            if mode != "init":
                m_in, m_loc = m_s[rows, :], m_new
                m_new = jnp.maximum(m_in, m_loc)
                alpha, beta = jnp.exp(m_in - m_new), jnp.exp(m_loc - m_new)
                l_new = alpha * l_s[rows, :] + beta * l_new
                a_new = alpha * a_s[rows, :] + beta * a_new
            if mode == "final":
                o_ref[rows, :] = (a_new / l_new).astype(BF16)
            else:
                m_s[rows, :] = m_new
                l_s[rows, :] = l_new
                a_s[rows, :] = a_new

    g = SWA_GROUP
    assert DILATIONS == (1, 4, 16) and g == 4 and SWA_SPAN // (4 * SWA_BLOCK) == g

    def body16(a, carry):
        attend([unit_d16(a, r4) for r4 in range(g)], "init")
        return carry
    lax.fori_loop(0, 16 // g, body16, 0)

    def body4(r, carry):
        attend([unit_d4(r, nb) for nb in range(g)], "fold")
        return carry
    lax.fori_loop(0, 4, body4, 0)

    attend([unit_d1(b, edge=(b == 0)) for b in range(g)], "final")

    def body1(i, carry):
        attend([unit_d1(i * g + j) for j in range(g)], "final")
        return carry
    lax.fori_loop(1, SWA_SPAN // SWA_BLOCK // g, body1, 0)


def _dilated_attention(q, k, v, pair0, n_pairs, after):
    s = q.shape[0]
    cur = lambda off: pl.BlockSpec((SWA_SPAN, LANES), lambda p, n: (n, off + p))
    prev = lambda off: pl.BlockSpec((SWA_SPAN, LANES), lambda p, n: (jnp.maximum(n - 1, 0), off + p))
    width = n_pairs * LANES
    return pl.pallas_call(
        _dilated_kernel,
        grid=(n_pairs, s // SWA_SPAN),
        in_specs=[cur(pair0), cur(pair0), prev(pair0), cur(pair0), prev(pair0), _ORDER_ONLY],
        out_specs=cur(0),
        out_shape=jax.ShapeDtypeStruct((s, width), BF16),
        scratch_shapes=[pltpu.VMEM((SWA_SPAN, LANES), F32)] * 4 + [pltpu.VMEM((2 * SWA_SPAN, LANES), F32)] * 2,
        compiler_params=_cparams("arbitrary", "arbitrary"),
        cost_estimate=pl.CostEstimate(
            flops=len(DILATIONS) * 4 * s * 2 * SWA_BLOCK * width,
            transcendentals=len(DILATIONS) * s * 2 * SWA_BLOCK * 2 * n_pairs,
            bytes_accessed=5 * s * width * 4 + s * width * 2),
        name="dilated_attention",
    )(q, k, k, v, v, after)


def _moba_route_kernel(q_ref, k_ref, v_ref, km_ref, sut_ref, ones_ref,
                       qrow_ref, kpad_ref, vt_ref, sel_ref, rank_ref, cnt_ref, base_ref):
    i = pl.program_id(0)

    @pl.when(i == 0)
    def _():
        base_ref[...] = jnp.zeros_like(base_ref)

    lane = lax.broadcasted_iota(jnp.int32, (MOBA_BLOCK, LANES), 1)
    is_feat = lane < HEAD_DIM
    row = lax.broadcasted_iota(jnp.int32, (MOBA_MAX_BLOCKS, MOBA_BLOCK), 0)
    past = row < i
    ones_rows = jnp.where(
        lax.broadcasted_iota(jnp.int32, (MOBA_VROWS - HEAD_DIM, MOBA_BLOCK), 0) == 0, 1.0, 0.0)
    ninf = float("-inf")
    gates = []
    for h in range(N_HEADS):
        cols = slice((h // 2) * LANES, (h // 2 + 1) * LANES)
        q, k = q_ref[:, cols], k_ref[:, cols]
        v_t = v_ref[:, cols].T
        if h % 2:
            q, k = pltpu.roll(q, HEAD_DIM, 1), pltpu.roll(k, HEAD_DIM, 1)
            v_t = v_t[HEAD_DIM:, :]
        else:
            v_t = v_t[:HEAD_DIM, :]
        qrow_ref[h] = jnp.where(is_feat, q * ATTN_SCALE, 0.0)
        kpad_ref[h] = jnp.where(is_feat, k, 0.0).astype(BF16)
        vt_ref[h, 0] = jnp.concatenate([v_t, ones_rows], axis=0).astype(BF16)
        gates.append(_dot_bf16x3(km_ref[h], jnp.where(is_feat, q, 0.0), dot=_dot_nt))
    for h, gate_t in enumerate(gates):
        cand = jnp.where(past, gate_t, ninf)
        picks = []
        for _ in range(MOBA_TOPK):
            best = jnp.max(cand, axis=0, keepdims=True)
            idx = jnp.min(jnp.where(cand == best, row, 2 * LANES), axis=0, keepdims=True)
            ok = best > ninf
            pick = (row == idx) & ok
            cand = jnp.where(pick, ninf, cand)
            picks.append((idx, ok, pick))
        chosen = jnp.where(picks[0][2] | picks[1][2] | picks[2][2], 1.0, 0.0).astype(BF16)
        base = base_ref[h]
        rank_full = base + _dot(chosen, sut_ref[...])
        base_ref[h] = base + _dot(chosen, ones_ref[...])
        for r, (idx, ok, pick) in enumerate(picks):
            rank = jnp.sum(jnp.where(pick, rank_full, 0.0), axis=0, keepdims=True)
            sel_ref[h, r:r + 1, :] = jnp.where(ok, idx, -1)
            rank_ref[h, r:r + 1, :] = rank.astype(jnp.int32)

    @pl.when(i == pl.num_programs(0) - 1)
    def _():
        cnt_ref[...] = base_ref[...]


def _moba_route(qc, kc, vc, km_rows):
    s = qc.shape[0]
    nb = s // MOBA_BLOCK
    row = pl.BlockSpec((MOBA_BLOCK, WIDTH), lambda i: (i, 0))
    per_head = lambda dt: (pl.BlockSpec((N_HEADS, MOBA_BLOCK, LANES), lambda i: (0, i, 0)),
                           jax.ShapeDtypeStruct((N_HEADS, s, LANES), dt))
    picks = (pl.BlockSpec((N_HEADS, MOBA_TOPK, MOBA_BLOCK), lambda i: (0, 0, i)),
             jax.ShapeDtypeStruct((N_HEADS, MOBA_TOPK, s), jnp.int32))
    outs = [per_head(F32), per_head(BF16),
            (pl.BlockSpec((N_HEADS, 1, MOBA_VROWS, MOBA_BLOCK), lambda i: (0, i, 0, 0)),
             jax.ShapeDtypeStruct((N_HEADS, nb, MOBA_VROWS, MOBA_BLOCK), BF16)),
            picks, picks,
            (pl.BlockSpec((N_HEADS, MOBA_MAX_BLOCKS, MOBA_BLOCK), lambda i: (0, 0, 0)),
             jax.ShapeDtypeStruct((N_HEADS, MOBA_MAX_BLOCKS, MOBA_BLOCK), F32))]
    qi = jnp.arange(MOBA_BLOCK)
    strict_upper = (qi[:, None] < qi[None, :]).astype(BF16)
    all_ones = jnp.ones((MOBA_BLOCK, MOBA_BLOCK), BF16)
    sq = pl.BlockSpec((MOBA_BLOCK, MOBA_BLOCK), lambda i: (0, 0))
    return pl.pallas_call(
        _moba_route_kernel,
        grid=(nb,),
        in_specs=[row, row, row,
                  pl.BlockSpec((N_HEADS, MOBA_MAX_BLOCKS, LANES), lambda i: (0, 0, 0)), sq, sq],
        out_specs=[o[0] for o in outs],
        out_shape=[o[1] for o in outs],
        scratch_shapes=[pltpu.VMEM((N_HEADS, MOBA_MAX_BLOCKS, MOBA_BLOCK), F32)],
        compiler_params=_cparams("arbitrary"),
        name="moba_route",
    )(qc, kc, vc, km_rows, strict_upper, all_ones)


def _moba_tiles_per_head(seq):
    tiles = MOBA_TOPK * seq // MOBA_PAIR_TILE + seq // MOBA_BLOCK
    return (tiles // MOBA_GROUP + 1) * MOBA_GROUP


def _moba_dispatch(sel, rank, counts, n_tiles):
    n_heads, nb = counts.shape
    padded = (counts + MOBA_PAIR_TILE - 1) // MOBA_PAIR_TILE * MOBA_PAIR_TILE
    ends = jnp.cumsum(padded, axis=1)
    offs = ends - padded
    rows_per_head = n_tiles * MOBA_PAIR_TILE
    head_base = (jnp.arange(n_heads, dtype=jnp.int32) * rows_per_head)[:, None, None]
    blocks = jnp.arange(nb, dtype=jnp.int32)
    off_sel = jnp.sum(jnp.where(sel[..., None] == blocks, offs[:, None, None, :], 0), axis=-1)
    null_row = rows_per_head - MOBA_PAIR_TILE + jnp.arange(sel.shape[-1], dtype=jnp.int32) % MOBA_PAIR_TILE
    pos = head_base + jnp.where(sel >= 0, off_sel + rank, null_row)
    tile_start = jnp.arange(n_tiles, dtype=jnp.int32) * MOBA_PAIR_TILE
    blk = jnp.sum(ends[:, None, :] <= tile_start[None, :, None], axis=2)
    tile_blk = jnp.where(tile_start[None, :] < ends[:, -1:], blk, -1)
    return pos.reshape(-1).astype(jnp.int32), tile_blk.reshape(-1).astype(jnp.int32)


def _sc_workers():
    info = plsc.get_sparse_core_info()
    return info.num_cores, info.num_cores * info.num_subcores


def _sc_scatter_rows(rows, pos, n_out):
    n_src = rows.shape[0]
    seq = n_src // N_HEADS
    n_cores, n_workers = _sc_workers()
    win = SC_INDEX_WINDOW
    wins_per_worker = n_src // win // n_workers
    wins_per_head = seq // win
    assert n_src % (win * n_workers) == 0
    mesh = plsc.VectorSubcoreMesh(core_axis_name="c", subcore_axis_name="s")

    per_set = 2
    n_groups = wins_per_worker // per_set
    assert wins_per_worker % (2 * per_set) == 0 and wins_per_head % wins_per_worker == 0

    @functools.partial(
        pl.kernel, mesh=mesh, out_type=jax.ShapeDtypeStruct((n_out, LANES), F32), name="moba_scatter_rows",
        cost_estimate=pl.CostEstimate(flops=0, transcendentals=0,
                                      bytes_accessed=(1 + MOBA_TOPK) * n_src * LANES * 4 + pos.size * 4),
        scratch_types=[pltpu.VMEM((wins_per_worker, win), jnp.int32)] * MOBA_TOPK
                      + [pltpu.VMEM((win, LANES), F32)] * (2 * per_set)
                      + [pltpu.SemaphoreType.DMA] * (2 * per_set + 2))
    def scatter(src_hbm, idx_hbm, out_hbm, *scratch):
        idx_all = scratch[:MOBA_TOPK]
        rows_v = scratch[MOBA_TOPK:MOBA_TOPK + 2 * per_set]
        load_sems, scat_sems = scratch[-2 * per_set - 2:-2], scratch[-2:]
        wid = lax.axis_index("s") * n_cores + lax.axis_index("c")
        base = wid * wins_per_worker
        h, qw0 = base // wins_per_head, base % wins_per_head
        for r in range(MOBA_TOPK):
            pltpu.sync_copy(idx_hbm.at[pl.ds((h * MOBA_TOPK + r) * wins_per_head + qw0, wins_per_worker)], idx_all[r])

        def load_copy(g, st, u):
            return pltpu.make_async_copy(src_hbm.at[pl.ds((base + g * per_set + u) * win, win)],
                                         rows_v[st * per_set + u], load_sems[st * per_set + u])

        def scat_copy(g, st, u, r):
            return pltpu.make_async_copy(rows_v[st * per_set + u], out_hbm.at[idx_all[r].at[g * per_set + u]],
                                         scat_sems[st])

        def issue(g, st):
            for u in range(per_set):
                load_copy(g, st, u).start()

        def drain(g, st):
            for u in range(per_set):
                load_copy(g, st, u).wait()
                for r in range(MOBA_TOPK):
                    scat_copy(g, st, u, r).start()

        def wait_scatters(g, st):
            for u in range(per_set):
                for r in range(MOBA_TOPK):
                    scat_copy(g, st, u, r).wait()

        issue(0, 0)

        @pl.loop(0, n_groups, step=2)
        def _(g):
            @pl.when(g > 0)
            def _():
                wait_scatters(g - 1, 1)
            issue(g + 1, 1)
            drain(g, 0)
            wait_scatters(g, 0)

            @pl.when(g + 2 < n_groups)
            def _():
                issue(g + 2, 0)
            drain(g + 1, 1)

        wait_scatters(n_groups - 1, 1)

    return scatter(rows, pos.reshape(pos.shape[0] // win, win))


def _sc_gather_rows(table, pos):
    n = pos.shape[0]
    n_cores, n_workers = _sc_workers()
    win = SC_INDEX_WINDOW
    wins_per_worker = n // win // n_workers
    assert n % (win * n_workers) == 0
    mesh = plsc.VectorSubcoreMesh(core_axis_name="c", subcore_axis_name="s")

    per_set = 2
    n_groups = wins_per_worker // per_set
    assert wins_per_worker % (2 * per_set) == 0

    @functools.partial(
        pl.kernel, mesh=mesh, out_type=jax.ShapeDtypeStruct((n, LANES), F32), name="moba_gather_rows",
        cost_estimate=pl.CostEstimate(flops=0, transcendentals=0, bytes_accessed=2 * n * LANES * 4 + n * 4),
        scratch_types=[pltpu.VMEM((wins_per_worker, win), jnp.int32)] + [pltpu.VMEM((win, LANES), F32)] * (2 * per_set)
                      + [pltpu.SemaphoreType.DMA] * (2 * per_set + 2))
    def gather(table_hbm, idx_hbm, out_hbm, idx_all, *scratch):
        rows_v = scratch[:2 * per_set]
        gather_sems, store_sems = scratch[2 * per_set:4 * per_set], scratch[4 * per_set:]
        wid = lax.axis_index("s") * n_cores + lax.axis_index("c")
        base = wid * wins_per_worker
        pltpu.sync_copy(idx_hbm.at[pl.ds(base, wins_per_worker)], idx_all)

        def gather_copy(g, st, u):
            return pltpu.make_async_copy(table_hbm.at[idx_all.at[g * per_set + u]], rows_v[st * per_set + u],
                                         gather_sems[st * per_set + u])

        def store_copy(g, st, u):
            return pltpu.make_async_copy(rows_v[st * per_set + u],
                                         out_hbm.at[pl.ds((base + g * per_set + u) * win, win)], store_sems[st])

        def issue(g, st):
            for u in range(per_set):
                gather_copy(g, st, u).start()

        def drain(g, st):
            for u in range(per_set):
                gather_copy(g, st, u).wait()
                store_copy(g, st, u).start()

        def wait_stores(g, st):
            for u in range(per_set):
                store_copy(g, st, u).wait()

        issue(0, 0)

        @pl.loop(0, n_groups, step=2)
        def _(g):
            @pl.when(g > 0)
            def _():
                wait_stores(g - 1, 1)
            issue(g + 1, 1)
            drain(g, 0)
            wait_stores(g, 0)

            @pl.when(g + 2 < n_groups)
            def _():
                issue(g + 2, 0)
            drain(g + 1, 1)

        wait_stores(n_groups - 1, 1)

    return gather(table, pos.reshape(n // win, win))


def _partial_cols(pv, m):
    n = pv.shape[1]
    stacked = jnp.concatenate([pv, jnp.zeros((LANES - MOBA_VROWS, n), F32)], axis=0)
    row = lax.broadcasted_iota(jnp.int32, (LANES, n), 0)
    return jnp.where(row == HEAD_DIM + 1, m, stacked)


def _partial_rows(pv, m):
    return _partial_cols(pv, m).T


def _moba_sparse_kernel(blk_ref, q_ref, k_ref, v_ref, _after_ref, o_ref, *, n_groups):
    h, g = pl.program_id(0), pl.program_id(1)
    first = (h * n_groups + g) * MOBA_GROUP

    @pl.when(blk_ref[first] >= 0)
    def _():
        work = []
        for u in range(MOBA_GROUP):
            blk = jnp.maximum(blk_ref[first + u], 0)
            rows = pl.ds(pl.multiple_of(blk * MOBA_BLOCK, MOBA_BLOCK), MOBA_BLOCK)
            q = q_ref[u * MOBA_PAIR_TILE:(u + 1) * MOBA_PAIR_TILE, :].astype(BF16)
            work.append((blk, _dot_nt(k_ref[0, rows, :], q)))
        for u, (blk, s) in enumerate(work):
            m = jnp.max(s, axis=0, keepdims=True)
            pv = _dot(v_ref[0, blk], jnp.exp(s - m).astype(BF16))
            o_ref[u * MOBA_PAIR_TILE:(u + 1) * MOBA_PAIR_TILE, :] = _partial_rows(pv, m)

    @pl.when(g == n_groups - 1)
    def _():
        lane = lax.broadcasted_iota(jnp.int32, (MOBA_PAIR_TILE, LANES), 1)
        o_ref[(MOBA_GROUP - 1) * MOBA_PAIR_TILE:, :] = jnp.where(lane == HEAD_DIM + 1, NEG, 0.0)


def _moba_sparse(q_sorted, k_pad, vt_aug, tile_blk, n_tiles, after):
    n_heads, s, _ = k_pad.shape
    n_groups = n_tiles // MOBA_GROUP
    rows = pl.BlockSpec((MOBA_GROUP * MOBA_PAIR_TILE, LANES), lambda h, g, blk: (h * n_groups + g, 0))
    grid_spec = pltpu.PrefetchScalarGridSpec(
        num_scalar_prefetch=1, grid=(n_heads, n_groups),
        in_specs=[rows, pl.BlockSpec((1, s, LANES), lambda h, g, blk: (h, 0, 0)),
                  pl.BlockSpec((1, s // MOBA_BLOCK, MOBA_VROWS, MOBA_BLOCK), lambda h, g, blk: (h, 0, 0, 0)),
                  _ORDER_ONLY],
        out_specs=rows)
    return pl.pallas_call(
        functools.partial(_moba_sparse_kernel, n_groups=n_groups),
        grid_spec=grid_spec,
        out_shape=jax.ShapeDtypeStruct(q_sorted.shape, F32),
        compiler_params=_cparams("arbitrary", "arbitrary"),
        name="moba_sparse",
    )(tile_blk, q_sorted, k_pad, vt_aug, after)


def _moba_merge_kernel(qa_ref, qb_ref, ka_ref, kb_ref, va_ref, vb_ref, ga_ref, gb_ref, _after_ref, o_ref):
    ki = lax.broadcasted_iota(jnp.int32, (MOBA_BLOCK, MOBA_BLOCK), 0)
    qi = lax.broadcasted_iota(jnp.int32, (MOBA_BLOCK, MOBA_BLOCK), 1)
    heads = ((qa_ref, ka_ref, va_ref, ga_ref), (qb_ref, kb_ref, vb_ref, gb_ref))
    chains = [(hh, b) for b in range(MOBA_MERGE_BLOCKS) for hh in range(2)]
    rows = lambda b: slice(b * MOBA_BLOCK, (b + 1) * MOBA_BLOCK)
    scores = [_dot_nt(heads[hh][1][0, rows(b), :], heads[hh][0][0, rows(b), :].astype(BF16))
              for hh, b in chains]
    outs = {}
    for (hh, b), s in zip(chains, scores):
        _, _, v_ref, g_ref = heads[hh]
        s = jnp.where(ki <= qi, s, NEG)
        m = jnp.max(s, axis=0, keepdims=True)
        parts = [_partial_cols(_dot(v_ref[0, b], jnp.exp(s - m).astype(BF16)), m)]
        parts += [g_ref[0, r, rows(b), :].T for r in range(MOBA_TOPK)]
        ms = [x[HEAD_DIM + 1:HEAD_DIM + 2, :] for x in parts]
        m_all = functools.reduce(jnp.maximum, ms)
        acc = sum(jnp.exp(mi - m_all) * x for mi, x in zip(ms, parts))
        outs[hh, b] = acc[:HEAD_DIM, :] / acc[HEAD_DIM:HEAD_DIM + 1, :]
    for b in range(MOBA_MERGE_BLOCKS):
        o_ref[rows(b), :] = jnp.concatenate([outs[0, b], outs[1, b]], axis=0).T.astype(BF16)


def _moba_merge(q_rows, k_pad, vt_aug, gathered, after):
    n_heads, s, _ = q_rows.shape
    t = MOBA_MERGE_BLOCKS * MOBA_BLOCK
    blk = lambda par: pl.BlockSpec((1, t, LANES), lambda p, i: (2 * p + par, i, 0))
    vspec = lambda par: pl.BlockSpec((1, MOBA_MERGE_BLOCKS, MOBA_VROWS, MOBA_BLOCK),
                                     lambda p, i: (2 * p + par, i, 0, 0))
    gspec = lambda par: pl.BlockSpec((1, MOBA_TOPK, t, LANES), lambda p, i: (2 * p + par, 0, i, 0))
    return pl.pallas_call(
        _moba_merge_kernel,
        grid=(n_heads // 2, s // t),
        in_specs=[blk(0), blk(1), blk(0), blk(1), vspec(0), vspec(1), gspec(0), gspec(1), _ORDER_ONLY],
        out_specs=pl.BlockSpec((t, LANES), lambda p, i: (i, p)),
        out_shape=jax.ShapeDtypeStruct((s, WIDTH), BF16),
        compiler_params=_cparams("arbitrary", "arbitrary"),
        name="moba_merge",
    )(q_rows, q_rows, k_pad, k_pad, vt_aug, vt_aug, gathered, gathered, after)


def _moba_attention(qc, kc, vc, kmean, overlap_a, overlap_b):
    s = qc.shape[0]
    n_tiles = _moba_tiles_per_head(s)
    q_rows, k_pad, vt_aug, sel, rank, counts_raw = _moba_route(qc, kc, vc, _moba_mean_rows(kmean))
    counts = counts_raw[:, :s // MOBA_BLOCK, 0].astype(jnp.int32)
    pos, tile_blk = _moba_dispatch(sel, rank, counts, n_tiles)
    q_sorted = _sc_scatter_rows(q_rows.reshape(N_HEADS * s, LANES), pos, N_HEADS * n_tiles * MOBA_PAIR_TILE)
    def chain(calls, anchor):
        outs = []
        for call in calls:
            outs.append(call(anchor))
            anchor = outs[-1]
        return outs

    out_a = chain(overlap_a, counts_raw)
    partial = _moba_sparse(q_sorted, k_pad, vt_aug, tile_blk, n_tiles, out_a[-1])
    gathered = _sc_gather_rows(partial, pos).reshape(N_HEADS, MOBA_TOPK, s, LANES)
    out_b = chain(overlap_b, partial)
    return out_a, out_b, _moba_merge(q_rows, k_pad, vt_aug, gathered, out_b[-1])


def _ssd_kernel(z_ref, xbc_ref, dt_ref, cw_ref, cb_ref, dtb_ref, alog_ref, dsk_ref, nw_ref, exp_ref, tri_ref,
                _after_ref, o_ref, state_ref, halo_ref):
    L = SSM_CHUNK

    @pl.when(pl.program_id(0) == 0)
    def _():
        state_ref[...] = jnp.zeros_like(state_ref)
        halo_ref[...] = jnp.zeros_like(halo_ref)

    cur = xbc_ref[...]
    prev8 = halo_ref[...]
    row8 = lax.broadcasted_iota(jnp.int32, (8, SSM_CONV_DIM), 0)
    conv = cb_ref[...] + cur * cw_ref[SSM_CONV - 1:SSM_CONV, :]
    for k in range(1, SSM_CONV):
        rolled = pltpu.roll(cur, k, 0)
        head = jnp.where(row8 < k, pltpu.roll(prev8, k, 0), rolled[0:8, :])
        shifted = jnp.concatenate([head, rolled[8:, :]], axis=0)
        conv = conv + shifted * cw_ref[SSM_CONV - 1 - k:SSM_CONV - k, :]
    halo_ref[...] = cur[L - 8:, :]
    act = _silu(conv)
    xs, bm, cm = act[:, :WIDTH], act[:, WIDTH:WIDTH + 2 * SSM_STATE], act[:, WIDTH + 2 * SSM_STATE:]

    dt = _softplus(dt_ref[...] + dtb_ref[...])
    da = dt * (-jnp.exp(alog_ref[...]))
    cs = _dot_exact_lhs(tri_ref[...], da)
    expand = exp_ref[...]
    cs_b = _dot_exact_rhs(cs, expand)
    dt_b = _dot_exact_rhs(dt, expand)
    cs_last = cs_b[L - 1:L, :]
    xdt = xs * dt_b
    cs_t = cs.T

    ri = lax.broadcasted_iota(jnp.int32, (L, L), 0)
    ci = lax.broadcasted_iota(jnp.int32, (L, L), 1)
    causal = ci <= ri
    lane = lax.broadcasted_iota(jnp.int32, (L, LANES), 1)
    bmb, cmb = bm.astype(BF16), cm.astype(BF16)
    xdtb = xdt.astype(BF16)
    y_pairs = []
    for p in range(N_HEADS // 2):
        g = (2 * p) // (N_HEADS // SSM_GROUPS)
        grp = slice(g * SSM_STATE, (g + 1) * SSM_STATE)
        cbm = _dot_nt(cmb[:, grp], bmb[:, grp])
        x_pair = xdtb[:, p * LANES:(p + 1) * LANES]
        halves = []
        for h in (2 * p, 2 * p + 1):
            seg = cs[:, h:h + 1] - cs_t[h:h + 1, :]
            decay = jnp.exp(jnp.where(causal, seg, NEG))
            halves.append(_dot((cbm * decay).astype(BF16), x_pair))
        y_pairs.append(jnp.where(lane < HEAD_DIM, halves[0], halves[1]))
    y_diag = jnp.concatenate(y_pairs, axis=1)

    state = state_ref[...]
    stb = state.astype(BF16)
    half = WIDTH // SSM_GROUPS
    y_off = jnp.concatenate(
        [_dot(cmb[:, g * SSM_STATE:(g + 1) * SSM_STATE], stb[:, g * half:(g + 1) * half])
         for g in range(SSM_GROUPS)], axis=1) * jnp.exp(cs_b)
    y = y_diag + y_off + xs * dsk_ref[...]

    w = (xdt * jnp.exp(cs_last - cs_b)).astype(BF16)
    bm_t = bm.T.astype(BF16)
    new = jnp.concatenate(
        [_dot(bm_t[g * SSM_STATE:(g + 1) * SSM_STATE, :], w[:, g * half:(g + 1) * half])
         for g in range(SSM_GROUPS)], axis=1)
    state_ref[...] = state * jnp.exp(cs_last) + new

    gt = y * _silu(z_ref[...])
    outs = []
    for g in range(SSM_GROUPS):
        gg = gt[:, g * half:(g + 1) * half]
        outs.append(gg * lax.rsqrt(jnp.mean(gg * gg, axis=-1, keepdims=True) + NORM_EPS))
    o_ref[...] = (jnp.concatenate(outs, axis=1) * nw_ref[...]).astype(BF16)


def _ssd_mixer(z, xbc, dt_raw, conv_w, conv_b, dt_bias, a_log, d_skip, norm_w, expand, tri, after):
    s = z.shape[0]
    L = SSM_CHUNK
    row = lambda n: pl.BlockSpec((L, n), lambda c: (c, 0))
    cst = lambda r, n: pl.BlockSpec((r, n), lambda c: (0, 0))
    return pl.pallas_call(
        _ssd_kernel,
        grid=(s // L,),
        in_specs=[row(WIDTH), row(SSM_CONV_DIM), row(DT_PAD), cst(SSM_CONV, SSM_CONV_DIM),
                  cst(1, SSM_CONV_DIM), cst(1, DT_PAD), cst(1, DT_PAD), cst(1, WIDTH), cst(1, WIDTH),
                  cst(DT_PAD, WIDTH), cst(L, L), _ORDER_ONLY],
        out_specs=row(WIDTH),
        out_shape=jax.ShapeDtypeStruct((s, WIDTH), BF16),
        scratch_shapes=[pltpu.VMEM((SSM_STATE, WIDTH), F32), pltpu.VMEM((8, SSM_CONV_DIM), F32)],
        compiler_params=_cparams("arbitrary"),
        cost_estimate=pl.CostEstimate(
            flops=2 * s * L * (SSM_GROUPS * SSM_STATE + WIDTH + 2 * N_HEADS * LANES) + 4 * s * SSM_STATE * WIDTH,
            transcendentals=s * (L * N_HEADS + SSM_CONV_DIM + 3 * WIDTH),
            bytes_accessed=s * (WIDTH + SSM_CONV_DIM + DT_PAD) * 4 + s * WIDTH * 2),
        name="ssd_mixer",
    )(z, xbc, dt_raw, conv_w, conv_b, dt_bias, a_log, d_skip, norm_w, expand, tri, after)


def _outproj_kernel(x_ref, *refs):
    *y_refs, w_ref, gate_ref, o_ref = refs
    y, row0 = 0.0, 0
    for y_ref in y_refs:
        y = y + _dot(y_ref[...], w_ref[row0:row0 + y_ref.shape[1], :])
        row0 += y_ref.shape[1]
    o_ref[...] = x_ref[...] + gate_ref[...] * y


def _out_projection(x, ys, w_all, gate, layer):
    s, d = x.shape
    tm = 512
    row = lambda n: pl.BlockSpec((tm, n), lambda i: (i, 0))
    assert sum(y.shape[1] for y in ys) == w_all.shape[1]
    return pl.pallas_call(
        _outproj_kernel,
        grid=(s // tm,),
        in_specs=[row(d)] + [row(y.shape[1]) for y in ys]
                 + [pl.BlockSpec((None, w_all.shape[1], d), lambda i: (layer, 0, 0)),
                    pl.BlockSpec((1, d), lambda i: (0, 0))],
        out_specs=row(d),
        out_shape=jax.ShapeDtypeStruct((s, d), F32),
        compiler_params=_cparams("arbitrary"),
        name="out_projection",
    )(x, *ys, w_all, gate)


def _ffn_kernel(x_ref, g_ref, sc_ref, sh_ref, gate_ref, wg_ref, wu_ref, wd_ref, o_ref, *, chunk):
    x = x_ref[...]
    h = _mod_norm(x, g_ref[...], sc_ref[...], sh_ref[...]).astype(BF16)
    hidden = wg_ref.shape[1]
    n_chunks = hidden // chunk
    cols = lambda c: slice(c * chunk, (c + 1) * chunk)
    gate_up = lambda c: (_dot(h, wg_ref[:, cols(c)]), _dot(h, wu_ref[:, cols(c)]))
    acc = jnp.zeros(x.shape, F32)
    nxt = gate_up(0)
    for c in range(n_chunks):
        g, u = nxt
        if c + 1 < n_chunks:
            nxt = gate_up(c + 1)
        acc = acc + _dot((_silu(g) * u).astype(BF16), wd_ref[cols(c), :])
    o_ref[...] = x + gate_ref[...] * acc


def _ffn(x, g, sc, sh, gate, w_gate, w_up, w_down, layer):
    s, d = x.shape
    hidden = w_gate.shape[2]
    tm = 512
    row = pl.BlockSpec((tm, d), lambda i: (i, 0))
    vec = pl.BlockSpec((1, d), lambda i: (0, 0))
    return pl.pallas_call(
        functools.partial(_ffn_kernel, chunk=256),
        grid=(s // tm,),
        in_specs=[row, vec, vec, vec, vec,
                  _resident((None, d, hidden), lambda i: (layer, 0, 0)),
                  _resident((None, d, hidden), lambda i: (layer, 0, 0)),
                  _resident((None, hidden, d), lambda i: (layer, 0, 0))],
        out_specs=row,
        out_shape=jax.ShapeDtypeStruct((s, d), F32),
        compiler_params=_cparams("arbitrary"),
        name="ffn",
    )(x, g, sc, sh, gate, w_gate, w_up, w_down)


def _rope_tables(seq):
    pos = jnp.arange(seq, dtype=F32)
    inv = ROPE_THETA ** (-jnp.arange(0, ROPE_DIM, 2, dtype=F32) / ROPE_DIM)
    ang = pos[:, None] * inv[None, :]
    cos, sin = jnp.cos(ang), jnp.sin(ang)
    ones = jnp.ones((seq, HEAD_DIM - ROPE_DIM), F32)
    zeros = jnp.zeros((seq, HEAD_DIM - ROPE_DIM), F32)
    zh = jnp.zeros((seq, ROPE_HALF), F32)
    cos_h = jnp.concatenate([cos, cos, ones], axis=1)
    sin_lo = jnp.concatenate([-sin, zh, zeros], axis=1)
    sin_hi = jnp.concatenate([zh, sin, zeros], axis=1)
    two = lambda t: jnp.concatenate([t, t], axis=1)
    return two(cos_h), two(sin_lo), two(sin_hi)


def _moba_mean_rows(kmean):
    nb = kmean.shape[0]
    km = kmean.reshape(nb, N_HEADS, HEAD_DIM).transpose(1, 0, 2)
    return jnp.pad(km, ((0, 0), (0, MOBA_MAX_BLOCKS - nb), (0, LANES - HEAD_DIM)))


def kernel(x, c, ada_w, ada_b, norm_mix, w_in, qn_swa, kn_swa, qn_moba, kn_moba, conv_w, conv_b, dt_bias,
           a_log, d_skip, ssm_norm, w_out, norm_ffn, w_gate, w_up, w_down):
    batch, seq, d = x.shape
    depth = ada_w.shape[0]
    assert batch == 1 and d == D_MODEL
    assert seq % SWA_SPAN == 0 and seq // MOBA_BLOCK <= MOBA_MAX_BLOCKS
    xs = x.reshape(seq, d)

    mod = _ada_modulation(c, ada_w, ada_b)
    rope_tabs = _rope_tables(seq)
    lane_head = jnp.arange(LANES) // HEAD_DIM
    hsum = (lane_head[:, None] == lane_head[None, :]).astype(BF16)
    expand = (jnp.arange(DT_PAD)[:, None] == (jnp.arange(WIDTH) // HEAD_DIM)[None, :]).astype(BF16)
    tri = (jnp.arange(SSM_CHUNK)[:, None] >= jnp.arange(SSM_CHUNK)[None, :]).astype(BF16)
    pad8 = lambda v: jnp.pad(v, (0, DT_PAD - N_HEADS)).reshape(1, DT_PAD)
    two = lambda g: jnp.concatenate([g, g]).reshape(1, LANES)

    w_in_b = jnp.pad(w_in, ((0, 0), (0, 0), (0, IN_PROJ_PAD - w_in.shape[2]))).astype(BF16)
    w_out_b, w_gate_b, w_up_b, w_down_b = (w.astype(BF16) for w in (w_out, w_gate, w_up, w_down))

    for l in range(depth):
        shift_m, scale_m, gate_m, shift_f, scale_f, gate_f = [mod[l, :, i * d:(i + 1) * d] for i in range(6)]
        gains = [two(g[l]) for g in (qn_swa, kn_swa, qn_moba, kn_moba)]
        qa, ka, va, qc, kc, vc, z, xbc, dt_raw, kmean = _in_projection(
            xs, norm_mix[l].reshape(1, d), scale_m, shift_m, w_in_b, l, gains, hsum, rope_tabs)

        n_pairs = WIDTH // LANES
        first = n_pairs - 1
        dilated_lo = functools.partial(_dilated_attention, qa, ka, va, 0, first)
        dilated_hi = functools.partial(_dilated_attention, qa, ka, va, first, n_pairs - first)
        ssd = functools.partial(
            _ssd_mixer, z, xbc, dt_raw, conv_w[l], conv_b[l].reshape(1, -1), pad8(dt_bias[l]), pad8(a_log[l]),
            jnp.repeat(d_skip[l], HEAD_DIM).reshape(1, WIDTH), ssm_norm[l].reshape(1, WIDTH), expand, tri)
        (y_a_lo,), (y_a_hi, y_b), y_c = _moba_attention(qc, kc, vc, kmean, (dilated_lo,), (dilated_hi, ssd))

        xs = _out_projection(xs, (y_a_lo, y_a_hi, y_b, y_c), w_out_b, gate_m, l)
        xs = _ffn(xs, norm_ffn[l].reshape(1, d), scale_f, shift_f, gate_f,
                  w_gate_b, w_up_b, w_down_b, l)
    return xs.reshape(batch, seq, d)
```

```python
import functools

import jax
import jax.numpy as jnp
from jax import lax
from jax.experimental import pallas as pl
from jax.experimental.pallas import tpu as pltpu
from jax.experimental.pallas import tpu_sc as plsc

F32 = jnp.float32
BF16 = jnp.bfloat16

D_MODEL = 1024
HEAD_DIM = 64
N_HEADS = 8
WIDTH = N_HEADS * HEAD_DIM
ATTN_SCALE = HEAD_DIM ** -0.5
NORM_EPS = 1e-6
NEG = -1e30

ROPE_THETA = 500000.0
ROPE_DIM = HEAD_DIM // 4
ROPE_HALF = ROPE_DIM // 2

DILATIONS = (1, 4, 16)
SWA_BLOCK = 128
SWA_SPAN = DILATIONS[-1] * SWA_BLOCK
SWA_GROUP = 4

SSM_STATE = 128
SSM_GROUPS = 2
SSM_CONV = 4
SSM_CHUNK = 256
SSM_CONV_DIM = WIDTH + 2 * SSM_GROUPS * SSM_STATE

MOBA_BLOCK = 256
MOBA_TOPK = 3
MOBA_PAIR_TILE = 128
MOBA_GROUP = 32
SC_INDEX_WINDOW = 128
MOBA_MAX_BLOCKS = 64
MOBA_MERGE_BLOCKS = 4
MOBA_VROWS = HEAD_DIM + 16

LANES = 128
DT_PAD = LANES
IN_PROJ_PAD = 6 * WIDTH + WIDTH + SSM_CONV_DIM + DT_PAD
OFF_A, OFF_C, OFF_Z, OFF_XBC, OFF_DT = 0, 3 * WIDTH, 6 * WIDTH, 7 * WIDTH, 7 * WIDTH + SSM_CONV_DIM

VMEM_LIMIT = 56 * 1024 * 1024
CAST_BLOCK_BYTES = 6 * 1024 * 1024


_ORDER_ONLY = pl.BlockSpec(memory_space=pl.ANY)


def _stream_rows(tm, d):
    return pl.BlockSpec((None, tm, d), lambda i: (0, i, 0))


def _resident(block_shape, index_map):
    return pl.BlockSpec(block_shape, index_map, pipeline_mode=pl.Buffered(1))


def _cparams(*sem):
    return pltpu.CompilerParams(dimension_semantics=sem, vmem_limit_bytes=VMEM_LIMIT)


def _split3(a):
    hi = a.astype(BF16)
    r1 = a - hi.astype(F32)
    mid = r1.astype(BF16)
    lo = (r1 - mid.astype(F32)).astype(BF16)
    return hi, mid, lo


def _dot(a, b):
    return jnp.dot(a, b, preferred_element_type=F32)


def _dot_nt(a, b):
    return lax.dot_general(a, b, (((1,), (1,)), ((), ())), preferred_element_type=F32)


def _dot_exact_rhs(a, b_bf16):
    hi, mid, lo = _split3(a)
    return _dot(hi, b_bf16) + _dot(mid, b_bf16) + _dot(lo, b_bf16)


def _dot_exact_lhs(a_bf16, b):
    hi, mid, lo = _split3(b)
    return _dot(a_bf16, hi) + _dot(a_bf16, mid) + _dot(a_bf16, lo)


def _dot_f32(a, b, dot=_dot):
    ah, am, al = _split3(a)
    bh, bm, bl = _split3(b)
    return (dot(ah, bh) + (dot(ah, bm) + dot(am, bh))
            + (dot(ah, bl) + dot(am, bm) + dot(al, bh)))


def _dot_bf16x3(a, b, dot=_dot):
    ah = a.astype(BF16)
    al = (a - ah.astype(F32)).astype(BF16)
    bh = b.astype(BF16)
    bl = (b - bh.astype(F32)).astype(BF16)
    return dot(ah, bh) + (dot(ah, bl) + dot(al, bh))


def _silu(x):
    return x / (1.0 + jnp.exp(-x))


def _softplus(x):
    return jnp.maximum(x, 0.0) + jnp.log1p(jnp.exp(-jnp.abs(x)))


def _cast_kernel(w_ref, o_ref):
    cols = w_ref.shape[1]
    o_ref[:, :cols] = w_ref[...].astype(BF16)
    if o_ref.shape[1] > cols:
        o_ref[:, cols:] = jnp.zeros((o_ref.shape[0], o_ref.shape[1] - cols), BF16)


def _cast_bf16(w, pad_cols=None):
    depth, rows, cols = w.shape
    out_cols = pad_cols or cols
    tr = max(t for t in range(8, rows + 1, 8) if rows % t == 0 and t * cols * 4 <= CAST_BLOCK_BYTES)
    return pl.pallas_call(
        _cast_kernel,
        grid=(depth, rows // tr),
        in_specs=[pl.BlockSpec((None, tr, cols), lambda l, i: (l, i, 0))],
        out_specs=pl.BlockSpec((None, tr, out_cols), lambda l, i: (l, i, 0)),
        out_shape=jax.ShapeDtypeStruct((depth, rows, out_cols), BF16),
        compiler_params=_cparams("arbitrary", "arbitrary"),
        name="cast_weights",
    )(w)


def _ada_kernel(c_ref, w_ref, b_ref, o_ref):
    c = c_ref[...]
    o_ref[0] = _dot_bf16x3(_silu(c), w_ref[0]) + b_ref[0]


def _ada_modulation(c, ada_w, ada_b):
    depth, d, n = ada_w.shape
    tn = 1024
    c8 = jnp.broadcast_to(c, (8, d))
    out = pl.pallas_call(
        _ada_kernel,
        grid=(depth, n // tn),
        in_specs=[pl.BlockSpec((8, d), lambda l, j: (0, 0)),
                  pl.BlockSpec((1, d, tn), lambda l, j: (l, 0, j)),
                  pl.BlockSpec((1, 1, tn), lambda l, j: (l, 0, j))],
        out_specs=pl.BlockSpec((1, 8, tn), lambda l, j: (l, 0, j)),
        out_shape=jax.ShapeDtypeStruct((depth, 8, n), F32),
        compiler_params=_cparams("arbitrary", "arbitrary"),
        name="ada_modulation",
    )(c8, ada_w, ada_b.reshape(depth, 1, n))
    return out[:, 0:1, :]


def _mod_norm(x, g, sc, sh):
    var = jnp.mean(x * x, axis=-1, keepdims=True)
    return (x * lax.rsqrt(var + NORM_EPS) * g) * (1.0 + sc) + sh


def _inproj_kernel(x_ref, g_ref, sc_ref, sh_ref, w_ref, gqa_ref, gka_ref, gqc_ref, gkc_ref, hsum_ref,
                   cos_ref, slo_ref, shi_ref,
                   qa_ref, ka_ref, va_ref, qc_ref, kc_ref, vc_ref, z_ref, xbc_ref, dt_ref, km_ref):
    h = _mod_norm(x_ref[...], g_ref[...], sc_ref[...], sh_ref[...]).astype(BF16)
    proj = lambda off, n=WIDTH: _dot(h, w_ref[:, off:off + n])
    raw = [proj(OFF_A), proj(OFF_A + WIDTH), proj(OFF_C), proj(OFF_C + WIDTH)]
    va_ref[...] = proj(OFF_A + 2 * WIDTH)
    vc_ref[...] = proj(OFF_C + 2 * WIDTH)
    z_ref[...] = proj(OFF_Z)

    hsum = hsum_ref[...]
    cos_t, slo, shi = cos_ref[...], slo_ref[...], shi_ref[...]
    rope = lambda x, gain_ref: _head_norm_rope(x, gain_ref[...], hsum, cos_t, slo, shi)
    n_blocks = x_ref.shape[0] // MOBA_BLOCK
    for p in range(WIDTH // LANES):
        lo, hi = p * LANES, (p + 1) * LANES
        qa_ref[:, lo:hi] = rope(raw[0][:, lo:hi], gqa_ref) * ATTN_SCALE
        ka_ref[:, lo:hi] = rope(raw[1][:, lo:hi], gka_ref)
        qc_ref[:, lo:hi] = rope(raw[2][:, lo:hi], gqc_ref)
        kc = rope(raw[3][:, lo:hi], gkc_ref)
        kc_ref[:, lo:hi] = kc
        for b in range(n_blocks):
            km_ref[b, :, lo:hi] = jnp.mean(kc[b * MOBA_BLOCK:(b + 1) * MOBA_BLOCK, :], axis=0, keepdims=True)
    xbc_ref[...] = proj(OFF_XBC, SSM_CONV_DIM)
    dt_ref[...] = proj(OFF_DT, DT_PAD)


def _in_projection(x, g, sc, sh, w_all, layer, gains, hsum, rope_tabs):
    _, s, d = x.shape
    tm = 512
    assert tm % MOBA_BLOCK == 0
    row = lambda n: pl.BlockSpec((tm, n), lambda i: (i, 0))
    vec = pl.BlockSpec((1, d), lambda i: (0, 0))
    cst = lambda r, n: pl.BlockSpec((r, n), lambda i: (0, 0))
    widths = [WIDTH] * 7 + [SSM_CONV_DIM, DT_PAD]
    return pl.pallas_call(
        _inproj_kernel,
        grid=(s // tm,),
        in_specs=[_stream_rows(tm, d), vec, vec, vec, _resident((None, d, IN_PROJ_PAD), lambda i: (layer, 0, 0))]
                 + [cst(1, LANES)] * 4 + [cst(LANES, LANES)] + [row(LANES)] * 3,
        out_specs=[row(n) for n in widths]
                  + [pl.BlockSpec((tm // MOBA_BLOCK, 1, WIDTH), lambda i: (i, 0, 0))],
        out_shape=[jax.ShapeDtypeStruct((s, n), F32) for n in widths]
                  + [jax.ShapeDtypeStruct((s // MOBA_BLOCK, 1, WIDTH), F32)],
        compiler_params=_cparams("arbitrary"),
        name="in_projection",
    )(x, g, sc, sh, w_all, *gains, hsum, *rope_tabs)


def _head_norm_rope(x, gain, hsum, cos_t, sin_lo, sin_hi):
    sq = x * x
    sq_hi = sq.astype(BF16)
    ss = _dot(sq_hi, hsum) + _dot((sq - sq_hi.astype(F32)).astype(BF16), hsum)
    y = x * lax.rsqrt(ss * (1.0 / HEAD_DIM) + NORM_EPS) * gain
    return (y * cos_t + pltpu.roll(y, ROPE_HALF, 1) * sin_hi
            + pltpu.roll(y, LANES - ROPE_HALF, 1) * sin_lo)


def _dilated_kernel(q_ref, kc_ref, kp_ref, vc_ref, vp_ref, _after_ref, o_ref, m_s, l_s, a_s, q4_s, k4_s, v4_s):
    has_prev = pl.program_id(1) > 0
    lane = lax.broadcasted_iota(jnp.int32, (SWA_BLOCK, LANES), 1)
    head_a = lane < HEAD_DIM
    qi = lax.broadcasted_iota(jnp.int32, (SWA_BLOCK, 2 * SWA_BLOCK), 0)
    ki = lax.broadcasted_iota(jnp.int32, (SWA_BLOCK, 2 * SWA_BLOCK), 1)
    band = (ki >= qi) & (ki <= qi + SWA_BLOCK)
    band_edge = band & ((ki >= SWA_BLOCK) | has_prev)

    quarter = SWA_SPAN // 4
    for r4 in range(4):
        src = pl.ds(r4, quarter, 4)
        q4_s[r4 * quarter:(r4 + 1) * quarter, :] = q_ref[src, :]
        for dst, prev, cur in ((k4_s, kp_ref, kc_ref), (v4_s, vp_ref, vc_ref)):
            dst[2 * r4 * quarter:(2 * r4 + 1) * quarter, :] = prev[src, :]
            dst[(2 * r4 + 1) * quarter:(2 * r4 + 2) * quarter, :] = cur[src, :]

    def unit_d16(a, r4):
        rows = pl.ds(4 * a + r4, SWA_BLOCK, 16)
        q = q4_s[pl.ds(r4 * quarter + a, SWA_BLOCK, 4), :]
        prev = pl.ds(2 * r4 * quarter + a, SWA_BLOCK, 4)
        cur = pl.ds((2 * r4 + 1) * quarter + a, SWA_BLOCK, 4)
        k_cat = jnp.concatenate([k4_s[prev, :], k4_s[cur, :]], axis=0)
        v_cat = jnp.concatenate([v4_s[prev, :], v4_s[cur, :]], axis=0)
        return rows, q, k_cat, v_cat, band_edge

    def unit_d4(r, nb):
        rows = pl.ds(4 * SWA_BLOCK * nb + r, SWA_BLOCK, 4)
        q = q4_s[pl.ds(r * quarter + SWA_BLOCK * nb, SWA_BLOCK), :]
        keys = pl.ds((2 * r + 1) * quarter + SWA_BLOCK * (nb - 1), 2 * SWA_BLOCK)
        return rows, q, k4_s[keys, :], v4_s[keys, :], (band_edge if nb == 0 else band)

    def unit_d1(b, edge=False):
        rows = pl.ds(SWA_BLOCK * b, SWA_BLOCK)
        if edge:
            prev = pl.ds(SWA_SPAN - SWA_BLOCK, SWA_BLOCK)
            k_cat = jnp.concatenate([kp_ref[prev, :], kc_ref[rows, :]], axis=0)
            v_cat = jnp.concatenate([vp_ref[prev, :], vc_ref[rows, :]], axis=0)
            return rows, q_ref[rows, :], k_cat, v_cat, band_edge
        keys = pl.ds(SWA_BLOCK * (b - 1), 2 * SWA_BLOCK)
        return rows, q_ref[rows, :], kc_ref[keys, :], vc_ref[keys, :], band

    def attend(units, mode):
        scores = []
        for _, q, k_cat, _, mask in units:
            kb = k_cat.astype(BF16)
            scores.append([jnp.where(mask, _dot_nt(jnp.where(sel, q, 0.0).astype(BF16), kb), NEG)
                           for sel in (head_a, ~head_a)])
        probs = []
        for pair in scores:
            stats = []
            for s in pair:
                m_loc = jnp.max(s, axis=-1, keepdims=True)
                p = jnp.exp(s - m_loc)
                stats.append((m_loc, jnp.sum(p, axis=-1, keepdims=True), p.astype(BF16)))
            probs.append(stats)
        for (rows, _, _, v_cat, _), ((m_a, s_a, p_a), (m_b, s_b, p_b)) in zip(units, probs):
            vb = v_cat.astype(BF16)
            m_new = jnp.where(head_a, m_a, m_b)
            l_new = jnp.where(head_a, s_a, s_b)
            a_new = jnp.where(head_a, _dot(p_a, vb), _dot(p_b, vb))
            if mode != "init":
                m_in, m_loc = m_s[rows, :], m_new
                m_new = jnp.maximum(m_in, m_loc)
                alpha, beta = jnp.exp(m_in - m_new), jnp.exp(m_loc - m_new)
                l_new = alpha * l_s[rows, :] + beta * l_new
                a_new = alpha * a_s[rows, :] + beta * a_new
            if mode == "final":
                o_ref[rows, :] = (a_new / l_new).astype(BF16)
            else:
                m_s[rows, :] = m_new
                l_s[rows, :] = l_new
                a_s[rows, :] = a_new

    g = SWA_GROUP
    assert DILATIONS == (1, 4, 16) and g == 4 and SWA_SPAN // (4 * SWA_BLOCK) == g

    def body16(a, carry):
        attend([unit_d16(a, r4) for r4 in range(g)], "init")
        return carry
    lax.fori_loop(0, 16 // g, body16, 0)

    def body4(r, carry):
        attend([unit_d4(r, nb) for nb in range(g)], "fold")
        return carry
    lax.fori_loop(0, 4, body4, 0)

    attend([unit_d1(b, edge=(b == 0)) for b in range(g)], "final")

    def body1(i, carry):
        attend([unit_d1(i * g + j) for j in range(g)], "final")
        return carry
    lax.fori_loop(1, SWA_SPAN // SWA_BLOCK // g, body1, 0)


def _dilated_attention(q, k, v, pair0, n_pairs, after):
    s = q.shape[0]
    cur = lambda off: pl.BlockSpec((SWA_SPAN, LANES), lambda p, n: (n, off + p))
    prev = lambda off: pl.BlockSpec((SWA_SPAN, LANES), lambda p, n: (jnp.maximum(n - 1, 0), off + p))
    width = n_pairs * LANES
    return pl.pallas_call(
        _dilated_kernel,
        grid=(n_pairs, s // SWA_SPAN),
        in_specs=[cur(pair0), cur(pair0), prev(pair0), cur(pair0), prev(pair0), _ORDER_ONLY],
        out_specs=cur(0),
        out_shape=jax.ShapeDtypeStruct((s, width), BF16),
        scratch_shapes=[pltpu.VMEM((SWA_SPAN, LANES), F32)] * 4 + [pltpu.VMEM((2 * SWA_SPAN, LANES), F32)] * 2,
        compiler_params=_cparams("arbitrary", "arbitrary"),
        cost_estimate=pl.CostEstimate(
            flops=len(DILATIONS) * 4 * s * 2 * SWA_BLOCK * width,
            transcendentals=len(DILATIONS) * s * 2 * SWA_BLOCK * 2 * n_pairs,
            bytes_accessed=5 * s * width * 4 + s * width * 2),
        name="dilated_attention",
    )(q, k, k, v, v, after)


def _moba_route_kernel(q_ref, k_ref, v_ref, km_ref, sut_ref, ones_ref,
                       qrow_ref, kpad_ref, vt_ref, sel_ref, rank_ref, cnt_ref, base_ref):
    i = pl.program_id(0)

    @pl.when(i == 0)
    def _():
        base_ref[...] = jnp.zeros_like(base_ref)

    lane = lax.broadcasted_iota(jnp.int32, (MOBA_BLOCK, LANES), 1)
    is_feat = lane < HEAD_DIM
    row = lax.broadcasted_iota(jnp.int32, (MOBA_MAX_BLOCKS, MOBA_BLOCK), 0)
    past = row < i
    ones_rows = jnp.where(
        lax.broadcasted_iota(jnp.int32, (MOBA_VROWS - HEAD_DIM, MOBA_BLOCK), 0) == 0, 1.0, 0.0)
    ninf = float("-inf")
    gates = []
    for h in range(N_HEADS):
        cols = slice((h // 2) * LANES, (h // 2 + 1) * LANES)
        q, k = q_ref[:, cols], k_ref[:, cols]
        v_t = v_ref[:, cols].T
        if h % 2:
            q, k = pltpu.roll(q, HEAD_DIM, 1), pltpu.roll(k, HEAD_DIM, 1)
            v_t = v_t[HEAD_DIM:, :]
        else:
            v_t = v_t[:HEAD_DIM, :]
        qrow_ref[h] = jnp.where(is_feat, q * ATTN_SCALE, 0.0)
        kpad_ref[h] = jnp.where(is_feat, k, 0.0).astype(BF16)
        vt_ref[h, 0] = jnp.concatenate([v_t, ones_rows], axis=0).astype(BF16)
        gates.append(_dot_bf16x3(km_ref[h], jnp.where(is_feat, q, 0.0), dot=_dot_nt))
    for h, gate_t in enumerate(gates):
        cand = jnp.where(past, gate_t, ninf)
        picks = []
        for _ in range(MOBA_TOPK):
            best = jnp.max(cand, axis=0, keepdims=True)
            idx = jnp.min(jnp.where(cand == best, row, 2 * LANES), axis=0, keepdims=True)
            ok = best > ninf
            pick = (row == idx) & ok
            cand = jnp.where(pick, ninf, cand)
            picks.append((idx, ok, pick))
        chosen = jnp.where(picks[0][2] | picks[1][2] | picks[2][2], 1.0, 0.0).astype(BF16)
        base = base_ref[h]
        rank_full = base + _dot(chosen, sut_ref[...])
        base_ref[h] = base + _dot(chosen, ones_ref[...])
        for r, (idx, ok, pick) in enumerate(picks):
            rank = jnp.sum(jnp.where(pick, rank_full, 0.0), axis=0, keepdims=True)
            sel_ref[h, r:r + 1, :] = jnp.where(ok, idx, -1)
            rank_ref[h, r:r + 1, :] = rank.astype(jnp.int32)

    @pl.when(i == pl.num_programs(0) - 1)
    def _():
        cnt_ref[...] = base_ref[...]


def _moba_route(qc, kc, vc, km_rows):
    s = qc.shape[0]
    nb = s // MOBA_BLOCK
    row = pl.BlockSpec((MOBA_BLOCK, WIDTH), lambda i: (i, 0))
    per_head = lambda dt: (pl.BlockSpec((N_HEADS, MOBA_BLOCK, LANES), lambda i: (0, i, 0)),
                           jax.ShapeDtypeStruct((N_HEADS, s, LANES), dt))
    picks = (pl.BlockSpec((N_HEADS, MOBA_TOPK, MOBA_BLOCK), lambda i: (0, 0, i)),
             jax.ShapeDtypeStruct((N_HEADS, MOBA_TOPK, s), jnp.int32))
    outs = [per_head(F32), per_head(BF16),
            (pl.BlockSpec((N_HEADS, 1, MOBA_VROWS, MOBA_BLOCK), lambda i: (0, i, 0, 0)),
             jax.ShapeDtypeStruct((N_HEADS, nb, MOBA_VROWS, MOBA_BLOCK), BF16)),
            picks, picks,
            (pl.BlockSpec((N_HEADS, MOBA_MAX_BLOCKS, MOBA_BLOCK), lambda i: (0, 0, 0)),
             jax.ShapeDtypeStruct((N_HEADS, MOBA_MAX_BLOCKS, MOBA_BLOCK), F32))]
    qi = jnp.arange(MOBA_BLOCK)
    strict_upper = (qi[:, None] < qi[None, :]).astype(BF16)
    all_ones = jnp.ones((MOBA_BLOCK, MOBA_BLOCK), BF16)
    sq = pl.BlockSpec((MOBA_BLOCK, MOBA_BLOCK), lambda i: (0, 0))
    return pl.pallas_call(
        _moba_route_kernel,
        grid=(nb,),
        in_specs=[row, row, row,
                  pl.BlockSpec((N_HEADS, MOBA_MAX_BLOCKS, LANES), lambda i: (0, 0, 0)), sq, sq],
        out_specs=[o[0] for o in outs],
        out_shape=[o[1] for o in outs],
        scratch_shapes=[pltpu.VMEM((N_HEADS, MOBA_MAX_BLOCKS, MOBA_BLOCK), F32)],
        compiler_params=_cparams("arbitrary"),
        name="moba_route",
    )(qc, kc, vc, km_rows, strict_upper, all_ones)


def _moba_tiles_per_head(seq):
    tiles = MOBA_TOPK * seq // MOBA_PAIR_TILE + seq // MOBA_BLOCK
    return (tiles // MOBA_GROUP + 1) * MOBA_GROUP


def _moba_dispatch(sel, rank, counts, n_tiles):
    n_heads, nb = counts.shape
    padded = (counts + MOBA_PAIR_TILE - 1) // MOBA_PAIR_TILE * MOBA_PAIR_TILE
    ends = jnp.cumsum(padded, axis=1)
    offs = ends - padded
    rows_per_head = n_tiles * MOBA_PAIR_TILE
    head_base = (jnp.arange(n_heads, dtype=jnp.int32) * rows_per_head)[:, None, None]
    blocks = jnp.arange(nb, dtype=jnp.int32)
    off_sel = jnp.sum(jnp.where(sel[..., None] == blocks, offs[:, None, None, :], 0), axis=-1)
    null_row = rows_per_head - MOBA_PAIR_TILE + jnp.arange(sel.shape[-1], dtype=jnp.int32) % MOBA_PAIR_TILE
    pos = head_base + jnp.where(sel >= 0, off_sel + rank, null_row)
    tile_start = jnp.arange(n_tiles, dtype=jnp.int32) * MOBA_PAIR_TILE
    blk = jnp.sum(ends[:, None, :] <= tile_start[None, :, None], axis=2)
    tile_blk = jnp.where(tile_start[None, :] < ends[:, -1:], blk, -1)
    return pos.reshape(-1).astype(jnp.int32), tile_blk.reshape(-1).astype(jnp.int32)


def _sc_workers():
    info = plsc.get_sparse_core_info()
    return info.num_cores, info.num_cores * info.num_subcores


def _sc_scatter_rows(rows, pos, n_out):
    n_src = rows.shape[0]
    seq = n_src // N_HEADS
    n_cores, n_workers = _sc_workers()
    win = SC_INDEX_WINDOW
    wins_per_worker = n_src // win // n_workers
    wins_per_head = seq // win
    assert n_src % (win * n_workers) == 0
    mesh = plsc.VectorSubcoreMesh(core_axis_name="c", subcore_axis_name="s")

    per_set = 2
    n_groups = wins_per_worker // per_set
    assert wins_per_worker % (2 * per_set) == 0 and wins_per_head % wins_per_worker == 0

    @functools.partial(
        pl.kernel, mesh=mesh, out_type=jax.ShapeDtypeStruct((n_out, LANES), F32), name="moba_scatter_rows",
        cost_estimate=pl.CostEstimate(flops=0, transcendentals=0,
                                      bytes_accessed=(1 + MOBA_TOPK) * n_src * LANES * 4 + pos.size * 4),
        scratch_types=[pltpu.VMEM((wins_per_worker, win), jnp.int32)] * MOBA_TOPK
                      + [pltpu.VMEM((win, LANES), F32)] * (2 * per_set)
                      + [pltpu.SemaphoreType.DMA] * (2 * per_set + 2))
    def scatter(src_hbm, idx_hbm, out_hbm, *scratch):
        idx_all = scratch[:MOBA_TOPK]
        rows_v = scratch[MOBA_TOPK:MOBA_TOPK + 2 * per_set]
        load_sems, scat_sems = scratch[-2 * per_set - 2:-2], scratch[-2:]
        wid = lax.axis_index("s") * n_cores + lax.axis_index("c")
        base = wid * wins_per_worker
        h, qw0 = base // wins_per_head, base % wins_per_head
        for r in range(MOBA_TOPK):
            pltpu.sync_copy(idx_hbm.at[pl.ds((h * MOBA_TOPK + r) * wins_per_head + qw0, wins_per_worker)], idx_all[r])

        def load_copy(g, st, u):
            return pltpu.make_async_copy(src_hbm.at[pl.ds((base + g * per_set + u) * win, win)],
                                         rows_v[st * per_set + u], load_sems[st * per_set + u])

        def scat_copy(g, st, u, r):
            return pltpu.make_async_copy(rows_v[st * per_set + u], out_hbm.at[idx_all[r].at[g * per_set + u]],
                                         scat_sems[st])

        def issue(g, st):
            for u in range(per_set):
                load_copy(g, st, u).start()

        def drain(g, st):
            for u in range(per_set):
                load_copy(g, st, u).wait()
                for r in range(MOBA_TOPK):
                    scat_copy(g, st, u, r).start()

        def wait_scatters(g, st):
            for u in range(per_set):
                for r in range(MOBA_TOPK):
                    scat_copy(g, st, u, r).wait()

        issue(0, 0)

        @pl.loop(0, n_groups, step=2)
        def _(g):
            @pl.when(g > 0)
            def _():
                wait_scatters(g - 1, 1)
            issue(g + 1, 1)
            drain(g, 0)
            wait_scatters(g, 0)

            @pl.when(g + 2 < n_groups)
            def _():
                issue(g + 2, 0)
            drain(g + 1, 1)

        wait_scatters(n_groups - 1, 1)

    return scatter(rows, pos.reshape(pos.shape[0] // win, win))


def _sc_gather_rows(table, pos):
    n = pos.shape[0]
    n_cores, n_workers = _sc_workers()
    win = SC_INDEX_WINDOW
    wins_per_worker = n // win // n_workers
    assert n % (win * n_workers) == 0
    mesh = plsc.VectorSubcoreMesh(core_axis_name="c", subcore_axis_name="s")

    per_set = 2
    n_groups = wins_per_worker // per_set
    assert wins_per_worker % (2 * per_set) == 0

    @functools.partial(
        pl.kernel, mesh=mesh, out_type=jax.ShapeDtypeStruct((n, LANES), F32), name="moba_gather_rows",
        cost_estimate=pl.CostEstimate(flops=0, transcendentals=0, bytes_accessed=2 * n * LANES * 4 + n * 4),
        scratch_types=[pltpu.VMEM((wins_per_worker, win), jnp.int32)] + [pltpu.VMEM((win, LANES), F32)] * (2 * per_set)
                      + [pltpu.SemaphoreType.DMA] * (2 * per_set + 2))
    def gather(table_hbm, idx_hbm, out_hbm, idx_all, *scratch):
        rows_v = scratch[:2 * per_set]
        gather_sems, store_sems = scratch[2 * per_set:4 * per_set], scratch[4 * per_set:]
        wid = lax.axis_index("s") * n_cores + lax.axis_index("c")
        base = wid * wins_per_worker
        pltpu.sync_copy(idx_hbm.at[pl.ds(base, wins_per_worker)], idx_all)

        def gather_copy(g, st, u):
            return pltpu.make_async_copy(table_hbm.at[idx_all.at[g * per_set + u]], rows_v[st * per_set + u],
                                         gather_sems[st * per_set + u])

        def store_copy(g, st, u):
            return pltpu.make_async_copy(rows_v[st * per_set + u],
                                         out_hbm.at[pl.ds((base + g * per_set + u) * win, win)], store_sems[st])

        def issue(g, st):
            for u in range(per_set):
                gather_copy(g, st, u).start()

        def drain(g, st):
            for u in range(per_set):
                gather_copy(g, st, u).wait()
                store_copy(g, st, u).start()

        def wait_stores(g, st):
            for u in range(per_set):
                store_copy(g, st, u).wait()

        issue(0, 0)

        @pl.loop(0, n_groups, step=2)
        def _(g):
            @pl.when(g > 0)
            def _():
                wait_stores(g - 1, 1)
            issue(g + 1, 1)
            drain(g, 0)
            wait_stores(g, 0)

            @pl.when(g + 2 < n_groups)
            def _():
                issue(g + 2, 0)
            drain(g + 1, 1)

        wait_stores(n_groups - 1, 1)

    return gather(table, pos.reshape(n // win, win))


def _partial_cols(pv, m):
    n = pv.shape[1]
    stacked = jnp.concatenate([pv, jnp.zeros((LANES - MOBA_VROWS, n), F32)], axis=0)
    row = lax.broadcasted_iota(jnp.int32, (LANES, n), 0)
    return jnp.where(row == HEAD_DIM + 1, m, stacked)


def _partial_rows(pv, m):
    return _partial_cols(pv, m).T


def _moba_sparse_kernel(blk_ref, q_ref, k_ref, v_ref, _after_ref, o_ref, *, n_groups):
    h, g = pl.program_id(0), pl.program_id(1)
    first = (h * n_groups + g) * MOBA_GROUP

    @pl.when(blk_ref[first] >= 0)
    def _():
        work = []
        for u in range(MOBA_GROUP):
            blk = jnp.maximum(blk_ref[first + u], 0)
            rows = pl.ds(pl.multiple_of(blk * MOBA_BLOCK, MOBA_BLOCK), MOBA_BLOCK)
            q = q_ref[u * MOBA_PAIR_TILE:(u + 1) * MOBA_PAIR_TILE, :].astype(BF16)
            work.append((blk, _dot_nt(k_ref[0, rows, :], q)))
        for u, (blk, s) in enumerate(work):
            m = jnp.max(s, axis=0, keepdims=True)
            pv = _dot(v_ref[0, blk], jnp.exp(s - m).astype(BF16))
            o_ref[u * MOBA_PAIR_TILE:(u + 1) * MOBA_PAIR_TILE, :] = _partial_rows(pv, m)

    @pl.when(g == n_groups - 1)
    def _():
        lane = lax.broadcasted_iota(jnp.int32, (MOBA_PAIR_TILE, LANES), 1)
        o_ref[(MOBA_GROUP - 1) * MOBA_PAIR_TILE:, :] = jnp.where(lane == HEAD_DIM + 1, NEG, 0.0)


def _moba_sparse(q_sorted, k_pad, vt_aug, tile_blk, n_tiles, after):
    n_heads, s, _ = k_pad.shape
    n_groups = n_tiles // MOBA_GROUP
    rows = pl.BlockSpec((MOBA_GROUP * MOBA_PAIR_TILE, LANES), lambda h, g, blk: (h * n_groups + g, 0))
    grid_spec = pltpu.PrefetchScalarGridSpec(
        num_scalar_prefetch=1, grid=(n_heads, n_groups),
        in_specs=[rows, pl.BlockSpec((1, s, LANES), lambda h, g, blk: (h, 0, 0)),
                  pl.BlockSpec((1, s // MOBA_BLOCK, MOBA_VROWS, MOBA_BLOCK), lambda h, g, blk: (h, 0, 0, 0)),
                  _ORDER_ONLY],
        out_specs=rows)
    return pl.pallas_call(
        functools.partial(_moba_sparse_kernel, n_groups=n_groups),
        grid_spec=grid_spec,
        out_shape=jax.ShapeDtypeStruct(q_sorted.shape, F32),
        compiler_params=_cparams("arbitrary", "arbitrary"),
        name="moba_sparse",
    )(tile_blk, q_sorted, k_pad, vt_aug, after)


def _moba_merge_kernel(qa_ref, qb_ref, ka_ref, kb_ref, va_ref, vb_ref, ga_ref, gb_ref, _after_ref, o_ref):
    ki = lax.broadcasted_iota(jnp.int32, (MOBA_BLOCK, MOBA_BLOCK), 0)
    qi = lax.broadcasted_iota(jnp.int32, (MOBA_BLOCK, MOBA_BLOCK), 1)
    heads = ((qa_ref, ka_ref, va_ref, ga_ref), (qb_ref, kb_ref, vb_ref, gb_ref))
    chains = [(hh, b) for b in range(MOBA_MERGE_BLOCKS) for hh in range(2)]
    rows = lambda b: slice(b * MOBA_BLOCK, (b + 1) * MOBA_BLOCK)
    scores = [_dot_nt(heads[hh][1][0, rows(b), :], heads[hh][0][0, rows(b), :].astype(BF16))
              for hh, b in chains]
    outs = {}
    for (hh, b), s in zip(chains, scores):
        _, _, v_ref, g_ref = heads[hh]
        s = jnp.where(ki <= qi, s, NEG)
        m = jnp.max(s, axis=0, keepdims=True)
        parts = [_partial_cols(_dot(v_ref[0, b], jnp.exp(s - m).astype(BF16)), m)]
        parts += [g_ref[0, r, rows(b), :].T for r in range(MOBA_TOPK)]
        ms = [x[HEAD_DIM + 1:HEAD_DIM + 2, :] for x in parts]
        m_all = functools.reduce(jnp.maximum, ms)
        acc = sum(jnp.exp(mi - m_all) * x for mi, x in zip(ms, parts))
        outs[hh, b] = acc[:HEAD_DIM, :] / acc[HEAD_DIM:HEAD_DIM + 1, :]
    for b in range(MOBA_MERGE_BLOCKS):
        o_ref[rows(b), :] = jnp.concatenate([outs[0, b], outs[1, b]], axis=0).T.astype(BF16)


def _moba_merge(q_rows, k_pad, vt_aug, gathered, after):
    n_heads, s, _ = q_rows.shape
    t = MOBA_MERGE_BLOCKS * MOBA_BLOCK
    blk = lambda par: pl.BlockSpec((1, t, LANES), lambda p, i: (2 * p + par, i, 0))
    vspec = lambda par: pl.BlockSpec((1, MOBA_MERGE_BLOCKS, MOBA_VROWS, MOBA_BLOCK),
                                     lambda p, i: (2 * p + par, i, 0, 0))
    gspec = lambda par: pl.BlockSpec((1, MOBA_TOPK, t, LANES), lambda p, i: (2 * p + par, 0, i, 0))
    return pl.pallas_call(
        _moba_merge_kernel,
        grid=(n_heads // 2, s // t),
        in_specs=[blk(0), blk(1), blk(0), blk(1), vspec(0), vspec(1), gspec(0), gspec(1), _ORDER_ONLY],
        out_specs=pl.BlockSpec((t, LANES), lambda p, i: (i, p)),
        out_shape=jax.ShapeDtypeStruct((s, WIDTH), BF16),
        compiler_params=_cparams("arbitrary", "arbitrary"),
        name="moba_merge",
    )(q_rows, q_rows, k_pad, k_pad, vt_aug, vt_aug, gathered, gathered, after)


def _moba_attention(qc, kc, vc, kmean, overlap_a, overlap_b):
    s = qc.shape[0]
    n_tiles = _moba_tiles_per_head(s)
    q_rows, k_pad, vt_aug, sel, rank, counts_raw = _moba_route(qc, kc, vc, _moba_mean_rows(kmean))
    counts = counts_raw[:, :s // MOBA_BLOCK, 0].astype(jnp.int32)
    pos, tile_blk = _moba_dispatch(sel, rank, counts, n_tiles)
    q_sorted = _sc_scatter_rows(q_rows.reshape(N_HEADS * s, LANES), pos, N_HEADS * n_tiles * MOBA_PAIR_TILE)
    def chain(calls, anchor):
        outs = []
        for call in calls:
            outs.append(call(anchor))
            anchor = outs[-1]
        return outs

    out_a = chain(overlap_a, counts_raw)
    partial = _moba_sparse(q_sorted, k_pad, vt_aug, tile_blk, n_tiles, out_a[-1])
    gathered = _sc_gather_rows(partial, pos).reshape(N_HEADS, MOBA_TOPK, s, LANES)
    out_b = chain(overlap_b, partial)
    return out_a, out_b, _moba_merge(q_rows, k_pad, vt_aug, gathered, out_b[-1])


def _ssd_kernel(z_ref, xbc_ref, dt_ref, cw_ref, cb_ref, dtb_ref, alog_ref, dsk_ref, nw_ref, exp_ref, tri_ref,
                _after_ref, o_ref, state_ref, halo_ref):
    L = SSM_CHUNK

    @pl.when(pl.program_id(0) == 0)
    def _():
        state_ref[...] = jnp.zeros_like(state_ref)
        halo_ref[...] = jnp.zeros_like(halo_ref)

    cur = xbc_ref[...]
    prev8 = halo_ref[...]
    row8 = lax.broadcasted_iota(jnp.int32, (8, SSM_CONV_DIM), 0)
    conv = cb_ref[...] + cur * cw_ref[SSM_CONV - 1:SSM_CONV, :]
    for k in range(1, SSM_CONV):
        rolled = pltpu.roll(cur, k, 0)
        head = jnp.where(row8 < k, pltpu.roll(prev8, k, 0), rolled[0:8, :])
        shifted = jnp.concatenate([head, rolled[8:, :]], axis=0)
        conv = conv + shifted * cw_ref[SSM_CONV - 1 - k:SSM_CONV - k, :]
    halo_ref[...] = cur[L - 8:, :]
    act = _silu(conv)
    xs, bm, cm = act[:, :WIDTH], act[:, WIDTH:WIDTH + 2 * SSM_STATE], act[:, WIDTH + 2 * SSM_STATE:]

    dt = _softplus(dt_ref[...] + dtb_ref[...])
    da = dt * (-jnp.exp(alog_ref[...]))
    cs = _dot_exact_lhs(tri_ref[...], da)
    expand = exp_ref[...]
    cs_b = _dot_exact_rhs(cs, expand)
    dt_b = _dot_exact_rhs(dt, expand)
    cs_last = cs_b[L - 1:L, :]
    xdt = xs * dt_b
    cs_t = cs.T

    ri = lax.broadcasted_iota(jnp.int32, (L, L), 0)
    ci = lax.broadcasted_iota(jnp.int32, (L, L), 1)
    causal = ci <= ri
    lane = lax.broadcasted_iota(jnp.int32, (L, LANES), 1)
    bmb, cmb = bm.astype(BF16), cm.astype(BF16)
    xdtb = xdt.astype(BF16)
    y_pairs = []
    for p in range(N_HEADS // 2):
        g = (2 * p) // (N_HEADS // SSM_GROUPS)
        grp = slice(g * SSM_STATE, (g + 1) * SSM_STATE)
        cbm = _dot_nt(cmb[:, grp], bmb[:, grp])
        x_pair = xdtb[:, p * LANES:(p + 1) * LANES]
        halves = []
        for h in (2 * p, 2 * p + 1):
            seg = cs[:, h:h + 1] - cs_t[h:h + 1, :]
            decay = jnp.exp(jnp.where(causal, seg, NEG))
            halves.append(_dot((cbm * decay).astype(BF16), x_pair))
        y_pairs.append(jnp.where(lane < HEAD_DIM, halves[0], halves[1]))
    y_diag = jnp.concatenate(y_pairs, axis=1)

    state = state_ref[...]
    stb = state.astype(BF16)
    half = WIDTH // SSM_GROUPS
    y_off = jnp.concatenate(
        [_dot(cmb[:, g * SSM_STATE:(g + 1) * SSM_STATE], stb[:, g * half:(g + 1) * half])
         for g in range(SSM_GROUPS)], axis=1) * jnp.exp(cs_b)
    y = y_diag + y_off + xs * dsk_ref[...]

    w = (xdt * jnp.exp(cs_last - cs_b)).astype(BF16)
    bm_t = bm.T.astype(BF16)
    new = jnp.concatenate(
        [_dot(bm_t[g * SSM_STATE:(g + 1) * SSM_STATE, :], w[:, g * half:(g + 1) * half])
         for g in range(SSM_GROUPS)], axis=1)
    state_ref[...] = state * jnp.exp(cs_last) + new

    gt = y * _silu(z_ref[...])
    outs = []
    for g in range(SSM_GROUPS):
        gg = gt[:, g * half:(g + 1) * half]
        outs.append(gg * lax.rsqrt(jnp.mean(gg * gg, axis=-1, keepdims=True) + NORM_EPS))
    o_ref[...] = (jnp.concatenate(outs, axis=1) * nw_ref[...]).astype(BF16)


def _ssd_mixer(z, xbc, dt_raw, conv_w, conv_b, dt_bias, a_log, d_skip, norm_w, expand, tri, after):
    s = z.shape[0]
    L = SSM_CHUNK
    row = lambda n: pl.BlockSpec((L, n), lambda c: (c, 0))
    cst = lambda r, n: pl.BlockSpec((r, n), lambda c: (0, 0))
    return pl.pallas_call(
        _ssd_kernel,
        grid=(s // L,),
        in_specs=[row(WIDTH), row(SSM_CONV_DIM), row(DT_PAD), cst(SSM_CONV, SSM_CONV_DIM),
                  cst(1, SSM_CONV_DIM), cst(1, DT_PAD), cst(1, DT_PAD), cst(1, WIDTH), cst(1, WIDTH),
                  cst(DT_PAD, WIDTH), cst(L, L), _ORDER_ONLY],
        out_specs=row(WIDTH),
        out_shape=jax.ShapeDtypeStruct((s, WIDTH), BF16),
        scratch_shapes=[pltpu.VMEM((SSM_STATE, WIDTH), F32), pltpu.VMEM((8, SSM_CONV_DIM), F32)],
        compiler_params=_cparams("arbitrary"),
        cost_estimate=pl.CostEstimate(
            flops=2 * s * L * (SSM_GROUPS * SSM_STATE + WIDTH + 2 * N_HEADS * LANES) + 4 * s * SSM_STATE * WIDTH,
            transcendentals=s * (L * N_HEADS + SSM_CONV_DIM + 3 * WIDTH),
            bytes_accessed=s * (WIDTH + SSM_CONV_DIM + DT_PAD) * 4 + s * WIDTH * 2),
        name="ssd_mixer",
    )(z, xbc, dt_raw, conv_w, conv_b, dt_bias, a_log, d_skip, norm_w, expand, tri, after)


def _outproj_kernel(x_ref, *refs):
    *y_refs, w_ref, gate_ref, o_ref = refs
    y, row0 = 0.0, 0
    for y_ref in y_refs:
        y = y + _dot(y_ref[...], w_ref[row0:row0 + y_ref.shape[1], :])
        row0 += y_ref.shape[1]
    o_ref[...] = x_ref[...] + gate_ref[...] * y


def _out_projection(x, ys, w_all, gate, layer):
    _, s, d = x.shape
    tm = 512
    row = lambda n: pl.BlockSpec((tm, n), lambda i: (i, 0))
    assert sum(y.shape[1] for y in ys) == w_all.shape[1]
    return pl.pallas_call(
        _outproj_kernel,
        grid=(s // tm,),
        in_specs=[_stream_rows(tm, d)] + [row(y.shape[1]) for y in ys]
                 + [pl.BlockSpec((None, w_all.shape[1], d), lambda i: (layer, 0, 0)),
                    pl.BlockSpec((1, d), lambda i: (0, 0))],
        out_specs=_stream_rows(tm, d),
        out_shape=jax.ShapeDtypeStruct((1, s, d), F32),
        compiler_params=_cparams("arbitrary"),
        name="out_projection",
    )(x, *ys, w_all, gate)


def _ffn_kernel(x_ref, g_ref, sc_ref, sh_ref, gate_ref, wg_ref, wu_ref, wd_ref, o_ref, *, chunk):
    x = x_ref[...]
    h = _mod_norm(x, g_ref[...], sc_ref[...], sh_ref[...]).astype(BF16)
    hidden = wg_ref.shape[1]
    n_chunks = hidden // chunk
    cols = lambda c: slice(c * chunk, (c + 1) * chunk)
    gate_up = lambda c: (_dot(h, wg_ref[:, cols(c)]), _dot(h, wu_ref[:, cols(c)]))
    acc = jnp.zeros(x.shape, F32)
    nxt = gate_up(0)
    for c in range(n_chunks):
        g, u = nxt
        if c + 1 < n_chunks:
            nxt = gate_up(c + 1)
        acc = acc + _dot((_silu(g) * u).astype(BF16), wd_ref[cols(c), :])
    o_ref[...] = x + gate_ref[...] * acc


def _ffn(x, g, sc, sh, gate, w_gate, w_up, w_down, layer):
    _, s, d = x.shape
    hidden = w_gate.shape[2]
    tm = 512
    row = _stream_rows(tm, d)
    vec = pl.BlockSpec((1, d), lambda i: (0, 0))
    return pl.pallas_call(
        functools.partial(_ffn_kernel, chunk=256),
        grid=(s // tm,),
        in_specs=[row, vec, vec, vec, vec,
                  _resident((None, d, hidden), lambda i: (layer, 0, 0)),
                  _resident((None, d, hidden), lambda i: (layer, 0, 0)),
                  _resident((None, hidden, d), lambda i: (layer, 0, 0))],
        out_specs=row,
        out_shape=jax.ShapeDtypeStruct((1, s, d), F32),
        compiler_params=_cparams("arbitrary"),
        name="ffn",
    )(x, g, sc, sh, gate, w_gate, w_up, w_down)


def _rope_tables(seq):
    pos = jnp.arange(seq, dtype=F32)
    inv = ROPE_THETA ** (-jnp.arange(0, ROPE_DIM, 2, dtype=F32) / ROPE_DIM)
    ang = pos[:, None] * inv[None, :]
    cos, sin = jnp.cos(ang), jnp.sin(ang)
    ones = jnp.ones((seq, HEAD_DIM - ROPE_DIM), F32)
    zeros = jnp.zeros((seq, HEAD_DIM - ROPE_DIM), F32)
    zh = jnp.zeros((seq, ROPE_HALF), F32)
    cos_h = jnp.concatenate([cos, cos, ones], axis=1)
    sin_lo = jnp.concatenate([-sin, zh, zeros], axis=1)
    sin_hi = jnp.concatenate([zh, sin, zeros], axis=1)
    two = lambda t: jnp.concatenate([t, t], axis=1)
    return two(cos_h), two(sin_lo), two(sin_hi)


def _moba_mean_rows(kmean):
    nb = kmean.shape[0]
    km = kmean.reshape(nb, N_HEADS, HEAD_DIM).transpose(1, 0, 2)
    return jnp.pad(km, ((0, 0), (0, MOBA_MAX_BLOCKS - nb), (0, LANES - HEAD_DIM)))


def kernel(x, c, ada_w, ada_b, norm_mix, w_in, qn_swa, kn_swa, qn_moba, kn_moba, conv_w, conv_b, dt_bias,
           a_log, d_skip, ssm_norm, w_out, norm_ffn, w_gate, w_up, w_down):
    batch, seq, d = x.shape
    depth = ada_w.shape[0]
    assert batch == 1 and d == D_MODEL
    assert seq % SWA_SPAN == 0 and seq // MOBA_BLOCK <= MOBA_MAX_BLOCKS
    xs = x

    mod = _ada_modulation(c, ada_w, ada_b)
    rope_tabs = _rope_tables(seq)
    lane_head = jnp.arange(LANES) // HEAD_DIM
    hsum = (lane_head[:, None] == lane_head[None, :]).astype(BF16)
    expand = (jnp.arange(DT_PAD)[:, None] == (jnp.arange(WIDTH) // HEAD_DIM)[None, :]).astype(BF16)
    tri = (jnp.arange(SSM_CHUNK)[:, None] >= jnp.arange(SSM_CHUNK)[None, :]).astype(BF16)
    pad8 = lambda v: jnp.pad(v, (0, DT_PAD - N_HEADS)).reshape(1, DT_PAD)
    two = lambda g: jnp.concatenate([g, g]).reshape(1, LANES)

    w_in_b = _cast_bf16(w_in, IN_PROJ_PAD)
    w_out_b, w_gate_b, w_up_b, w_down_b = (_cast_bf16(w) for w in (w_out, w_gate, w_up, w_down))

    for l in range(depth):
        shift_m, scale_m, gate_m, shift_f, scale_f, gate_f = [mod[l, :, i * d:(i + 1) * d] for i in range(6)]
        gains = [two(g[l]) for g in (qn_swa, kn_swa, qn_moba, kn_moba)]
        qa, ka, va, qc, kc, vc, z, xbc, dt_raw, kmean = _in_projection(
            xs, norm_mix[l].reshape(1, d), scale_m, shift_m, w_in_b, l, gains, hsum, rope_tabs)

        n_pairs = WIDTH // LANES
        first = n_pairs - 1
        dilated_lo = functools.partial(_dilated_attention, qa, ka, va, 0, first)
        dilated_hi = functools.partial(_dilated_attention, qa, ka, va, first, n_pairs - first)
        ssd = functools.partial(
            _ssd_mixer, z, xbc, dt_raw, conv_w[l], conv_b[l].reshape(1, -1), pad8(dt_bias[l]), pad8(a_log[l]),
            jnp.repeat(d_skip[l], HEAD_DIM).reshape(1, WIDTH), ssm_norm[l].reshape(1, WIDTH), expand, tri)
        (y_a_lo,), (y_a_hi, y_b), y_c = _moba_attention(qc, kc, vc, kmean, (dilated_lo,), (dilated_hi, ssd))

        xs = _out_projection(xs, (y_a_lo, y_a_hi, y_b, y_c), w_out_b, gate_m, l)
        xs = _ffn(xs, norm_ffn[l].reshape(1, d), scale_f, shift_f, gate_f,
                  w_gate_b, w_up_b, w_down_b, l)
    return xs
```

```python
import functools

import jax
import jax.numpy as jnp
from jax import lax
from jax.experimental import pallas as pl
from jax.experimental.pallas import tpu as pltpu
from jax.experimental.pallas import tpu_sc as plsc

F32 = jnp.float32
BF16 = jnp.bfloat16

D_MODEL = 1024
HEAD_DIM = 64
N_HEADS = 8
WIDTH = N_HEADS * HEAD_DIM
ATTN_SCALE = HEAD_DIM ** -0.5
NORM_EPS = 1e-6
NEG = -1e30

ROPE_THETA = 500000.0
ROPE_DIM = HEAD_DIM // 4
ROPE_HALF = ROPE_DIM // 2

DILATIONS = (1, 4, 16)
SWA_BLOCK = 128
SWA_SPAN = DILATIONS[-1] * SWA_BLOCK
SWA_GROUP = 4

SSM_STATE = 128
SSM_GROUPS = 2
SSM_CONV = 4
SSM_CHUNK = 256
SSM_CONV_DIM = WIDTH + 2 * SSM_GROUPS * SSM_STATE

MOBA_BLOCK = 256
MOBA_TOPK = 3
MOBA_PAIR_TILE = 128
MOBA_GROUP = 32
SC_INDEX_WINDOW = 128
MOBA_MAX_BLOCKS = 64
MOBA_MERGE_BLOCKS = 8
MOBA_VROWS = HEAD_DIM + 16

LANES = 128
DT_PAD = LANES
IN_PROJ_PAD = 6 * WIDTH + WIDTH + SSM_CONV_DIM + DT_PAD
OFF_A, OFF_C, OFF_Z, OFF_XBC, OFF_DT = 0, 3 * WIDTH, 6 * WIDTH, 7 * WIDTH, 7 * WIDTH + SSM_CONV_DIM

VMEM_LIMIT = 56 * 1024 * 1024
CAST_BLOCK_BYTES = 6 * 1024 * 1024


_ORDER_ONLY = pl.BlockSpec(memory_space=pl.ANY)


def _stream_rows(tm, d):
    return pl.BlockSpec((None, tm, d), lambda i: (0, i, 0))


def _resident(block_shape, index_map):
    return pl.BlockSpec(block_shape, index_map, pipeline_mode=pl.Buffered(1))


def _cparams(*sem):
    return pltpu.CompilerParams(dimension_semantics=sem, vmem_limit_bytes=VMEM_LIMIT)


def _split3(a):
    hi = a.astype(BF16)
    r1 = a - hi.astype(F32)
    mid = r1.astype(BF16)
    lo = (r1 - mid.astype(F32)).astype(BF16)
    return hi, mid, lo


def _dot(a, b):
    return jnp.dot(a, b, preferred_element_type=F32)


def _dot_nt(a, b):
    return lax.dot_general(a, b, (((1,), (1,)), ((), ())), preferred_element_type=F32)


def _dot_exact_rhs(a, b_bf16):
    hi, mid, lo = _split3(a)
    return _dot(hi, b_bf16) + _dot(mid, b_bf16) + _dot(lo, b_bf16)


def _dot_exact_lhs(a_bf16, b):
    hi, mid, lo = _split3(b)
    return _dot(a_bf16, hi) + _dot(a_bf16, mid) + _dot(a_bf16, lo)


def _dot_f32(a, b, dot=_dot):
    ah, am, al = _split3(a)
    bh, bm, bl = _split3(b)
    return (dot(ah, bh) + (dot(ah, bm) + dot(am, bh))
            + (dot(ah, bl) + dot(am, bm) + dot(al, bh)))


def _dot_bf16x3(a, b, dot=_dot):
    ah = a.astype(BF16)
    al = (a - ah.astype(F32)).astype(BF16)
    bh = b.astype(BF16)
    bl = (b - bh.astype(F32)).astype(BF16)
    return dot(ah, bh) + (dot(ah, bl) + dot(al, bh))


def _silu(x):
    return x / (1.0 + jnp.exp(-x))


def _softplus(x):
    return jnp.maximum(x, 0.0) + jnp.log1p(jnp.exp(-jnp.abs(x)))


def _cast_kernel(w_ref, o_ref):
    cols = w_ref.shape[1]
    o_ref[:, :cols] = w_ref[...].astype(BF16)
    if o_ref.shape[1] > cols:
        o_ref[:, cols:] = jnp.zeros((o_ref.shape[0], o_ref.shape[1] - cols), BF16)


def _cast_bf16(w, pad_cols=None):
    depth, rows, cols = w.shape
    out_cols = pad_cols or cols
    tr = max(t for t in range(8, rows + 1, 8) if rows % t == 0 and t * cols * 4 <= CAST_BLOCK_BYTES)
    return pl.pallas_call(
        _cast_kernel,
        grid=(depth, rows // tr),
        in_specs=[pl.BlockSpec((None, tr, cols), lambda l, i: (l, i, 0))],
        out_specs=pl.BlockSpec((None, tr, out_cols), lambda l, i: (l, i, 0)),
        out_shape=jax.ShapeDtypeStruct((depth, rows, out_cols), BF16),
        compiler_params=_cparams("arbitrary", "arbitrary"),
        name="cast_weights",
    )(w)


def _ada_kernel(c_ref, w_ref, b_ref, o_ref):
    c = c_ref[...]
    o_ref[0] = _dot_bf16x3(_silu(c), w_ref[0]) + b_ref[0]


def _ada_modulation(c, ada_w, ada_b):
    depth, d, n = ada_w.shape
    tn = 1024
    c8 = jnp.broadcast_to(c, (8, d))
    out = pl.pallas_call(
        _ada_kernel,
        grid=(depth, n // tn),
        in_specs=[pl.BlockSpec((8, d), lambda l, j: (0, 0)),
                  pl.BlockSpec((1, d, tn), lambda l, j: (l, 0, j)),
                  pl.BlockSpec((1, 1, tn), lambda l, j: (l, 0, j))],
        out_specs=pl.BlockSpec((1, 8, tn), lambda l, j: (l, 0, j)),
        out_shape=jax.ShapeDtypeStruct((depth, 8, n), F32),
        compiler_params=_cparams("arbitrary", "arbitrary"),
        name="ada_modulation",
    )(c8, ada_w, ada_b.reshape(depth, 1, n))
    return out[:, 0:1, :]


def _mod_norm(x, g, sc, sh):
    var = jnp.mean(x * x, axis=-1, keepdims=True)
    return (x * lax.rsqrt(var + NORM_EPS) * g) * (1.0 + sc) + sh


def _inproj_kernel(x_ref, g_ref, sc_ref, sh_ref, w_ref, gqa_ref, gka_ref, gqc_ref, gkc_ref, hsum_ref,
                   cos_ref, slo_ref, shi_ref,
                   qa_ref, ka_ref, va_ref, qc_ref, kc_ref, vc_ref, z_ref, xbc_ref, dt_ref, km_ref):
    h = _mod_norm(x_ref[...], g_ref[...], sc_ref[...], sh_ref[...]).astype(BF16)
    proj = lambda off, n=WIDTH: _dot(h, w_ref[:, off:off + n])
    raw = [proj(OFF_A), proj(OFF_A + WIDTH), proj(OFF_C), proj(OFF_C + WIDTH)]
    va_ref[...] = proj(OFF_A + 2 * WIDTH)
    vc_ref[...] = proj(OFF_C + 2 * WIDTH)
    z_ref[...] = proj(OFF_Z)

    hsum = hsum_ref[...]
    cos_t, slo, shi = cos_ref[...], slo_ref[...], shi_ref[...]
    rope = lambda x, gain_ref: _head_norm_rope(x, gain_ref[...], hsum, cos_t, slo, shi)
    n_blocks = x_ref.shape[0] // MOBA_BLOCK
    for p in range(WIDTH // LANES):
        lo, hi = p * LANES, (p + 1) * LANES
        qa_ref[:, lo:hi] = rope(raw[0][:, lo:hi], gqa_ref) * ATTN_SCALE
        ka_ref[:, lo:hi] = rope(raw[1][:, lo:hi], gka_ref)
        qc_ref[:, lo:hi] = rope(raw[2][:, lo:hi], gqc_ref)
        kc = rope(raw[3][:, lo:hi], gkc_ref)
        kc_ref[:, lo:hi] = kc
        for b in range(n_blocks):
            km_ref[b, :, lo:hi] = jnp.mean(kc[b * MOBA_BLOCK:(b + 1) * MOBA_BLOCK, :], axis=0, keepdims=True)
    xbc_ref[...] = proj(OFF_XBC, SSM_CONV_DIM)
    dt_ref[...] = proj(OFF_DT, DT_PAD)


def _in_projection(x, g, sc, sh, w_all, layer, gains, hsum, rope_tabs):
    _, s, d = x.shape
    tm = 512
    assert tm % MOBA_BLOCK == 0
    row = lambda n: pl.BlockSpec((tm, n), lambda i: (i, 0))
    vec = pl.BlockSpec((1, d), lambda i: (0, 0))
    cst = lambda r, n: pl.BlockSpec((r, n), lambda i: (0, 0))
    widths = [WIDTH] * 7 + [SSM_CONV_DIM, DT_PAD]
    return pl.pallas_call(
        _inproj_kernel,
        grid=(s // tm,),
        in_specs=[_stream_rows(tm, d), vec, vec, vec, _resident((None, d, IN_PROJ_PAD), lambda i: (layer, 0, 0))]
                 + [cst(1, LANES)] * 4 + [cst(LANES, LANES)] + [row(LANES)] * 3,
        out_specs=[row(n) for n in widths]
                  + [pl.BlockSpec((tm // MOBA_BLOCK, 1, WIDTH), lambda i: (i, 0, 0))],
        out_shape=[jax.ShapeDtypeStruct((s, n), F32) for n in widths]
                  + [jax.ShapeDtypeStruct((s // MOBA_BLOCK, 1, WIDTH), F32)],
        compiler_params=_cparams("arbitrary"),
        name="in_projection",
    )(x, g, sc, sh, w_all, *gains, hsum, *rope_tabs)


def _head_norm_rope(x, gain, hsum, cos_t, sin_lo, sin_hi):
    sq = x * x
    sq_hi = sq.astype(BF16)
    ss = _dot(sq_hi, hsum) + _dot((sq - sq_hi.astype(F32)).astype(BF16), hsum)
    y = x * lax.rsqrt(ss * (1.0 / HEAD_DIM) + NORM_EPS) * gain
    return (y * cos_t + pltpu.roll(y, ROPE_HALF, 1) * sin_hi
            + pltpu.roll(y, LANES - ROPE_HALF, 1) * sin_lo)


def _dilated_kernel(q_ref, kc_ref, kp_ref, vc_ref, vp_ref, _after_ref, o_ref, m_s, l_s, a_s, q4_s, k4_s, v4_s):
    has_prev = pl.program_id(1) > 0
    lane = lax.broadcasted_iota(jnp.int32, (SWA_BLOCK, LANES), 1)
    head_a = lane < HEAD_DIM
    qi = lax.broadcasted_iota(jnp.int32, (SWA_BLOCK, 2 * SWA_BLOCK), 0)
    ki = lax.broadcasted_iota(jnp.int32, (SWA_BLOCK, 2 * SWA_BLOCK), 1)
    band = (ki >= qi) & (ki <= qi + SWA_BLOCK)
    band_edge = band & ((ki >= SWA_BLOCK) | has_prev)

    quarter = SWA_SPAN // 4
    for r4 in range(4):
        src = pl.ds(r4, quarter, 4)
        q4_s[r4 * quarter:(r4 + 1) * quarter, :] = q_ref[src, :]
        for dst, prev, cur in ((k4_s, kp_ref, kc_ref), (v4_s, vp_ref, vc_ref)):
            dst[2 * r4 * quarter:(2 * r4 + 1) * quarter, :] = prev[src, :]
            dst[(2 * r4 + 1) * quarter:(2 * r4 + 2) * quarter, :] = cur[src, :]

    def unit_d16(a, r4):
        rows = pl.ds(4 * a + r4, SWA_BLOCK, 16)
        q = q4_s[pl.ds(r4 * quarter + a, SWA_BLOCK, 4), :]
        prev = pl.ds(2 * r4 * quarter + a, SWA_BLOCK, 4)
        cur = pl.ds((2 * r4 + 1) * quarter + a, SWA_BLOCK, 4)
        k_cat = jnp.concatenate([k4_s[prev, :], k4_s[cur, :]], axis=0)
        v_cat = jnp.concatenate([v4_s[prev, :], v4_s[cur, :]], axis=0)
        return rows, q, k_cat, v_cat, band_edge

    def unit_d4(r, nb):
        rows = pl.ds(4 * SWA_BLOCK * nb + r, SWA_BLOCK, 4)
        q = q4_s[pl.ds(r * quarter + SWA_BLOCK * nb, SWA_BLOCK), :]
        keys = pl.ds((2 * r + 1) * quarter + SWA_BLOCK * (nb - 1), 2 * SWA_BLOCK)
        return rows, q, k4_s[keys, :], v4_s[keys, :], (band_edge if nb == 0 else band)

    def unit_d1(b, edge=False):
        rows = pl.ds(SWA_BLOCK * b, SWA_BLOCK)
        if edge:
            prev = pl.ds(SWA_SPAN - SWA_BLOCK, SWA_BLOCK)
            k_cat = jnp.concatenate([kp_ref[prev, :], kc_ref[rows, :]], axis=0)
            v_cat = jnp.concatenate([vp_ref[prev, :], vc_ref[rows, :]], axis=0)
            return rows, q_ref[rows, :], k_cat, v_cat, band_edge
        keys = pl.ds(SWA_BLOCK * (b - 1), 2 * SWA_BLOCK)
        return rows, q_ref[rows, :], kc_ref[keys, :], vc_ref[keys, :], band

    def attend(units, mode):
        scores = []
        for _, q, k_cat, _, mask in units:
            kb = k_cat.astype(BF16)
            scores.append([jnp.where(mask, _dot_nt(jnp.where(sel, q, 0.0).astype(BF16), kb), NEG)
                           for sel in (head_a, ~head_a)])
        probs = []
        for pair in scores:
            stats = []
            for s in pair:
                m_loc = jnp.max(s, axis=-1, keepdims=True)
                p = jnp.exp(s - m_loc)
                stats.append((m_loc, jnp.sum(p, axis=-1, keepdims=True), p.astype(BF16)))
            probs.append(stats)
        for (rows, _, _, v_cat, _), ((m_a, s_a, p_a), (m_b, s_b, p_b)) in zip(units, probs):
            vb = v_cat.astype(BF16)
            m_new = jnp.where(head_a, m_a, m_b)
            l_new = jnp.where(head_a, s_a, s_b)
            a_new = jnp.where(head_a, _dot(p_a, vb), _dot(p_b, vb))
            if mode != "init":
                m_in, m_loc = m_s[rows, :], m_new
                m_new = jnp.maximum(m_in, m_loc)
                alpha, beta = jnp.exp(m_in - m_new), jnp.exp(m_loc - m_new)
                l_new = alpha * l_s[rows, :] + beta * l_new
                a_new = alpha * a_s[rows, :] + beta * a_new
            if mode == "final":
                o_ref[rows, :] = (a_new / l_new).astype(BF16)
            else:
                m_s[rows, :] = m_new
                l_s[rows, :] = l_new
                a_s[rows, :] = a_new

    g = SWA_GROUP
    assert DILATIONS == (1, 4, 16) and g == 4 and SWA_SPAN // (4 * SWA_BLOCK) == g

    def body16(a, carry):
        attend([unit_d16(a, r4) for r4 in range(g)], "init")
        return carry
    lax.fori_loop(0, 16 // g, body16, 0)

    def body4(r, carry):
        attend([unit_d4(r, nb) for nb in range(g)], "fold")
        return carry
    lax.fori_loop(0, 4, body4, 0)

    attend([unit_d1(b, edge=(b == 0)) for b in range(g)], "final")

    def body1(i, carry):
        attend([unit_d1(i * g + j) for j in range(g)], "final")
        return carry
    lax.fori_loop(1, SWA_SPAN // SWA_BLOCK // g, body1, 0)


def _dilated_attention(q, k, v, pair0, n_pairs, after):
    s = q.shape[0]
    cur = lambda off: pl.BlockSpec((SWA_SPAN, LANES), lambda p, n: (n, off + p))
    prev = lambda off: pl.BlockSpec((SWA_SPAN, LANES), lambda p, n: (jnp.maximum(n - 1, 0), off + p))
    width = n_pairs * LANES
    return pl.pallas_call(
        _dilated_kernel,
        grid=(n_pairs, s // SWA_SPAN),
        in_specs=[cur(pair0), cur(pair0), prev(pair0), cur(pair0), prev(pair0), _ORDER_ONLY],
        out_specs=cur(0),
        out_shape=jax.ShapeDtypeStruct((s, width), BF16),
        scratch_shapes=[pltpu.VMEM((SWA_SPAN, LANES), F32)] * 4 + [pltpu.VMEM((2 * SWA_SPAN, LANES), F32)] * 2,
        compiler_params=_cparams("arbitrary", "arbitrary"),
        cost_estimate=pl.CostEstimate(
            flops=len(DILATIONS) * 4 * s * 2 * SWA_BLOCK * width,
            transcendentals=len(DILATIONS) * s * 2 * SWA_BLOCK * 2 * n_pairs,
            bytes_accessed=5 * s * width * 4 + s * width * 2),
        name="dilated_attention",
    )(q, k, k, v, v, after)


def _moba_route_kernel(q_ref, k_ref, v_ref, km_ref, sut_ref, ones_ref,
                       qrow_ref, kpad_ref, vaug_ref, vt_ref, sel_ref, rank_ref, cnt_ref, base_ref):
    i = pl.program_id(0)

    @pl.when(i == 0)
    def _():
        base_ref[...] = jnp.zeros_like(base_ref)

    lane = lax.broadcasted_iota(jnp.int32, (MOBA_BLOCK, LANES), 1)
    is_feat = lane < HEAD_DIM
    row = lax.broadcasted_iota(jnp.int32, (MOBA_MAX_BLOCKS, MOBA_BLOCK), 0)
    past = row < i
    ones_rows = jnp.where(
        lax.broadcasted_iota(jnp.int32, (MOBA_VROWS - HEAD_DIM, MOBA_BLOCK), 0) == 0, 1.0, 0.0)
    ninf = float("-inf")
    gates = []
    for h in range(N_HEADS):
        cols = slice((h // 2) * LANES, (h // 2 + 1) * LANES)
        q, k = q_ref[:, cols], k_ref[:, cols]
        v = v_ref[:, cols]
        v_t = v.T
        if h % 2:
            q, k, v = (pltpu.roll(x, HEAD_DIM, 1) for x in (q, k, v))
            v_t = v_t[HEAD_DIM:, :]
        else:
            v_t = v_t[:HEAD_DIM, :]
        qrow_ref[h] = jnp.where(is_feat, q * ATTN_SCALE, 0.0)
        kpad_ref[h] = jnp.where(is_feat, k, 0.0).astype(BF16)
        vaug_ref[h] = jnp.where(is_feat, v, jnp.where(lane == HEAD_DIM, 1.0, 0.0)).astype(BF16)
        vt_ref[h, 0] = jnp.concatenate([v_t, ones_rows], axis=0).astype(BF16)
        gates.append(_dot_bf16x3(km_ref[h], jnp.where(is_feat, q, 0.0), dot=_dot_nt))
    for h, gate_t in enumerate(gates):
        cand = jnp.where(past, gate_t, ninf)
        picks = []
        for _ in range(MOBA_TOPK):
            best = jnp.max(cand, axis=0, keepdims=True)
            idx = jnp.min(jnp.where(cand == best, row, 2 * LANES), axis=0, keepdims=True)
            ok = best > ninf
            pick = (row == idx) & ok
            cand = jnp.where(pick, ninf, cand)
            picks.append((idx, ok, pick))
        chosen = jnp.where(picks[0][2] | picks[1][2] | picks[2][2], 1.0, 0.0).astype(BF16)
        base = base_ref[h]
        rank_full = base + _dot(chosen, sut_ref[...])
        base_ref[h] = base + _dot(chosen, ones_ref[...])
        for r, (idx, ok, pick) in enumerate(picks):
            rank = jnp.sum(jnp.where(pick, rank_full, 0.0), axis=0, keepdims=True)
            sel_ref[h, r:r + 1, :] = jnp.where(ok, idx, -1)
            rank_ref[h, r:r + 1, :] = rank.astype(jnp.int32)

    @pl.when(i == pl.num_programs(0) - 1)
    def _():
        cnt_ref[...] = base_ref[...]


def _moba_route(qc, kc, vc, km_rows):
    s = qc.shape[0]
    nb = s // MOBA_BLOCK
    row = pl.BlockSpec((MOBA_BLOCK, WIDTH), lambda i: (i, 0))
    per_head = lambda dt: (pl.BlockSpec((N_HEADS, MOBA_BLOCK, LANES), lambda i: (0, i, 0)),
                           jax.ShapeDtypeStruct((N_HEADS, s, LANES), dt))
    picks = (pl.BlockSpec((N_HEADS, MOBA_TOPK, MOBA_BLOCK), lambda i: (0, 0, i)),
             jax.ShapeDtypeStruct((N_HEADS, MOBA_TOPK, s), jnp.int32))
    outs = [per_head(F32), per_head(BF16), per_head(BF16),
            (pl.BlockSpec((N_HEADS, 1, MOBA_VROWS, MOBA_BLOCK), lambda i: (0, i, 0, 0)),
             jax.ShapeDtypeStruct((N_HEADS, nb, MOBA_VROWS, MOBA_BLOCK), BF16)),
            picks, picks,
            (pl.BlockSpec((N_HEADS, MOBA_MAX_BLOCKS, MOBA_BLOCK), lambda i: (0, 0, 0)),
             jax.ShapeDtypeStruct((N_HEADS, MOBA_MAX_BLOCKS, MOBA_BLOCK), F32))]
    qi = jnp.arange(MOBA_BLOCK)
    strict_upper = (qi[:, None] < qi[None, :]).astype(BF16)
    all_ones = jnp.ones((MOBA_BLOCK, MOBA_BLOCK), BF16)
    sq = pl.BlockSpec((MOBA_BLOCK, MOBA_BLOCK), lambda i: (0, 0))
    return pl.pallas_call(
        _moba_route_kernel,
        grid=(nb,),
        in_specs=[row, row, row,
                  pl.BlockSpec((N_HEADS, MOBA_MAX_BLOCKS, LANES), lambda i: (0, 0, 0)), sq, sq],
        out_specs=[o[0] for o in outs],
        out_shape=[o[1] for o in outs],
        scratch_shapes=[pltpu.VMEM((N_HEADS, MOBA_MAX_BLOCKS, MOBA_BLOCK), F32)],
        compiler_params=_cparams("arbitrary"),
        name="moba_route",
    )(qc, kc, vc, km_rows, strict_upper, all_ones)


def _moba_tiles_per_head(seq):
    tiles = MOBA_TOPK * seq // MOBA_PAIR_TILE + seq // MOBA_BLOCK
    return (tiles // MOBA_GROUP + 1) * MOBA_GROUP


def _moba_dispatch(sel, rank, counts, n_tiles):
    n_heads, nb = counts.shape
    padded = (counts + MOBA_PAIR_TILE - 1) // MOBA_PAIR_TILE * MOBA_PAIR_TILE
    ends = jnp.cumsum(padded, axis=1)
    offs = ends - padded
    rows_per_head = n_tiles * MOBA_PAIR_TILE
    head_base = (jnp.arange(n_heads, dtype=jnp.int32) * rows_per_head)[:, None, None]
    blocks = jnp.arange(nb, dtype=jnp.int32)
    off_sel = jnp.sum(jnp.where(sel[..., None] == blocks, offs[:, None, None, :], 0), axis=-1)
    null_row = rows_per_head - MOBA_PAIR_TILE + jnp.arange(sel.shape[-1], dtype=jnp.int32) % MOBA_PAIR_TILE
    pos = head_base + jnp.where(sel >= 0, off_sel + rank, null_row)
    tile_start = jnp.arange(n_tiles, dtype=jnp.int32) * MOBA_PAIR_TILE
    blk = jnp.sum(ends[:, None, :] <= tile_start[None, :, None], axis=2)
    tile_blk = jnp.where(tile_start[None, :] < ends[:, -1:], blk, -1)
    return pos.reshape(-1).astype(jnp.int32), tile_blk.reshape(-1).astype(jnp.int32)


def _sc_workers():
    info = plsc.get_sparse_core_info()
    return info.num_cores, info.num_cores * info.num_subcores


def _sc_scatter_rows(rows, pos, n_out):
    n_src = rows.shape[0]
    seq = n_src // N_HEADS
    n_cores, n_workers = _sc_workers()
    win = SC_INDEX_WINDOW
    wins_per_worker = n_src // win // n_workers
    wins_per_head = seq // win
    assert n_src % (win * n_workers) == 0
    mesh = plsc.VectorSubcoreMesh(core_axis_name="c", subcore_axis_name="s")

    per_set = 2
    n_groups = wins_per_worker // per_set
    assert wins_per_worker % (2 * per_set) == 0 and wins_per_head % wins_per_worker == 0

    @functools.partial(
        pl.kernel, mesh=mesh, out_type=jax.ShapeDtypeStruct((n_out, LANES), F32), name="moba_scatter_rows",
        cost_estimate=pl.CostEstimate(flops=0, transcendentals=0,
                                      bytes_accessed=(1 + MOBA_TOPK) * n_src * LANES * 4 + pos.size * 4),
        scratch_types=[pltpu.VMEM((wins_per_worker, win), jnp.int32)] * MOBA_TOPK
                      + [pltpu.VMEM((win, LANES), F32)] * (2 * per_set)
                      + [pltpu.SemaphoreType.DMA] * (2 * per_set + 2))
    def scatter(src_hbm, idx_hbm, out_hbm, *scratch):
        idx_all = scratch[:MOBA_TOPK]
        rows_v = scratch[MOBA_TOPK:MOBA_TOPK + 2 * per_set]
        load_sems, scat_sems = scratch[-2 * per_set - 2:-2], scratch[-2:]
        wid = lax.axis_index("s") * n_cores + lax.axis_index("c")
        base = wid * wins_per_worker
        h, qw0 = base // wins_per_head, base % wins_per_head
        for r in range(MOBA_TOPK):
            pltpu.sync_copy(idx_hbm.at[pl.ds((h * MOBA_TOPK + r) * wins_per_head + qw0, wins_per_worker)], idx_all[r])

        def load_copy(g, st, u):
            return pltpu.make_async_copy(src_hbm.at[pl.ds((base + g * per_set + u) * win, win)],
                                         rows_v[st * per_set + u], load_sems[st * per_set + u])

        def scat_copy(g, st, u, r):
            return pltpu.make_async_copy(rows_v[st * per_set + u], out_hbm.at[idx_all[r].at[g * per_set + u]],
                                         scat_sems[st])

        def issue(g, st):
            for u in range(per_set):
                load_copy(g, st, u).start()

        def drain(g, st):
            for u in range(per_set):
                load_copy(g, st, u).wait()
                for r in range(MOBA_TOPK):
                    scat_copy(g, st, u, r).start()

        def wait_scatters(g, st):
            for u in range(per_set):
                for r in range(MOBA_TOPK):
                    scat_copy(g, st, u, r).wait()

        issue(0, 0)

        @pl.loop(0, n_groups, step=2)
        def _(g):
            @pl.when(g > 0)
            def _():
                wait_scatters(g - 1, 1)
            issue(g + 1, 1)
            drain(g, 0)
            wait_scatters(g, 0)

            @pl.when(g + 2 < n_groups)
            def _():
                issue(g + 2, 0)
            drain(g + 1, 1)

        wait_scatters(n_groups - 1, 1)

    return scatter(rows, pos.reshape(pos.shape[0] // win, win))


def _sc_gather_rows(table, pos):
    n = pos.shape[0]
    n_cores, n_workers = _sc_workers()
    win = SC_INDEX_WINDOW
    wins_per_worker = n // win // n_workers
    assert n % (win * n_workers) == 0
    mesh = plsc.VectorSubcoreMesh(core_axis_name="c", subcore_axis_name="s")

    per_set = 2
    n_groups = wins_per_worker // per_set
    assert wins_per_worker % (2 * per_set) == 0

    @functools.partial(
        pl.kernel, mesh=mesh, out_type=jax.ShapeDtypeStruct((n, LANES), F32), name="moba_gather_rows",
        cost_estimate=pl.CostEstimate(flops=0, transcendentals=0, bytes_accessed=2 * n * LANES * 4 + n * 4),
        scratch_types=[pltpu.VMEM((wins_per_worker, win), jnp.int32)] + [pltpu.VMEM((win, LANES), F32)] * (2 * per_set)
                      + [pltpu.SemaphoreType.DMA] * (2 * per_set + 2))
    def gather(table_hbm, idx_hbm, out_hbm, idx_all, *scratch):
        rows_v = scratch[:2 * per_set]
        gather_sems, store_sems = scratch[2 * per_set:4 * per_set], scratch[4 * per_set:]
        wid = lax.axis_index("s") * n_cores + lax.axis_index("c")
        base = wid * wins_per_worker
        pltpu.sync_copy(idx_hbm.at[pl.ds(base, wins_per_worker)], idx_all)

        def gather_copy(g, st, u):
            return pltpu.make_async_copy(table_hbm.at[idx_all.at[g * per_set + u]], rows_v[st * per_set + u],
                                         gather_sems[st * per_set + u])

        def store_copy(g, st, u):
            return pltpu.make_async_copy(rows_v[st * per_set + u],
                                         out_hbm.at[pl.ds((base + g * per_set + u) * win, win)], store_sems[st])

        def issue(g, st):
            for u in range(per_set):
                gather_copy(g, st, u).start()

        def drain(g, st):
            for u in range(per_set):
                gather_copy(g, st, u).wait()
                store_copy(g, st, u).start()

        def wait_stores(g, st):
            for u in range(per_set):
                store_copy(g, st, u).wait()

        issue(0, 0)

        @pl.loop(0, n_groups, step=2)
        def _(g):
            @pl.when(g > 0)
            def _():
                wait_stores(g - 1, 1)
            issue(g + 1, 1)
            drain(g, 0)
            wait_stores(g, 0)

            @pl.when(g + 2 < n_groups)
            def _():
                issue(g + 2, 0)
            drain(g + 1, 1)

        wait_stores(n_groups - 1, 1)

    return gather(table, pos.reshape(n // win, win))


def _partial_cols(pv, m):
    n = pv.shape[1]
    stacked = jnp.concatenate([pv, jnp.zeros((LANES - MOBA_VROWS, n), F32)], axis=0)
    row = lax.broadcasted_iota(jnp.int32, (LANES, n), 0)
    return jnp.where(row == HEAD_DIM + 1, m, stacked)


def _partial_rows(pv, m):
    return _partial_cols(pv, m).T


def _moba_sparse_kernel(blk_ref, q_ref, k_ref, v_ref, _after_ref, o_ref, *, n_groups):
    h, g = pl.program_id(0), pl.program_id(1)
    first = (h * n_groups + g) * MOBA_GROUP

    @pl.when(blk_ref[first] >= 0)
    def _():
        work = []
        for u in range(MOBA_GROUP):
            blk = jnp.maximum(blk_ref[first + u], 0)
            rows = pl.ds(pl.multiple_of(blk * MOBA_BLOCK, MOBA_BLOCK), MOBA_BLOCK)
            q = q_ref[u * MOBA_PAIR_TILE:(u + 1) * MOBA_PAIR_TILE, :].astype(BF16)
            work.append((rows, _dot_nt(q, k_ref[0, rows, :])))
        lane = lax.broadcasted_iota(jnp.int32, (MOBA_PAIR_TILE, LANES), 1)
        for u, (rows, s) in enumerate(work):
            m = jnp.max(s, axis=-1, keepdims=True)
            pv = _dot(jnp.exp(s - m).astype(BF16), v_ref[0, rows, :])
            o_ref[u * MOBA_PAIR_TILE:(u + 1) * MOBA_PAIR_TILE, :] = jnp.where(lane == HEAD_DIM + 1, m, pv)

    @pl.when(g == n_groups - 1)
    def _():
        lane = lax.broadcasted_iota(jnp.int32, (MOBA_PAIR_TILE, LANES), 1)
        o_ref[(MOBA_GROUP - 1) * MOBA_PAIR_TILE:, :] = jnp.where(lane == HEAD_DIM + 1, NEG, 0.0)


def _moba_sparse(q_sorted, k_pad, v_aug, tile_blk, n_tiles, after):
    n_heads, s, _ = k_pad.shape
    n_groups = n_tiles // MOBA_GROUP
    rows = pl.BlockSpec((MOBA_GROUP * MOBA_PAIR_TILE, LANES), lambda h, g, blk: (h * n_groups + g, 0))
    grid_spec = pltpu.PrefetchScalarGridSpec(
        num_scalar_prefetch=1, grid=(n_heads, n_groups),
        in_specs=[rows, pl.BlockSpec((1, s, LANES), lambda h, g, blk: (h, 0, 0)),
                  pl.BlockSpec((1, s, LANES), lambda h, g, blk: (h, 0, 0)), _ORDER_ONLY],
        out_specs=rows)
    return pl.pallas_call(
        functools.partial(_moba_sparse_kernel, n_groups=n_groups),
        grid_spec=grid_spec,
        out_shape=jax.ShapeDtypeStruct(q_sorted.shape, F32),
        compiler_params=_cparams("arbitrary", "arbitrary"),
        name="moba_sparse",
    )(tile_blk, q_sorted, k_pad, v_aug, after)


def _moba_merge_kernel(qa_ref, qb_ref, ka_ref, kb_ref, va_ref, vb_ref, ga_ref, gb_ref, _after_ref, o_ref):
    ki = lax.broadcasted_iota(jnp.int32, (MOBA_BLOCK, MOBA_BLOCK), 0)
    qi = lax.broadcasted_iota(jnp.int32, (MOBA_BLOCK, MOBA_BLOCK), 1)
    heads = ((qa_ref, ka_ref, va_ref, ga_ref), (qb_ref, kb_ref, vb_ref, gb_ref))
    chains = [(hh, b) for b in range(MOBA_MERGE_BLOCKS) for hh in range(2)]
    rows = lambda b: slice(b * MOBA_BLOCK, (b + 1) * MOBA_BLOCK)
    scores = [_dot_nt(heads[hh][1][0, rows(b), :], heads[hh][0][0, rows(b), :].astype(BF16))
              for hh, b in chains]
    outs = {}
    for (hh, b), s in zip(chains, scores):
        _, _, v_ref, g_ref = heads[hh]
        s = jnp.where(ki <= qi, s, NEG)
        m = jnp.max(s, axis=0, keepdims=True)
        parts = [_partial_cols(_dot(v_ref[0, b], jnp.exp(s - m).astype(BF16)), m)]
        parts += [g_ref[0, r, rows(b), :].T for r in range(MOBA_TOPK)]
        ms = [x[HEAD_DIM + 1:HEAD_DIM + 2, :] for x in parts]
        m_all = functools.reduce(jnp.maximum, ms)
        acc = sum(jnp.exp(mi - m_all) * x for mi, x in zip(ms, parts))
        outs[hh, b] = acc[:HEAD_DIM, :] / acc[HEAD_DIM:HEAD_DIM + 1, :]
    for b in range(MOBA_MERGE_BLOCKS):
        o_ref[rows(b), :] = jnp.concatenate([outs[0, b], outs[1, b]], axis=0).T.astype(BF16)


def _moba_merge(q_rows, k_pad, vt_aug, gathered, after):
    n_heads, s, _ = q_rows.shape
    t = MOBA_MERGE_BLOCKS * MOBA_BLOCK
    blk = lambda par: pl.BlockSpec((1, t, LANES), lambda p, i: (2 * p + par, i, 0))
    vspec = lambda par: pl.BlockSpec((1, MOBA_MERGE_BLOCKS, MOBA_VROWS, MOBA_BLOCK),
                                     lambda p, i: (2 * p + par, i, 0, 0))
    gspec = lambda par: pl.BlockSpec((1, MOBA_TOPK, t, LANES), lambda p, i: (2 * p + par, 0, i, 0))
    return pl.pallas_call(
        _moba_merge_kernel,
        grid=(n_heads // 2, s // t),
        in_specs=[blk(0), blk(1), blk(0), blk(1), vspec(0), vspec(1), gspec(0), gspec(1), _ORDER_ONLY],
        out_specs=pl.BlockSpec((t, LANES), lambda p, i: (i, p)),
        out_shape=jax.ShapeDtypeStruct((s, WIDTH), BF16),
        compiler_params=_cparams("arbitrary", "arbitrary"),
        name="moba_merge",
    )(q_rows, q_rows, k_pad, k_pad, vt_aug, vt_aug, gathered, gathered, after)


def _moba_attention(qc, kc, vc, kmean, overlap_a, overlap_b):
    s = qc.shape[0]
    n_tiles = _moba_tiles_per_head(s)
    q_rows, k_pad, v_aug, vt_aug, sel, rank, counts_raw = _moba_route(qc, kc, vc, _moba_mean_rows(kmean))
    counts = counts_raw[:, :s // MOBA_BLOCK, 0].astype(jnp.int32)
    pos, tile_blk = _moba_dispatch(sel, rank, counts, n_tiles)
    q_sorted = _sc_scatter_rows(q_rows.reshape(N_HEADS * s, LANES), pos, N_HEADS * n_tiles * MOBA_PAIR_TILE)
    def chain(calls, anchor):
        outs = []
        for call in calls:
            outs.append(call(anchor))
            anchor = outs[-1]
        return outs

    out_a = chain(overlap_a, counts_raw)
    partial = _moba_sparse(q_sorted, k_pad, v_aug, tile_blk, n_tiles, out_a[-1])
    gathered = _sc_gather_rows(partial, pos).reshape(N_HEADS, MOBA_TOPK, s, LANES)
    out_b = chain(overlap_b, partial)
    return out_a, out_b, _moba_merge(q_rows, k_pad, vt_aug, gathered, out_b[-1])


def _ssd_kernel(z_ref, xbc_ref, dt_ref, cw_ref, cb_ref, dtb_ref, alog_ref, dsk_ref, nw_ref, exp_ref, tri_ref,
                _after_ref, o_ref, state_ref, halo_ref):
    L = SSM_CHUNK

    @pl.when(pl.program_id(0) == 0)
    def _():
        state_ref[...] = jnp.zeros_like(state_ref)
        halo_ref[...] = jnp.zeros_like(halo_ref)

    cur = xbc_ref[...]
    prev8 = halo_ref[...]
    row8 = lax.broadcasted_iota(jnp.int32, (8, SSM_CONV_DIM), 0)
    conv = cb_ref[...] + cur * cw_ref[SSM_CONV - 1:SSM_CONV, :]
    for k in range(1, SSM_CONV):
        rolled = pltpu.roll(cur, k, 0)
        head = jnp.where(row8 < k, pltpu.roll(prev8, k, 0), rolled[0:8, :])
        shifted = jnp.concatenate([head, rolled[8:, :]], axis=0)
        conv = conv + shifted * cw_ref[SSM_CONV - 1 - k:SSM_CONV - k, :]
    halo_ref[...] = cur[L - 8:, :]
    act = _silu(conv)
    xs, bm, cm = act[:, :WIDTH], act[:, WIDTH:WIDTH + 2 * SSM_STATE], act[:, WIDTH + 2 * SSM_STATE:]

    dt = _softplus(dt_ref[...] + dtb_ref[...])
    da = dt * (-jnp.exp(alog_ref[...]))
    cs = _dot_exact_lhs(tri_ref[...], da)
    expand = exp_ref[...]
    cs_b = _dot_exact_rhs(cs, expand)
    dt_b = _dot_exact_rhs(dt, expand)
    cs_last = cs_b[L - 1:L, :]
    xdt = xs * dt_b
    cs_t = cs.T

    ri = lax.broadcasted_iota(jnp.int32, (L, L), 0)
    ci = lax.broadcasted_iota(jnp.int32, (L, L), 1)
    causal = ci <= ri
    lane = lax.broadcasted_iota(jnp.int32, (L, LANES), 1)
    bmb, cmb = bm.astype(BF16), cm.astype(BF16)
    xdtb = xdt.astype(BF16)
    y_pairs = []
    for p in range(N_HEADS // 2):
        g = (2 * p) // (N_HEADS // SSM_GROUPS)
        grp = slice(g * SSM_STATE, (g + 1) * SSM_STATE)
        cbm = _dot_nt(cmb[:, grp], bmb[:, grp])
        x_pair = xdtb[:, p * LANES:(p + 1) * LANES]
        halves = []
        for h in (2 * p, 2 * p + 1):
            seg = cs[:, h:h + 1] - cs_t[h:h + 1, :]
            decay = jnp.exp(jnp.where(causal, seg, NEG))
            halves.append(_dot((cbm * decay).astype(BF16), x_pair))
        y_pairs.append(jnp.where(lane < HEAD_DIM, halves[0], halves[1]))
    y_diag = jnp.concatenate(y_pairs, axis=1)

    state = state_ref[...]
    stb = state.astype(BF16)
    half = WIDTH // SSM_GROUPS
    y_off = jnp.concatenate(
        [_dot(cmb[:, g * SSM_STATE:(g + 1) * SSM_STATE], stb[:, g * half:(g + 1) * half])
         for g in range(SSM_GROUPS)], axis=1) * jnp.exp(cs_b)
    y = y_diag + y_off + xs * dsk_ref[...]

    w = (xdt * jnp.exp(cs_last - cs_b)).astype(BF16)
    bm_t = bm.T.astype(BF16)
    new = jnp.concatenate(
        [_dot(bm_t[g * SSM_STATE:(g + 1) * SSM_STATE, :], w[:, g * half:(g + 1) * half])
         for g in range(SSM_GROUPS)], axis=1)
    state_ref[...] = state * jnp.exp(cs_last) + new

    gt = y * _silu(z_ref[...])
    outs = []
    for g in range(SSM_GROUPS):
        gg = gt[:, g * half:(g + 1) * half]
        outs.append(gg * lax.rsqrt(jnp.mean(gg * gg, axis=-1, keepdims=True) + NORM_EPS))
    o_ref[...] = (jnp.concatenate(outs, axis=1) * nw_ref[...]).astype(BF16)


def _ssd_mixer(z, xbc, dt_raw, conv_w, conv_b, dt_bias, a_log, d_skip, norm_w, expand, tri, after):
    s = z.shape[0]
    L = SSM_CHUNK
    row = lambda n: pl.BlockSpec((L, n), lambda c: (c, 0))
    cst = lambda r, n: pl.BlockSpec((r, n), lambda c: (0, 0))
    return pl.pallas_call(
        _ssd_kernel,
        grid=(s // L,),
        in_specs=[row(WIDTH), row(SSM_CONV_DIM), row(DT_PAD), cst(SSM_CONV, SSM_CONV_DIM),
                  cst(1, SSM_CONV_DIM), cst(1, DT_PAD), cst(1, DT_PAD), cst(1, WIDTH), cst(1, WIDTH),
                  cst(DT_PAD, WIDTH), cst(L, L), _ORDER_ONLY],
        out_specs=row(WIDTH),
        out_shape=jax.ShapeDtypeStruct((s, WIDTH), BF16),
        scratch_shapes=[pltpu.VMEM((SSM_STATE, WIDTH), F32), pltpu.VMEM((8, SSM_CONV_DIM), F32)],
        compiler_params=_cparams("arbitrary"),
        cost_estimate=pl.CostEstimate(
            flops=2 * s * L * (SSM_GROUPS * SSM_STATE + WIDTH + 2 * N_HEADS * LANES) + 4 * s * SSM_STATE * WIDTH,
            transcendentals=s * (L * N_HEADS + SSM_CONV_DIM + 3 * WIDTH),
            bytes_accessed=s * (WIDTH + SSM_CONV_DIM + DT_PAD) * 4 + s * WIDTH * 2),
        name="ssd_mixer",
    )(z, xbc, dt_raw, conv_w, conv_b, dt_bias, a_log, d_skip, norm_w, expand, tri, after)


def _outproj_kernel(x_ref, *refs):
    *y_refs, w_ref, gate_ref, o_ref = refs
    y, row0 = 0.0, 0
    for y_ref in y_refs:
        y = y + _dot(y_ref[...], w_ref[row0:row0 + y_ref.shape[1], :])
        row0 += y_ref.shape[1]
    o_ref[...] = x_ref[...] + gate_ref[...] * y


def _out_projection(x, ys, w_all, gate, layer):
    _, s, d = x.shape
    tm = 512
    row = lambda n: pl.BlockSpec((tm, n), lambda i: (i, 0))
    assert sum(y.shape[1] for y in ys) == w_all.shape[1]
    return pl.pallas_call(
        _outproj_kernel,
        grid=(s // tm,),
        in_specs=[_stream_rows(tm, d)] + [row(y.shape[1]) for y in ys]
                 + [pl.BlockSpec((None, w_all.shape[1], d), lambda i: (layer, 0, 0)),
                    pl.BlockSpec((1, d), lambda i: (0, 0))],
        out_specs=_stream_rows(tm, d),
        out_shape=jax.ShapeDtypeStruct((1, s, d), F32),
        compiler_params=_cparams("arbitrary"),
        name="out_projection",
    )(x, *ys, w_all, gate)


def _ffn_kernel(x_ref, g_ref, sc_ref, sh_ref, gate_ref, wg_ref, wu_ref, wd_ref, o_ref, *, chunk):
    x = x_ref[...]
    h = _mod_norm(x, g_ref[...], sc_ref[...], sh_ref[...]).astype(BF16)
    hidden = wg_ref.shape[1]
    n_chunks = hidden // chunk
    cols = lambda c: slice(c * chunk, (c + 1) * chunk)
    gate_up = lambda c: (_dot(h, wg_ref[:, cols(c)]), _dot(h, wu_ref[:, cols(c)]))
    acc = jnp.zeros(x.shape, F32)
    nxt = gate_up(0)
    for c in range(n_chunks):
        g, u = nxt
        if c + 1 < n_chunks:
            nxt = gate_up(c + 1)
        acc = acc + _dot((_silu(g) * u).astype(BF16), wd_ref[cols(c), :])
    o_ref[...] = x + gate_ref[...] * acc


def _ffn(x, g, sc, sh, gate, w_gate, w_up, w_down, layer):
    _, s, d = x.shape
    hidden = w_gate.shape[2]
    tm = 512
    row = _stream_rows(tm, d)
    vec = pl.BlockSpec((1, d), lambda i: (0, 0))
    return pl.pallas_call(
        functools.partial(_ffn_kernel, chunk=256),
        grid=(s // tm,),
        in_specs=[row, vec, vec, vec, vec,
                  _resident((None, d, hidden), lambda i: (layer, 0, 0)),
                  _resident((None, d, hidden), lambda i: (layer, 0, 0)),
                  _resident((None, hidden, d), lambda i: (layer, 0, 0))],
        out_specs=row,
        out_shape=jax.ShapeDtypeStruct((1, s, d), F32),
        compiler_params=_cparams("arbitrary"),
        name="ffn",
    )(x, g, sc, sh, gate, w_gate, w_up, w_down)


def _rope_tables(seq):
    pos = jnp.arange(seq, dtype=F32)
    inv = ROPE_THETA ** (-jnp.arange(0, ROPE_DIM, 2, dtype=F32) / ROPE_DIM)
    ang = pos[:, None] * inv[None, :]
    cos, sin = jnp.cos(ang), jnp.sin(ang)
    ones = jnp.ones((seq, HEAD_DIM - ROPE_DIM), F32)
    zeros = jnp.zeros((seq, HEAD_DIM - ROPE_DIM), F32)
    zh = jnp.zeros((seq, ROPE_HALF), F32)
    cos_h = jnp.concatenate([cos, cos, ones], axis=1)
    sin_lo = jnp.concatenate([-sin, zh, zeros], axis=1)
    sin_hi = jnp.concatenate([zh, sin, zeros], axis=1)
    two = lambda t: jnp.concatenate([t, t], axis=1)
    return two(cos_h), two(sin_lo), two(sin_hi)


def _moba_mean_rows(kmean):
    nb = kmean.shape[0]
    km = kmean.reshape(nb, N_HEADS, HEAD_DIM).transpose(1, 0, 2)
    return jnp.pad(km, ((0, 0), (0, MOBA_MAX_BLOCKS - nb), (0, LANES - HEAD_DIM)))


def kernel(x, c, ada_w, ada_b, norm_mix, w_in, qn_swa, kn_swa, qn_moba, kn_moba, conv_w, conv_b, dt_bias,
           a_log, d_skip, ssm_norm, w_out, norm_ffn, w_gate, w_up, w_down):
    batch, seq, d = x.shape
    depth = ada_w.shape[0]
    assert batch == 1 and d == D_MODEL
    assert seq % SWA_SPAN == 0 and seq // MOBA_BLOCK <= MOBA_MAX_BLOCKS
    xs = x

    mod = _ada_modulation(c, ada_w, ada_b)
    rope_tabs = _rope_tables(seq)
    lane_head = jnp.arange(LANES) // HEAD_DIM
    hsum = (lane_head[:, None] == lane_head[None, :]).astype(BF16)
    expand = (jnp.arange(DT_PAD)[:, None] == (jnp.arange(WIDTH) // HEAD_DIM)[None, :]).astype(BF16)
    tri = (jnp.arange(SSM_CHUNK)[:, None] >= jnp.arange(SSM_CHUNK)[None, :]).astype(BF16)
    pad8 = lambda v: jnp.pad(v, (0, DT_PAD - N_HEADS)).reshape(1, DT_PAD)
    two = lambda g: jnp.concatenate([g, g]).reshape(1, LANES)

    w_in_b = _cast_bf16(w_in, IN_PROJ_PAD)
    w_out_b, w_gate_b, w_up_b, w_down_b = (_cast_bf16(w) for w in (w_out, w_gate, w_up, w_down))

    for l in range(depth):
        shift_m, scale_m, gate_m, shift_f, scale_f, gate_f = [mod[l, :, i * d:(i + 1) * d] for i in range(6)]
        gains = [two(g[l]) for g in (qn_swa, kn_swa, qn_moba, kn_moba)]
        qa, ka, va, qc, kc, vc, z, xbc, dt_raw, kmean = _in_projection(
            xs, norm_mix[l].reshape(1, d), scale_m, shift_m, w_in_b, l, gains, hsum, rope_tabs)

        n_pairs = WIDTH // LANES
        first = n_pairs - 1
        dilated_lo = functools.partial(_dilated_attention, qa, ka, va, 0, first)
        dilated_hi = functools.partial(_dilated_attention, qa, ka, va, first, n_pairs - first)
        ssd = functools.partial(
            _ssd_mixer, z, xbc, dt_raw, conv_w[l], conv_b[l].reshape(1, -1), pad8(dt_bias[l]), pad8(a_log[l]),
            jnp.repeat(d_skip[l], HEAD_DIM).reshape(1, WIDTH), ssm_norm[l].reshape(1, WIDTH), expand, tri)
        (y_a_lo,), (y_a_hi, y_b), y_c = _moba_attention(qc, kc, vc, kmean, (dilated_lo,), (dilated_hi, ssd))

        xs = _out_projection(xs, (y_a_lo, y_a_hi, y_b, y_c), w_out_b, gate_m, l)
        xs = _ffn(xs, norm_ffn[l].reshape(1, d), scale_f, shift_f, gate_f,
                  w_gate_b, w_up_b, w_down_b, l)
    return xs
```

```python
import functools

import jax
import jax.numpy as jnp
from jax import lax
from jax.experimental import pallas as pl
from jax.experimental.pallas import tpu as pltpu
from jax.experimental.pallas import tpu_sc as plsc

F32 = jnp.float32
BF16 = jnp.bfloat16

D_MODEL = 1024
HEAD_DIM = 64
N_HEADS = 8
WIDTH = N_HEADS * HEAD_DIM
ATTN_SCALE = HEAD_DIM ** -0.5
NORM_EPS = 1e-6
NEG = -1e30

ROPE_THETA = 500000.0
ROPE_DIM = HEAD_DIM // 4
ROPE_HALF = ROPE_DIM // 2

DILATIONS = (1, 4, 16)
SWA_BLOCK = 128
SWA_SPAN = DILATIONS[-1] * SWA_BLOCK
SWA_GROUP = 4

SSM_STATE = 128
SSM_GROUPS = 2
SSM_CONV = 4
SSM_CHUNK = 256
SSM_CONV_DIM = WIDTH + 2 * SSM_GROUPS * SSM_STATE

MOBA_BLOCK = 256
MOBA_TOPK = 3
MOBA_PAIR_TILE = 128
MOBA_GROUP = 32
SC_INDEX_WINDOW = 128
MOBA_MAX_BLOCKS = 64
MOBA_MERGE_BLOCKS = 8
MOBA_VROWS = HEAD_DIM + 16

LANES = 128
DT_PAD = LANES
IN_PROJ_PAD = 6 * WIDTH + WIDTH + SSM_CONV_DIM + DT_PAD
OFF_A, OFF_C, OFF_Z, OFF_XBC, OFF_DT = 0, 3 * WIDTH, 6 * WIDTH, 7 * WIDTH, 7 * WIDTH + SSM_CONV_DIM

VMEM_LIMIT = 56 * 1024 * 1024
CAST_BLOCK_BYTES = 6 * 1024 * 1024


_ORDER_ONLY = pl.BlockSpec(memory_space=pl.ANY)


def _stream_rows(tm, d):
    return pl.BlockSpec((None, tm, d), lambda i: (0, i, 0))


def _resident(block_shape, index_map):
    return pl.BlockSpec(block_shape, index_map, pipeline_mode=pl.Buffered(1))


def _cparams(*sem):
    return pltpu.CompilerParams(dimension_semantics=sem, vmem_limit_bytes=VMEM_LIMIT)


def _split3(a):
    hi = a.astype(BF16)
    r1 = a - hi.astype(F32)
    mid = r1.astype(BF16)
    lo = (r1 - mid.astype(F32)).astype(BF16)
    return hi, mid, lo


def _dot(a, b):
    return jnp.dot(a, b, preferred_element_type=F32)


def _dot_nt(a, b):
    return lax.dot_general(a, b, (((1,), (1,)), ((), ())), preferred_element_type=F32)


def _dot_exact_rhs(a, b_bf16):
    hi, mid, lo = _split3(a)
    return _dot(hi, b_bf16) + _dot(mid, b_bf16) + _dot(lo, b_bf16)


def _dot_exact_lhs(a_bf16, b):
    hi, mid, lo = _split3(b)
    return _dot(a_bf16, hi) + _dot(a_bf16, mid) + _dot(a_bf16, lo)


def _dot_bf16x3(a, b, dot=_dot):
    ah = a.astype(BF16)
    al = (a - ah.astype(F32)).astype(BF16)
    bh = b.astype(BF16)
    bl = (b - bh.astype(F32)).astype(BF16)
    return dot(ah, bh) + (dot(ah, bl) + dot(al, bh))


def _silu(x):
    return x / (1.0 + jnp.exp(-x))


def _softplus(x):
    return jnp.maximum(x, 0.0) + jnp.log1p(jnp.exp(-jnp.abs(x)))


def _cast_kernel(w_ref, o_ref):
    cols = w_ref.shape[1]
    o_ref[:, :cols] = w_ref[...].astype(BF16)
    if o_ref.shape[1] > cols:
        o_ref[:, cols:] = jnp.zeros((o_ref.shape[0], o_ref.shape[1] - cols), BF16)


def _cast_bf16(w, pad_cols=None):
    depth, rows, cols = w.shape
    out_cols = pad_cols or cols
    tr = max(t for t in range(8, rows + 1, 8) if rows % t == 0 and t * cols * 4 <= CAST_BLOCK_BYTES)
    return pl.pallas_call(
        _cast_kernel,
        grid=(depth, rows // tr),
        in_specs=[pl.BlockSpec((None, tr, cols), lambda l, i: (l, i, 0))],
        out_specs=pl.BlockSpec((None, tr, out_cols), lambda l, i: (l, i, 0)),
        out_shape=jax.ShapeDtypeStruct((depth, rows, out_cols), BF16),
        compiler_params=_cparams("arbitrary", "arbitrary"),
        name="cast_weights",
    )(w)


def _ada_kernel(c_ref, w_ref, b_ref, o_ref):
    c = c_ref[...]
    o_ref[0] = _dot_bf16x3(_silu(c), w_ref[0]) + b_ref[0]


def _ada_modulation(c, ada_w, ada_b):
    depth, d, n = ada_w.shape
    tn = 1024
    c8 = jnp.broadcast_to(c, (8, d))
    out = pl.pallas_call(
        _ada_kernel,
        grid=(depth, n // tn),
        in_specs=[pl.BlockSpec((8, d), lambda l, j: (0, 0)),
                  pl.BlockSpec((1, d, tn), lambda l, j: (l, 0, j)),
                  pl.BlockSpec((1, 1, tn), lambda l, j: (l, 0, j))],
        out_specs=pl.BlockSpec((1, 8, tn), lambda l, j: (l, 0, j)),
        out_shape=jax.ShapeDtypeStruct((depth, 8, n), F32),
        compiler_params=_cparams("arbitrary", "arbitrary"),
        name="ada_modulation",
    )(c8, ada_w, ada_b.reshape(depth, 1, n))
    return out[:, 0:1, :]


def _mod_norm(x, g, sc, sh):
    var = jnp.mean(x * x, axis=-1, keepdims=True)
    return (x * lax.rsqrt(var + NORM_EPS) * g) * (1.0 + sc) + sh


def _inproj_kernel(x_ref, g_ref, sc_ref, sh_ref, w_ref, gqa_ref, gka_ref, gqc_ref, gkc_ref, hsum_ref,
                   cos_ref, slo_ref, shi_ref,
                   qa_ref, ka_ref, va_ref, qc_ref, kc_ref, vc_ref, z_ref, xbc_ref, dt_ref, km_ref):
    h = _mod_norm(x_ref[...], g_ref[...], sc_ref[...], sh_ref[...]).astype(BF16)
    proj = lambda off, n=WIDTH: _dot(h, w_ref[:, off:off + n])
    raw = [proj(OFF_A), proj(OFF_A + WIDTH), proj(OFF_C), proj(OFF_C + WIDTH)]
    va_ref[...] = proj(OFF_A + 2 * WIDTH)
    vc_ref[...] = proj(OFF_C + 2 * WIDTH)
    z_ref[...] = proj(OFF_Z)

    hsum = hsum_ref[...]
    cos_t, slo, shi = cos_ref[...], slo_ref[...], shi_ref[...]
    rope = lambda x, gain_ref: _head_norm_rope(x, gain_ref[...], hsum, cos_t, slo, shi)
    n_blocks = x_ref.shape[0] // MOBA_BLOCK
    for p in range(WIDTH // LANES):
        lo, hi = p * LANES, (p + 1) * LANES
        qa_ref[:, lo:hi] = rope(raw[0][:, lo:hi], gqa_ref) * ATTN_SCALE
        ka_ref[:, lo:hi] = rope(raw[1][:, lo:hi], gka_ref)
        qc_ref[:, lo:hi] = rope(raw[2][:, lo:hi], gqc_ref)
        kc = rope(raw[3][:, lo:hi], gkc_ref)
        kc_ref[:, lo:hi] = kc
        for b in range(n_blocks):
            km_ref[b, :, lo:hi] = jnp.mean(kc[b * MOBA_BLOCK:(b + 1) * MOBA_BLOCK, :], axis=0, keepdims=True)
    xbc_ref[...] = proj(OFF_XBC, SSM_CONV_DIM)
    dt_ref[...] = proj(OFF_DT, DT_PAD)


def _in_projection(x, g, sc, sh, w_all, layer, gains, hsum, rope_tabs):
    _, s, d = x.shape
    tm = 512
    assert tm % MOBA_BLOCK == 0
    row = lambda n: pl.BlockSpec((tm, n), lambda i: (i, 0))
    vec = pl.BlockSpec((1, d), lambda i: (0, 0))
    cst = lambda r, n: pl.BlockSpec((r, n), lambda i: (0, 0))
    widths = [WIDTH] * 7 + [SSM_CONV_DIM, DT_PAD]
    return pl.pallas_call(
        _inproj_kernel,
        grid=(s // tm,),
        in_specs=[_stream_rows(tm, d), vec, vec, vec, _resident((None, d, IN_PROJ_PAD), lambda i: (layer, 0, 0))]
                 + [cst(1, LANES)] * 4 + [cst(LANES, LANES)] + [row(LANES)] * 3,
        out_specs=[row(n) for n in widths]
                  + [pl.BlockSpec((tm // MOBA_BLOCK, 1, WIDTH), lambda i: (i, 0, 0))],
        out_shape=[jax.ShapeDtypeStruct((s, n), F32) for n in widths]
                  + [jax.ShapeDtypeStruct((s // MOBA_BLOCK, 1, WIDTH), F32)],
        compiler_params=_cparams("arbitrary"),
        name="in_projection",
    )(x, g, sc, sh, w_all, *gains, hsum, *rope_tabs)


def _head_norm_rope(x, gain, hsum, cos_t, sin_lo, sin_hi):
    sq = x * x
    sq_hi = sq.astype(BF16)
    ss = _dot(sq_hi, hsum) + _dot((sq - sq_hi.astype(F32)).astype(BF16), hsum)
    y = x * lax.rsqrt(ss * (1.0 / HEAD_DIM) + NORM_EPS) * gain
    return (y * cos_t + pltpu.roll(y, ROPE_HALF, 1) * sin_hi
            + pltpu.roll(y, LANES - ROPE_HALF, 1) * sin_lo)


def _dilated_kernel(q_ref, kc_ref, kp_ref, vc_ref, vp_ref, _after_ref, o_ref, m_s, l_s, a_s, q4_s, k4_s, v4_s):
    has_prev = pl.program_id(1) > 0
    lane = lax.broadcasted_iota(jnp.int32, (SWA_BLOCK, LANES), 1)
    head_a = lane < HEAD_DIM
    qi = lax.broadcasted_iota(jnp.int32, (SWA_BLOCK, 2 * SWA_BLOCK), 0)
    ki = lax.broadcasted_iota(jnp.int32, (SWA_BLOCK, 2 * SWA_BLOCK), 1)
    band = (ki >= qi) & (ki <= qi + SWA_BLOCK)
    band_edge = band & ((ki >= SWA_BLOCK) | has_prev)

    quarter = SWA_SPAN // 4
    for r4 in range(4):
        src = pl.ds(r4, quarter, 4)
        q4_s[r4 * quarter:(r4 + 1) * quarter, :] = q_ref[src, :]
        for dst, prev, cur in ((k4_s, kp_ref, kc_ref), (v4_s, vp_ref, vc_ref)):
            dst[2 * r4 * quarter:(2 * r4 + 1) * quarter, :] = prev[src, :]
            dst[(2 * r4 + 1) * quarter:(2 * r4 + 2) * quarter, :] = cur[src, :]

    def unit_d16(a, r4):
        rows = pl.ds(4 * a + r4, SWA_BLOCK, 16)
        q = q4_s[pl.ds(r4 * quarter + a, SWA_BLOCK, 4), :]
        prev = pl.ds(2 * r4 * quarter + a, SWA_BLOCK, 4)
        cur = pl.ds((2 * r4 + 1) * quarter + a, SWA_BLOCK, 4)
        k_cat = jnp.concatenate([k4_s[prev, :], k4_s[cur, :]], axis=0)
        v_cat = jnp.concatenate([v4_s[prev, :], v4_s[cur, :]], axis=0)
        return rows, q, k_cat, v_cat, band_edge

    def unit_d4(r, nb):
        rows = pl.ds(4 * SWA_BLOCK * nb + r, SWA_BLOCK, 4)
        q = q4_s[pl.ds(r * quarter + SWA_BLOCK * nb, SWA_BLOCK), :]
        keys = pl.ds((2 * r + 1) * quarter + SWA_BLOCK * (nb - 1), 2 * SWA_BLOCK)
        return rows, q, k4_s[keys, :], v4_s[keys, :], (band_edge if nb == 0 else band)

    def unit_d1(b, edge=False):
        rows = pl.ds(SWA_BLOCK * b, SWA_BLOCK)
        if edge:
            prev = pl.ds(SWA_SPAN - SWA_BLOCK, SWA_BLOCK)
            k_cat = jnp.concatenate([kp_ref[prev, :], kc_ref[rows, :]], axis=0)
            v_cat = jnp.concatenate([vp_ref[prev, :], vc_ref[rows, :]], axis=0)
            return rows, q_ref[rows, :], k_cat, v_cat, band_edge
        keys = pl.ds(SWA_BLOCK * (b - 1), 2 * SWA_BLOCK)
        return rows, q_ref[rows, :], kc_ref[keys, :], vc_ref[keys, :], band

    def attend(units, mode):
        scores = []
        for _, q, k_cat, _, mask in units:
            kb = k_cat.astype(BF16)
            scores.append([jnp.where(mask, _dot_nt(jnp.where(sel, q, 0.0).astype(BF16), kb), NEG)
                           for sel in (head_a, ~head_a)])
        probs = []
        for pair in scores:
            stats = []
            for s in pair:
                m_loc = jnp.max(s, axis=-1, keepdims=True)
                p = jnp.exp(s - m_loc)
                stats.append((m_loc, jnp.sum(p, axis=-1, keepdims=True), p.astype(BF16)))
            probs.append(stats)
        for (rows, _, _, v_cat, _), ((m_a, s_a, p_a), (m_b, s_b, p_b)) in zip(units, probs):
            vb = v_cat.astype(BF16)
            m_new = jnp.where(head_a, m_a, m_b)
            l_new = jnp.where(head_a, s_a, s_b)
            a_new = jnp.where(head_a, _dot(p_a, vb), _dot(p_b, vb))
            if mode != "init":
                m_in, m_loc = m_s[rows, :], m_new
                m_new = jnp.maximum(m_in, m_loc)
                alpha, beta = jnp.exp(m_in - m_new), jnp.exp(m_loc - m_new)
                l_new = alpha * l_s[rows, :] + beta * l_new
                a_new = alpha * a_s[rows, :] + beta * a_new
            if mode == "final":
                o_ref[rows, :] = (a_new / l_new).astype(BF16)
            else:
                m_s[rows, :] = m_new
                l_s[rows, :] = l_new
                a_s[rows, :] = a_new

    g = SWA_GROUP
    assert DILATIONS == (1, 4, 16) and g == 4 and SWA_SPAN // (4 * SWA_BLOCK) == g

    def body16(a, carry):
        attend([unit_d16(a, r4) for r4 in range(g)], "init")
        return carry
    lax.fori_loop(0, 16 // g, body16, 0)

    def body4(r, carry):
        attend([unit_d4(r, nb) for nb in range(g)], "fold")
        return carry
    lax.fori_loop(0, 4, body4, 0)

    attend([unit_d1(b, edge=(b == 0)) for b in range(g)], "final")

    def body1(i, carry):
        attend([unit_d1(i * g + j) for j in range(g)], "final")
        return carry
    lax.fori_loop(1, SWA_SPAN // SWA_BLOCK // g, body1, 0)


def _dilated_attention(q, k, v, pair0, n_pairs, after):
    s = q.shape[0]
    cur = lambda off: pl.BlockSpec((SWA_SPAN, LANES), lambda p, n: (n, off + p))
    prev = lambda off: pl.BlockSpec((SWA_SPAN, LANES), lambda p, n: (jnp.maximum(n - 1, 0), off + p))
    width = n_pairs * LANES
    return pl.pallas_call(
        _dilated_kernel,
        grid=(n_pairs, s // SWA_SPAN),
        in_specs=[cur(pair0), cur(pair0), prev(pair0), cur(pair0), prev(pair0), _ORDER_ONLY],
        out_specs=cur(0),
        out_shape=jax.ShapeDtypeStruct((s, width), BF16),
        scratch_shapes=[pltpu.VMEM((SWA_SPAN, LANES), F32)] * 4 + [pltpu.VMEM((2 * SWA_SPAN, LANES), F32)] * 2,
        compiler_params=_cparams("arbitrary", "arbitrary"),
        cost_estimate=pl.CostEstimate(
            flops=len(DILATIONS) * 4 * s * 2 * SWA_BLOCK * width,
            transcendentals=len(DILATIONS) * s * 2 * SWA_BLOCK * 2 * n_pairs,
            bytes_accessed=5 * s * width * 4 + s * width * 2),
        name="dilated_attention",
    )(q, k, k, v, v, after)


def _moba_route_kernel(q_ref, k_ref, v_ref, km_ref, sut_ref, ones_ref,
                       qrow_ref, kpad_ref, vaug_ref, vt_ref, sel_ref, rank_ref, cnt_ref, base_ref):
    i = pl.program_id(0)

    @pl.when(i == 0)
    def _():
        base_ref[...] = jnp.zeros_like(base_ref)

    lane = lax.broadcasted_iota(jnp.int32, (MOBA_BLOCK, LANES), 1)
    is_feat = lane < HEAD_DIM
    row = lax.broadcasted_iota(jnp.int32, (MOBA_MAX_BLOCKS, MOBA_BLOCK), 0)
    past = row < i
    ones_rows = jnp.where(
        lax.broadcasted_iota(jnp.int32, (MOBA_VROWS - HEAD_DIM, MOBA_BLOCK), 0) == 0, 1.0, 0.0)
    ninf = float("-inf")
    gates = []
    for h in range(N_HEADS):
        cols = slice((h // 2) * LANES, (h // 2 + 1) * LANES)
        q, k = q_ref[:, cols], k_ref[:, cols]
        v = v_ref[:, cols]
        v_t = v.T
        if h % 2:
            q, k, v = (pltpu.roll(x, HEAD_DIM, 1) for x in (q, k, v))
            v_t = v_t[HEAD_DIM:, :]
        else:
            v_t = v_t[:HEAD_DIM, :]
        qrow_ref[h] = jnp.where(is_feat, q * ATTN_SCALE, 0.0)
        kpad_ref[h] = jnp.where(is_feat, k, 0.0).astype(BF16)
        vaug_ref[h] = jnp.where(is_feat, v, jnp.where(lane == HEAD_DIM, 1.0, 0.0)).astype(BF16)
        vt_ref[h, 0] = jnp.concatenate([v_t, ones_rows], axis=0).astype(BF16)
        gates.append(_dot_bf16x3(km_ref[h], jnp.where(is_feat, q, 0.0), dot=_dot_nt))
    for h, gate_t in enumerate(gates):
        cand = jnp.where(past, gate_t, ninf)
        picks = []
        for _ in range(MOBA_TOPK):
            best = jnp.max(cand, axis=0, keepdims=True)
            idx = jnp.min(jnp.where(cand == best, row, 2 * LANES), axis=0, keepdims=True)
            ok = best > ninf
            pick = (row == idx) & ok
            cand = jnp.where(pick, ninf, cand)
            picks.append((idx, ok, pick))
        chosen = jnp.where(picks[0][2] | picks[1][2] | picks[2][2], 1.0, 0.0).astype(BF16)
        base = base_ref[h]
        rank_full = base + _dot(chosen, sut_ref[...])
        base_ref[h] = base + _dot(chosen, ones_ref[...])
        for r, (idx, ok, pick) in enumerate(picks):
            rank = jnp.sum(jnp.where(pick, rank_full, 0.0), axis=0, keepdims=True)
            sel_ref[h, r:r + 1, :] = jnp.where(ok, idx, -1)
            rank_ref[h, r:r + 1, :] = rank.astype(jnp.int32)

    @pl.when(i == pl.num_programs(0) - 1)
    def _():
        cnt_ref[...] = base_ref[...]


def _moba_route(qc, kc, vc, km_rows):
    s = qc.shape[0]
    nb = s // MOBA_BLOCK
    row = pl.BlockSpec((MOBA_BLOCK, WIDTH), lambda i: (i, 0))
    per_head = lambda dt: (pl.BlockSpec((N_HEADS, MOBA_BLOCK, LANES), lambda i: (0, i, 0)),
                           jax.ShapeDtypeStruct((N_HEADS, s, LANES), dt))
    picks = (pl.BlockSpec((N_HEADS, MOBA_TOPK, MOBA_BLOCK), lambda i: (0, 0, i)),
             jax.ShapeDtypeStruct((N_HEADS, MOBA_TOPK, s), jnp.int32))
    outs = [per_head(F32), per_head(BF16), per_head(BF16),
            (pl.BlockSpec((N_HEADS, 1, MOBA_VROWS, MOBA_BLOCK), lambda i: (0, i, 0, 0)),
             jax.ShapeDtypeStruct((N_HEADS, nb, MOBA_VROWS, MOBA_BLOCK), BF16)),
            picks, picks,
            (pl.BlockSpec((N_HEADS, MOBA_MAX_BLOCKS, MOBA_BLOCK), lambda i: (0, 0, 0)),
             jax.ShapeDtypeStruct((N_HEADS, MOBA_MAX_BLOCKS, MOBA_BLOCK), F32))]
    qi = jnp.arange(MOBA_BLOCK)
    strict_upper = (qi[:, None] < qi[None, :]).astype(BF16)
    all_ones = jnp.ones((MOBA_BLOCK, MOBA_BLOCK), BF16)
    sq = pl.BlockSpec((MOBA_BLOCK, MOBA_BLOCK), lambda i: (0, 0))
    return pl.pallas_call(
        _moba_route_kernel,
        grid=(nb,),
        in_specs=[row, row, row,
                  pl.BlockSpec((N_HEADS, MOBA_MAX_BLOCKS, LANES), lambda i: (0, 0, 0)), sq, sq],
        out_specs=[o[0] for o in outs],
        out_shape=[o[1] for o in outs],
        scratch_shapes=[pltpu.VMEM((N_HEADS, MOBA_MAX_BLOCKS, MOBA_BLOCK), F32)],
        compiler_params=_cparams("arbitrary"),
        name="moba_route",
    )(qc, kc, vc, km_rows, strict_upper, all_ones)


def _moba_tiles_per_head(seq):
    tiles = MOBA_TOPK * seq // MOBA_PAIR_TILE + seq // MOBA_BLOCK
    return (tiles // MOBA_GROUP + 1) * MOBA_GROUP


def _moba_dispatch(sel, rank, counts, n_tiles):
    n_heads, nb = counts.shape
    padded = (counts + MOBA_PAIR_TILE - 1) // MOBA_PAIR_TILE * MOBA_PAIR_TILE
    ends = jnp.cumsum(padded, axis=1)
    offs = ends - padded
    rows_per_head = n_tiles * MOBA_PAIR_TILE
    head_base = (jnp.arange(n_heads, dtype=jnp.int32) * rows_per_head)[:, None, None]
    blocks = jnp.arange(nb, dtype=jnp.int32)
    off_sel = jnp.sum(jnp.where(sel[..., None] == blocks, offs[:, None, None, :], 0), axis=-1)
    null_row = rows_per_head - MOBA_PAIR_TILE + jnp.arange(sel.shape[-1], dtype=jnp.int32) % MOBA_PAIR_TILE
    pos = head_base + jnp.where(sel >= 0, off_sel + rank, null_row)
    tile_start = jnp.arange(n_tiles, dtype=jnp.int32) * MOBA_PAIR_TILE
    blk = jnp.sum(ends[:, None, :] <= tile_start[None, :, None], axis=2)
    tile_blk = jnp.where(tile_start[None, :] < ends[:, -1:], blk, -1)
    return pos.reshape(-1).astype(jnp.int32), tile_blk.reshape(-1).astype(jnp.int32)


def _sc_workers():
    info = plsc.get_sparse_core_info()
    return info.num_cores, info.num_cores * info.num_subcores


def _sc_scatter_rows(rows, pos, n_out):
    n_src = rows.shape[0]
    seq = n_src // N_HEADS
    n_cores, n_workers = _sc_workers()
    win = SC_INDEX_WINDOW
    wins_per_worker = n_src // win // n_workers
    wins_per_head = seq // win
    assert n_src % (win * n_workers) == 0
    mesh = plsc.VectorSubcoreMesh(core_axis_name="c", subcore_axis_name="s")

    per_set = 2
    n_groups = wins_per_worker // per_set
    assert wins_per_worker % (2 * per_set) == 0 and wins_per_head % wins_per_worker == 0

    @functools.partial(
        pl.kernel, mesh=mesh, out_type=jax.ShapeDtypeStruct((n_out, LANES), F32), name="moba_scatter_rows",
        cost_estimate=pl.CostEstimate(flops=0, transcendentals=0,
                                      bytes_accessed=(1 + MOBA_TOPK) * n_src * LANES * 4 + pos.size * 4),
        scratch_types=[pltpu.VMEM((wins_per_worker, win), jnp.int32)] * MOBA_TOPK
                      + [pltpu.VMEM((win, LANES), F32)] * (2 * per_set)
                      + [pltpu.SemaphoreType.DMA] * (2 * per_set + 2))
    def scatter(src_hbm, idx_hbm, out_hbm, *scratch):
        idx_all = scratch[:MOBA_TOPK]
        rows_v = scratch[MOBA_TOPK:MOBA_TOPK + 2 * per_set]
        load_sems, scat_sems = scratch[-2 * per_set - 2:-2], scratch[-2:]
        wid = lax.axis_index("s") * n_cores + lax.axis_index("c")
        base = wid * wins_per_worker
        h, qw0 = base // wins_per_head, base % wins_per_head
        for r in range(MOBA_TOPK):
            pltpu.sync_copy(idx_hbm.at[pl.ds((h * MOBA_TOPK + r) * wins_per_head + qw0, wins_per_worker)], idx_all[r])

        def load_copy(g, st, u):
            return pltpu.make_async_copy(src_hbm.at[pl.ds((base + g * per_set + u) * win, win)],
                                         rows_v[st * per_set + u], load_sems[st * per_set + u])

        def scat_copy(g, st, u, r):
            return pltpu.make_async_copy(rows_v[st * per_set + u], out_hbm.at[idx_all[r].at[g * per_set + u]],
                                         scat_sems[st])

        def issue(g, st):
            for u in range(per_set):
                load_copy(g, st, u).start()

        def drain(g, st):
            for u in range(per_set):
                load_copy(g, st, u).wait()
                for r in range(MOBA_TOPK):
                    scat_copy(g, st, u, r).start()

        def wait_scatters(g, st):
            for u in range(per_set):
                for r in range(MOBA_TOPK):
                    scat_copy(g, st, u, r).wait()

        issue(0, 0)

        @pl.loop(0, n_groups, step=2)
        def _(g):
            @pl.when(g > 0)
            def _():
                wait_scatters(g - 1, 1)
            issue(g + 1, 1)
            drain(g, 0)
            wait_scatters(g, 0)

            @pl.when(g + 2 < n_groups)
            def _():
                issue(g + 2, 0)
            drain(g + 1, 1)

        wait_scatters(n_groups - 1, 1)

    return scatter(rows, pos.reshape(pos.shape[0] // win, win))


def _sc_gather_rows(table, pos):
    n = pos.shape[0]
    n_cores, n_workers = _sc_workers()
    win = SC_INDEX_WINDOW
    wins_per_worker = n // win // n_workers
    assert n % (win * n_workers) == 0
    mesh = plsc.VectorSubcoreMesh(core_axis_name="c", subcore_axis_name="s")

    per_set = 2
    n_groups = wins_per_worker // per_set
    assert wins_per_worker % (2 * per_set) == 0

    @functools.partial(
        pl.kernel, mesh=mesh, out_type=jax.ShapeDtypeStruct((n, LANES), F32), name="moba_gather_rows",
        cost_estimate=pl.CostEstimate(flops=0, transcendentals=0, bytes_accessed=2 * n * LANES * 4 + n * 4),
        scratch_types=[pltpu.VMEM((wins_per_worker, win), jnp.int32)] + [pltpu.VMEM((win, LANES), F32)] * (2 * per_set)
                      + [pltpu.SemaphoreType.DMA] * (2 * per_set + 2))
    def gather(table_hbm, idx_hbm, out_hbm, idx_all, *scratch):
        rows_v = scratch[:2 * per_set]
        gather_sems, store_sems = scratch[2 * per_set:4 * per_set], scratch[4 * per_set:]
        wid = lax.axis_index("s") * n_cores + lax.axis_index("c")
        base = wid * wins_per_worker
        pltpu.sync_copy(idx_hbm.at[pl.ds(base, wins_per_worker)], idx_all)

        def gather_copy(g, st, u):
            return pltpu.make_async_copy(table_hbm.at[idx_all.at[g * per_set + u]], rows_v[st * per_set + u],
                                         gather_sems[st * per_set + u])

        def store_copy(g, st, u):
            return pltpu.make_async_copy(rows_v[st * per_set + u],
                                         out_hbm.at[pl.ds((base + g * per_set + u) * win, win)], store_sems[st])

        def issue(g, st):
            for u in range(per_set):
                gather_copy(g, st, u).start()

        def drain(g, st):
            for u in range(per_set):
                gather_copy(g, st, u).wait()
                store_copy(g, st, u).start()

        def wait_stores(g, st):
            for u in range(per_set):
                store_copy(g, st, u).wait()

        issue(0, 0)

        @pl.loop(0, n_groups, step=2)
        def _(g):
            @pl.when(g > 0)
            def _():
                wait_stores(g - 1, 1)
            issue(g + 1, 1)
            drain(g, 0)
            wait_stores(g, 0)

            @pl.when(g + 2 < n_groups)
            def _():
                issue(g + 2, 0)
            drain(g + 1, 1)

        wait_stores(n_groups - 1, 1)

    return gather(table, pos.reshape(n // win, win))


def _partial_cols(pv, m):
    n = pv.shape[1]
    stacked = jnp.concatenate([pv, jnp.zeros((LANES - MOBA_VROWS, n), F32)], axis=0)
    row = lax.broadcasted_iota(jnp.int32, (LANES, n), 0)
    return jnp.where(row == HEAD_DIM + 1, m, stacked)


def _moba_sparse_kernel(blk_ref, q_ref, k_ref, v_ref, _after_ref, o_ref, *, n_groups):
    h, g = pl.program_id(0), pl.program_id(1)
    first = (h * n_groups + g) * MOBA_GROUP

    @pl.when(blk_ref[first] >= 0)
    def _():
        work = []
        for u in range(MOBA_GROUP):
            blk = jnp.maximum(blk_ref[first + u], 0)
            rows = pl.ds(pl.multiple_of(blk * MOBA_BLOCK, MOBA_BLOCK), MOBA_BLOCK)
            q = q_ref[u * MOBA_PAIR_TILE:(u + 1) * MOBA_PAIR_TILE, :].astype(BF16)
            work.append((rows, _dot_nt(q, k_ref[0, rows, :])))
        lane = lax.broadcasted_iota(jnp.int32, (MOBA_PAIR_TILE, LANES), 1)
        for u, (rows, s) in enumerate(work):
            m = jnp.max(s, axis=-1, keepdims=True)
            pv = _dot(jnp.exp(s - m).astype(BF16), v_ref[0, rows, :])
            o_ref[u * MOBA_PAIR_TILE:(u + 1) * MOBA_PAIR_TILE, :] = jnp.where(lane == HEAD_DIM + 1, m, pv)

    @pl.when(g == n_groups - 1)
    def _():
        lane = lax.broadcasted_iota(jnp.int32, (MOBA_PAIR_TILE, LANES), 1)
        o_ref[(MOBA_GROUP - 1) * MOBA_PAIR_TILE:, :] = jnp.where(lane == HEAD_DIM + 1, NEG, 0.0)


def _moba_sparse(q_sorted, k_pad, v_aug, tile_blk, n_tiles, after):
    n_heads, s, _ = k_pad.shape
    n_groups = n_tiles // MOBA_GROUP
    rows = pl.BlockSpec((MOBA_GROUP * MOBA_PAIR_TILE, LANES), lambda h, g, blk: (h * n_groups + g, 0))
    grid_spec = pltpu.PrefetchScalarGridSpec(
        num_scalar_prefetch=1, grid=(n_heads, n_groups),
        in_specs=[rows, pl.BlockSpec((1, s, LANES), lambda h, g, blk: (h, 0, 0)),
                  pl.BlockSpec((1, s, LANES), lambda h, g, blk: (h, 0, 0)), _ORDER_ONLY],
        out_specs=rows)
    return pl.pallas_call(
        functools.partial(_moba_sparse_kernel, n_groups=n_groups),
        grid_spec=grid_spec,
        out_shape=jax.ShapeDtypeStruct(q_sorted.shape, F32),
        compiler_params=_cparams("arbitrary", "arbitrary"),
        name="moba_sparse",
    )(tile_blk, q_sorted, k_pad, v_aug, after)


def _moba_merge_kernel(qa_ref, qb_ref, ka_ref, kb_ref, va_ref, vb_ref, ga_ref, gb_ref, _after_ref, o_ref):
    ki = lax.broadcasted_iota(jnp.int32, (MOBA_BLOCK, MOBA_BLOCK), 0)
    qi = lax.broadcasted_iota(jnp.int32, (MOBA_BLOCK, MOBA_BLOCK), 1)
    heads = ((qa_ref, ka_ref, va_ref, ga_ref), (qb_ref, kb_ref, vb_ref, gb_ref))
    chains = [(hh, b) for b in range(MOBA_MERGE_BLOCKS) for hh in range(2)]
    rows = lambda b: slice(b * MOBA_BLOCK, (b + 1) * MOBA_BLOCK)
    scores = [_dot_nt(heads[hh][1][0, rows(b), :], heads[hh][0][0, rows(b), :].astype(BF16))
              for hh, b in chains]
    outs = {}
    for (hh, b), s in zip(chains, scores):
        _, _, v_ref, g_ref = heads[hh]
        s = jnp.where(ki <= qi, s, NEG)
        m = jnp.max(s, axis=0, keepdims=True)
        parts = [_partial_cols(_dot(v_ref[0, b], jnp.exp(s - m).astype(BF16)), m)]
        parts += [g_ref[0, r, rows(b), :].T for r in range(MOBA_TOPK)]
        ms = [x[HEAD_DIM + 1:HEAD_DIM + 2, :] for x in parts]
        m_all = functools.reduce(jnp.maximum, ms)
        acc = sum(jnp.exp(mi - m_all) * x for mi, x in zip(ms, parts))
        outs[hh, b] = acc[:HEAD_DIM, :] / acc[HEAD_DIM:HEAD_DIM + 1, :]
    for b in range(MOBA_MERGE_BLOCKS):
        o_ref[rows(b), :] = jnp.concatenate([outs[0, b], outs[1, b]], axis=0).T.astype(BF16)


def _moba_merge(q_rows, k_pad, vt_aug, gathered, after):
    n_heads, s, _ = q_rows.shape
    t = MOBA_MERGE_BLOCKS * MOBA_BLOCK
    blk = lambda par: pl.BlockSpec((1, t, LANES), lambda p, i: (2 * p + par, i, 0))
    vspec = lambda par: pl.BlockSpec((1, MOBA_MERGE_BLOCKS, MOBA_VROWS, MOBA_BLOCK),
                                     lambda p, i: (2 * p + par, i, 0, 0))
    gspec = lambda par: pl.BlockSpec((1, MOBA_TOPK, t, LANES), lambda p, i: (2 * p + par, 0, i, 0))
    return pl.pallas_call(
        _moba_merge_kernel,
        grid=(n_heads // 2, s // t),
        in_specs=[blk(0), blk(1), blk(0), blk(1), vspec(0), vspec(1), gspec(0), gspec(1), _ORDER_ONLY],
        out_specs=pl.BlockSpec((t, LANES), lambda p, i: (i, p)),
        out_shape=jax.ShapeDtypeStruct((s, WIDTH), BF16),
        compiler_params=_cparams("arbitrary", "arbitrary"),
        name="moba_merge",
    )(q_rows, q_rows, k_pad, k_pad, vt_aug, vt_aug, gathered, gathered, after)


def _moba_attention(qc, kc, vc, kmean, overlap_a, overlap_b):
    s = qc.shape[0]
    n_tiles = _moba_tiles_per_head(s)
    q_rows, k_pad, v_aug, vt_aug, sel, rank, counts_raw = _moba_route(qc, kc, vc, _moba_mean_rows(kmean))
    counts = counts_raw[:, :s // MOBA_BLOCK, 0].astype(jnp.int32)
    pos, tile_blk = _moba_dispatch(sel, rank, counts, n_tiles)
    q_sorted = _sc_scatter_rows(q_rows.reshape(N_HEADS * s, LANES), pos, N_HEADS * n_tiles * MOBA_PAIR_TILE)
    def chain(calls, anchor):
        outs = []
        for call in calls:
            outs.append(call(anchor))
            anchor = outs[-1]
        return outs

    out_a = chain(overlap_a, counts_raw)
    partial = _moba_sparse(q_sorted, k_pad, v_aug, tile_blk, n_tiles, out_a[-1])
    gathered = _sc_gather_rows(partial, pos).reshape(N_HEADS, MOBA_TOPK, s, LANES)
    out_b = chain(overlap_b, partial)
    return out_a, out_b, _moba_merge(q_rows, k_pad, vt_aug, gathered, out_b[-1])


def _ssd_kernel(z_ref, xbc_ref, dt_ref, cw_ref, cb_ref, dtb_ref, alog_ref, dsk_ref, nw_ref, exp_ref, tri_ref,
                _after_ref, o_ref, state_ref, halo_ref):
    L = SSM_CHUNK

    @pl.when(pl.program_id(0) == 0)
    def _():
        state_ref[...] = jnp.zeros_like(state_ref)
        halo_ref[...] = jnp.zeros_like(halo_ref)

    cur = xbc_ref[...]
    prev8 = halo_ref[...]
    row8 = lax.broadcasted_iota(jnp.int32, (8, SSM_CONV_DIM), 0)
    conv = cb_ref[...] + cur * cw_ref[SSM_CONV - 1:SSM_CONV, :]
    for k in range(1, SSM_CONV):
        rolled = pltpu.roll(cur, k, 0)
        head = jnp.where(row8 < k, pltpu.roll(prev8, k, 0), rolled[0:8, :])
        shifted = jnp.concatenate([head, rolled[8:, :]], axis=0)
        conv = conv + shifted * cw_ref[SSM_CONV - 1 - k:SSM_CONV - k, :]
    halo_ref[...] = cur[L - 8:, :]
    act = _silu(conv)
    xs, bm, cm = act[:, :WIDTH], act[:, WIDTH:WIDTH + 2 * SSM_STATE], act[:, WIDTH + 2 * SSM_STATE:]

    dt = _softplus(dt_ref[...] + dtb_ref[...])
    da = dt * (-jnp.exp(alog_ref[...]))
    cs = _dot_exact_lhs(tri_ref[...], da)
    expand = exp_ref[...]
    cs_b = _dot_exact_rhs(cs, expand)
    dt_b = _dot_exact_rhs(dt, expand)
    cs_last = cs_b[L - 1:L, :]
    xdt = xs * dt_b
    cs_t = cs.T

    ri = lax.broadcasted_iota(jnp.int32, (L, L), 0)
    ci = lax.broadcasted_iota(jnp.int32, (L, L), 1)
    causal = ci <= ri
    lane = lax.broadcasted_iota(jnp.int32, (L, LANES), 1)
    bmb, cmb = bm.astype(BF16), cm.astype(BF16)
    xdtb = xdt.astype(BF16)
    y_pairs = []
    for p in range(N_HEADS // 2):
        g = (2 * p) // (N_HEADS // SSM_GROUPS)
        grp = slice(g * SSM_STATE, (g + 1) * SSM_STATE)
        cbm = _dot_nt(cmb[:, grp], bmb[:, grp])
        x_pair = xdtb[:, p * LANES:(p + 1) * LANES]
        halves = []
        for h in (2 * p, 2 * p + 1):
            seg = cs[:, h:h + 1] - cs_t[h:h + 1, :]
            decay = jnp.exp(jnp.where(causal, seg, NEG))
            halves.append(_dot((cbm * decay).astype(BF16), x_pair))
        y_pairs.append(jnp.where(lane < HEAD_DIM, halves[0], halves[1]))
    y_diag = jnp.concatenate(y_pairs, axis=1)

    state = state_ref[...]
    stb = state.astype(BF16)
    half = WIDTH // SSM_GROUPS
    y_off = jnp.concatenate(
        [_dot(cmb[:, g * SSM_STATE:(g + 1) * SSM_STATE], stb[:, g * half:(g + 1) * half])
         for g in range(SSM_GROUPS)], axis=1) * jnp.exp(cs_b)
    y = y_diag + y_off + xs * dsk_ref[...]

    w = (xdt * jnp.exp(cs_last - cs_b)).astype(BF16)
    bm_t = bm.T.astype(BF16)
    new = jnp.concatenate(
        [_dot(bm_t[g * SSM_STATE:(g + 1) * SSM_STATE, :], w[:, g * half:(g + 1) * half])
         for g in range(SSM_GROUPS)], axis=1)
    state_ref[...] = state * jnp.exp(cs_last) + new

    gt = y * _silu(z_ref[...])
    outs = []
    for g in range(SSM_GROUPS):
        gg = gt[:, g * half:(g + 1) * half]
        outs.append(gg * lax.rsqrt(jnp.mean(gg * gg, axis=-1, keepdims=True) + NORM_EPS))
    o_ref[...] = (jnp.concatenate(outs, axis=1) * nw_ref[...]).astype(BF16)


def _ssd_mixer(z, xbc, dt_raw, conv_w, conv_b, dt_bias, a_log, d_skip, norm_w, expand, tri, after):
    s = z.shape[0]
    L = SSM_CHUNK
    row = lambda n: pl.BlockSpec((L, n), lambda c: (c, 0))
    cst = lambda r, n: pl.BlockSpec((r, n), lambda c: (0, 0))
    return pl.pallas_call(
        _ssd_kernel,
        grid=(s // L,),
        in_specs=[row(WIDTH), row(SSM_CONV_DIM), row(DT_PAD), cst(SSM_CONV, SSM_CONV_DIM),
                  cst(1, SSM_CONV_DIM), cst(1, DT_PAD), cst(1, DT_PAD), cst(1, WIDTH), cst(1, WIDTH),
                  cst(DT_PAD, WIDTH), cst(L, L), _ORDER_ONLY],
        out_specs=row(WIDTH),
        out_shape=jax.ShapeDtypeStruct((s, WIDTH), BF16),
        scratch_shapes=[pltpu.VMEM((SSM_STATE, WIDTH), F32), pltpu.VMEM((8, SSM_CONV_DIM), F32)],
        compiler_params=_cparams("arbitrary"),
        cost_estimate=pl.CostEstimate(
            flops=2 * s * L * (SSM_GROUPS * SSM_STATE + WIDTH + 2 * N_HEADS * LANES) + 4 * s * SSM_STATE * WIDTH,
            transcendentals=s * (L * N_HEADS + SSM_CONV_DIM + 3 * WIDTH),
            bytes_accessed=s * (WIDTH + SSM_CONV_DIM + DT_PAD) * 4 + s * WIDTH * 2),
        name="ssd_mixer",
    )(z, xbc, dt_raw, conv_w, conv_b, dt_bias, a_log, d_skip, norm_w, expand, tri, after)


def _mix_ffn_kernel(x_ref, *refs, chunk):
    *y_refs, wo_ref, gm_ref, g_ref, sc_ref, sh_ref, gate_ref, wg_ref, wu_ref, wd_ref, o_ref = refs
    y, row0 = 0.0, 0
    for y_ref in y_refs:
        y = y + _dot(y_ref[...], wo_ref[row0:row0 + y_ref.shape[1], :])
        row0 += y_ref.shape[1]
    x = x_ref[...] + gm_ref[...] * y
    h = _mod_norm(x, g_ref[...], sc_ref[...], sh_ref[...]).astype(BF16)
    hidden = wg_ref.shape[1]
    n_chunks = hidden // chunk
    cols = lambda c: slice(c * chunk, (c + 1) * chunk)
    gate_up = lambda c: (_dot(h, wg_ref[:, cols(c)]), _dot(h, wu_ref[:, cols(c)]))
    acc = jnp.zeros(x.shape, F32)
    nxt = gate_up(0)
    for c in range(n_chunks):
        g, u = nxt
        if c + 1 < n_chunks:
            nxt = gate_up(c + 1)
        acc = acc + _dot((_silu(g) * u).astype(BF16), wd_ref[cols(c), :])
    o_ref[...] = x + gate_ref[...] * acc


def _mix_ffn(x, ys, w_out, gate_m, g, sc, sh, gate_f, w_gate, w_up, w_down, layer):
    _, s, d = x.shape
    hidden = w_gate.shape[2]
    tm = 512
    row = _stream_rows(tm, d)
    vec = pl.BlockSpec((1, d), lambda i: (0, 0))
    assert sum(y.shape[1] for y in ys) == w_out.shape[1]
    return pl.pallas_call(
        functools.partial(_mix_ffn_kernel, chunk=256),
        grid=(s // tm,),
        in_specs=[row] + [pl.BlockSpec((tm, y.shape[1]), lambda i: (i, 0)) for y in ys]
                 + [_resident((None, w_out.shape[1], d), lambda i: (layer, 0, 0)), vec, vec, vec, vec, vec,
                    _resident((None, d, hidden), lambda i: (layer, 0, 0)),
                    _resident((None, d, hidden), lambda i: (layer, 0, 0)),
                    _resident((None, hidden, d), lambda i: (layer, 0, 0))],
        out_specs=row,
        out_shape=jax.ShapeDtypeStruct((1, s, d), F32),
        compiler_params=_cparams("arbitrary"),
        name="mix_ffn",
    )(x, *ys, w_out, gate_m, g, sc, sh, gate_f, w_gate, w_up, w_down)


def _rope_tables(seq):
    pos = jnp.arange(seq, dtype=F32)
    inv = ROPE_THETA ** (-jnp.arange(0, ROPE_DIM, 2, dtype=F32) / ROPE_DIM)
    ang = pos[:, None] * inv[None, :]
    cos, sin = jnp.cos(ang), jnp.sin(ang)
    ones = jnp.ones((seq, HEAD_DIM - ROPE_DIM), F32)
    zeros = jnp.zeros((seq, HEAD_DIM - ROPE_DIM), F32)
    zh = jnp.zeros((seq, ROPE_HALF), F32)
    cos_h = jnp.concatenate([cos, cos, ones], axis=1)
    sin_lo = jnp.concatenate([-sin, zh, zeros], axis=1)
    sin_hi = jnp.concatenate([zh, sin, zeros], axis=1)
    two = lambda t: jnp.concatenate([t, t], axis=1)
    return two(cos_h), two(sin_lo), two(sin_hi)


def _moba_mean_rows(kmean):
    nb = kmean.shape[0]
    km = kmean.reshape(nb, N_HEADS, HEAD_DIM).transpose(1, 0, 2)
    return jnp.pad(km, ((0, 0), (0, MOBA_MAX_BLOCKS - nb), (0, LANES - HEAD_DIM)))


def kernel(x, c, ada_w, ada_b, norm_mix, w_in, qn_swa, kn_swa, qn_moba, kn_moba, conv_w, conv_b, dt_bias,
           a_log, d_skip, ssm_norm, w_out, norm_ffn, w_gate, w_up, w_down):
    batch, seq, d = x.shape
    depth = ada_w.shape[0]
    assert batch == 1 and d == D_MODEL
    assert seq % SWA_SPAN == 0 and seq // MOBA_BLOCK <= MOBA_MAX_BLOCKS
    xs = x

    mod = _ada_modulation(c, ada_w, ada_b)
    rope_tabs = _rope_tables(seq)
    lane_head = jnp.arange(LANES) // HEAD_DIM
    hsum = (lane_head[:, None] == lane_head[None, :]).astype(BF16)
    expand = (jnp.arange(DT_PAD)[:, None] == (jnp.arange(WIDTH) // HEAD_DIM)[None, :]).astype(BF16)
    tri = (jnp.arange(SSM_CHUNK)[:, None] >= jnp.arange(SSM_CHUNK)[None, :]).astype(BF16)
    pad8 = lambda v: jnp.pad(v, (0, DT_PAD - N_HEADS)).reshape(1, DT_PAD)
    two = lambda g: jnp.concatenate([g, g]).reshape(1, LANES)

    w_in_b = _cast_bf16(w_in, IN_PROJ_PAD)
    w_out_b, w_gate_b, w_up_b, w_down_b = (_cast_bf16(w) for w in (w_out, w_gate, w_up, w_down))

    for l in range(depth):
        shift_m, scale_m, gate_m, shift_f, scale_f, gate_f = [mod[l, :, i * d:(i + 1) * d] for i in range(6)]
        gains = [two(g[l]) for g in (qn_swa, kn_swa, qn_moba, kn_moba)]
        qa, ka, va, qc, kc, vc, z, xbc, dt_raw, kmean = _in_projection(
            xs, norm_mix[l].reshape(1, d), scale_m, shift_m, w_in_b, l, gains, hsum, rope_tabs)

        n_pairs = WIDTH // LANES
        first = n_pairs - 1
        dilated_lo = functools.partial(_dilated_attention, qa, ka, va, 0, first)
        dilated_hi = functools.partial(_dilated_attention, qa, ka, va, first, n_pairs - first)
        ssd = functools.partial(
            _ssd_mixer, z, xbc, dt_raw, conv_w[l], conv_b[l].reshape(1, -1), pad8(dt_bias[l]), pad8(a_log[l]),
            jnp.repeat(d_skip[l], HEAD_DIM).reshape(1, WIDTH), ssm_norm[l].reshape(1, WIDTH), expand, tri)
        (y_a_lo,), (y_a_hi, y_b), y_c = _moba_attention(qc, kc, vc, kmean, (dilated_lo,), (dilated_hi, ssd))

        xs = _mix_ffn(xs, (y_a_lo, y_a_hi, y_b, y_c), w_out_b, gate_m, norm_ffn[l].reshape(1, d), scale_f,
                      shift_f, gate_f, w_gate_b, w_up_b, w_down_b, l)
    return xs
```

```python
import functools

import jax
import jax.numpy as jnp
from jax import lax
from jax.experimental import pallas as pl
from jax.experimental.pallas import tpu as pltpu
from jax.experimental.pallas import tpu_sc as plsc

F32 = jnp.float32
BF16 = jnp.bfloat16

D_MODEL = 1024
HEAD_DIM = 64
N_HEADS = 8
WIDTH = N_HEADS * HEAD_DIM
ATTN_SCALE = HEAD_DIM ** -0.5
NORM_EPS = 1e-6
NEG = -1e30

ROPE_THETA = 500000.0
ROPE_DIM = HEAD_DIM // 4
ROPE_HALF = ROPE_DIM // 2

DILATIONS = (1, 4, 16)
SWA_BLOCK = 128
SWA_SPAN = DILATIONS[-1] * SWA_BLOCK
SWA_GROUP = 4

SSM_STATE = 128
SSM_GROUPS = 2
SSM_CONV = 4
SSM_CHUNK = 256
SSM_CONV_DIM = WIDTH + 2 * SSM_GROUPS * SSM_STATE

MOBA_BLOCK = 256
MOBA_TOPK = 3
MOBA_PAIR_TILE = 128
MOBA_GROUP = 64
SC_INDEX_WINDOW = 128
MOBA_MAX_BLOCKS = 64
MOBA_MERGE_BLOCKS = 8
MOBA_VROWS = HEAD_DIM + 16

LANES = 128
DT_PAD = LANES
IN_PROJ_PAD = 6 * WIDTH + WIDTH + SSM_CONV_DIM + DT_PAD
OFF_A, OFF_C, OFF_Z, OFF_XBC, OFF_DT = 0, 3 * WIDTH, 6 * WIDTH, 7 * WIDTH, 7 * WIDTH + SSM_CONV_DIM

VMEM_LIMIT = 56 * 1024 * 1024
CAST_BLOCK_BYTES = 6 * 1024 * 1024


_ORDER_ONLY = pl.BlockSpec(memory_space=pl.ANY)


def _stream_rows(tm, d):
    return pl.BlockSpec((None, tm, d), lambda i: (0, i, 0))


def _resident(block_shape, index_map):
    return pl.BlockSpec(block_shape, index_map, pipeline_mode=pl.Buffered(1))


def _cparams(*sem):
    return pltpu.CompilerParams(dimension_semantics=sem, vmem_limit_bytes=VMEM_LIMIT)


def _split3(a):
    hi = a.astype(BF16)
    r1 = a - hi.astype(F32)
    mid = r1.astype(BF16)
    lo = (r1 - mid.astype(F32)).astype(BF16)
    return hi, mid, lo


def _dot(a, b):
    return jnp.dot(a, b, preferred_element_type=F32)


def _dot_nt(a, b):
    return lax.dot_general(a, b, (((1,), (1,)), ((), ())), preferred_element_type=F32)


def _dot_exact_rhs(a, b_bf16):
    hi, mid, lo = _split3(a)
    return _dot(hi, b_bf16) + _dot(mid, b_bf16) + _dot(lo, b_bf16)


def _dot_exact_lhs(a_bf16, b):
    hi, mid, lo = _split3(b)
    return _dot(a_bf16, hi) + _dot(a_bf16, mid) + _dot(a_bf16, lo)


def _dot_bf16x3(a, b, dot=_dot):
    ah = a.astype(BF16)
    al = (a - ah.astype(F32)).astype(BF16)
    bh = b.astype(BF16)
    bl = (b - bh.astype(F32)).astype(BF16)
    return dot(ah, bh) + (dot(ah, bl) + dot(al, bh))


def _silu(x):
    return x / (1.0 + jnp.exp(-x))


def _softplus(x):
    return jnp.maximum(x, 0.0) + jnp.log1p(jnp.exp(-jnp.abs(x)))


def _cast_kernel(w_ref, o_ref):
    cols = w_ref.shape[1]
    o_ref[:, :cols] = w_ref[...].astype(BF16)
    if o_ref.shape[1] > cols:
        o_ref[:, cols:] = jnp.zeros((o_ref.shape[0], o_ref.shape[1] - cols), BF16)


def _cast_bf16(w, pad_cols=None):
    depth, rows, cols = w.shape
    out_cols = pad_cols or cols
    tr = max(t for t in range(8, rows + 1, 8) if rows % t == 0 and t * cols * 4 <= CAST_BLOCK_BYTES)
    return pl.pallas_call(
        _cast_kernel,
        grid=(depth, rows // tr),
        in_specs=[pl.BlockSpec((None, tr, cols), lambda l, i: (l, i, 0))],
        out_specs=pl.BlockSpec((None, tr, out_cols), lambda l, i: (l, i, 0)),
        out_shape=jax.ShapeDtypeStruct((depth, rows, out_cols), BF16),
        compiler_params=_cparams("arbitrary", "arbitrary"),
        name="cast_weights",
    )(w)


def _ada_kernel(c_ref, w_ref, b_ref, o_ref):
    c = c_ref[...]
    o_ref[0] = _dot_bf16x3(_silu(c), w_ref[0]) + b_ref[0]


def _ada_modulation(c, ada_w, ada_b):
    depth, d, n = ada_w.shape
    tn = 1024
    c8 = jnp.broadcast_to(c, (8, d))
    out = pl.pallas_call(
        _ada_kernel,
        grid=(depth, n // tn),
        in_specs=[pl.BlockSpec((8, d), lambda l, j: (0, 0)),
                  pl.BlockSpec((1, d, tn), lambda l, j: (l, 0, j)),
                  pl.BlockSpec((1, 1, tn), lambda l, j: (l, 0, j))],
        out_specs=pl.BlockSpec((1, 8, tn), lambda l, j: (l, 0, j)),
        out_shape=jax.ShapeDtypeStruct((depth, 8, n), F32),
        compiler_params=_cparams("arbitrary", "arbitrary"),
        name="ada_modulation",
    )(c8, ada_w, ada_b.reshape(depth, 1, n))
    return out[:, 0:1, :]


def _mod_norm(x, g, sc, sh):
    var = jnp.mean(x * x, axis=-1, keepdims=True)
    return (x * lax.rsqrt(var + NORM_EPS) * g) * (1.0 + sc) + sh


def _inproj_kernel(x_ref, g_ref, sc_ref, sh_ref, w_ref, gqa_ref, gka_ref, gqc_ref, gkc_ref, hsum_ref,
                   cos_ref, slo_ref, shi_ref,
                   qa_ref, ka_ref, va_ref, qc_ref, kc_ref, vc_ref, z_ref, xbc_ref, dt_ref, km_ref):
    h = _mod_norm(x_ref[...], g_ref[...], sc_ref[...], sh_ref[...]).astype(BF16)
    proj = lambda off, n=WIDTH: _dot(h, w_ref[:, off:off + n])
    raw = [proj(OFF_A), proj(OFF_A + WIDTH), proj(OFF_C), proj(OFF_C + WIDTH)]
    va_ref[...] = proj(OFF_A + 2 * WIDTH)
    vc_ref[...] = proj(OFF_C + 2 * WIDTH)
    z_ref[...] = proj(OFF_Z)

    hsum = hsum_ref[...]
    cos_t, slo, shi = cos_ref[...], slo_ref[...], shi_ref[...]
    rope = lambda x, gain_ref: _head_norm_rope(x, gain_ref[...], hsum, cos_t, slo, shi)
    n_blocks = x_ref.shape[0] // MOBA_BLOCK
    for p in range(WIDTH // LANES):
        lo, hi = p * LANES, (p + 1) * LANES
        qa_ref[:, lo:hi] = rope(raw[0][:, lo:hi], gqa_ref) * ATTN_SCALE
        ka_ref[:, lo:hi] = rope(raw[1][:, lo:hi], gka_ref)
        qc_ref[:, lo:hi] = rope(raw[2][:, lo:hi], gqc_ref)
        kc = rope(raw[3][:, lo:hi], gkc_ref)
        kc_ref[:, lo:hi] = kc
        for b in range(n_blocks):
            km_ref[b, :, lo:hi] = jnp.mean(kc[b * MOBA_BLOCK:(b + 1) * MOBA_BLOCK, :], axis=0, keepdims=True)
    xbc_ref[...] = proj(OFF_XBC, SSM_CONV_DIM)
    dt_ref[...] = proj(OFF_DT, DT_PAD)


def _in_projection(x, g, sc, sh, w_all, layer, gains, hsum, rope_tabs):
    _, s, d = x.shape
    tm = 512
    assert tm % MOBA_BLOCK == 0
    row = lambda n: pl.BlockSpec((tm, n), lambda i: (i, 0))
    vec = pl.BlockSpec((1, d), lambda i: (0, 0))
    cst = lambda r, n: pl.BlockSpec((r, n), lambda i: (0, 0))
    widths = [WIDTH] * 7 + [SSM_CONV_DIM, DT_PAD]
    return pl.pallas_call(
        _inproj_kernel,
        grid=(s // tm,),
        in_specs=[_stream_rows(tm, d), vec, vec, vec, _resident((None, d, IN_PROJ_PAD), lambda i: (layer, 0, 0))]
                 + [cst(1, LANES)] * 4 + [cst(LANES, LANES)] + [row(LANES)] * 3,
        out_specs=[row(n) for n in widths]
                  + [pl.BlockSpec((tm // MOBA_BLOCK, 1, WIDTH), lambda i: (i, 0, 0))],
        out_shape=[jax.ShapeDtypeStruct((s, n), F32) for n in widths]
                  + [jax.ShapeDtypeStruct((s // MOBA_BLOCK, 1, WIDTH), F32)],
        compiler_params=_cparams("arbitrary"),
        name="in_projection",
    )(x, g, sc, sh, w_all, *gains, hsum, *rope_tabs)


def _head_norm_rope(x, gain, hsum, cos_t, sin_lo, sin_hi):
    sq = x * x
    sq_hi = sq.astype(BF16)
    ss = _dot(sq_hi, hsum) + _dot((sq - sq_hi.astype(F32)).astype(BF16), hsum)
    y = x * lax.rsqrt(ss * (1.0 / HEAD_DIM) + NORM_EPS) * gain
    return (y * cos_t + pltpu.roll(y, ROPE_HALF, 1) * sin_hi
            + pltpu.roll(y, LANES - ROPE_HALF, 1) * sin_lo)


def _dilated_kernel(q_ref, kc_ref, kp_ref, vc_ref, vp_ref, _after_ref, o_ref, m_s, l_s, a_s, q4_s, k4_s, v4_s):
    has_prev = pl.program_id(1) > 0
    lane = lax.broadcasted_iota(jnp.int32, (SWA_BLOCK, LANES), 1)
    head_a = lane < HEAD_DIM
    qi = lax.broadcasted_iota(jnp.int32, (SWA_BLOCK, 2 * SWA_BLOCK), 0)
    ki = lax.broadcasted_iota(jnp.int32, (SWA_BLOCK, 2 * SWA_BLOCK), 1)
    band = (ki >= qi) & (ki <= qi + SWA_BLOCK)
    band_edge = band & ((ki >= SWA_BLOCK) | has_prev)

    quarter = SWA_SPAN // 4
    for r4 in range(4):
        src = pl.ds(r4, quarter, 4)
        q4_s[r4 * quarter:(r4 + 1) * quarter, :] = q_ref[src, :]
        for dst, prev, cur in ((k4_s, kp_ref, kc_ref), (v4_s, vp_ref, vc_ref)):
            dst[2 * r4 * quarter:(2 * r4 + 1) * quarter, :] = prev[src, :]
            dst[(2 * r4 + 1) * quarter:(2 * r4 + 2) * quarter, :] = cur[src, :]

    def unit_d16(a, r4):
        rows = pl.ds(4 * a + r4, SWA_BLOCK, 16)
        q = q4_s[pl.ds(r4 * quarter + a, SWA_BLOCK, 4), :]
        prev = pl.ds(2 * r4 * quarter + a, SWA_BLOCK, 4)
        cur = pl.ds((2 * r4 + 1) * quarter + a, SWA_BLOCK, 4)
        k_cat = jnp.concatenate([k4_s[prev, :], k4_s[cur, :]], axis=0)
        v_cat = jnp.concatenate([v4_s[prev, :], v4_s[cur, :]], axis=0)
        return rows, q, k_cat, v_cat, band_edge

    def unit_d4(r, nb):
        rows = pl.ds(4 * SWA_BLOCK * nb + r, SWA_BLOCK, 4)
        q = q4_s[pl.ds(r * quarter + SWA_BLOCK * nb, SWA_BLOCK), :]
        keys = pl.ds((2 * r + 1) * quarter + SWA_BLOCK * (nb - 1), 2 * SWA_BLOCK)
        return rows, q, k4_s[keys, :], v4_s[keys, :], (band_edge if nb == 0 else band)

    def unit_d1(b, edge=False):
        rows = pl.ds(SWA_BLOCK * b, SWA_BLOCK)
        if edge:
            prev = pl.ds(SWA_SPAN - SWA_BLOCK, SWA_BLOCK)
            k_cat = jnp.concatenate([kp_ref[prev, :], kc_ref[rows, :]], axis=0)
            v_cat = jnp.concatenate([vp_ref[prev, :], vc_ref[rows, :]], axis=0)
            return rows, q_ref[rows, :], k_cat, v_cat, band_edge
        keys = pl.ds(SWA_BLOCK * (b - 1), 2 * SWA_BLOCK)
        return rows, q_ref[rows, :], kc_ref[keys, :], vc_ref[keys, :], band

    def attend(units, mode):
        scores = []
        for _, q, k_cat, _, mask in units:
            kb = k_cat.astype(BF16)
            scores.append([jnp.where(mask, _dot_nt(jnp.where(sel, q, 0.0).astype(BF16), kb), NEG)
                           for sel in (head_a, ~head_a)])
        probs = []
        for pair in scores:
            stats = []
            for s in pair:
                m_loc = jnp.max(s, axis=-1, keepdims=True)
                p = jnp.exp(s - m_loc)
                stats.append((m_loc, jnp.sum(p, axis=-1, keepdims=True), p.astype(BF16)))
            probs.append(stats)
        for (rows, _, _, v_cat, _), ((m_a, s_a, p_a), (m_b, s_b, p_b)) in zip(units, probs):
            vb = v_cat.astype(BF16)
            m_new = jnp.where(head_a, m_a, m_b)
            l_new = jnp.where(head_a, s_a, s_b)
            a_new = jnp.where(head_a, _dot(p_a, vb), _dot(p_b, vb))
            if mode != "init":
                m_in, m_loc = m_s[rows, :], m_new
                m_new = jnp.maximum(m_in, m_loc)
                alpha, beta = jnp.exp(m_in - m_new), jnp.exp(m_loc - m_new)
                l_new = alpha * l_s[rows, :] + beta * l_new
                a_new = alpha * a_s[rows, :] + beta * a_new
            if mode == "final":
                o_ref[rows, :] = (a_new / l_new).astype(BF16)
            else:
                m_s[rows, :] = m_new
                l_s[rows, :] = l_new
                a_s[rows, :] = a_new

    g = SWA_GROUP
    assert DILATIONS == (1, 4, 16) and g == 4 and SWA_SPAN // (4 * SWA_BLOCK) == g

    def body16(a, carry):
        attend([unit_d16(a, r4) for r4 in range(g)], "init")
        return carry
    lax.fori_loop(0, 16 // g, body16, 0)

    def body4(r, carry):
        attend([unit_d4(r, nb) for nb in range(g)], "fold")
        return carry
    lax.fori_loop(0, 4, body4, 0)

    attend([unit_d1(b, edge=(b == 0)) for b in range(g)], "final")

    def body1(i, carry):
        attend([unit_d1(i * g + j) for j in range(g)], "final")
        return carry
    lax.fori_loop(1, SWA_SPAN // SWA_BLOCK // g, body1, 0)


def _dilated_attention(q, k, v, pair0, n_pairs, after):
    s = q.shape[0]
    cur = lambda off: pl.BlockSpec((SWA_SPAN, LANES), lambda p, n: (n, off + p))
    prev = lambda off: pl.BlockSpec((SWA_SPAN, LANES), lambda p, n: (jnp.maximum(n - 1, 0), off + p))
    width = n_pairs * LANES
    return pl.pallas_call(
        _dilated_kernel,
        grid=(n_pairs, s // SWA_SPAN),
        in_specs=[cur(pair0), cur(pair0), prev(pair0), cur(pair0), prev(pair0), _ORDER_ONLY],
        out_specs=cur(0),
        out_shape=jax.ShapeDtypeStruct((s, width), BF16),
        scratch_shapes=[pltpu.VMEM((SWA_SPAN, LANES), F32)] * 4 + [pltpu.VMEM((2 * SWA_SPAN, LANES), F32)] * 2,
        compiler_params=_cparams("arbitrary", "arbitrary"),
        cost_estimate=pl.CostEstimate(
            flops=len(DILATIONS) * 4 * s * 2 * SWA_BLOCK * width,
            transcendentals=len(DILATIONS) * s * 2 * SWA_BLOCK * 2 * n_pairs,
            bytes_accessed=5 * s * width * 4 + s * width * 2),
        name="dilated_attention",
    )(q, k, k, v, v, after)


def _moba_route_kernel(q_ref, k_ref, v_ref, km_ref, sut_ref, ones_ref,
                       qrow_ref, kpad_ref, vaug_ref, vt_ref, sel_ref, rank_ref, cnt_ref, base_ref):
    i = pl.program_id(0)

    @pl.when(i == 0)
    def _():
        base_ref[...] = jnp.zeros_like(base_ref)

    lane = lax.broadcasted_iota(jnp.int32, (MOBA_BLOCK, LANES), 1)
    is_feat = lane < HEAD_DIM
    row = lax.broadcasted_iota(jnp.int32, (MOBA_MAX_BLOCKS, MOBA_BLOCK), 0)
    past = row < i
    ones_rows = jnp.where(
        lax.broadcasted_iota(jnp.int32, (MOBA_VROWS - HEAD_DIM, MOBA_BLOCK), 0) == 0, 1.0, 0.0)
    ninf = float("-inf")
    gates = []
    for h in range(N_HEADS):
        cols = slice((h // 2) * LANES, (h // 2 + 1) * LANES)
        q, k = q_ref[:, cols], k_ref[:, cols]
        v = v_ref[:, cols]
        v_t = v.T
        if h % 2:
            q, k, v = (pltpu.roll(x, HEAD_DIM, 1) for x in (q, k, v))
            v_t = v_t[HEAD_DIM:, :]
        else:
            v_t = v_t[:HEAD_DIM, :]
        qrow_ref[h] = jnp.where(is_feat, q * ATTN_SCALE, 0.0)
        kpad_ref[h] = jnp.where(is_feat, k, 0.0).astype(BF16)
        vaug_ref[h] = jnp.where(is_feat, v, jnp.where(lane == HEAD_DIM, 1.0, 0.0)).astype(BF16)
        vt_ref[h, 0] = jnp.concatenate([v_t, ones_rows], axis=0).astype(BF16)
        gates.append(_dot_bf16x3(km_ref[h], jnp.where(is_feat, q, 0.0), dot=_dot_nt))
    for h, gate_t in enumerate(gates):
        cand = jnp.where(past, gate_t, ninf)
        picks = []
        for _ in range(MOBA_TOPK):
            best = jnp.max(cand, axis=0, keepdims=True)
            idx = jnp.min(jnp.where(cand == best, row, 2 * LANES), axis=0, keepdims=True)
            ok = best > ninf
            pick = (row == idx) & ok
            cand = jnp.where(pick, ninf, cand)
            picks.append((idx, ok, pick))
        chosen = jnp.where(picks[0][2] | picks[1][2] | picks[2][2], 1.0, 0.0).astype(BF16)
        base = base_ref[h]
        rank_full = base + _dot(chosen, sut_ref[...])
        base_ref[h] = base + _dot(chosen, ones_ref[...])
        for r, (idx, ok, pick) in enumerate(picks):
            rank = jnp.sum(jnp.where(pick, rank_full, 0.0), axis=0, keepdims=True)
            sel_ref[h, r:r + 1, :] = jnp.where(ok, idx, -1)
            rank_ref[h, r:r + 1, :] = rank.astype(jnp.int32)

    @pl.when(i == pl.num_programs(0) - 1)
    def _():
        cnt_ref[...] = base_ref[...]


def _moba_route(qc, kc, vc, km_rows):
    s = qc.shape[0]
    nb = s // MOBA_BLOCK
    row = pl.BlockSpec((MOBA_BLOCK, WIDTH), lambda i: (i, 0))
    per_head = lambda dt: (pl.BlockSpec((N_HEADS, MOBA_BLOCK, LANES), lambda i: (0, i, 0)),
                           jax.ShapeDtypeStruct((N_HEADS, s, LANES), dt))
    picks = (pl.BlockSpec((N_HEADS, MOBA_TOPK, MOBA_BLOCK), lambda i: (0, 0, i)),
             jax.ShapeDtypeStruct((N_HEADS, MOBA_TOPK, s), jnp.int32))
    outs = [per_head(F32), per_head(BF16), per_head(BF16),
            (pl.BlockSpec((N_HEADS, 1, MOBA_VROWS, MOBA_BLOCK), lambda i: (0, i, 0, 0)),
             jax.ShapeDtypeStruct((N_HEADS, nb, MOBA_VROWS, MOBA_BLOCK), BF16)),
            picks, picks,
            (pl.BlockSpec((N_HEADS, MOBA_MAX_BLOCKS, MOBA_BLOCK), lambda i: (0, 0, 0)),
             jax.ShapeDtypeStruct((N_HEADS, MOBA_MAX_BLOCKS, MOBA_BLOCK), F32))]
    qi = jnp.arange(MOBA_BLOCK)
    strict_upper = (qi[:, None] < qi[None, :]).astype(BF16)
    all_ones = jnp.ones((MOBA_BLOCK, MOBA_BLOCK), BF16)
    sq = pl.BlockSpec((MOBA_BLOCK, MOBA_BLOCK), lambda i: (0, 0))
    return pl.pallas_call(
        _moba_route_kernel,
        grid=(nb,),
        in_specs=[row, row, row,
                  pl.BlockSpec((N_HEADS, MOBA_MAX_BLOCKS, LANES), lambda i: (0, 0, 0)), sq, sq],
        out_specs=[o[0] for o in outs],
        out_shape=[o[1] for o in outs],
        scratch_shapes=[pltpu.VMEM((N_HEADS, MOBA_MAX_BLOCKS, MOBA_BLOCK), F32)],
        compiler_params=_cparams("arbitrary"),
        name="moba_route",
    )(qc, kc, vc, km_rows, strict_upper, all_ones)


def _moba_tiles_per_head(seq):
    tiles = MOBA_TOPK * seq // MOBA_PAIR_TILE + seq // MOBA_BLOCK
    return (tiles // MOBA_GROUP + 1) * MOBA_GROUP


def _moba_dispatch(sel, rank, counts, n_tiles):
    n_heads, nb = counts.shape
    padded = (counts + MOBA_PAIR_TILE - 1) // MOBA_PAIR_TILE * MOBA_PAIR_TILE
    ends = jnp.cumsum(padded, axis=1)
    offs = ends - padded
    rows_per_head = n_tiles * MOBA_PAIR_TILE
    head_base = (jnp.arange(n_heads, dtype=jnp.int32) * rows_per_head)[:, None, None]
    blocks = jnp.arange(nb, dtype=jnp.int32)
    off_sel = jnp.sum(jnp.where(sel[..., None] == blocks, offs[:, None, None, :], 0), axis=-1)
    null_row = rows_per_head - MOBA_PAIR_TILE + jnp.arange(sel.shape[-1], dtype=jnp.int32) % MOBA_PAIR_TILE
    pos = head_base + jnp.where(sel >= 0, off_sel + rank, null_row)
    tile_start = jnp.arange(n_tiles, dtype=jnp.int32) * MOBA_PAIR_TILE
    blk = jnp.sum(ends[:, None, :] <= tile_start[None, :, None], axis=2)
    tile_blk = jnp.where(tile_start[None, :] < ends[:, -1:], blk, -1)
    return pos.reshape(-1).astype(jnp.int32), tile_blk.reshape(-1).astype(jnp.int32)


def _sc_workers():
    info = plsc.get_sparse_core_info()
    return info.num_cores, info.num_cores * info.num_subcores


def _sc_scatter_rows(rows, pos, n_out):
    n_src = rows.shape[0]
    seq = n_src // N_HEADS
    n_cores, n_workers = _sc_workers()
    win = SC_INDEX_WINDOW
    wins_per_worker = n_src // win // n_workers
    wins_per_head = seq // win
    assert n_src % (win * n_workers) == 0
    mesh = plsc.VectorSubcoreMesh(core_axis_name="c", subcore_axis_name="s")

    per_set = 2
    n_groups = wins_per_worker // per_set
    assert wins_per_worker % (2 * per_set) == 0 and wins_per_head % wins_per_worker == 0

    @functools.partial(
        pl.kernel, mesh=mesh, out_type=jax.ShapeDtypeStruct((n_out, LANES), F32), name="moba_scatter_rows",
        cost_estimate=pl.CostEstimate(flops=0, transcendentals=0,
                                      bytes_accessed=(1 + MOBA_TOPK) * n_src * LANES * 4 + pos.size * 4),
        scratch_types=[pltpu.VMEM((wins_per_worker, win), jnp.int32)] * MOBA_TOPK
                      + [pltpu.VMEM((win, LANES), F32)] * (2 * per_set)
                      + [pltpu.SemaphoreType.DMA] * (2 * per_set + 2))
    def scatter(src_hbm, idx_hbm, out_hbm, *scratch):
        idx_all = scratch[:MOBA_TOPK]
        rows_v = scratch[MOBA_TOPK:MOBA_TOPK + 2 * per_set]
        load_sems, scat_sems = scratch[-2 * per_set - 2:-2], scratch[-2:]
        wid = lax.axis_index("s") * n_cores + lax.axis_index("c")
        base = wid * wins_per_worker
        h, qw0 = base // wins_per_head, base % wins_per_head
        for r in range(MOBA_TOPK):
            pltpu.sync_copy(idx_hbm.at[pl.ds((h * MOBA_TOPK + r) * wins_per_head + qw0, wins_per_worker)], idx_all[r])

        def load_copy(g, st, u):
            return pltpu.make_async_copy(src_hbm.at[pl.ds((base + g * per_set + u) * win, win)],
                                         rows_v[st * per_set + u], load_sems[st * per_set + u])

        def scat_copy(g, st, u, r):
            return pltpu.make_async_copy(rows_v[st * per_set + u], out_hbm.at[idx_all[r].at[g * per_set + u]],
                                         scat_sems[st])

        def issue(g, st):
            for u in range(per_set):
                load_copy(g, st, u).start()

        def drain(g, st):
            for u in range(per_set):
                load_copy(g, st, u).wait()
                for r in range(MOBA_TOPK):
                    scat_copy(g, st, u, r).start()

        def wait_scatters(g, st):
            for u in range(per_set):
                for r in range(MOBA_TOPK):
                    scat_copy(g, st, u, r).wait()

        issue(0, 0)

        @pl.loop(0, n_groups, step=2)
        def _(g):
            @pl.when(g > 0)
            def _():
                wait_scatters(g - 1, 1)
            issue(g + 1, 1)
            drain(g, 0)
            wait_scatters(g, 0)

            @pl.when(g + 2 < n_groups)
            def _():
                issue(g + 2, 0)
            drain(g + 1, 1)

        wait_scatters(n_groups - 1, 1)

    return scatter(rows, pos.reshape(pos.shape[0] // win, win))


def _sc_gather_rows(table, pos):
    n = pos.shape[0]
    n_cores, n_workers = _sc_workers()
    win = SC_INDEX_WINDOW
    wins_per_worker = n // win // n_workers
    assert n % (win * n_workers) == 0
    mesh = plsc.VectorSubcoreMesh(core_axis_name="c", subcore_axis_name="s")

    per_set = 2
    n_groups = wins_per_worker // per_set
    assert wins_per_worker % (2 * per_set) == 0

    @functools.partial(
        pl.kernel, mesh=mesh, out_type=jax.ShapeDtypeStruct((n, LANES), F32), name="moba_gather_rows",
        cost_estimate=pl.CostEstimate(flops=0, transcendentals=0, bytes_accessed=2 * n * LANES * 4 + n * 4),
        scratch_types=[pltpu.VMEM((wins_per_worker, win), jnp.int32)] + [pltpu.VMEM((win, LANES), F32)] * (2 * per_set)
                      + [pltpu.SemaphoreType.DMA] * (2 * per_set + 2))
    def gather(table_hbm, idx_hbm, out_hbm, idx_all, *scratch):
        rows_v = scratch[:2 * per_set]
        gather_sems, store_sems = scratch[2 * per_set:4 * per_set], scratch[4 * per_set:]
        wid = lax.axis_index("s") * n_cores + lax.axis_index("c")
        base = wid * wins_per_worker
        pltpu.sync_copy(idx_hbm.at[pl.ds(base, wins_per_worker)], idx_all)

        def gather_copy(g, st, u):
            return pltpu.make_async_copy(table_hbm.at[idx_all.at[g * per_set + u]], rows_v[st * per_set + u],
                                         gather_sems[st * per_set + u])

        def store_copy(g, st, u):
            return pltpu.make_async_copy(rows_v[st * per_set + u],
                                         out_hbm.at[pl.ds((base + g * per_set + u) * win, win)], store_sems[st])

        def issue(g, st):
            for u in range(per_set):
                gather_copy(g, st, u).start()

        def drain(g, st):
            for u in range(per_set):
                gather_copy(g, st, u).wait()
                store_copy(g, st, u).start()

        def wait_stores(g, st):
            for u in range(per_set):
                store_copy(g, st, u).wait()

        issue(0, 0)

        @pl.loop(0, n_groups, step=2)
        def _(g):
            @pl.when(g > 0)
            def _():
                wait_stores(g - 1, 1)
            issue(g + 1, 1)
            drain(g, 0)
            wait_stores(g, 0)

            @pl.when(g + 2 < n_groups)
            def _():
                issue(g + 2, 0)
            drain(g + 1, 1)

        wait_stores(n_groups - 1, 1)

    return gather(table, pos.reshape(n // win, win))


def _partial_cols(pv, m):
    n = pv.shape[1]
    stacked = jnp.concatenate([pv, jnp.zeros((LANES - MOBA_VROWS, n), F32)], axis=0)
    row = lax.broadcasted_iota(jnp.int32, (LANES, n), 0)
    return jnp.where(row == HEAD_DIM + 1, m, stacked)


def _moba_sparse_kernel(blk_ref, q_ref, k_ref, v_ref, _after_ref, o_ref, *, n_groups):
    h, g = pl.program_id(0), pl.program_id(1)
    first = (h * n_groups + g) * MOBA_GROUP

    @pl.when(blk_ref[first] >= 0)
    def _():
        work = []
        for u in range(MOBA_GROUP):
            blk = jnp.maximum(blk_ref[first + u], 0)
            rows = pl.ds(pl.multiple_of(blk * MOBA_BLOCK, MOBA_BLOCK), MOBA_BLOCK)
            q = q_ref[u * MOBA_PAIR_TILE:(u + 1) * MOBA_PAIR_TILE, :].astype(BF16)
            work.append((rows, _dot_nt(q, k_ref[0, rows, :])))
        lane = lax.broadcasted_iota(jnp.int32, (MOBA_PAIR_TILE, LANES), 1)
        for u, (rows, s) in enumerate(work):
            m = jnp.max(s, axis=-1, keepdims=True)
            pv = _dot(jnp.exp(s - m).astype(BF16), v_ref[0, rows, :])
            o_ref[u * MOBA_PAIR_TILE:(u + 1) * MOBA_PAIR_TILE, :] = jnp.where(lane == HEAD_DIM + 1, m, pv)

    @pl.when(g == n_groups - 1)
    def _():
        lane = lax.broadcasted_iota(jnp.int32, (MOBA_PAIR_TILE, LANES), 1)
        o_ref[(MOBA_GROUP - 1) * MOBA_PAIR_TILE:, :] = jnp.where(lane == HEAD_DIM + 1, NEG, 0.0)


def _moba_sparse(q_sorted, k_pad, v_aug, tile_blk, n_tiles, after):
    n_heads, s, _ = k_pad.shape
    n_groups = n_tiles // MOBA_GROUP
    rows = pl.BlockSpec((MOBA_GROUP * MOBA_PAIR_TILE, LANES), lambda h, g, blk: (h * n_groups + g, 0))
    grid_spec = pltpu.PrefetchScalarGridSpec(
        num_scalar_prefetch=1, grid=(n_heads, n_groups),
        in_specs=[rows, pl.BlockSpec((1, s, LANES), lambda h, g, blk: (h, 0, 0)),
                  pl.BlockSpec((1, s, LANES), lambda h, g, blk: (h, 0, 0)), _ORDER_ONLY],
        out_specs=rows)
    return pl.pallas_call(
        functools.partial(_moba_sparse_kernel, n_groups=n_groups),
        grid_spec=grid_spec,
        out_shape=jax.ShapeDtypeStruct(q_sorted.shape, F32),
        compiler_params=_cparams("arbitrary", "arbitrary"),
        name="moba_sparse",
    )(tile_blk, q_sorted, k_pad, v_aug, after)


def _moba_merge_kernel(qa_ref, qb_ref, ka_ref, kb_ref, va_ref, vb_ref, ga_ref, gb_ref, _after_ref, o_ref):
    ki = lax.broadcasted_iota(jnp.int32, (MOBA_BLOCK, MOBA_BLOCK), 0)
    qi = lax.broadcasted_iota(jnp.int32, (MOBA_BLOCK, MOBA_BLOCK), 1)
    heads = ((qa_ref, ka_ref, va_ref, ga_ref), (qb_ref, kb_ref, vb_ref, gb_ref))
    chains = [(hh, b) for b in range(MOBA_MERGE_BLOCKS) for hh in range(2)]
    rows = lambda b: slice(b * MOBA_BLOCK, (b + 1) * MOBA_BLOCK)
    scores = [_dot_nt(heads[hh][1][0, rows(b), :], heads[hh][0][0, rows(b), :].astype(BF16))
              for hh, b in chains]
    outs = {}
    for (hh, b), s in zip(chains, scores):
        _, _, v_ref, g_ref = heads[hh]
        s = jnp.where(ki <= qi, s, NEG)
        m = jnp.max(s, axis=0, keepdims=True)
        parts = [_partial_cols(_dot(v_ref[0, b], jnp.exp(s - m).astype(BF16)), m)]
        parts += [g_ref[0, r, rows(b), :].T for r in range(MOBA_TOPK)]
        ms = [x[HEAD_DIM + 1:HEAD_DIM + 2, :] for x in parts]
        m_all = functools.reduce(jnp.maximum, ms)
        acc = sum(jnp.exp(mi - m_all) * x for mi, x in zip(ms, parts))
        outs[hh, b] = acc[:HEAD_DIM, :] / acc[HEAD_DIM:HEAD_DIM + 1, :]
    for b in range(MOBA_MERGE_BLOCKS):
        o_ref[rows(b), :] = jnp.concatenate([outs[0, b], outs[1, b]], axis=0).T.astype(BF16)


def _moba_merge(q_rows, k_pad, vt_aug, gathered, after):
    n_heads, s, _ = q_rows.shape
    t = MOBA_MERGE_BLOCKS * MOBA_BLOCK
    blk = lambda par: pl.BlockSpec((1, t, LANES), lambda p, i: (2 * p + par, i, 0))
    vspec = lambda par: pl.BlockSpec((1, MOBA_MERGE_BLOCKS, MOBA_VROWS, MOBA_BLOCK),
                                     lambda p, i: (2 * p + par, i, 0, 0))
    gspec = lambda par: pl.BlockSpec((1, MOBA_TOPK, t, LANES), lambda p, i: (2 * p + par, 0, i, 0))
    return pl.pallas_call(
        _moba_merge_kernel,
        grid=(n_heads // 2, s // t),
        in_specs=[blk(0), blk(1), blk(0), blk(1), vspec(0), vspec(1), gspec(0), gspec(1), _ORDER_ONLY],
        out_specs=pl.BlockSpec((t, LANES), lambda p, i: (i, p)),
        out_shape=jax.ShapeDtypeStruct((s, WIDTH), BF16),
        compiler_params=_cparams("arbitrary", "arbitrary"),
        name="moba_merge",
    )(q_rows, q_rows, k_pad, k_pad, vt_aug, vt_aug, gathered, gathered, after)


def _moba_attention(qc, kc, vc, kmean, overlap_a, overlap_b):
    s = qc.shape[0]
    n_tiles = _moba_tiles_per_head(s)
    q_rows, k_pad, v_aug, vt_aug, sel, rank, counts_raw = _moba_route(qc, kc, vc, _moba_mean_rows(kmean))
    counts = counts_raw[:, :s // MOBA_BLOCK, 0].astype(jnp.int32)
    pos, tile_blk = _moba_dispatch(sel, rank, counts, n_tiles)
    q_sorted = _sc_scatter_rows(q_rows.reshape(N_HEADS * s, LANES), pos, N_HEADS * n_tiles * MOBA_PAIR_TILE)
    def chain(calls, anchor):
        outs = []
        for call in calls:
            outs.append(call(anchor))
            anchor = outs[-1]
        return outs

    out_a = chain(overlap_a, counts_raw)
    partial = _moba_sparse(q_sorted, k_pad, v_aug, tile_blk, n_tiles, out_a[-1])
    gathered = _sc_gather_rows(partial, pos).reshape(N_HEADS, MOBA_TOPK, s, LANES)
    out_b = chain(overlap_b, partial)
    return out_a, out_b, _moba_merge(q_rows, k_pad, vt_aug, gathered, out_b[-1])


def _ssd_kernel(z_ref, xbc_ref, dt_ref, cw_ref, cb_ref, dtb_ref, alog_ref, dsk_ref, nw_ref, exp_ref, tri_ref,
                _after_ref, o_ref, state_ref, halo_ref):
    L = SSM_CHUNK

    @pl.when(pl.program_id(0) == 0)
    def _():
        state_ref[...] = jnp.zeros_like(state_ref)
        halo_ref[...] = jnp.zeros_like(halo_ref)

    cur = xbc_ref[...]
    prev8 = halo_ref[...]
    row8 = lax.broadcasted_iota(jnp.int32, (8, SSM_CONV_DIM), 0)
    conv = cb_ref[...] + cur * cw_ref[SSM_CONV - 1:SSM_CONV, :]
    for k in range(1, SSM_CONV):
        rolled = pltpu.roll(cur, k, 0)
        head = jnp.where(row8 < k, pltpu.roll(prev8, k, 0), rolled[0:8, :])
        shifted = jnp.concatenate([head, rolled[8:, :]], axis=0)
        conv = conv + shifted * cw_ref[SSM_CONV - 1 - k:SSM_CONV - k, :]
    halo_ref[...] = cur[L - 8:, :]
    act = _silu(conv)
    xs, bm, cm = act[:, :WIDTH], act[:, WIDTH:WIDTH + 2 * SSM_STATE], act[:, WIDTH + 2 * SSM_STATE:]

    dt = _softplus(dt_ref[...] + dtb_ref[...])
    da = dt * (-jnp.exp(alog_ref[...]))
    cs = _dot_exact_lhs(tri_ref[...], da)
    expand = exp_ref[...]
    cs_b = _dot_exact_rhs(cs, expand)
    dt_b = _dot_exact_rhs(dt, expand)
    cs_last = cs_b[L - 1:L, :]
    xdt = xs * dt_b
    cs_t = cs.T

    ri = lax.broadcasted_iota(jnp.int32, (L, L), 0)
    ci = lax.broadcasted_iota(jnp.int32, (L, L), 1)
    causal = ci <= ri
    lane = lax.broadcasted_iota(jnp.int32, (L, LANES), 1)
    bmb, cmb = bm.astype(BF16), cm.astype(BF16)
    xdtb = xdt.astype(BF16)
    y_pairs = []
    for p in range(N_HEADS // 2):
        g = (2 * p) // (N_HEADS // SSM_GROUPS)
        grp = slice(g * SSM_STATE, (g + 1) * SSM_STATE)
        cbm = _dot_nt(cmb[:, grp], bmb[:, grp])
        x_pair = xdtb[:, p * LANES:(p + 1) * LANES]
        halves = []
        for h in (2 * p, 2 * p + 1):
            seg = cs[:, h:h + 1] - cs_t[h:h + 1, :]
            decay = jnp.exp(jnp.where(causal, seg, NEG))
            halves.append(_dot((cbm * decay).astype(BF16), x_pair))
        y_pairs.append(jnp.where(lane < HEAD_DIM, halves[0], halves[1]))
    y_diag = jnp.concatenate(y_pairs, axis=1)

    state = state_ref[...]
    stb = state.astype(BF16)
    half = WIDTH // SSM_GROUPS
    y_off = jnp.concatenate(
        [_dot(cmb[:, g * SSM_STATE:(g + 1) * SSM_STATE], stb[:, g * half:(g + 1) * half])
         for g in range(SSM_GROUPS)], axis=1) * jnp.exp(cs_b)
    y = y_diag + y_off + xs * dsk_ref[...]

    w = (xdt * jnp.exp(cs_last - cs_b)).astype(BF16)
    bm_t = bm.T.astype(BF16)
    new = jnp.concatenate(
        [_dot(bm_t[g * SSM_STATE:(g + 1) * SSM_STATE, :], w[:, g * half:(g + 1) * half])
         for g in range(SSM_GROUPS)], axis=1)
    state_ref[...] = state * jnp.exp(cs_last) + new

    gt = y * _silu(z_ref[...])
    outs = []
    for g in range(SSM_GROUPS):
        gg = gt[:, g * half:(g + 1) * half]
        outs.append(gg * lax.rsqrt(jnp.mean(gg * gg, axis=-1, keepdims=True) + NORM_EPS))
    o_ref[...] = (jnp.concatenate(outs, axis=1) * nw_ref[...]).astype(BF16)


def _ssd_mixer(z, xbc, dt_raw, conv_w, conv_b, dt_bias, a_log, d_skip, norm_w, expand, tri, after):
    s = z.shape[0]
    L = SSM_CHUNK
    row = lambda n: pl.BlockSpec((L, n), lambda c: (c, 0))
    cst = lambda r, n: pl.BlockSpec((r, n), lambda c: (0, 0))
    return pl.pallas_call(
        _ssd_kernel,
        grid=(s // L,),
        in_specs=[row(WIDTH), row(SSM_CONV_DIM), row(DT_PAD), cst(SSM_CONV, SSM_CONV_DIM),
                  cst(1, SSM_CONV_DIM), cst(1, DT_PAD), cst(1, DT_PAD), cst(1, WIDTH), cst(1, WIDTH),
                  cst(DT_PAD, WIDTH), cst(L, L), _ORDER_ONLY],
        out_specs=row(WIDTH),
        out_shape=jax.ShapeDtypeStruct((s, WIDTH), BF16),
        scratch_shapes=[pltpu.VMEM((SSM_STATE, WIDTH), F32), pltpu.VMEM((8, SSM_CONV_DIM), F32)],
        compiler_params=_cparams("arbitrary"),
        cost_estimate=pl.CostEstimate(
            flops=2 * s * L * (SSM_GROUPS * SSM_STATE + WIDTH + 2 * N_HEADS * LANES) + 4 * s * SSM_STATE * WIDTH,
            transcendentals=s * (L * N_HEADS + SSM_CONV_DIM + 3 * WIDTH),
            bytes_accessed=s * (WIDTH + SSM_CONV_DIM + DT_PAD) * 4 + s * WIDTH * 2),
        name="ssd_mixer",
    )(z, xbc, dt_raw, conv_w, conv_b, dt_bias, a_log, d_skip, norm_w, expand, tri, after)


def _mix_ffn_kernel(x_ref, *refs, chunk):
    *y_refs, wo_ref, gm_ref, g_ref, sc_ref, sh_ref, gate_ref, wg_ref, wu_ref, wd_ref, o_ref = refs
    y, row0 = 0.0, 0
    for y_ref in y_refs:
        y = y + _dot(y_ref[...], wo_ref[row0:row0 + y_ref.shape[1], :])
        row0 += y_ref.shape[1]
    x = x_ref[...] + gm_ref[...] * y
    h = _mod_norm(x, g_ref[...], sc_ref[...], sh_ref[...]).astype(BF16)
    hidden = wg_ref.shape[1]
    n_chunks = hidden // chunk
    cols = lambda c: slice(c * chunk, (c + 1) * chunk)
    gate_up = lambda c: (_dot(h, wg_ref[:, cols(c)]), _dot(h, wu_ref[:, cols(c)]))
    acc = jnp.zeros(x.shape, F32)
    nxt = gate_up(0)
    for c in range(n_chunks):
        g, u = nxt
        if c + 1 < n_chunks:
            nxt = gate_up(c + 1)
        acc = acc + _dot((_silu(g) * u).astype(BF16), wd_ref[cols(c), :])
    o_ref[...] = x + gate_ref[...] * acc


def _mix_ffn(x, ys, w_out, gate_m, g, sc, sh, gate_f, w_gate, w_up, w_down, layer):
    _, s, d = x.shape
    hidden = w_gate.shape[2]
    tm = 512
    row = _stream_rows(tm, d)
    vec = pl.BlockSpec((1, d), lambda i: (0, 0))
    assert sum(y.shape[1] for y in ys) == w_out.shape[1]
    return pl.pallas_call(
        functools.partial(_mix_ffn_kernel, chunk=256),
        grid=(s // tm,),
        in_specs=[row] + [pl.BlockSpec((tm, y.shape[1]), lambda i: (i, 0)) for y in ys]
                 + [_resident((None, w_out.shape[1], d), lambda i: (layer, 0, 0)), vec, vec, vec, vec, vec,
                    _resident((None, d, hidden), lambda i: (layer, 0, 0)),
                    _resident((None, d, hidden), lambda i: (layer, 0, 0)),
                    _resident((None, hidden, d), lambda i: (layer, 0, 0))],
        out_specs=row,
        out_shape=jax.ShapeDtypeStruct((1, s, d), F32),
        compiler_params=_cparams("arbitrary"),
        name="mix_ffn",
    )(x, *ys, w_out, gate_m, g, sc, sh, gate_f, w_gate, w_up, w_down)


def _rope_tables(seq):
    pos = jnp.arange(seq, dtype=F32)
    inv = ROPE_THETA ** (-jnp.arange(0, ROPE_DIM, 2, dtype=F32) / ROPE_DIM)
    j = jnp.arange(LANES) % HEAD_DIM
    inv_lane = jnp.tile(jnp.concatenate([inv, inv, jnp.zeros(HEAD_DIM - ROPE_DIM, F32)]), LANES // HEAD_DIM)
    ang = pos[:, None] * inv_lane[None, :]
    cos, sin = jnp.cos(ang), jnp.sin(ang)
    cos_t = jnp.where(j < ROPE_DIM, cos, 1.0)
    sin_lo = jnp.where(j < ROPE_HALF, -sin, 0.0)
    sin_hi = jnp.where((j >= ROPE_HALF) & (j < ROPE_DIM), sin, 0.0)
    return cos_t, sin_lo, sin_hi


def _moba_mean_rows(kmean):
    nb = kmean.shape[0]
    km = kmean.reshape(nb, N_HEADS, HEAD_DIM).transpose(1, 0, 2)
    return jnp.pad(km, ((0, 0), (0, MOBA_MAX_BLOCKS - nb), (0, LANES - HEAD_DIM)))


def kernel(x, c, ada_w, ada_b, norm_mix, w_in, qn_swa, kn_swa, qn_moba, kn_moba, conv_w, conv_b, dt_bias,
           a_log, d_skip, ssm_norm, w_out, norm_ffn, w_gate, w_up, w_down):
    batch, seq, d = x.shape
    depth = ada_w.shape[0]
    assert batch == 1 and d == D_MODEL
    assert seq % SWA_SPAN == 0 and seq // MOBA_BLOCK <= MOBA_MAX_BLOCKS
    xs = x

    mod = _ada_modulation(c, ada_w, ada_b)
    rope_tabs = _rope_tables(seq)
    lane_head = jnp.arange(LANES) // HEAD_DIM
    hsum = (lane_head[:, None] == lane_head[None, :]).astype(BF16)
    expand = (jnp.arange(DT_PAD)[:, None] == (jnp.arange(WIDTH) // HEAD_DIM)[None, :]).astype(BF16)
    tri = (jnp.arange(SSM_CHUNK)[:, None] >= jnp.arange(SSM_CHUNK)[None, :]).astype(BF16)
    pad8 = lambda v: jnp.pad(v, (0, DT_PAD - N_HEADS)).reshape(1, DT_PAD)
    two = lambda g: jnp.concatenate([g, g]).reshape(1, LANES)

    w_in_b = _cast_bf16(w_in, IN_PROJ_PAD)
    w_out_b, w_gate_b, w_up_b, w_down_b = (_cast_bf16(w) for w in (w_out, w_gate, w_up, w_down))

    for l in range(depth):
        shift_m, scale_m, gate_m, shift_f, scale_f, gate_f = [mod[l, :, i * d:(i + 1) * d] for i in range(6)]
        gains = [two(g[l]) for g in (qn_swa, kn_swa, qn_moba, kn_moba)]
        qa, ka, va, qc, kc, vc, z, xbc, dt_raw, kmean = _in_projection(
            xs, norm_mix[l].reshape(1, d), scale_m, shift_m, w_in_b, l, gains, hsum, rope_tabs)

        n_pairs = WIDTH // LANES
        first = n_pairs - 1
        dilated_lo = functools.partial(_dilated_attention, qa, ka, va, 0, first)
        dilated_hi = functools.partial(_dilated_attention, qa, ka, va, first, n_pairs - first)
        ssd = functools.partial(
            _ssd_mixer, z, xbc, dt_raw, conv_w[l], conv_b[l].reshape(1, -1), pad8(dt_bias[l]), pad8(a_log[l]),
            jnp.repeat(d_skip[l], HEAD_DIM).reshape(1, WIDTH), ssm_norm[l].reshape(1, WIDTH), expand, tri)
        (y_a_lo,), (y_a_hi, y_b), y_c = _moba_attention(qc, kc, vc, kmean, (dilated_lo,), (dilated_hi, ssd))

        xs = _mix_ffn(xs, (y_a_lo, y_a_hi, y_b, y_c), w_out_b, gate_m, norm_ffn[l].reshape(1, d), scale_f,
                      shift_f, gate_f, w_gate_b, w_up_b, w_down_b, l)
    return xs
```

```python
import functools

import jax
import jax.numpy as jnp
from jax import lax
from jax.experimental import pallas as pl
from jax.experimental.pallas import tpu as pltpu
from jax.experimental.pallas import tpu_sc as plsc

F32 = jnp.float32
BF16 = jnp.bfloat16

D_MODEL = 1024
HEAD_DIM = 64
N_HEADS = 8
WIDTH = N_HEADS * HEAD_DIM
ATTN_SCALE = HEAD_DIM ** -0.5
NORM_EPS = 1e-6
NEG = -1e30

ROPE_THETA = 500000.0
ROPE_DIM = HEAD_DIM // 4
ROPE_HALF = ROPE_DIM // 2

DILATIONS = (1, 4, 16)
SWA_BLOCK = 128
SWA_SPAN = DILATIONS[-1] * SWA_BLOCK
SWA_GROUP = 4

SSM_STATE = 128
SSM_GROUPS = 2
SSM_CONV = 4
SSM_CHUNK = 256
SSM_CONV_DIM = WIDTH + 2 * SSM_GROUPS * SSM_STATE

MOBA_BLOCK = 256
MOBA_TOPK = 3
MOBA_PAIR_TILE = 128
MOBA_GROUP = 64
SC_INDEX_WINDOW = 128
MOBA_MAX_BLOCKS = 64
MOBA_MERGE_BLOCKS = 8
MOBA_VROWS = HEAD_DIM + 16

LANES = 128
DT_PAD = LANES
IN_PROJ_PAD = 6 * WIDTH + WIDTH + SSM_CONV_DIM + DT_PAD
OFF_A, OFF_C, OFF_Z, OFF_XBC, OFF_DT = 0, 3 * WIDTH, 6 * WIDTH, 7 * WIDTH, 7 * WIDTH + SSM_CONV_DIM

VMEM_LIMIT = 56 * 1024 * 1024
CAST_BLOCK_BYTES = 6 * 1024 * 1024


_ORDER_ONLY = pl.BlockSpec(memory_space=pl.ANY)


def _stream_rows(tm, d):
    return pl.BlockSpec((None, tm, d), lambda i: (0, i, 0))


def _resident(block_shape, index_map):
    return pl.BlockSpec(block_shape, index_map, pipeline_mode=pl.Buffered(1))


def _cparams(*sem):
    return pltpu.CompilerParams(dimension_semantics=sem, vmem_limit_bytes=VMEM_LIMIT)


def _split3(a):
    hi = a.astype(BF16)
    r1 = a - hi.astype(F32)
    mid = r1.astype(BF16)
    lo = (r1 - mid.astype(F32)).astype(BF16)
    return hi, mid, lo


def _dot(a, b):
    return jnp.dot(a, b, preferred_element_type=F32)


def _dot_nt(a, b):
    return lax.dot_general(a, b, (((1,), (1,)), ((), ())), preferred_element_type=F32)


def _dot_exact_rhs(a, b_bf16):
    hi, mid, lo = _split3(a)
    return _dot(hi, b_bf16) + _dot(mid, b_bf16) + _dot(lo, b_bf16)


def _dot_exact_lhs(a_bf16, b):
    hi, mid, lo = _split3(b)
    return _dot(a_bf16, hi) + _dot(a_bf16, mid) + _dot(a_bf16, lo)


def _dot_bf16x3(a, b, dot=_dot):
    ah = a.astype(BF16)
    al = (a - ah.astype(F32)).astype(BF16)
    bh = b.astype(BF16)
    bl = (b - bh.astype(F32)).astype(BF16)
    return dot(ah, bh) + (dot(ah, bl) + dot(al, bh))


def _silu(x):
    return x / (1.0 + jnp.exp(-x))


def _softplus(x):
    return jnp.maximum(x, 0.0) + jnp.log1p(jnp.exp(-jnp.abs(x)))


def _cast_kernel(w_ref, o_ref):
    cols = w_ref.shape[1]
    o_ref[:, :cols] = w_ref[...].astype(BF16)
    if o_ref.shape[1] > cols:
        o_ref[:, cols:] = jnp.zeros((o_ref.shape[0], o_ref.shape[1] - cols), BF16)


def _cast_bf16(w, pad_cols=None):
    depth, rows, cols = w.shape
    out_cols = pad_cols or cols
    tr = max(t for t in range(8, rows + 1, 8) if rows % t == 0 and t * cols * 4 <= CAST_BLOCK_BYTES)
    return pl.pallas_call(
        _cast_kernel,
        grid=(depth, rows // tr),
        in_specs=[pl.BlockSpec((None, tr, cols), lambda l, i: (l, i, 0))],
        out_specs=pl.BlockSpec((None, tr, out_cols), lambda l, i: (l, i, 0)),
        out_shape=jax.ShapeDtypeStruct((depth, rows, out_cols), BF16),
        compiler_params=_cparams("arbitrary", "arbitrary"),
        name="cast_weights",
    )(w)


def _ada_kernel(c_ref, w_ref, b_ref, o_ref):
    c = c_ref[...]
    o_ref[0] = _dot_bf16x3(_silu(c), w_ref[0]) + b_ref[0]


def _ada_modulation(c, ada_w, ada_b):
    depth, d, n = ada_w.shape
    tn = 2048
    c8 = jnp.broadcast_to(c, (8, d))
    out = pl.pallas_call(
        _ada_kernel,
        grid=(depth, n // tn),
        in_specs=[pl.BlockSpec((8, d), lambda l, j: (0, 0)),
                  pl.BlockSpec((1, d, tn), lambda l, j: (l, 0, j)),
                  pl.BlockSpec((1, 1, tn), lambda l, j: (l, 0, j))],
        out_specs=pl.BlockSpec((1, 8, tn), lambda l, j: (l, 0, j)),
        out_shape=jax.ShapeDtypeStruct((depth, 8, n), F32),
        compiler_params=_cparams("arbitrary", "arbitrary"),
        name="ada_modulation",
    )(c8, ada_w, ada_b.reshape(depth, 1, n))
    return out[:, 0:1, :]


def _mod_norm(x, g, sc, sh):
    var = jnp.mean(x * x, axis=-1, keepdims=True)
    return (x * lax.rsqrt(var + NORM_EPS) * g) * (1.0 + sc) + sh


def _inproj_kernel(x_ref, g_ref, sc_ref, sh_ref, w_ref, gqa_ref, gka_ref, gqc_ref, gkc_ref, hsum_ref,
                   cos_ref, slo_ref, shi_ref,
                   qa_ref, ka_ref, va_ref, qc_ref, kc_ref, vc_ref, z_ref, xbc_ref, dt_ref, km_ref):
    h = _mod_norm(x_ref[...], g_ref[...], sc_ref[...], sh_ref[...]).astype(BF16)
    proj = lambda off, n=WIDTH: _dot(h, w_ref[:, off:off + n])
    raw = [proj(OFF_A), proj(OFF_A + WIDTH), proj(OFF_C), proj(OFF_C + WIDTH)]
    va_ref[...] = proj(OFF_A + 2 * WIDTH)
    vc_ref[...] = proj(OFF_C + 2 * WIDTH)
    z_ref[...] = proj(OFF_Z)

    hsum = hsum_ref[...]
    cos_t, slo, shi = cos_ref[...], slo_ref[...], shi_ref[...]
    rope = lambda x, gain_ref: _head_norm_rope(x, gain_ref[...], hsum, cos_t, slo, shi)
    n_blocks = x_ref.shape[0] // MOBA_BLOCK
    for p in range(WIDTH // LANES):
        lo, hi = p * LANES, (p + 1) * LANES
        qa_ref[:, lo:hi] = rope(raw[0][:, lo:hi], gqa_ref) * ATTN_SCALE
        ka_ref[:, lo:hi] = rope(raw[1][:, lo:hi], gka_ref)
        qc_ref[:, lo:hi] = rope(raw[2][:, lo:hi], gqc_ref)
        kc = rope(raw[3][:, lo:hi], gkc_ref)
        kc_ref[:, lo:hi] = kc
        for b in range(n_blocks):
            km_ref[b, :, lo:hi] = jnp.mean(kc[b * MOBA_BLOCK:(b + 1) * MOBA_BLOCK, :], axis=0, keepdims=True)
    xbc_ref[...] = proj(OFF_XBC, SSM_CONV_DIM)
    dt_ref[...] = proj(OFF_DT, DT_PAD)


def _in_projection(x, g, sc, sh, w_all, layer, gains, hsum, rope_tabs):
    _, s, d = x.shape
    tm = 512
    assert tm % MOBA_BLOCK == 0
    row = lambda n: pl.BlockSpec((tm, n), lambda i: (i, 0))
    vec = pl.BlockSpec((1, d), lambda i: (0, 0))
    cst = lambda r, n: pl.BlockSpec((r, n), lambda i: (0, 0))
    widths = [WIDTH] * 7 + [SSM_CONV_DIM, DT_PAD]
    return pl.pallas_call(
        _inproj_kernel,
        grid=(s // tm,),
        in_specs=[_stream_rows(tm, d), vec, vec, vec, _resident((None, d, IN_PROJ_PAD), lambda i: (layer, 0, 0))]
                 + [cst(1, LANES)] * 4 + [cst(LANES, LANES)] + [row(LANES)] * 3,
        out_specs=[row(n) for n in widths]
                  + [pl.BlockSpec((tm // MOBA_BLOCK, 1, WIDTH), lambda i: (i, 0, 0))],
        out_shape=[jax.ShapeDtypeStruct((s, n), F32) for n in widths]
                  + [jax.ShapeDtypeStruct((s // MOBA_BLOCK, 1, WIDTH), F32)],
        compiler_params=_cparams("arbitrary"),
        name="in_projection",
    )(x, g, sc, sh, w_all, *gains, hsum, *rope_tabs)


def _head_norm_rope(x, gain, hsum, cos_t, sin_lo, sin_hi):
    sq = x * x
    sq_hi = sq.astype(BF16)
    ss = _dot(sq_hi, hsum) + _dot((sq - sq_hi.astype(F32)).astype(BF16), hsum)
    y = x * lax.rsqrt(ss * (1.0 / HEAD_DIM) + NORM_EPS) * gain
    return (y * cos_t + pltpu.roll(y, ROPE_HALF, 1) * sin_hi
            + pltpu.roll(y, LANES - ROPE_HALF, 1) * sin_lo)


def _dilated_kernel(q_ref, kc_ref, kp_ref, vc_ref, vp_ref, _after_ref, o_ref, m_s, l_s, a_s, q4_s, k4_s, v4_s):
    has_prev = pl.program_id(1) > 0
    lane = lax.broadcasted_iota(jnp.int32, (SWA_BLOCK, LANES), 1)
    head_a = lane < HEAD_DIM
    qi = lax.broadcasted_iota(jnp.int32, (SWA_BLOCK, 2 * SWA_BLOCK), 0)
    ki = lax.broadcasted_iota(jnp.int32, (SWA_BLOCK, 2 * SWA_BLOCK), 1)
    band = (ki >= qi) & (ki <= qi + SWA_BLOCK)
    band_edge = band & ((ki >= SWA_BLOCK) | has_prev)

    quarter = SWA_SPAN // 4
    for r4 in range(4):
        src = pl.ds(r4, quarter, 4)
        q4_s[r4 * quarter:(r4 + 1) * quarter, :] = q_ref[src, :]
        for dst, prev, cur in ((k4_s, kp_ref, kc_ref), (v4_s, vp_ref, vc_ref)):
            dst[2 * r4 * quarter:(2 * r4 + 1) * quarter, :] = prev[src, :]
            dst[(2 * r4 + 1) * quarter:(2 * r4 + 2) * quarter, :] = cur[src, :]

    def unit_d16(a, r4):
        rows = pl.ds(4 * a + r4, SWA_BLOCK, 16)
        q = q4_s[pl.ds(r4 * quarter + a, SWA_BLOCK, 4), :]
        prev = pl.ds(2 * r4 * quarter + a, SWA_BLOCK, 4)
        cur = pl.ds((2 * r4 + 1) * quarter + a, SWA_BLOCK, 4)
        k_cat = jnp.concatenate([k4_s[prev, :], k4_s[cur, :]], axis=0)
        v_cat = jnp.concatenate([v4_s[prev, :], v4_s[cur, :]], axis=0)
        return rows, q, k_cat, v_cat, band_edge

    def unit_d4(r, nb):
        rows = pl.ds(4 * SWA_BLOCK * nb + r, SWA_BLOCK, 4)
        q = q4_s[pl.ds(r * quarter + SWA_BLOCK * nb, SWA_BLOCK), :]
        keys = pl.ds((2 * r + 1) * quarter + SWA_BLOCK * (nb - 1), 2 * SWA_BLOCK)
        return rows, q, k4_s[keys, :], v4_s[keys, :], (band_edge if nb == 0 else band)

    def unit_d1(b, edge=False):
        rows = pl.ds(SWA_BLOCK * b, SWA_BLOCK)
        if edge:
            prev = pl.ds(SWA_SPAN - SWA_BLOCK, SWA_BLOCK)
            k_cat = jnp.concatenate([kp_ref[prev, :], kc_ref[rows, :]], axis=0)
            v_cat = jnp.concatenate([vp_ref[prev, :], vc_ref[rows, :]], axis=0)
            return rows, q_ref[rows, :], k_cat, v_cat, band_edge
        keys = pl.ds(SWA_BLOCK * (b - 1), 2 * SWA_BLOCK)
        return rows, q_ref[rows, :], kc_ref[keys, :], vc_ref[keys, :], band

    def attend(units, mode):
        scores = []
        for _, q, k_cat, _, mask in units:
            kb = k_cat.astype(BF16)
            scores.append([jnp.where(mask, _dot_nt(jnp.where(sel, q, 0.0).astype(BF16), kb), NEG)
                           for sel in (head_a, ~head_a)])
        probs = []
        for pair in scores:
            stats = []
            for s in pair:
                m_loc = jnp.max(s, axis=-1, keepdims=True)
                p = jnp.exp(s - m_loc)
                stats.append((m_loc, jnp.sum(p, axis=-1, keepdims=True), p.astype(BF16)))
            probs.append(stats)
        for (rows, _, _, v_cat, _), ((m_a, s_a, p_a), (m_b, s_b, p_b)) in zip(units, probs):
            vb = v_cat.astype(BF16)
            m_new = jnp.where(head_a, m_a, m_b)
            l_new = jnp.where(head_a, s_a, s_b)
            a_new = jnp.where(head_a, _dot(p_a, vb), _dot(p_b, vb))
            if mode != "init":
                m_in, m_loc = m_s[rows, :], m_new
                m_new = jnp.maximum(m_in, m_loc)
                alpha, beta = jnp.exp(m_in - m_new), jnp.exp(m_loc - m_new)
                l_new = alpha * l_s[rows, :] + beta * l_new
                a_new = alpha * a_s[rows, :] + beta * a_new
            if mode == "final":
                o_ref[rows, :] = (a_new / l_new).astype(BF16)
            else:
                m_s[rows, :] = m_new
                l_s[rows, :] = l_new
                a_s[rows, :] = a_new

    g = SWA_GROUP
    assert DILATIONS == (1, 4, 16) and g == 4 and SWA_SPAN // (4 * SWA_BLOCK) == g

    def body16(a, carry):
        attend([unit_d16(a, r4) for r4 in range(g)], "init")
        return carry
    lax.fori_loop(0, 16 // g, body16, 0)

    def body4(r, carry):
        attend([unit_d4(r, nb) for nb in range(g)], "fold")
        return carry
    lax.fori_loop(0, 4, body4, 0)

    attend([unit_d1(b, edge=(b == 0)) for b in range(g)], "final")

    def body1(i, carry):
        attend([unit_d1(i * g + j) for j in range(g)], "final")
        return carry
    lax.fori_loop(1, SWA_SPAN // SWA_BLOCK // g, body1, 0)


def _dilated_attention(q, k, v, pair0, n_pairs, after):
    s = q.shape[0]
    cur = lambda off: pl.BlockSpec((SWA_SPAN, LANES), lambda p, n: (n, off + p))
    prev = lambda off: pl.BlockSpec((SWA_SPAN, LANES), lambda p, n: (jnp.maximum(n - 1, 0), off + p))
    width = n_pairs * LANES
    return pl.pallas_call(
        _dilated_kernel,
        grid=(n_pairs, s // SWA_SPAN),
        in_specs=[cur(pair0), cur(pair0), prev(pair0), cur(pair0), prev(pair0), _ORDER_ONLY],
        out_specs=cur(0),
        out_shape=jax.ShapeDtypeStruct((s, width), BF16),
        scratch_shapes=[pltpu.VMEM((SWA_SPAN, LANES), F32)] * 4 + [pltpu.VMEM((2 * SWA_SPAN, LANES), F32)] * 2,
        compiler_params=_cparams("arbitrary", "arbitrary"),
        cost_estimate=pl.CostEstimate(
            flops=len(DILATIONS) * 4 * s * 2 * SWA_BLOCK * width,
            transcendentals=len(DILATIONS) * s * 2 * SWA_BLOCK * 2 * n_pairs,
            bytes_accessed=5 * s * width * 4 + s * width * 2),
        name="dilated_attention",
    )(q, k, k, v, v, after)


def _moba_route_kernel(q_ref, k_ref, v_ref, km_ref, sut_ref, ones_ref,
                       qrow_ref, kpad_ref, vaug_ref, vt_ref, sel_ref, rank_ref, cnt_ref, base_ref):
    i = pl.program_id(0)

    @pl.when(i == 0)
    def _():
        base_ref[...] = jnp.zeros_like(base_ref)

    lane = lax.broadcasted_iota(jnp.int32, (MOBA_BLOCK, LANES), 1)
    is_feat = lane < HEAD_DIM
    row = lax.broadcasted_iota(jnp.int32, (MOBA_MAX_BLOCKS, MOBA_BLOCK), 0)
    past = row < i
    ones_rows = jnp.where(
        lax.broadcasted_iota(jnp.int32, (MOBA_VROWS - HEAD_DIM, MOBA_BLOCK), 0) == 0, 1.0, 0.0)
    ninf = float("-inf")
    gates = []
    for h in range(N_HEADS):
        cols = slice((h // 2) * LANES, (h // 2 + 1) * LANES)
        q, k = q_ref[:, cols], k_ref[:, cols]
        v = v_ref[:, cols]
        v_t = v.T
        if h % 2:
            q, k, v = (pltpu.roll(x, HEAD_DIM, 1) for x in (q, k, v))
            v_t = v_t[HEAD_DIM:, :]
        else:
            v_t = v_t[:HEAD_DIM, :]
        qrow_ref[h] = jnp.where(is_feat, q * ATTN_SCALE, 0.0)
        kpad_ref[h] = jnp.where(is_feat, k, 0.0).astype(BF16)
        vaug_ref[h] = jnp.where(is_feat, v, jnp.where(lane == HEAD_DIM, 1.0, 0.0)).astype(BF16)
        vt_ref[h, 0] = jnp.concatenate([v_t, ones_rows], axis=0).astype(BF16)
        gates.append(_dot_bf16x3(km_ref[h], jnp.where(is_feat, q, 0.0), dot=_dot_nt))
    for h, gate_t in enumerate(gates):
        cand = jnp.where(past, gate_t, ninf)
        picks = []
        for _ in range(MOBA_TOPK):
            best = jnp.max(cand, axis=0, keepdims=True)
            idx = jnp.min(jnp.where(cand == best, row, 2 * LANES), axis=0, keepdims=True)
            ok = best > ninf
            pick = (row == idx) & ok
            cand = jnp.where(pick, ninf, cand)
            picks.append((idx, ok, pick))
        chosen = jnp.where(picks[0][2] | picks[1][2] | picks[2][2], 1.0, 0.0).astype(BF16)
        base = base_ref[h]
        rank_full = base + _dot(chosen, sut_ref[...])
        base_ref[h] = base + _dot(chosen, ones_ref[...])
        for r, (idx, ok, pick) in enumerate(picks):
            rank = jnp.sum(jnp.where(pick, rank_full, 0.0), axis=0, keepdims=True)
            sel_ref[h, r:r + 1, :] = jnp.where(ok, idx, -1)
            rank_ref[h, r:r + 1, :] = rank.astype(jnp.int32)

    @pl.when(i == pl.num_programs(0) - 1)
    def _():
        cnt_ref[...] = base_ref[...]


def _moba_route(qc, kc, vc, km_rows):
    s = qc.shape[0]
    nb = s // MOBA_BLOCK
    row = pl.BlockSpec((MOBA_BLOCK, WIDTH), lambda i: (i, 0))
    per_head = lambda dt: (pl.BlockSpec((N_HEADS, MOBA_BLOCK, LANES), lambda i: (0, i, 0)),
                           jax.ShapeDtypeStruct((N_HEADS, s, LANES), dt))
    picks = (pl.BlockSpec((N_HEADS, MOBA_TOPK, MOBA_BLOCK), lambda i: (0, 0, i)),
             jax.ShapeDtypeStruct((N_HEADS, MOBA_TOPK, s), jnp.int32))
    outs = [per_head(F32), per_head(BF16), per_head(BF16),
            (pl.BlockSpec((N_HEADS, 1, MOBA_VROWS, MOBA_BLOCK), lambda i: (0, i, 0, 0)),
             jax.ShapeDtypeStruct((N_HEADS, nb, MOBA_VROWS, MOBA_BLOCK), BF16)),
            picks, picks,
            (pl.BlockSpec((N_HEADS, MOBA_MAX_BLOCKS, MOBA_BLOCK), lambda i: (0, 0, 0)),
             jax.ShapeDtypeStruct((N_HEADS, MOBA_MAX_BLOCKS, MOBA_BLOCK), F32))]
    qi = jnp.arange(MOBA_BLOCK)
    strict_upper = (qi[:, None] < qi[None, :]).astype(BF16)
    all_ones = jnp.ones((MOBA_BLOCK, MOBA_BLOCK), BF16)
    sq = pl.BlockSpec((MOBA_BLOCK, MOBA_BLOCK), lambda i: (0, 0))
    return pl.pallas_call(
        _moba_route_kernel,
        grid=(nb,),
        in_specs=[row, row, row,
                  pl.BlockSpec((N_HEADS, MOBA_MAX_BLOCKS, LANES), lambda i: (0, 0, 0)), sq, sq],
        out_specs=[o[0] for o in outs],
        out_shape=[o[1] for o in outs],
        scratch_shapes=[pltpu.VMEM((N_HEADS, MOBA_MAX_BLOCKS, MOBA_BLOCK), F32)],
        compiler_params=_cparams("arbitrary"),
        name="moba_route",
    )(qc, kc, vc, km_rows, strict_upper, all_ones)


def _moba_tiles_per_head(seq):
    tiles = MOBA_TOPK * seq // MOBA_PAIR_TILE + seq // MOBA_BLOCK
    return (tiles // MOBA_GROUP + 1) * MOBA_GROUP


def _moba_dispatch(sel, rank, counts, n_tiles):
    n_heads, nb = counts.shape
    padded = (counts + MOBA_PAIR_TILE - 1) // MOBA_PAIR_TILE * MOBA_PAIR_TILE
    ends = jnp.cumsum(padded, axis=1)
    offs = ends - padded
    rows_per_head = n_tiles * MOBA_PAIR_TILE
    head_base = (jnp.arange(n_heads, dtype=jnp.int32) * rows_per_head)[:, None, None]
    blocks = jnp.arange(nb, dtype=jnp.int32)
    off_sel = jnp.sum(jnp.where(sel[..., None] == blocks, offs[:, None, None, :], 0), axis=-1)
    null_row = rows_per_head - MOBA_PAIR_TILE + jnp.arange(sel.shape[-1], dtype=jnp.int32) % MOBA_PAIR_TILE
    pos = head_base + jnp.where(sel >= 0, off_sel + rank, null_row)
    tile_start = jnp.arange(n_tiles, dtype=jnp.int32) * MOBA_PAIR_TILE
    blk = jnp.sum(ends[:, None, :] <= tile_start[None, :, None], axis=2)
    tile_blk = jnp.where(tile_start[None, :] < ends[:, -1:], blk, -1)
    return pos.reshape(-1).astype(jnp.int32), tile_blk.reshape(-1).astype(jnp.int32)


def _sc_workers():
    info = plsc.get_sparse_core_info()
    return info.num_cores, info.num_cores * info.num_subcores


def _sc_scatter_rows(rows, pos, n_out):
    n_src = rows.shape[0]
    seq = n_src // N_HEADS
    n_cores, n_workers = _sc_workers()
    win = SC_INDEX_WINDOW
    wins_per_worker = n_src // win // n_workers
    wins_per_head = seq // win
    assert n_src % (win * n_workers) == 0
    mesh = plsc.VectorSubcoreMesh(core_axis_name="c", subcore_axis_name="s")

    per_set = 2
    n_groups = wins_per_worker // per_set
    assert wins_per_worker % (2 * per_set) == 0 and wins_per_head % wins_per_worker == 0

    @functools.partial(
        pl.kernel, mesh=mesh, out_type=jax.ShapeDtypeStruct((n_out, LANES), F32), name="moba_scatter_rows",
        cost_estimate=pl.CostEstimate(flops=0, transcendentals=0,
                                      bytes_accessed=(1 + MOBA_TOPK) * n_src * LANES * 4 + pos.size * 4),
        scratch_types=[pltpu.VMEM((wins_per_worker, win), jnp.int32)] * MOBA_TOPK
                      + [pltpu.VMEM((win, LANES), F32)] * (2 * per_set)
                      + [pltpu.SemaphoreType.DMA] * (2 * per_set + 2))
    def scatter(src_hbm, idx_hbm, out_hbm, *scratch):
        idx_all = scratch[:MOBA_TOPK]
        rows_v = scratch[MOBA_TOPK:MOBA_TOPK + 2 * per_set]
        load_sems, scat_sems = scratch[-2 * per_set - 2:-2], scratch[-2:]
        wid = lax.axis_index("s") * n_cores + lax.axis_index("c")
        base = wid * wins_per_worker
        h, qw0 = base // wins_per_head, base % wins_per_head
        for r in range(MOBA_TOPK):
            pltpu.sync_copy(idx_hbm.at[pl.ds((h * MOBA_TOPK + r) * wins_per_head + qw0, wins_per_worker)], idx_all[r])

        def load_copy(g, st, u):
            return pltpu.make_async_copy(src_hbm.at[pl.ds((base + g * per_set + u) * win, win)],
                                         rows_v[st * per_set + u], load_sems[st * per_set + u])

        def scat_copy(g, st, u, r):
            return pltpu.make_async_copy(rows_v[st * per_set + u], out_hbm.at[idx_all[r].at[g * per_set + u]],
                                         scat_sems[st])

        def issue(g, st):
            for u in range(per_set):
                load_copy(g, st, u).start()

        def drain(g, st):
            for u in range(per_set):
                load_copy(g, st, u).wait()
                for r in range(MOBA_TOPK):
                    scat_copy(g, st, u, r).start()

        def wait_scatters(g, st):
            for u in range(per_set):
                for r in range(MOBA_TOPK):
                    scat_copy(g, st, u, r).wait()

        issue(0, 0)

        @pl.loop(0, n_groups, step=2)
        def _(g):
            @pl.when(g > 0)
            def _():
                wait_scatters(g - 1, 1)
            issue(g + 1, 1)
            drain(g, 0)
            wait_scatters(g, 0)

            @pl.when(g + 2 < n_groups)
            def _():
                issue(g + 2, 0)
            drain(g + 1, 1)

        wait_scatters(n_groups - 1, 1)

    return scatter(rows, pos.reshape(pos.shape[0] // win, win))


def _sc_gather_rows(table, pos):
    n = pos.shape[0]
    n_cores, n_workers = _sc_workers()
    win = SC_INDEX_WINDOW
    wins_per_worker = n // win // n_workers
    assert n % (win * n_workers) == 0
    mesh = plsc.VectorSubcoreMesh(core_axis_name="c", subcore_axis_name="s")

    per_set = 2
    n_groups = wins_per_worker // per_set
    assert wins_per_worker % (2 * per_set) == 0

    @functools.partial(
        pl.kernel, mesh=mesh, out_type=jax.ShapeDtypeStruct((n, LANES), F32), name="moba_gather_rows",
        cost_estimate=pl.CostEstimate(flops=0, transcendentals=0, bytes_accessed=2 * n * LANES * 4 + n * 4),
        scratch_types=[pltpu.VMEM((wins_per_worker, win), jnp.int32)] + [pltpu.VMEM((win, LANES), F32)] * (2 * per_set)
                      + [pltpu.SemaphoreType.DMA] * (2 * per_set + 2))
    def gather(table_hbm, idx_hbm, out_hbm, idx_all, *scratch):
        rows_v = scratch[:2 * per_set]
        gather_sems, store_sems = scratch[2 * per_set:4 * per_set], scratch[4 * per_set:]
        wid = lax.axis_index("s") * n_cores + lax.axis_index("c")
        base = wid * wins_per_worker
        pltpu.sync_copy(idx_hbm.at[pl.ds(base, wins_per_worker)], idx_all)

        def gather_copy(g, st, u):
            return pltpu.make_async_copy(table_hbm.at[idx_all.at[g * per_set + u]], rows_v[st * per_set + u],
                                         gather_sems[st * per_set + u])

        def store_copy(g, st, u):
            return pltpu.make_async_copy(rows_v[st * per_set + u],
                                         out_hbm.at[pl.ds((base + g * per_set + u) * win, win)], store_sems[st])

        def issue(g, st):
            for u in range(per_set):
                gather_copy(g, st, u).start()

        def drain(g, st):
            for u in range(per_set):
                gather_copy(g, st, u).wait()
                store_copy(g, st, u).start()

        def wait_stores(g, st):
            for u in range(per_set):
                store_copy(g, st, u).wait()

        issue(0, 0)

        @pl.loop(0, n_groups, step=2)
        def _(g):
            @pl.when(g > 0)
            def _():
                wait_stores(g - 1, 1)
            issue(g + 1, 1)
            drain(g, 0)
            wait_stores(g, 0)

            @pl.when(g + 2 < n_groups)
            def _():
                issue(g + 2, 0)
            drain(g + 1, 1)

        wait_stores(n_groups - 1, 1)

    return gather(table, pos.reshape(n // win, win))


def _partial_cols(pv, m):
    n = pv.shape[1]
    stacked = jnp.concatenate([pv, jnp.zeros((LANES - MOBA_VROWS, n), F32)], axis=0)
    row = lax.broadcasted_iota(jnp.int32, (LANES, n), 0)
    return jnp.where(row == HEAD_DIM + 1, m, stacked)


def _moba_sparse_kernel(blk_ref, q_ref, k_ref, v_ref, _after_ref, o_ref, *, n_groups):
    h, g = pl.program_id(0), pl.program_id(1)
    first = (h * n_groups + g) * MOBA_GROUP

    @pl.when(blk_ref[first] >= 0)
    def _():
        work = []
        for u in range(MOBA_GROUP):
            blk = jnp.maximum(blk_ref[first + u], 0)
            rows = pl.ds(pl.multiple_of(blk * MOBA_BLOCK, MOBA_BLOCK), MOBA_BLOCK)
            q = q_ref[u * MOBA_PAIR_TILE:(u + 1) * MOBA_PAIR_TILE, :].astype(BF16)
            work.append((rows, _dot_nt(q, k_ref[0, rows, :])))
        lane = lax.broadcasted_iota(jnp.int32, (MOBA_PAIR_TILE, LANES), 1)
        for u, (rows, s) in enumerate(work):
            m = jnp.max(s, axis=-1, keepdims=True)
            pv = _dot(jnp.exp(s - m).astype(BF16), v_ref[0, rows, :])
            o_ref[u * MOBA_PAIR_TILE:(u + 1) * MOBA_PAIR_TILE, :] = jnp.where(lane == HEAD_DIM + 1, m, pv)

    @pl.when(g == n_groups - 1)
    def _():
        lane = lax.broadcasted_iota(jnp.int32, (MOBA_PAIR_TILE, LANES), 1)
        o_ref[(MOBA_GROUP - 1) * MOBA_PAIR_TILE:, :] = jnp.where(lane == HEAD_DIM + 1, NEG, 0.0)


def _moba_sparse(q_sorted, k_pad, v_aug, tile_blk, n_tiles, after):
    n_heads, s, _ = k_pad.shape
    n_groups = n_tiles // MOBA_GROUP
    rows = pl.BlockSpec((MOBA_GROUP * MOBA_PAIR_TILE, LANES), lambda h, g, blk: (h * n_groups + g, 0))
    grid_spec = pltpu.PrefetchScalarGridSpec(
        num_scalar_prefetch=1, grid=(n_heads, n_groups),
        in_specs=[rows, pl.BlockSpec((1, s, LANES), lambda h, g, blk: (h, 0, 0)),
                  pl.BlockSpec((1, s, LANES), lambda h, g, blk: (h, 0, 0)), _ORDER_ONLY],
        out_specs=rows)
    return pl.pallas_call(
        functools.partial(_moba_sparse_kernel, n_groups=n_groups),
        grid_spec=grid_spec,
        out_shape=jax.ShapeDtypeStruct(q_sorted.shape, F32),
        compiler_params=_cparams("arbitrary", "arbitrary"),
        name="moba_sparse",
    )(tile_blk, q_sorted, k_pad, v_aug, after)


def _moba_merge_kernel(qa_ref, qb_ref, ka_ref, kb_ref, va_ref, vb_ref, ga_ref, gb_ref, _after_ref, o_ref):
    ki = lax.broadcasted_iota(jnp.int32, (MOBA_BLOCK, MOBA_BLOCK), 0)
    qi = lax.broadcasted_iota(jnp.int32, (MOBA_BLOCK, MOBA_BLOCK), 1)
    heads = ((qa_ref, ka_ref, va_ref, ga_ref), (qb_ref, kb_ref, vb_ref, gb_ref))
    chains = [(hh, b) for b in range(MOBA_MERGE_BLOCKS) for hh in range(2)]
    rows = lambda b: slice(b * MOBA_BLOCK, (b + 1) * MOBA_BLOCK)
    scores = [_dot_nt(heads[hh][1][0, rows(b), :], heads[hh][0][0, rows(b), :].astype(BF16))
              for hh, b in chains]
    outs = {}
    for (hh, b), s in zip(chains, scores):
        _, _, v_ref, g_ref = heads[hh]
        s = jnp.where(ki <= qi, s, NEG)
        m = jnp.max(s, axis=0, keepdims=True)
        parts = [_partial_cols(_dot(v_ref[0, b], jnp.exp(s - m).astype(BF16)), m)]
        parts += [g_ref[0, r, rows(b), :].T for r in range(MOBA_TOPK)]
        ms = [x[HEAD_DIM + 1:HEAD_DIM + 2, :] for x in parts]
        m_all = functools.reduce(jnp.maximum, ms)
        acc = sum(jnp.exp(mi - m_all) * x for mi, x in zip(ms, parts))
        outs[hh, b] = acc[:HEAD_DIM, :] / acc[HEAD_DIM:HEAD_DIM + 1, :]
    for b in range(MOBA_MERGE_BLOCKS):
        o_ref[rows(b), :] = jnp.concatenate([outs[0, b], outs[1, b]], axis=0).T.astype(BF16)


def _moba_merge(q_rows, k_pad, vt_aug, gathered, after):
    n_heads, s, _ = q_rows.shape
    t = MOBA_MERGE_BLOCKS * MOBA_BLOCK
    blk = lambda par: pl.BlockSpec((1, t, LANES), lambda p, i: (2 * p + par, i, 0))
    vspec = lambda par: pl.BlockSpec((1, MOBA_MERGE_BLOCKS, MOBA_VROWS, MOBA_BLOCK),
                                     lambda p, i: (2 * p + par, i, 0, 0))
    gspec = lambda par: pl.BlockSpec((1, MOBA_TOPK, t, LANES), lambda p, i: (2 * p + par, 0, i, 0))
    return pl.pallas_call(
        _moba_merge_kernel,
        grid=(n_heads // 2, s // t),
        in_specs=[blk(0), blk(1), blk(0), blk(1), vspec(0), vspec(1), gspec(0), gspec(1), _ORDER_ONLY],
        out_specs=pl.BlockSpec((t, LANES), lambda p, i: (i, p)),
        out_shape=jax.ShapeDtypeStruct((s, WIDTH), BF16),
        compiler_params=_cparams("arbitrary", "arbitrary"),
        name="moba_merge",
    )(q_rows, q_rows, k_pad, k_pad, vt_aug, vt_aug, gathered, gathered, after)


def _moba_attention(qc, kc, vc, kmean, overlap_a, overlap_b):
    s = qc.shape[0]
    n_tiles = _moba_tiles_per_head(s)
    q_rows, k_pad, v_aug, vt_aug, sel, rank, counts_raw = _moba_route(qc, kc, vc, _moba_mean_rows(kmean))
    counts = counts_raw[:, :s // MOBA_BLOCK, 0].astype(jnp.int32)
    pos, tile_blk = _moba_dispatch(sel, rank, counts, n_tiles)
    q_sorted = _sc_scatter_rows(q_rows.reshape(N_HEADS * s, LANES), pos, N_HEADS * n_tiles * MOBA_PAIR_TILE)
    def chain(calls, anchor):
        outs = []
        for call in calls:
            outs.append(call(anchor))
            anchor = outs[-1]
        return outs

    out_a = chain(overlap_a, counts_raw)
    partial = _moba_sparse(q_sorted, k_pad, v_aug, tile_blk, n_tiles, out_a[-1])
    gathered = _sc_gather_rows(partial, pos).reshape(N_HEADS, MOBA_TOPK, s, LANES)
    out_b = chain(overlap_b, partial)
    return out_a, out_b, _moba_merge(q_rows, k_pad, vt_aug, gathered, out_b[-1])


def _ssd_kernel(z_ref, xbc_ref, dt_ref, cw_ref, cb_ref, dtb_ref, alog_ref, dsk_ref, nw_ref, exp_ref, tri_ref,
                _after_ref, o_ref, state_ref, halo_ref):
    L = SSM_CHUNK

    @pl.when(pl.program_id(0) == 0)
    def _():
        state_ref[...] = jnp.zeros_like(state_ref)
        halo_ref[...] = jnp.zeros_like(halo_ref)

    cur = xbc_ref[...]
    prev8 = halo_ref[...]
    row8 = lax.broadcasted_iota(jnp.int32, (8, SSM_CONV_DIM), 0)
    conv = cb_ref[...] + cur * cw_ref[SSM_CONV - 1:SSM_CONV, :]
    for k in range(1, SSM_CONV):
        rolled = pltpu.roll(cur, k, 0)
        head = jnp.where(row8 < k, pltpu.roll(prev8, k, 0), rolled[0:8, :])
        shifted = jnp.concatenate([head, rolled[8:, :]], axis=0)
        conv = conv + shifted * cw_ref[SSM_CONV - 1 - k:SSM_CONV - k, :]
    halo_ref[...] = cur[L - 8:, :]
    act = _silu(conv)
    xs, bm, cm = act[:, :WIDTH], act[:, WIDTH:WIDTH + 2 * SSM_STATE], act[:, WIDTH + 2 * SSM_STATE:]

    dt = _softplus(dt_ref[...] + dtb_ref[...])
    da = dt * (-jnp.exp(alog_ref[...]))
    cs = _dot_exact_lhs(tri_ref[...], da)
    expand = exp_ref[...]
    cs_b = _dot_exact_rhs(cs, expand)
    dt_b = _dot_exact_rhs(dt, expand)
    cs_last = cs_b[L - 1:L, :]
    xdt = xs * dt_b
    cs_t = cs.T

    ri = lax.broadcasted_iota(jnp.int32, (L, L), 0)
    ci = lax.broadcasted_iota(jnp.int32, (L, L), 1)
    causal = ci <= ri
    lane = lax.broadcasted_iota(jnp.int32, (L, LANES), 1)
    bmb, cmb = bm.astype(BF16), cm.astype(BF16)
    xdtb = xdt.astype(BF16)
    y_pairs = []
    for p in range(N_HEADS // 2):
        g = (2 * p) // (N_HEADS // SSM_GROUPS)
        grp = slice(g * SSM_STATE, (g + 1) * SSM_STATE)
        cbm = _dot_nt(cmb[:, grp], bmb[:, grp])
        x_pair = xdtb[:, p * LANES:(p + 1) * LANES]
        halves = []
        for h in (2 * p, 2 * p + 1):
            seg = cs[:, h:h + 1] - cs_t[h:h + 1, :]
            decay = jnp.exp(jnp.where(causal, seg, NEG))
            halves.append(_dot((cbm * decay).astype(BF16), x_pair))
        y_pairs.append(jnp.where(lane < HEAD_DIM, halves[0], halves[1]))
    y_diag = jnp.concatenate(y_pairs, axis=1)

    state = state_ref[...]
    stb = state.astype(BF16)
    half = WIDTH // SSM_GROUPS
    y_off = jnp.concatenate(
        [_dot(cmb[:, g * SSM_STATE:(g + 1) * SSM_STATE], stb[:, g * half:(g + 1) * half])
         for g in range(SSM_GROUPS)], axis=1) * jnp.exp(cs_b)
    y = y_diag + y_off + xs * dsk_ref[...]

    w = (xdt * jnp.exp(cs_last - cs_b)).astype(BF16)
    bm_t = bm.T.astype(BF16)
    new = jnp.concatenate(
        [_dot(bm_t[g * SSM_STATE:(g + 1) * SSM_STATE, :], w[:, g * half:(g + 1) * half])
         for g in range(SSM_GROUPS)], axis=1)
    state_ref[...] = state * jnp.exp(cs_last) + new

    gt = y * _silu(z_ref[...])
    outs = []
    for g in range(SSM_GROUPS):
        gg = gt[:, g * half:(g + 1) * half]
        outs.append(gg * lax.rsqrt(jnp.mean(gg * gg, axis=-1, keepdims=True) + NORM_EPS))
    o_ref[...] = (jnp.concatenate(outs, axis=1) * nw_ref[...]).astype(BF16)


def _ssd_mixer(z, xbc, dt_raw, conv_w, conv_b, dt_bias, a_log, d_skip, norm_w, expand, tri, after):
    s = z.shape[0]
    L = SSM_CHUNK
    row = lambda n: pl.BlockSpec((L, n), lambda c: (c, 0))
    cst = lambda r, n: pl.BlockSpec((r, n), lambda c: (0, 0))
    return pl.pallas_call(
        _ssd_kernel,
        grid=(s // L,),
        in_specs=[row(WIDTH), row(SSM_CONV_DIM), row(DT_PAD), cst(SSM_CONV, SSM_CONV_DIM),
                  cst(1, SSM_CONV_DIM), cst(1, DT_PAD), cst(1, DT_PAD), cst(1, WIDTH), cst(1, WIDTH),
                  cst(DT_PAD, WIDTH), cst(L, L), _ORDER_ONLY],
        out_specs=row(WIDTH),
        out_shape=jax.ShapeDtypeStruct((s, WIDTH), BF16),
        scratch_shapes=[pltpu.VMEM((SSM_STATE, WIDTH), F32), pltpu.VMEM((8, SSM_CONV_DIM), F32)],
        compiler_params=_cparams("arbitrary"),
        cost_estimate=pl.CostEstimate(
            flops=2 * s * L * (SSM_GROUPS * SSM_STATE + WIDTH + 2 * N_HEADS * LANES) + 4 * s * SSM_STATE * WIDTH,
            transcendentals=s * (L * N_HEADS + SSM_CONV_DIM + 3 * WIDTH),
            bytes_accessed=s * (WIDTH + SSM_CONV_DIM + DT_PAD) * 4 + s * WIDTH * 2),
        name="ssd_mixer",
    )(z, xbc, dt_raw, conv_w, conv_b, dt_bias, a_log, d_skip, norm_w, expand, tri, after)


def _mix_ffn_kernel(x_ref, *refs, chunk):
    *y_refs, wo_ref, gm_ref, g_ref, sc_ref, sh_ref, gate_ref, wg_ref, wu_ref, wd_ref, o_ref = refs
    y, row0 = 0.0, 0
    for y_ref in y_refs:
        y = y + _dot(y_ref[...], wo_ref[row0:row0 + y_ref.shape[1], :])
        row0 += y_ref.shape[1]
    x = x_ref[...] + gm_ref[...] * y
    h = _mod_norm(x, g_ref[...], sc_ref[...], sh_ref[...]).astype(BF16)
    hidden = wg_ref.shape[1]
    n_chunks = hidden // chunk
    cols = lambda c: slice(c * chunk, (c + 1) * chunk)
    gate_up = lambda c: (_dot(h, wg_ref[:, cols(c)]), _dot(h, wu_ref[:, cols(c)]))
    acc = jnp.zeros(x.shape, F32)
    nxt = gate_up(0)
    for c in range(n_chunks):
        g, u = nxt
        if c + 1 < n_chunks:
            nxt = gate_up(c + 1)
        acc = acc + _dot((_silu(g) * u).astype(BF16), wd_ref[cols(c), :])
    o_ref[...] = x + gate_ref[...] * acc


def _mix_ffn(x, ys, w_out, gate_m, g, sc, sh, gate_f, w_gate, w_up, w_down, layer):
    _, s, d = x.shape
    hidden = w_gate.shape[2]
    tm = 512
    row = _stream_rows(tm, d)
    vec = pl.BlockSpec((1, d), lambda i: (0, 0))
    assert sum(y.shape[1] for y in ys) == w_out.shape[1]
    return pl.pallas_call(
        functools.partial(_mix_ffn_kernel, chunk=256),
        grid=(s // tm,),
        in_specs=[row] + [pl.BlockSpec((tm, y.shape[1]), lambda i: (i, 0)) for y in ys]
                 + [_resident((None, w_out.shape[1], d), lambda i: (layer, 0, 0)), vec, vec, vec, vec, vec,
                    _resident((None, d, hidden), lambda i: (layer, 0, 0)),
                    _resident((None, d, hidden), lambda i: (layer, 0, 0)),
                    _resident((None, hidden, d), lambda i: (layer, 0, 0))],
        out_specs=row,
        out_shape=jax.ShapeDtypeStruct((1, s, d), F32),
        compiler_params=_cparams("arbitrary"),
        name="mix_ffn",
    )(x, *ys, w_out, gate_m, g, sc, sh, gate_f, w_gate, w_up, w_down)


def _rope_tables(seq):
    pos = jnp.arange(seq, dtype=F32)
    inv = ROPE_THETA ** (-jnp.arange(0, ROPE_DIM, 2, dtype=F32) / ROPE_DIM)
    j = jnp.arange(LANES) % HEAD_DIM
    inv_lane = jnp.tile(jnp.concatenate([inv, inv, jnp.zeros(HEAD_DIM - ROPE_DIM, F32)]), LANES // HEAD_DIM)
    ang = pos[:, None] * inv_lane[None, :]
    cos, sin = jnp.cos(ang), jnp.sin(ang)
    cos_t = jnp.where(j < ROPE_DIM, cos, 1.0)
    sin_lo = jnp.where(j < ROPE_HALF, -sin, 0.0)
    sin_hi = jnp.where((j >= ROPE_HALF) & (j < ROPE_DIM), sin, 0.0)
    return cos_t, sin_lo, sin_hi


def _moba_mean_rows(kmean):
    nb = kmean.shape[0]
    km = kmean.reshape(nb, N_HEADS, HEAD_DIM).transpose(1, 0, 2)
    return jnp.pad(km, ((0, 0), (0, MOBA_MAX_BLOCKS - nb), (0, LANES - HEAD_DIM)))


def kernel(x, c, ada_w, ada_b, norm_mix, w_in, qn_swa, kn_swa, qn_moba, kn_moba, conv_w, conv_b, dt_bias,
           a_log, d_skip, ssm_norm, w_out, norm_ffn, w_gate, w_up, w_down):
    batch, seq, d = x.shape
    depth = ada_w.shape[0]
    assert batch == 1 and d == D_MODEL
    assert seq % SWA_SPAN == 0 and seq // MOBA_BLOCK <= MOBA_MAX_BLOCKS
    xs = x

    mod = _ada_modulation(c, ada_w, ada_b)
    rope_tabs = _rope_tables(seq)
    lane_head = jnp.arange(LANES) // HEAD_DIM
    hsum = (lane_head[:, None] == lane_head[None, :]).astype(BF16)
    expand = (jnp.arange(DT_PAD)[:, None] == (jnp.arange(WIDTH) // HEAD_DIM)[None, :]).astype(BF16)
    tri = (jnp.arange(SSM_CHUNK)[:, None] >= jnp.arange(SSM_CHUNK)[None, :]).astype(BF16)
    pad8 = lambda v: jnp.pad(v, (0, DT_PAD - N_HEADS)).reshape(1, DT_PAD)
    two = lambda g: jnp.concatenate([g, g]).reshape(1, LANES)

    w_in_b = _cast_bf16(w_in, IN_PROJ_PAD)
    w_out_b, w_gate_b, w_up_b, w_down_b = (_cast_bf16(w) for w in (w_out, w_gate, w_up, w_down))

    for l in range(depth):
        shift_m, scale_m, gate_m, shift_f, scale_f, gate_f = [mod[l, :, i * d:(i + 1) * d] for i in range(6)]
        gains = [two(g[l]) for g in (qn_swa, kn_swa, qn_moba, kn_moba)]
        qa, ka, va, qc, kc, vc, z, xbc, dt_raw, kmean = _in_projection(
            xs, norm_mix[l].reshape(1, d), scale_m, shift_m, w_in_b, l, gains, hsum, rope_tabs)

        n_pairs = WIDTH // LANES
        first = n_pairs - 1
        dilated_lo = functools.partial(_dilated_attention, qa, ka, va, 0, first)
        dilated_hi = functools.partial(_dilated_attention, qa, ka, va, first, n_pairs - first)
        ssd = functools.partial(
            _ssd_mixer, z, xbc, dt_raw, conv_w[l], conv_b[l].reshape(1, -1), pad8(dt_bias[l]), pad8(a_log[l]),
            jnp.repeat(d_skip[l], HEAD_DIM).reshape(1, WIDTH), ssm_norm[l].reshape(1, WIDTH), expand, tri)
        (y_a_lo,), (y_a_hi, y_b), y_c = _moba_attention(qc, kc, vc, kmean, (dilated_lo,), (dilated_hi, ssd))

        xs = _mix_ffn(xs, (y_a_lo, y_a_hi, y_b, y_c), w_out_b, gate_m, norm_ffn[l].reshape(1, d), scale_f,
                      shift_f, gate_f, w_gate_b, w_up_b, w_down_b, l)
    return xs
```

```python
import functools

import jax
import jax.numpy as jnp
from jax import lax
from jax.experimental import pallas as pl
from jax.experimental.pallas import tpu as pltpu
from jax.experimental.pallas import tpu_sc as plsc

F32 = jnp.float32
BF16 = jnp.bfloat16

D_MODEL = 1024
HEAD_DIM = 64
N_HEADS = 8
WIDTH = N_HEADS * HEAD_DIM
ATTN_SCALE = HEAD_DIM ** -0.5
NORM_EPS = 1e-6
NEG = -1e30

ROPE_THETA = 500000.0
ROPE_DIM = HEAD_DIM // 4
ROPE_HALF = ROPE_DIM // 2

DILATIONS = (1, 4, 16)
SWA_BLOCK = 128
SWA_SPAN = DILATIONS[-1] * SWA_BLOCK
SWA_GROUP = 4

SSM_STATE = 128
SSM_GROUPS = 2
SSM_CONV = 4
SSM_CHUNK = 256
SSM_CONV_DIM = WIDTH + 2 * SSM_GROUPS * SSM_STATE

MOBA_BLOCK = 256
MOBA_TOPK = 3
MOBA_PAIR_TILE = 128
MOBA_GROUP = 64
SC_INDEX_WINDOW = 128
MOBA_MAX_BLOCKS = 64
MOBA_MERGE_BLOCKS = 8
MOBA_VROWS = HEAD_DIM + 16

LANES = 128
DT_PAD = LANES
IN_PROJ_PAD = 6 * WIDTH + WIDTH + SSM_CONV_DIM + DT_PAD
OFF_A, OFF_C, OFF_Z, OFF_XBC, OFF_DT = 0, 3 * WIDTH, 6 * WIDTH, 7 * WIDTH, 7 * WIDTH + SSM_CONV_DIM

VMEM_LIMIT = 56 * 1024 * 1024
CAST_BLOCK_BYTES = 6 * 1024 * 1024


_ORDER_ONLY = pl.BlockSpec(memory_space=pl.ANY)


def _stream_rows(tm, d):
    return pl.BlockSpec((None, tm, d), lambda i: (0, i, 0))


def _resident(block_shape, index_map):
    return pl.BlockSpec(block_shape, index_map, pipeline_mode=pl.Buffered(1))


def _cparams(*sem):
    return pltpu.CompilerParams(dimension_semantics=sem, vmem_limit_bytes=VMEM_LIMIT)


def _split3(a):
    hi = a.astype(BF16)
    r1 = a - hi.astype(F32)
    mid = r1.astype(BF16)
    lo = (r1 - mid.astype(F32)).astype(BF16)
    return hi, mid, lo


def _dot(a, b):
    return jnp.dot(a, b, preferred_element_type=F32)


def _dot_nt(a, b):
    return lax.dot_general(a, b, (((1,), (1,)), ((), ())), preferred_element_type=F32)


def _dot_exact_rhs(a, b_bf16):
    hi, mid, lo = _split3(a)
    return _dot(hi, b_bf16) + _dot(mid, b_bf16) + _dot(lo, b_bf16)


def _dot_exact_lhs(a_bf16, b):
    hi, mid, lo = _split3(b)
    return _dot(a_bf16, hi) + _dot(a_bf16, mid) + _dot(a_bf16, lo)


def _dot_bf16x3(a, b, dot=_dot):
    ah = a.astype(BF16)
    al = (a - ah.astype(F32)).astype(BF16)
    bh = b.astype(BF16)
    bl = (b - bh.astype(F32)).astype(BF16)
    return dot(ah, bh) + (dot(ah, bl) + dot(al, bh))


def _silu(x):
    return x / (1.0 + jnp.exp(-x))


def _softplus(x):
    return jnp.maximum(x, 0.0) + jnp.log1p(jnp.exp(-jnp.abs(x)))


def _cast_kernel(w_ref, o_ref):
    cols = w_ref.shape[1]
    o_ref[:, :cols] = w_ref[...].astype(BF16)
    if o_ref.shape[1] > cols:
        o_ref[:, cols:] = jnp.zeros((o_ref.shape[0], o_ref.shape[1] - cols), BF16)


def _cast_bf16(w, pad_cols=None):
    depth, rows, cols = w.shape
    out_cols = pad_cols or cols
    tr = max(t for t in range(8, rows + 1, 8) if rows % t == 0 and t * cols * 4 <= CAST_BLOCK_BYTES)
    return pl.pallas_call(
        _cast_kernel,
        grid=(depth, rows // tr),
        in_specs=[pl.BlockSpec((None, tr, cols), lambda l, i: (l, i, 0))],
        out_specs=pl.BlockSpec((None, tr, out_cols), lambda l, i: (l, i, 0)),
        out_shape=jax.ShapeDtypeStruct((depth, rows, out_cols), BF16),
        compiler_params=_cparams("arbitrary", "arbitrary"),
        name="cast_weights",
    )(w)


def _ada_kernel(c_ref, w_ref, b_ref, o_ref):
    c = c_ref[...]
    o_ref[0] = _dot_bf16x3(_silu(c), w_ref[0]) + b_ref[0]


def _ada_modulation(c, ada_w, ada_b):
    depth, d, n = ada_w.shape
    tn = 2048
    c8 = jnp.broadcast_to(c, (8, d))
    out = pl.pallas_call(
        _ada_kernel,
        grid=(depth, n // tn),
        in_specs=[pl.BlockSpec((8, d), lambda l, j: (0, 0)),
                  pl.BlockSpec((1, d, tn), lambda l, j: (l, 0, j)),
                  pl.BlockSpec((1, 1, tn), lambda l, j: (l, 0, j))],
        out_specs=pl.BlockSpec((1, 8, tn), lambda l, j: (l, 0, j)),
        out_shape=jax.ShapeDtypeStruct((depth, 8, n), F32),
        compiler_params=_cparams("arbitrary", "arbitrary"),
        name="ada_modulation",
    )(c8, ada_w, ada_b.reshape(depth, 1, n))
    return out[:, 0:1, :]


def _mod_norm(x, g, sc, sh):
    var = jnp.mean(x * x, axis=-1, keepdims=True)
    return (x * lax.rsqrt(var + NORM_EPS) * g) * (1.0 + sc) + sh


def _inproj_kernel(x_ref, g_ref, sc_ref, sh_ref, w_ref, gqa_ref, gka_ref, gqc_ref, gkc_ref, hsum_ref,
                   cos_ref, slo_ref, shi_ref,
                   qa_ref, ka_ref, va_ref, qc_ref, kc_ref, vc_ref, z_ref, xbc_ref, dt_ref, km_ref):
    h = _mod_norm(x_ref[...], g_ref[...], sc_ref[...], sh_ref[...]).astype(BF16)
    proj = lambda off, n=WIDTH: _dot(h, w_ref[:, off:off + n])
    raw = [proj(OFF_A), proj(OFF_A + WIDTH), proj(OFF_C), proj(OFF_C + WIDTH)]
    va_ref[...] = proj(OFF_A + 2 * WIDTH)
    vc_ref[...] = proj(OFF_C + 2 * WIDTH)
    z_ref[...] = proj(OFF_Z)

    hsum = hsum_ref[...]
    cos_t, slo, shi = cos_ref[...], slo_ref[...], shi_ref[...]
    rope = lambda x, gain_ref: _head_norm_rope(x, gain_ref[...], hsum, cos_t, slo, shi)
    n_blocks = x_ref.shape[0] // MOBA_BLOCK
    for p in range(WIDTH // LANES):
        lo, hi = p * LANES, (p + 1) * LANES
        qa_ref[:, lo:hi] = rope(raw[0][:, lo:hi], gqa_ref) * ATTN_SCALE
        ka_ref[:, lo:hi] = rope(raw[1][:, lo:hi], gka_ref)
        qc_ref[:, lo:hi] = rope(raw[2][:, lo:hi], gqc_ref)
        kc = rope(raw[3][:, lo:hi], gkc_ref)
        kc_ref[:, lo:hi] = kc
        for b in range(n_blocks):
            km_ref[b, :, lo:hi] = jnp.mean(kc[b * MOBA_BLOCK:(b + 1) * MOBA_BLOCK, :], axis=0, keepdims=True)
    xbc_ref[...] = proj(OFF_XBC, SSM_CONV_DIM)
    dt_ref[...] = proj(OFF_DT, DT_PAD)


def _in_projection(x, g, sc, sh, w_all, layer, gains, hsum, rope_tabs):
    _, s, d = x.shape
    tm = 512
    assert tm % MOBA_BLOCK == 0
    row = lambda n: pl.BlockSpec((tm, n), lambda i: (i, 0))
    vec = pl.BlockSpec((1, d), lambda i: (0, 0))
    cst = lambda r, n: pl.BlockSpec((r, n), lambda i: (0, 0))
    widths = [WIDTH] * 7 + [SSM_CONV_DIM, DT_PAD]
    return pl.pallas_call(
        _inproj_kernel,
        grid=(s // tm,),
        in_specs=[_stream_rows(tm, d), vec, vec, vec, _resident((None, d, IN_PROJ_PAD), lambda i: (layer, 0, 0))]
                 + [cst(1, LANES)] * 4 + [cst(LANES, LANES)] + [row(LANES)] * 3,
        out_specs=[row(n) for n in widths]
                  + [pl.BlockSpec((tm // MOBA_BLOCK, 1, WIDTH), lambda i: (i, 0, 0))],
        out_shape=[jax.ShapeDtypeStruct((s, n), F32) for n in widths]
                  + [jax.ShapeDtypeStruct((s // MOBA_BLOCK, 1, WIDTH), F32)],
        compiler_params=_cparams("arbitrary"),
        name="in_projection",
    )(x, g, sc, sh, w_all, *gains, hsum, *rope_tabs)


def _head_norm_rope(x, gain, hsum, cos_t, sin_lo, sin_hi):
    sq = x * x
    sq_hi = sq.astype(BF16)
    ss = _dot(sq_hi, hsum) + _dot((sq - sq_hi.astype(F32)).astype(BF16), hsum)
    y = x * lax.rsqrt(ss * (1.0 / HEAD_DIM) + NORM_EPS) * gain
    return (y * cos_t + pltpu.roll(y, ROPE_HALF, 1) * sin_hi
            + pltpu.roll(y, LANES - ROPE_HALF, 1) * sin_lo)


def _dilated_kernel(q_ref, kc_ref, kp_ref, vc_ref, vp_ref, _after_ref, o_ref, m_s, l_s, a_s, q4_s, k4_s, v4_s):
    has_prev = pl.program_id(1) > 0
    lane = lax.broadcasted_iota(jnp.int32, (SWA_BLOCK, LANES), 1)
    head_a = lane < HEAD_DIM
    qi = lax.broadcasted_iota(jnp.int32, (SWA_BLOCK, 2 * SWA_BLOCK), 0)
    ki = lax.broadcasted_iota(jnp.int32, (SWA_BLOCK, 2 * SWA_BLOCK), 1)
    band = (ki >= qi) & (ki <= qi + SWA_BLOCK)
    band_edge = band & ((ki >= SWA_BLOCK) | has_prev)

    quarter = SWA_SPAN // 4
    for r4 in range(4):
        src = pl.ds(r4, quarter, 4)
        q4_s[r4 * quarter:(r4 + 1) * quarter, :] = q_ref[src, :]
        for dst, prev, cur in ((k4_s, kp_ref, kc_ref), (v4_s, vp_ref, vc_ref)):
            dst[2 * r4 * quarter:(2 * r4 + 1) * quarter, :] = prev[src, :]
            dst[(2 * r4 + 1) * quarter:(2 * r4 + 2) * quarter, :] = cur[src, :]

    def unit_d16(a, r4):
        rows = pl.ds(4 * a + r4, SWA_BLOCK, 16)
        q = q4_s[pl.ds(r4 * quarter + a, SWA_BLOCK, 4), :]
        prev = pl.ds(2 * r4 * quarter + a, SWA_BLOCK, 4)
        cur = pl.ds((2 * r4 + 1) * quarter + a, SWA_BLOCK, 4)
        k_cat = jnp.concatenate([k4_s[prev, :], k4_s[cur, :]], axis=0)
        v_cat = jnp.concatenate([v4_s[prev, :], v4_s[cur, :]], axis=0)
        return rows, q, k_cat, v_cat, band_edge

    def unit_d4(r, nb):
        rows = pl.ds(4 * SWA_BLOCK * nb + r, SWA_BLOCK, 4)
        q = q4_s[pl.ds(r * quarter + SWA_BLOCK * nb, SWA_BLOCK), :]
        keys = pl.ds((2 * r + 1) * quarter + SWA_BLOCK * (nb - 1), 2 * SWA_BLOCK)
        return rows, q, k4_s[keys, :], v4_s[keys, :], (band_edge if nb == 0 else band)

    def unit_d1(b, edge=False):
        rows = pl.ds(SWA_BLOCK * b, SWA_BLOCK)
        if edge:
            prev = pl.ds(SWA_SPAN - SWA_BLOCK, SWA_BLOCK)
            k_cat = jnp.concatenate([kp_ref[prev, :], kc_ref[rows, :]], axis=0)
            v_cat = jnp.concatenate([vp_ref[prev, :], vc_ref[rows, :]], axis=0)
            return rows, q_ref[rows, :], k_cat, v_cat, band_edge
        keys = pl.ds(SWA_BLOCK * (b - 1), 2 * SWA_BLOCK)
        return rows, q_ref[rows, :], kc_ref[keys, :], vc_ref[keys, :], band

    def attend(units, mode):
        scores = []
        for _, q, k_cat, _, mask in units:
            kb = k_cat.astype(BF16)
            scores.append([jnp.where(mask, _dot_nt(jnp.where(sel, q, 0.0).astype(BF16), kb), NEG)
                           for sel in (head_a, ~head_a)])
        probs = []
        for pair in scores:
            stats = []
            for s in pair:
                m_loc = jnp.max(s, axis=-1, keepdims=True)
                p = jnp.exp(s - m_loc)
                stats.append((m_loc, jnp.sum(p, axis=-1, keepdims=True), p.astype(BF16)))
            probs.append(stats)
        for (rows, _, _, v_cat, _), ((m_a, s_a, p_a), (m_b, s_b, p_b)) in zip(units, probs):
            vb = v_cat.astype(BF16)
            m_new = jnp.where(head_a, m_a, m_b)
            l_new = jnp.where(head_a, s_a, s_b)
            a_new = jnp.where(head_a, _dot(p_a, vb), _dot(p_b, vb))
            if mode != "init":
                m_in, m_loc = m_s[rows, :], m_new
                m_new = jnp.maximum(m_in, m_loc)
                alpha, beta = jnp.exp(m_in - m_new), jnp.exp(m_loc - m_new)
                l_new = alpha * l_s[rows, :] + beta * l_new
                a_new = alpha * a_s[rows, :] + beta * a_new
            if mode == "final":
                o_ref[rows, :] = (a_new / l_new).astype(BF16)
            else:
                m_s[rows, :] = m_new
                l_s[rows, :] = l_new
                a_s[rows, :] = a_new

    g = SWA_GROUP
    assert DILATIONS == (1, 4, 16) and g == 4 and SWA_SPAN // (4 * SWA_BLOCK) == g

    def body16(a, carry):
        attend([unit_d16(a, r4) for r4 in range(g)], "init")
        return carry
    lax.fori_loop(0, 16 // g, body16, 0)

    def body4(r, carry):
        attend([unit_d4(r, nb) for nb in range(g)], "fold")
        return carry
    lax.fori_loop(0, 4, body4, 0)

    attend([unit_d1(b, edge=(b == 0)) for b in range(g)], "final")

    def body1(i, carry):
        attend([unit_d1(i * g + j) for j in range(g)], "final")
        return carry
    lax.fori_loop(1, SWA_SPAN // SWA_BLOCK // g, body1, 0)


def _dilated_attention(q, k, v, pair0, n_pairs, after):
    s = q.shape[0]
    cur = lambda off: pl.BlockSpec((SWA_SPAN, LANES), lambda p, n: (n, off + p))
    prev = lambda off: pl.BlockSpec((SWA_SPAN, LANES), lambda p, n: (jnp.maximum(n - 1, 0), off + p))
    width = n_pairs * LANES
    return pl.pallas_call(
        _dilated_kernel,
        grid=(n_pairs, s // SWA_SPAN),
        in_specs=[cur(pair0), cur(pair0), prev(pair0), cur(pair0), prev(pair0), _ORDER_ONLY],
        out_specs=cur(0),
        out_shape=jax.ShapeDtypeStruct((s, width), BF16),
        scratch_shapes=[pltpu.VMEM((SWA_SPAN, LANES), F32)] * 4 + [pltpu.VMEM((2 * SWA_SPAN, LANES), F32)] * 2,
        compiler_params=_cparams("arbitrary", "arbitrary"),
        cost_estimate=pl.CostEstimate(
            flops=len(DILATIONS) * 4 * s * 2 * SWA_BLOCK * width,
            transcendentals=len(DILATIONS) * s * 2 * SWA_BLOCK * 2 * n_pairs,
            bytes_accessed=5 * s * width * 4 + s * width * 2),
        name="dilated_attention",
    )(q, k, k, v, v, after)


def _moba_route_kernel(q_ref, k_ref, v_ref, km_ref, sut_ref, ones_ref,
                       qrow_ref, kpad_ref, vaug_ref, vt_ref, sel_ref, rank_ref, cnt_ref, base_ref):
    i = pl.program_id(0)

    @pl.when(i == 0)
    def _():
        base_ref[...] = jnp.zeros_like(base_ref)

    lane = lax.broadcasted_iota(jnp.int32, (MOBA_BLOCK, LANES), 1)
    is_feat = lane < HEAD_DIM
    row = lax.broadcasted_iota(jnp.int32, (MOBA_MAX_BLOCKS, MOBA_BLOCK), 0)
    past = row < i
    ones_rows = jnp.where(
        lax.broadcasted_iota(jnp.int32, (MOBA_VROWS - HEAD_DIM, MOBA_BLOCK), 0) == 0, 1.0, 0.0)
    ninf = float("-inf")
    gates = []
    for h in range(N_HEADS):
        cols = slice((h // 2) * LANES, (h // 2 + 1) * LANES)
        q, k = q_ref[:, cols], k_ref[:, cols]
        v = v_ref[:, cols]
        v_t = v.T
        if h % 2:
            q, k, v = (pltpu.roll(x, HEAD_DIM, 1) for x in (q, k, v))
            v_t = v_t[HEAD_DIM:, :]
        else:
            v_t = v_t[:HEAD_DIM, :]
        qrow_ref[h] = jnp.where(is_feat, q * ATTN_SCALE, 0.0)
        kpad_ref[h] = jnp.where(is_feat, k, 0.0).astype(BF16)
        vaug_ref[h] = jnp.where(is_feat, v, jnp.where(lane == HEAD_DIM, 1.0, 0.0)).astype(BF16)
        vt_ref[h, 0] = jnp.concatenate([v_t, ones_rows], axis=0).astype(BF16)
        gates.append(_dot_bf16x3(km_ref[h], jnp.where(is_feat, q, 0.0), dot=_dot_nt))
    for h, gate_t in enumerate(gates):
        cand = jnp.where(past, gate_t, ninf)
        picks = []
        for _ in range(MOBA_TOPK):
            best = jnp.max(cand, axis=0, keepdims=True)
            idx = jnp.min(jnp.where(cand == best, row, 2 * LANES), axis=0, keepdims=True)
            ok = best > ninf
            pick = (row == idx) & ok
            cand = jnp.where(pick, ninf, cand)
            picks.append((idx, ok, pick))
        chosen = jnp.where(picks[0][2] | picks[1][2] | picks[2][2], 1.0, 0.0).astype(BF16)
        base = base_ref[h]
        rank_full = base + _dot(chosen, sut_ref[...])
        base_ref[h] = base + _dot(chosen, ones_ref[...])
        for r, (idx, ok, pick) in enumerate(picks):
            rank = jnp.sum(jnp.where(pick, rank_full, 0.0), axis=0, keepdims=True)
            sel_ref[h, r:r + 1, :] = jnp.where(ok, idx, -1)
            rank_ref[h, r:r + 1, :] = rank.astype(jnp.int32)

    @pl.when(i == pl.num_programs(0) - 1)
    def _():
        cnt_ref[...] = base_ref[...]


def _moba_route(qc, kc, vc, km_rows):
    s = qc.shape[0]
    nb = s // MOBA_BLOCK
    row = pl.BlockSpec((MOBA_BLOCK, WIDTH), lambda i: (i, 0))
    per_head = lambda dt: (pl.BlockSpec((N_HEADS, MOBA_BLOCK, LANES), lambda i: (0, i, 0)),
                           jax.ShapeDtypeStruct((N_HEADS, s, LANES), dt))
    picks = (pl.BlockSpec((N_HEADS, MOBA_TOPK, MOBA_BLOCK), lambda i: (0, 0, i)),
             jax.ShapeDtypeStruct((N_HEADS, MOBA_TOPK, s), jnp.int32))
    outs = [per_head(F32), per_head(BF16), per_head(BF16),
            (pl.BlockSpec((N_HEADS, 1, MOBA_VROWS, MOBA_BLOCK), lambda i: (0, i, 0, 0)),
             jax.ShapeDtypeStruct((N_HEADS, nb, MOBA_VROWS, MOBA_BLOCK), BF16)),
            picks, picks,
            (pl.BlockSpec((N_HEADS, MOBA_MAX_BLOCKS, MOBA_BLOCK), lambda i: (0, 0, 0)),
             jax.ShapeDtypeStruct((N_HEADS, MOBA_MAX_BLOCKS, MOBA_BLOCK), F32))]
    qi = jnp.arange(MOBA_BLOCK)
    strict_upper = (qi[:, None] < qi[None, :]).astype(BF16)
    all_ones = jnp.ones((MOBA_BLOCK, MOBA_BLOCK), BF16)
    sq = pl.BlockSpec((MOBA_BLOCK, MOBA_BLOCK), lambda i: (0, 0))
    return pl.pallas_call(
        _moba_route_kernel,
        grid=(nb,),
        in_specs=[row, row, row,
                  pl.BlockSpec((N_HEADS, MOBA_MAX_BLOCKS, LANES), lambda i: (0, 0, 0)), sq, sq],
        out_specs=[o[0] for o in outs],
        out_shape=[o[1] for o in outs],
        scratch_shapes=[pltpu.VMEM((N_HEADS, MOBA_MAX_BLOCKS, MOBA_BLOCK), F32)],
        compiler_params=_cparams("arbitrary"),
        name="moba_route",
    )(qc, kc, vc, km_rows, strict_upper, all_ones)


def _moba_tiles_per_head(seq):
    tiles = MOBA_TOPK * seq // MOBA_PAIR_TILE + seq // MOBA_BLOCK
    return (tiles // MOBA_GROUP + 1) * MOBA_GROUP


def _moba_dispatch(sel, rank, counts, n_tiles):
    n_heads, nb = counts.shape
    padded = (counts + MOBA_PAIR_TILE - 1) // MOBA_PAIR_TILE * MOBA_PAIR_TILE
    ends = jnp.cumsum(padded, axis=1)
    offs = ends - padded
    rows_per_head = n_tiles * MOBA_PAIR_TILE
    head_base = (jnp.arange(n_heads, dtype=jnp.int32) * rows_per_head)[:, None, None]
    blocks = jnp.arange(nb, dtype=jnp.int32)
    off_sel = jnp.sum(jnp.where(sel[..., None] == blocks, offs[:, None, None, :], 0), axis=-1)
    null_row = rows_per_head - MOBA_PAIR_TILE + jnp.arange(sel.shape[-1], dtype=jnp.int32) % MOBA_PAIR_TILE
    pos = head_base + jnp.where(sel >= 0, off_sel + rank, null_row)
    tile_start = jnp.arange(n_tiles, dtype=jnp.int32) * MOBA_PAIR_TILE
    blk = jnp.sum(ends[:, None, :] <= tile_start[None, :, None], axis=2)
    tile_blk = jnp.where(tile_start[None, :] < ends[:, -1:], blk, -1)
    return pos.reshape(-1).astype(jnp.int32), tile_blk.reshape(-1).astype(jnp.int32)


def _sc_workers():
    info = plsc.get_sparse_core_info()
    return info.num_cores, info.num_cores * info.num_subcores


def _sc_scatter_rows(rows, pos, n_out):
    n_src = rows.shape[0]
    seq = n_src // N_HEADS
    n_cores, n_workers = _sc_workers()
    win = SC_INDEX_WINDOW
    wins_per_worker = n_src // win // n_workers
    wins_per_head = seq // win
    assert n_src % (win * n_workers) == 0
    mesh = plsc.VectorSubcoreMesh(core_axis_name="c", subcore_axis_name="s")

    per_set = 2
    n_groups = wins_per_worker // per_set
    assert wins_per_worker % (2 * per_set) == 0 and wins_per_head % wins_per_worker == 0

    @functools.partial(
        pl.kernel, mesh=mesh, out_type=jax.ShapeDtypeStruct((n_out, LANES), F32), name="moba_scatter_rows",
        cost_estimate=pl.CostEstimate(flops=0, transcendentals=0,
                                      bytes_accessed=(1 + MOBA_TOPK) * n_src * LANES * 4 + pos.size * 4),
        scratch_types=[pltpu.VMEM((wins_per_worker, win), jnp.int32)] * MOBA_TOPK
                      + [pltpu.VMEM((win, LANES), F32)] * (2 * per_set)
                      + [pltpu.SemaphoreType.DMA] * (2 * per_set + 2))
    def scatter(src_hbm, idx_hbm, out_hbm, *scratch):
        idx_all = scratch[:MOBA_TOPK]
        rows_v = scratch[MOBA_TOPK:MOBA_TOPK + 2 * per_set]
        load_sems, scat_sems = scratch[-2 * per_set - 2:-2], scratch[-2:]
        wid = lax.axis_index("s") * n_cores + lax.axis_index("c")
        base = wid * wins_per_worker
        h, qw0 = base // wins_per_head, base % wins_per_head
        for r in range(MOBA_TOPK):
            pltpu.sync_copy(idx_hbm.at[pl.ds((h * MOBA_TOPK + r) * wins_per_head + qw0, wins_per_worker)], idx_all[r])

        def load_copy(g, st, u):
            return pltpu.make_async_copy(src_hbm.at[pl.ds((base + g * per_set + u) * win, win)],
                                         rows_v[st * per_set + u], load_sems[st * per_set + u])

        def scat_copy(g, st, u, r):
            return pltpu.make_async_copy(rows_v[st * per_set + u], out_hbm.at[idx_all[r].at[g * per_set + u]],
                                         scat_sems[st])

        def issue(g, st):
            for u in range(per_set):
                load_copy(g, st, u).start()

        def drain(g, st):
            for u in range(per_set):
                load_copy(g, st, u).wait()
                for r in range(MOBA_TOPK):
                    scat_copy(g, st, u, r).start()

        def wait_scatters(g, st):
            for u in range(per_set):
                for r in range(MOBA_TOPK):
                    scat_copy(g, st, u, r).wait()

        issue(0, 0)

        @pl.loop(0, n_groups, step=2)
        def _(g):
            @pl.when(g > 0)
            def _():
                wait_scatters(g - 1, 1)
            issue(g + 1, 1)
            drain(g, 0)
            wait_scatters(g, 0)

            @pl.when(g + 2 < n_groups)
            def _():
                issue(g + 2, 0)
            drain(g + 1, 1)

        wait_scatters(n_groups - 1, 1)

    return scatter(rows, pos.reshape(pos.shape[0] // win, win))


def _sc_gather_rows(table, pos):
    n = pos.shape[0]
    n_cores, n_workers = _sc_workers()
    win = SC_INDEX_WINDOW
    wins_per_worker = n // win // n_workers
    assert n % (win * n_workers) == 0
    mesh = plsc.VectorSubcoreMesh(core_axis_name="c", subcore_axis_name="s")

    per_set = 2
    n_groups = wins_per_worker // per_set
    assert wins_per_worker % (2 * per_set) == 0

    @functools.partial(
        pl.kernel, mesh=mesh, out_type=jax.ShapeDtypeStruct((n, LANES), F32), name="moba_gather_rows",
        cost_estimate=pl.CostEstimate(flops=0, transcendentals=0, bytes_accessed=2 * n * LANES * 4 + n * 4),
        scratch_types=[pltpu.VMEM((wins_per_worker, win), jnp.int32)] + [pltpu.VMEM((win, LANES), F32)] * (2 * per_set)
                      + [pltpu.SemaphoreType.DMA] * (2 * per_set + 2))
    def gather(table_hbm, idx_hbm, out_hbm, idx_all, *scratch):
        rows_v = scratch[:2 * per_set]
        gather_sems, store_sems = scratch[2 * per_set:4 * per_set], scratch[4 * per_set:]
        wid = lax.axis_index("s") * n_cores + lax.axis_index("c")
        base = wid * wins_per_worker
        pltpu.sync_copy(idx_hbm.at[pl.ds(base, wins_per_worker)], idx_all)

        def gather_copy(g, st, u):
            return pltpu.make_async_copy(table_hbm.at[idx_all.at[g * per_set + u]], rows_v[st * per_set + u],
                                         gather_sems[st * per_set + u])

        def store_copy(g, st, u):
            return pltpu.make_async_copy(rows_v[st * per_set + u],
                                         out_hbm.at[pl.ds((base + g * per_set + u) * win, win)], store_sems[st])

        def issue(g, st):
            for u in range(per_set):
                gather_copy(g, st, u).start()

        def drain(g, st):
            for u in range(per_set):
                gather_copy(g, st, u).wait()
                store_copy(g, st, u).start()

        def wait_stores(g, st):
            for u in range(per_set):
                store_copy(g, st, u).wait()

        issue(0, 0)

        @pl.loop(0, n_groups, step=2)
        def _(g):
            @pl.when(g > 0)
            def _():
                wait_stores(g - 1, 1)
            issue(g + 1, 1)
            drain(g, 0)
            wait_stores(g, 0)

            @pl.when(g + 2 < n_groups)
            def _():
                issue(g + 2, 0)
            drain(g + 1, 1)

        wait_stores(n_groups - 1, 1)

    return gather(table, pos.reshape(n // win, win))


def _partial_cols(pv, m):
    n = pv.shape[1]
    stacked = jnp.concatenate([pv, jnp.zeros((LANES - MOBA_VROWS, n), F32)], axis=0)
    row = lax.broadcasted_iota(jnp.int32, (LANES, n), 0)
    return jnp.where(row == HEAD_DIM + 1, m, stacked)


def _moba_sparse_kernel(blk_ref, q_ref, k_ref, v_ref, _after_ref, o_ref, *, n_groups):
    h, g = pl.program_id(0), pl.program_id(1)
    first = (h * n_groups + g) * MOBA_GROUP

    @pl.when(blk_ref[first] >= 0)
    def _():
        work = []
        for u in range(MOBA_GROUP):
            blk = jnp.maximum(blk_ref[first + u], 0)
            rows = pl.ds(pl.multiple_of(blk * MOBA_BLOCK, MOBA_BLOCK), MOBA_BLOCK)
            q = q_ref[u * MOBA_PAIR_TILE:(u + 1) * MOBA_PAIR_TILE, :].astype(BF16)
            work.append((rows, _dot_nt(q, k_ref[0, rows, :])))
        lane = lax.broadcasted_iota(jnp.int32, (MOBA_PAIR_TILE, LANES), 1)
        for u, (rows, s) in enumerate(work):
            m = jnp.max(s, axis=-1, keepdims=True)
            pv = _dot(jnp.exp(s - m).astype(BF16), v_ref[0, rows, :])
            o_ref[u * MOBA_PAIR_TILE:(u + 1) * MOBA_PAIR_TILE, :] = jnp.where(lane == HEAD_DIM + 1, m, pv)

    @pl.when(g == n_groups - 1)
    def _():
        lane = lax.broadcasted_iota(jnp.int32, (MOBA_PAIR_TILE, LANES), 1)
        o_ref[(MOBA_GROUP - 1) * MOBA_PAIR_TILE:, :] = jnp.where(lane == HEAD_DIM + 1, NEG, 0.0)


def _moba_sparse(q_sorted, k_pad, v_aug, tile_blk, n_tiles, after):
    n_heads, s, _ = k_pad.shape
    n_groups = n_tiles // MOBA_GROUP
    rows = pl.BlockSpec((MOBA_GROUP * MOBA_PAIR_TILE, LANES), lambda h, g, blk: (h * n_groups + g, 0))
    grid_spec = pltpu.PrefetchScalarGridSpec(
        num_scalar_prefetch=1, grid=(n_heads, n_groups),
        in_specs=[rows, pl.BlockSpec((1, s, LANES), lambda h, g, blk: (h, 0, 0)),
                  pl.BlockSpec((1, s, LANES), lambda h, g, blk: (h, 0, 0)), _ORDER_ONLY],
        out_specs=rows)
    return pl.pallas_call(
        functools.partial(_moba_sparse_kernel, n_groups=n_groups),
        grid_spec=grid_spec,
        out_shape=jax.ShapeDtypeStruct(q_sorted.shape, F32),
        compiler_params=_cparams("arbitrary", "arbitrary"),
        name="moba_sparse",
    )(tile_blk, q_sorted, k_pad, v_aug, after)


def _moba_merge_kernel(qa_ref, qb_ref, ka_ref, kb_ref, va_ref, vb_ref, ga_ref, gb_ref, _after_ref, o_ref):
    ki = lax.broadcasted_iota(jnp.int32, (MOBA_BLOCK, MOBA_BLOCK), 0)
    qi = lax.broadcasted_iota(jnp.int32, (MOBA_BLOCK, MOBA_BLOCK), 1)
    heads = ((qa_ref, ka_ref, va_ref, ga_ref), (qb_ref, kb_ref, vb_ref, gb_ref))
    chains = [(hh, b) for b in range(MOBA_MERGE_BLOCKS) for hh in range(2)]
    rows = lambda b: slice(b * MOBA_BLOCK, (b + 1) * MOBA_BLOCK)
    scores = [_dot_nt(heads[hh][1][0, rows(b), :], heads[hh][0][0, rows(b), :].astype(BF16))
              for hh, b in chains]
    outs = {}
    for (hh, b), s in zip(chains, scores):
        _, _, v_ref, g_ref = heads[hh]
        s = jnp.where(ki <= qi, s, NEG)
        m = jnp.max(s, axis=0, keepdims=True)
        parts = [_partial_cols(_dot(v_ref[0, b], jnp.exp(s - m).astype(BF16)), m)]
        parts += [g_ref[0, r, rows(b), :].T for r in range(MOBA_TOPK)]
        ms = [x[HEAD_DIM + 1:HEAD_DIM + 2, :] for x in parts]
        m_all = functools.reduce(jnp.maximum, ms)
        acc = sum(jnp.exp(mi - m_all) * x for mi, x in zip(ms, parts))
        outs[hh, b] = acc[:HEAD_DIM, :] / acc[HEAD_DIM:HEAD_DIM + 1, :]
    for b in range(MOBA_MERGE_BLOCKS):
        o_ref[rows(b), :] = jnp.concatenate([outs[0, b], outs[1, b]], axis=0).T.astype(BF16)


def _moba_merge(q_rows, k_pad, vt_aug, gathered, after):
    n_heads, s, _ = q_rows.shape
    t = MOBA_MERGE_BLOCKS * MOBA_BLOCK
    blk = lambda par: pl.BlockSpec((1, t, LANES), lambda p, i: (2 * p + par, i, 0))
    vspec = lambda par: pl.BlockSpec((1, MOBA_MERGE_BLOCKS, MOBA_VROWS, MOBA_BLOCK),
                                     lambda p, i: (2 * p + par, i, 0, 0))
    gspec = lambda par: pl.BlockSpec((1, MOBA_TOPK, t, LANES), lambda p, i: (2 * p + par, 0, i, 0))
    return pl.pallas_call(
        _moba_merge_kernel,
        grid=(n_heads // 2, s // t),
        in_specs=[blk(0), blk(1), blk(0), blk(1), vspec(0), vspec(1), gspec(0), gspec(1), _ORDER_ONLY],
        out_specs=pl.BlockSpec((t, LANES), lambda p, i: (i, p)),
        out_shape=jax.ShapeDtypeStruct((s, WIDTH), BF16),
        compiler_params=_cparams("arbitrary", "arbitrary"),
        name="moba_merge",
    )(q_rows, q_rows, k_pad, k_pad, vt_aug, vt_aug, gathered, gathered, after)


def _moba_attention(qc, kc, vc, kmean, overlap_a, overlap_b):
    s = qc.shape[0]
    n_tiles = _moba_tiles_per_head(s)
    q_rows, k_pad, v_aug, vt_aug, sel, rank, counts_raw = _moba_route(qc, kc, vc, _moba_mean_rows(kmean))
    counts = counts_raw[:, :s // MOBA_BLOCK, 0].astype(jnp.int32)
    pos, tile_blk = _moba_dispatch(sel, rank, counts, n_tiles)
    q_sorted = _sc_scatter_rows(q_rows.reshape(N_HEADS * s, LANES), pos, N_HEADS * n_tiles * MOBA_PAIR_TILE)
    def chain(calls, anchor):
        outs = []
        for call in calls:
            outs.append(call(anchor))
            anchor = outs[-1]
        return outs

    out_a = chain(overlap_a, counts_raw)
    partial = _moba_sparse(q_sorted, k_pad, v_aug, tile_blk, n_tiles, out_a[-1])
    gathered = _sc_gather_rows(partial, pos).reshape(N_HEADS, MOBA_TOPK, s, LANES)
    out_b = chain(overlap_b, partial)
    return out_a, out_b, _moba_merge(q_rows, k_pad, vt_aug, gathered, out_b[-1])


def _ssd_kernel(z_ref, xbc_ref, dt_ref, cw_ref, cb_ref, dtb_ref, alog_ref, dsk_ref, nw_ref, exp_ref, tri_ref,
                _after_ref, o_ref, state_ref, halo_ref):
    L = SSM_CHUNK

    @pl.when(pl.program_id(0) == 0)
    def _():
        state_ref[...] = jnp.zeros_like(state_ref)
        halo_ref[...] = jnp.zeros_like(halo_ref)

    cur = xbc_ref[...]
    prev8 = halo_ref[...]
    row8 = lax.broadcasted_iota(jnp.int32, (8, SSM_CONV_DIM), 0)
    conv = cb_ref[...] + cur * cw_ref[SSM_CONV - 1:SSM_CONV, :]
    for k in range(1, SSM_CONV):
        rolled = pltpu.roll(cur, k, 0)
        head = jnp.where(row8 < k, pltpu.roll(prev8, k, 0), rolled[0:8, :])
        shifted = jnp.concatenate([head, rolled[8:, :]], axis=0)
        conv = conv + shifted * cw_ref[SSM_CONV - 1 - k:SSM_CONV - k, :]
    halo_ref[...] = cur[L - 8:, :]
    act = _silu(conv)
    xs, bm, cm = act[:, :WIDTH], act[:, WIDTH:WIDTH + 2 * SSM_STATE], act[:, WIDTH + 2 * SSM_STATE:]

    dt = _softplus(dt_ref[...] + dtb_ref[...])
    da = dt * (-jnp.exp(alog_ref[...]))
    cs = _dot_exact_lhs(tri_ref[...], da)
    expand = exp_ref[...]
    cs_b = _dot_exact_rhs(cs, expand)
    dt_b = _dot_exact_rhs(dt, expand)
    cs_last = cs_b[L - 1:L, :]
    xdt = xs * dt_b
    cs_t = cs.T

    ri = lax.broadcasted_iota(jnp.int32, (L, L), 0)
    ci = lax.broadcasted_iota(jnp.int32, (L, L), 1)
    causal = ci <= ri
    lane = lax.broadcasted_iota(jnp.int32, (L, LANES), 1)
    bmb, cmb = bm.astype(BF16), cm.astype(BF16)
    xdtb = xdt.astype(BF16)
    y_pairs = []
    for p in range(N_HEADS // 2):
        g = (2 * p) // (N_HEADS // SSM_GROUPS)
        grp = slice(g * SSM_STATE, (g + 1) * SSM_STATE)
        cbm = _dot_nt(cmb[:, grp], bmb[:, grp])
        x_pair = xdtb[:, p * LANES:(p + 1) * LANES]
        halves = []
        for h in (2 * p, 2 * p + 1):
            seg = cs[:, h:h + 1] - cs_t[h:h + 1, :]
            decay = jnp.exp(jnp.where(causal, seg, NEG))
            halves.append(_dot((cbm * decay).astype(BF16), x_pair))
        y_pairs.append(jnp.where(lane < HEAD_DIM, halves[0], halves[1]))
    y_diag = jnp.concatenate(y_pairs, axis=1)

    state = state_ref[...]
    stb = state.astype(BF16)
    half = WIDTH // SSM_GROUPS
    y_off = jnp.concatenate(
        [_dot(cmb[:, g * SSM_STATE:(g + 1) * SSM_STATE], stb[:, g * half:(g + 1) * half])
         for g in range(SSM_GROUPS)], axis=1) * jnp.exp(cs_b)
    y = y_diag + y_off + xs * dsk_ref[...]

    w = (xdt * jnp.exp(cs_last - cs_b)).astype(BF16)
    bm_t = bm.T.astype(BF16)
    new = jnp.concatenate(
        [_dot(bm_t[g * SSM_STATE:(g + 1) * SSM_STATE, :], w[:, g * half:(g + 1) * half])
         for g in range(SSM_GROUPS)], axis=1)
    state_ref[...] = state * jnp.exp(cs_last) + new

    gt = y * _silu(z_ref[...])
    outs = []
    for g in range(SSM_GROUPS):
        gg = gt[:, g * half:(g + 1) * half]
        outs.append(gg * lax.rsqrt(jnp.mean(gg * gg, axis=-1, keepdims=True) + NORM_EPS))
    o_ref[...] = (jnp.concatenate(outs, axis=1) * nw_ref[...]).astype(BF16)


def _ssd_mixer(z, xbc, dt_raw, conv_w, conv_b, dt_bias, a_log, d_skip, norm_w, expand, tri, after):
    s = z.shape[0]
    L = SSM_CHUNK
    row = lambda n: pl.BlockSpec((L, n), lambda c: (c, 0))
    cst = lambda r, n: pl.BlockSpec((r, n), lambda c: (0, 0))
    return pl.pallas_call(
        _ssd_kernel,
        grid=(s // L,),
        in_specs=[row(WIDTH), row(SSM_CONV_DIM), row(DT_PAD), cst(SSM_CONV, SSM_CONV_DIM),
                  cst(1, SSM_CONV_DIM), cst(1, DT_PAD), cst(1, DT_PAD), cst(1, WIDTH), cst(1, WIDTH),
                  cst(DT_PAD, WIDTH), cst(L, L), _ORDER_ONLY],
        out_specs=row(WIDTH),
        out_shape=jax.ShapeDtypeStruct((s, WIDTH), BF16),
        scratch_shapes=[pltpu.VMEM((SSM_STATE, WIDTH), F32), pltpu.VMEM((8, SSM_CONV_DIM), F32)],
        compiler_params=_cparams("arbitrary"),
        cost_estimate=pl.CostEstimate(
            flops=2 * s * L * (SSM_GROUPS * SSM_STATE + WIDTH + 2 * N_HEADS * LANES) + 4 * s * SSM_STATE * WIDTH,
            transcendentals=s * (L * N_HEADS + SSM_CONV_DIM + 3 * WIDTH),
            bytes_accessed=s * (WIDTH + SSM_CONV_DIM + DT_PAD) * 4 + s * WIDTH * 2),
        name="ssd_mixer",
    )(z, xbc, dt_raw, conv_w, conv_b, dt_bias, a_log, d_skip, norm_w, expand, tri, after)


def _mix_ffn_kernel(x_ref, *refs, chunk):
    *y_refs, wo_ref, gm_ref, g_ref, sc_ref, sh_ref, gate_ref, wg_ref, wu_ref, wd_ref, o_ref = refs
    y, row0 = 0.0, 0
    for y_ref in y_refs:
        y = y + _dot(y_ref[...], wo_ref[row0:row0 + y_ref.shape[1], :])
        row0 += y_ref.shape[1]
    x = x_ref[...] + gm_ref[...] * y
    h = _mod_norm(x, g_ref[...], sc_ref[...], sh_ref[...]).astype(BF16)
    hidden = wg_ref.shape[1]
    n_chunks = hidden // chunk
    cols = lambda c: slice(c * chunk, (c + 1) * chunk)
    gate_up = lambda c: (_dot(h, wg_ref[:, cols(c)]), _dot(h, wu_ref[:, cols(c)]))
    acc = jnp.zeros(x.shape, F32)
    nxt = gate_up(0)
    for c in range(n_chunks):
        g, u = nxt
        if c + 1 < n_chunks:
            nxt = gate_up(c + 1)
        acc = acc + _dot((_silu(g) * u).astype(BF16), wd_ref[cols(c), :])
    o_ref[...] = x + gate_ref[...] * acc


def _mix_ffn(x, ys, w_out, gate_m, g, sc, sh, gate_f, w_gate, w_up, w_down, layer):
    _, s, d = x.shape
    hidden = w_gate.shape[2]
    tm = 512
    row = _stream_rows(tm, d)
    vec = pl.BlockSpec((1, d), lambda i: (0, 0))
    assert sum(y.shape[1] for y in ys) == w_out.shape[1]
    return pl.pallas_call(
        functools.partial(_mix_ffn_kernel, chunk=256),
        grid=(s // tm,),
        in_specs=[row] + [pl.BlockSpec((tm, y.shape[1]), lambda i: (i, 0)) for y in ys]
                 + [_resident((None, w_out.shape[1], d), lambda i: (layer, 0, 0)), vec, vec, vec, vec, vec,
                    _resident((None, d, hidden), lambda i: (layer, 0, 0)),
                    _resident((None, d, hidden), lambda i: (layer, 0, 0)),
                    _resident((None, hidden, d), lambda i: (layer, 0, 0))],
        out_specs=row,
        out_shape=jax.ShapeDtypeStruct((1, s, d), F32),
        compiler_params=_cparams("arbitrary"),
        name="mix_ffn",
    )(x, *ys, w_out, gate_m, g, sc, sh, gate_f, w_gate, w_up, w_down)


def _rope_tables(seq):
    pos = jnp.arange(seq, dtype=F32)
    inv = ROPE_THETA ** (-jnp.arange(0, ROPE_DIM, 2, dtype=F32) / ROPE_DIM)
    j = jnp.arange(LANES) % HEAD_DIM
    inv_lane = jnp.tile(jnp.concatenate([inv, inv, jnp.zeros(HEAD_DIM - ROPE_DIM, F32)]), LANES // HEAD_DIM)
    ang = pos[:, None] * inv_lane[None, :]
    cos, sin = jnp.cos(ang), jnp.sin(ang)
    cos_t = jnp.where(j < ROPE_DIM, cos, 1.0)
    sin_lo = jnp.where(j < ROPE_HALF, -sin, 0.0)
    sin_hi = jnp.where((j >= ROPE_HALF) & (j < ROPE_DIM), sin, 0.0)
    return cos_t, sin_lo, sin_hi


def _moba_mean_rows(kmean):
    nb = kmean.shape[0]
    km = kmean.reshape(nb, N_HEADS, HEAD_DIM).transpose(1, 0, 2)
    return jnp.pad(km, ((0, 0), (0, MOBA_MAX_BLOCKS - nb), (0, LANES - HEAD_DIM)))


def kernel(x, c, ada_w, ada_b, norm_mix, w_in, qn_swa, kn_swa, qn_moba, kn_moba, conv_w, conv_b, dt_bias,
           a_log, d_skip, ssm_norm, w_out, norm_ffn, w_gate, w_up, w_down):
    batch, seq, d = x.shape
    depth = ada_w.shape[0]
    assert batch == 1 and d == D_MODEL
    assert seq % SWA_SPAN == 0 and seq // MOBA_BLOCK <= MOBA_MAX_BLOCKS
    xs = x

    mod = _ada_modulation(c, ada_w, ada_b)
    rope_tabs = _rope_tables(seq)
    lane_head = jnp.arange(LANES) // HEAD_DIM
    hsum = (lane_head[:, None] == lane_head[None, :]).astype(BF16)
    expand = (jnp.arange(DT_PAD)[:, None] == (jnp.arange(WIDTH) // HEAD_DIM)[None, :]).astype(BF16)
    tri = (jnp.arange(SSM_CHUNK)[:, None] >= jnp.arange(SSM_CHUNK)[None, :]).astype(BF16)
    pad8 = lambda v: jnp.pad(v, (0, DT_PAD - N_HEADS)).reshape(1, DT_PAD)
    two = lambda g: jnp.concatenate([g, g]).reshape(1, LANES)

    w_in_b = _cast_bf16(w_in, IN_PROJ_PAD)
    w_out_b, w_gate_b, w_up_b, w_down_b = (_cast_bf16(w) for w in (w_out, w_gate, w_up, w_down))

    for l in range(depth):
        shift_m, scale_m, gate_m, shift_f, scale_f, gate_f = [mod[l, :, i * d:(i + 1) * d] for i in range(6)]
        gains = [two(g[l]) for g in (qn_swa, kn_swa, qn_moba, kn_moba)]
        qa, ka, va, qc, kc, vc, z, xbc, dt_raw, kmean = _in_projection(
            xs, norm_mix[l].reshape(1, d), scale_m, shift_m, w_in_b, l, gains, hsum, rope_tabs)

        dilated = functools.partial(_dilated_attention, qa, ka, va, 0, WIDTH // LANES)
        ssd = functools.partial(
            _ssd_mixer, z, xbc, dt_raw, conv_w[l], conv_b[l].reshape(1, -1), pad8(dt_bias[l]), pad8(a_log[l]),
            jnp.repeat(d_skip[l], HEAD_DIM).reshape(1, WIDTH), ssm_norm[l].reshape(1, WIDTH), expand, tri)
        (y_a,), (y_b,), y_c = _moba_attention(qc, kc, vc, kmean, (dilated,), (ssd,))

        xs = _mix_ffn(xs, (y_a, y_b, y_c), w_out_b, gate_m, norm_ffn[l].reshape(1, d), scale_f,
                      shift_f, gate_f, w_gate_b, w_up_b, w_down_b, l)
    return xs
```
